```python
import functools
import jax, jax.numpy as jnp
from jax import lax
import numpy as np

D_MODEL = 4096
BATCH = 2
SEQ = 4096
DEPTH = 2

GRID_W = 64
CTX_LEN = 256
HEAD_DIM = 128
BLOCK = 128
A_HEADS = 12
A_KV_HEADS = 4
WINDOW = 128
B_GROUPS = 8
B_GROUP_DIM = 128
B_WIDTH = B_GROUPS * B_GROUP_DIM
CHUNK = 128
C_HEADS = 12
C_KV_HEADS = 4
A_WIDTH = A_HEADS * HEAD_DIM
C_WIDTH = C_HEADS * HEAD_DIM
MIX_WIDTH = A_WIDTH + B_WIDTH + C_WIDTH
A_K_OFF = A_WIDTH
A_V_OFF = A_K_OFF + A_KV_HEADS * HEAD_DIM
B_OFF = A_V_OFF + A_KV_HEADS * HEAD_DIM
C_Q_OFF = B_OFF + 2 * B_WIDTH
C_K_OFF = C_Q_OFF + C_WIDTH
C_V_OFF = C_K_OFF + C_KV_HEADS * HEAD_DIM
IN_WIDTH = C_V_OFF + C_KV_HEADS * HEAD_DIM
IN_SPLITS = (A_K_OFF, A_V_OFF, B_OFF, C_Q_OFF, C_K_OFF, C_V_OFF)
N_MOD = 6
D_FF = 11008
N_EXPERTS = 8
TOP_K = 2
D_FF_EXPERT = 4096
N_DENSE = (DEPTH + 1) // 2
N_MOE = DEPTH // 2
ROPE_THETA = 10000.0
EPS = 1e-6

kernel_name = "hybrid_parallel_groups_flow_backbone"

F32 = jnp.float32


def rms_norm(x, g):
    xf = x.astype(F32)
    y = xf * lax.rsqrt(jnp.mean(xf * xf, axis=-1, keepdims=True) + EPS)
    return (y * g.astype(F32)).astype(x.dtype)


def layer_norm(x, g, b):
    xf = x.astype(F32)
    mu = jnp.mean(xf, axis=-1, keepdims=True)
    var = jnp.mean(jnp.square(xf - mu), axis=-1, keepdims=True)
    y = (xf - mu) * lax.rsqrt(var + EPS)
    return (y * g.astype(F32) + b.astype(F32)).astype(x.dtype)


def axial_rope_tables(n_tokens):
    rows = n_tokens // GRID_W
    r, w = jnp.meshgrid(jnp.arange(rows, dtype=F32), jnp.arange(GRID_W, dtype=F32), indexing="ij")
    pos = jnp.stack([r.reshape(-1), w.reshape(-1)], axis=-1)
    n_freq = HEAD_DIM // 4
    inv_freq = ROPE_THETA ** (-jnp.arange(n_freq, dtype=F32) / n_freq)
    ang = pos[:, :, None] * inv_freq
    return jnp.cos(ang), jnp.sin(ang)


def apply_axial_rope(x, cos, sin):
    xf = x.astype(F32).reshape(x.shape[:-1] + (2, 2, HEAD_DIM // 4))
    x1, x2 = xf[..., 0, :], xf[..., 1, :]
    c, s = cos[None, :, None], sin[None, :, None]
    out = jnp.stack([x1 * c - x2 * s, x2 * c + x1 * s], axis=-2)
    return out.reshape(x.shape).astype(x.dtype)


def heads(t, n):
    return t.reshape(t.shape[:-1] + (n, HEAD_DIM))


def dense_attention(q, k, v, sink):
    bsz, nq, nh, hd = q.shape
    n_kv = k.shape[2]
    grp = nh // n_kv
    qg = q.reshape(bsz, nq, n_kv, grp, hd)
    s = jnp.einsum('bqhgd,bkhd->bhgqk', qg, k).astype(F32) * (hd ** -0.5)
    if sink is not None:
        s_sink = jnp.broadcast_to(sink.astype(F32).reshape(n_kv, grp, 1, 1), (bsz, n_kv, grp, nq, 1))
        p = jax.nn.softmax(jnp.concatenate([s, s_sink], axis=-1), axis=-1)[..., :-1]
    else:
        p = jax.nn.softmax(s, axis=-1)
    o = jnp.einsum('bhgqk,bkhd->bqhgd', p.astype(v.dtype), v)
    return o.reshape(bsz, nq, nh * hd)


def window_attention(q, k, v, k_ctx, v_ctx, sink):
    bsz, n, nh, hd = q.shape
    nb = n // BLOCK
    grp = nh // A_KV_HEADS
    qb = q.reshape(bsz, nb, BLOCK, A_KV_HEADS, grp, hd)

    def band(t):
        tb = t.reshape(bsz, nb, BLOCK, A_KV_HEADS, hd)
        tp = jnp.pad(tb, ((0, 0), (1, 1), (0, 0), (0, 0), (0, 0)))
        return jnp.concatenate([tp[:, :-2], tp[:, 1:-1], tp[:, 2:]], axis=2)

    kb, vb = band(k), band(v)
    qpos = jnp.arange(nb)[:, None] * BLOCK + jnp.arange(BLOCK)[None, :]
    kpos = (jnp.arange(nb)[:, None] - 1) * BLOCK + jnp.arange(3 * BLOCK)[None, :]
    valid = ((jnp.abs(kpos[:, None, :] - qpos[:, :, None]) <= WINDOW)
             & (kpos[:, None, :] >= 0) & (kpos[:, None, :] < n))
    scale = hd ** -0.5
    s_loc = jnp.einsum('bnqhgd,bnkhd->bnhgqk', qb, kb).astype(F32) * scale
    s_loc = jnp.where(valid[None, :, None, None], s_loc, -jnp.inf)
    s_ctx = jnp.einsum('bnqhgd,bchd->bnhgqc', qb, k_ctx).astype(F32) * scale
    s_sink = jnp.broadcast_to(sink.astype(F32).reshape(A_KV_HEADS, grp)[None, None, :, :, None, None],
                              s_loc.shape[:-1] + (1,))
    p = jax.nn.softmax(jnp.concatenate([s_loc, s_ctx, s_sink], axis=-1), axis=-1)
    kw = 3 * BLOCK
    n_ctx = k_ctx.shape[1]
    p_loc = p[..., :kw].astype(v.dtype)
    p_ctx = p[..., kw:kw + n_ctx].astype(v.dtype)
    o = (jnp.einsum('bnhgqk,bnkhd->bnqhgd', p_loc, vb)
         + jnp.einsum('bnhgqc,bchd->bnqhgd', p_ctx, v_ctx))
    return o.reshape(bsz, n, nh * hd)


def global_attention(q, k, v):
    bsz, n, nh, hd = q.shape
    nb = n // BLOCK
    qb = q.reshape(bsz, nb, BLOCK, nh, hd).transpose(1, 0, 2, 3, 4)
    o = lax.map(lambda qi: dense_attention(qi, k, v, None), qb)
    return o.transpose(1, 0, 2, 3).reshape(bsz, n, nh * hd)


def chunk_gmlp(z, ln_g, ln_b, ws, bs):
    z = jax.nn.gelu(z, approximate=False)
    u, v = jnp.split(z, 2, axis=-1)
    v = layer_norm(v, ln_g, ln_b)
    bsz, n, _ = v.shape
    vc = v.reshape(bsz, n // CHUNK, CHUNK, B_GROUPS, B_GROUP_DIM)
    mixed = jnp.einsum('gpq,bnqgc->bnpgc', ws, vc) + bs.T[None, None, :, :, None]
    return u * mixed.reshape(bsz, n, B_WIDTH)


def swiglu(h, w_gate, w_up, w_down):
    return (jax.nn.silu(h @ w_gate) * (h @ w_up)) @ w_down


def moe_swiglu(h, w_router, w_gate, w_up, w_down):
    logits = (h @ w_router).astype(F32)
    top_v, top_i = lax.top_k(logits, TOP_K)
    top_w = jax.nn.softmax(top_v, axis=-1)
    gates = jnp.sum(jax.nn.one_hot(top_i, N_EXPERTS, dtype=F32) * top_w[..., None], axis=-2)
    out = jnp.zeros(h.shape[:-1] + (w_down.shape[-1],), h.dtype)
    for e in range(N_EXPERTS):
        he = jax.nn.silu(h @ w_gate[e]) * (h @ w_up[e])
        out = out + gates[..., e:e + 1].astype(h.dtype) * (he @ w_down[e])
    return out


def hybrid_layer(x, xc, c, c_ctx, cos, sin, w_mod, b_mod, g_pre_mix, g_post_mix, g_pre_ffn, g_post_ffn,
                 w_in, w_out, sink_a, qn_c, kn_c, gm_ln_g, gm_ln_b, gm_ws, gm_bs, ffn, need_ctx):
    mod = jax.nn.silu(c) @ w_mod + b_mod
    mod_c = jax.nn.silu(c_ctx) @ w_mod + b_mod
    sh1, sc1, gt1, sh2, sc2, gt2 = [m[:, None, :] for m in jnp.split(mod, N_MOD, axis=-1)]
    csh1, csc1, cgt1, csh2, csc2, cgt2 = jnp.split(mod_c, N_MOD, axis=-1)

    h = rms_norm(x, g_pre_mix) * (1.0 + sc1) + sh1
    hc = rms_norm(xc, g_pre_mix) * (1.0 + csc1) + csh1
    a_q, a_k, a_v, b_z, c_q, c_k, c_v = jnp.split(h @ w_in, IN_SPLITS, axis=-1)
    if need_ctx:
        x_a_q, x_a_k, x_a_v, x_b_z, x_c_q, x_c_k, x_c_v = jnp.split(hc @ w_in, IN_SPLITS, axis=-1)
    else:
        x_a_k, x_a_v = jnp.split(hc @ w_in[:, A_K_OFF:B_OFF], 2, axis=-1)
        x_c_k, x_c_v = jnp.split(hc @ w_in[:, C_K_OFF:], 2, axis=-1)

    k_a_ctx, v_a_ctx = heads(x_a_k, A_KV_HEADS), heads(x_a_v, A_KV_HEADS)
    out_a = window_attention(apply_axial_rope(heads(a_q, A_HEADS), cos, sin),
                             apply_axial_rope(heads(a_k, A_KV_HEADS), cos, sin),
                             heads(a_v, A_KV_HEADS), k_a_ctx, v_a_ctx, sink_a)
    out_b = chunk_gmlp(b_z, gm_ln_g, gm_ln_b, gm_ws, gm_bs)
    k_c_ctx = rms_norm(heads(x_c_k, C_KV_HEADS), kn_c)
    v_c_ctx = heads(x_c_v, C_KV_HEADS)
    q_c = apply_axial_rope(rms_norm(heads(c_q, C_HEADS), qn_c), cos, sin)
    k_c = apply_axial_rope(rms_norm(heads(c_k, C_KV_HEADS), kn_c), cos, sin)
    k_all = jnp.concatenate([k_c_ctx, k_c], axis=1)
    v_all = jnp.concatenate([v_c_ctx, heads(c_v, C_KV_HEADS)], axis=1)
    out_c = global_attention(q_c, k_all, v_all)

    y = jnp.concatenate([out_a, out_b, out_c], axis=-1) @ w_out
    x = x + gt1 * rms_norm(y, g_post_mix)
    h2 = rms_norm(x, g_pre_ffn) * (1.0 + sc2) + sh2
    x = x + gt2 * rms_norm(ffn(h2), g_post_ffn)

    if need_ctx:
        ca = dense_attention(heads(x_a_q, A_HEADS), k_a_ctx, v_a_ctx, sink_a)
        cb = chunk_gmlp(x_b_z, gm_ln_g, gm_ln_b, gm_ws, gm_bs)
        cc = dense_attention(rms_norm(heads(x_c_q, C_HEADS), qn_c), k_c_ctx, v_c_ctx, None)
        yc = jnp.concatenate([ca, cb, cc], axis=-1) @ w_out
        xc = xc + cgt1 * rms_norm(yc, g_post_mix)
        hc2 = rms_norm(xc, g_pre_ffn) * (1.0 + csc2) + csh2
        xc = xc + cgt2 * rms_norm(ffn(hc2), g_post_ffn)
    return x, xc


def setup_inputs(seed: int = 0) -> dict:
    key = jax.random.key(seed)
    ks = jax.random.split(key, 32)

    def nrm(k, shape, scale):
        return jax.random.normal(k, shape, dtype=F32) * scale

    L, D = DEPTH, D_MODEL
    return {
        "x": nrm(ks[0], (BATCH, SEQ, D), 1.0),
        "c": nrm(ks[1], (BATCH, D), 1.0),
        "ctx": nrm(ks[2], (BATCH, CTX_LEN, D), 1.0),
        "c_ctx": nrm(ks[3], (D,), 1.0),
        "w_mod": nrm(ks[4], (L, D, N_MOD * D), 0.5 * D ** -0.5),
        "b_mod": nrm(ks[5], (L, N_MOD * D), 0.01),
        "g_pre_mix": 1.0 + nrm(ks[6], (L, D), 0.02),
        "g_post_mix": 1.0 + nrm(ks[7], (L, D), 0.02),
        "g_pre_ffn": 1.0 + nrm(ks[8], (L, D), 0.02),
        "g_post_ffn": 1.0 + nrm(ks[9], (L, D), 0.02),
        "w_in": nrm(ks[10], (L, D, IN_WIDTH), D ** -0.5),
        "w_out": nrm(ks[11], (L, MIX_WIDTH, D), MIX_WIDTH ** -0.5),
        "sink_a": nrm(ks[12], (L, A_HEADS), 1.0),
        "qn_c": 1.0 + nrm(ks[13], (L, HEAD_DIM), 0.02),
        "kn_c": 1.0 + nrm(ks[14], (L, HEAD_DIM), 0.02),
        "gm_ln_g": 1.0 + nrm(ks[15], (L, B_WIDTH), 0.02),
        "gm_ln_b": nrm(ks[16], (L, B_WIDTH), 0.02),
        "gm_ws": nrm(ks[17], (L, B_GROUPS, CHUNK, CHUNK), CHUNK ** -0.5),
        "gm_bs": 1.0 + nrm(ks[18], (L, B_GROUPS, CHUNK), 0.02),
        "ffn_w_gate": nrm(ks[19], (N_DENSE, D, D_FF), D ** -0.5),
        "ffn_w_up": nrm(ks[20], (N_DENSE, D, D_FF), D ** -0.5),
        "ffn_w_down": nrm(ks[21], (N_DENSE, D_FF, D), D_FF ** -0.5),
        "moe_router": nrm(ks[22], (N_MOE, D, N_EXPERTS), D ** -0.5),
        "moe_w_gate": nrm(ks[23], (N_MOE, N_EXPERTS, D, D_FF_EXPERT), D ** -0.5),
        "moe_w_up": nrm(ks[24], (N_MOE, N_EXPERTS, D, D_FF_EXPERT), D ** -0.5),
        "moe_w_down": nrm(ks[25], (N_MOE, N_EXPERTS, D_FF_EXPERT, D), D_FF_EXPERT ** -0.5),
    }


def reference(x, c, ctx, c_ctx, w_mod, b_mod, g_pre_mix, g_post_mix, g_pre_ffn, g_post_ffn, w_in, w_out,
              sink_a, qn_c, kn_c, gm_ln_g, gm_ln_b, gm_ws, gm_bs, ffn_w_gate, ffn_w_up, ffn_w_down,
              moe_router, moe_w_gate, moe_w_up, moe_w_down):
    cos, sin = axial_rope_tables(x.shape[1])
    xc = ctx
    for l in range(DEPTH):
        i = l // 2
        if l % 2 == 0:
            ffn = functools.partial(swiglu, w_gate=ffn_w_gate[i], w_up=ffn_w_up[i], w_down=ffn_w_down[i])
        else:
            ffn = functools.partial(moe_swiglu, w_router=moe_router[i], w_gate=moe_w_gate[i],
                                    w_up=moe_w_up[i], w_down=moe_w_down[i])
        x, xc = hybrid_layer(x, xc, c, c_ctx, cos, sin, w_mod[l], b_mod[l], g_pre_mix[l], g_post_mix[l],
                             g_pre_ffn[l], g_post_ffn[l], w_in[l], w_out[l], sink_a[l], qn_c[l], kn_c[l],
                             gm_ln_g[l], gm_ln_b[l], gm_ws[l], gm_bs[l], ffn, l < DEPTH - 1)
    return x
```

```python
import functools
import math

import jax
import jax.numpy as jnp
from jax import lax
from jax.experimental import pallas as pl
from jax.experimental.pallas import tpu as pltpu

F32 = jnp.float32
BF16 = jnp.bfloat16

GRID_W = 64
HEAD_DIM = 128
BLOCK = 128
WINDOW = 128
A_KV_HEADS = 4
C_KV_HEADS = 4
B_GROUPS = 8
CHUNK = 128
N_MOD = 6
TOP_K = 2
ROPE_THETA = 10000.0
EPS = 1e-6

V7X_LANES = 128
V7X_VMEM_SCOPED_CAP = 60000 * 1024

ROW_TILE = 512
COL_TILE = 512
FFN_COL_TILE = 256
EW_ROWS = 256
CAST_ROWS = 256
Q_TILE = 256
K_TILE = 512


def _nbytes(shape, dtype):
    return math.prod(shape) * jnp.dtype(dtype).itemsize


def _params(semantics, blocks, scratch=(), temps=()):
    need = 2 * sum(_nbytes(s, d) for s, d in blocks)
    need += sum(_nbytes(s, d) for s, d in scratch)
    need += sum(_nbytes(s, d) for s, d in temps)
    limit = min(V7X_VMEM_SCOPED_CAP, max(need + need // 4, 16 * 1024 * 1024))
    return pltpu.CompilerParams(dimension_semantics=semantics, vmem_limit_bytes=limit)


def _cast_weight(w_ref, wbf_ref):
    rows = w_ref.shape[0]
    step = CAST_ROWS if rows % CAST_ROWS == 0 else V7X_LANES
    assert rows % step == 0

    def body(r, carry):
        sl = pl.ds(pl.multiple_of(r * step, step), step)
        wbf_ref[sl, :] = w_ref[sl, :].astype(BF16)
        return carry

    lax.fori_loop(0, rows // step, body, 0)


def _dot(a, b):
    return jnp.dot(a, b, preferred_element_type=F32)


def _dot_nt(a, b):
    return lax.dot_general(a, b, (((1,), (1,)), ((), ())), preferred_element_type=F32)


def _rms(x, gain):
    return x * lax.rsqrt(jnp.mean(x * x, axis=-1, keepdims=True) + EPS) * gain


def _mod_kernel(c_ref, w_ref, b_ref, o_ref):
    s = jax.nn.silu(c_ref[...])
    o_ref[...] = _dot(s.astype(BF16), w_ref[...].astype(BF16)) + b_ref[...]


def _modulation(c8, w_mod, b_mod):
    n_layers, d, width = w_mod.shape
    tn = COL_TILE
    assert width % tn == 0
    blocks = [((8, d), F32), ((d, tn), F32), ((1, tn), F32), ((8, tn), F32)]
    return pl.pallas_call(
        _mod_kernel,
        out_shape=jax.ShapeDtypeStruct((n_layers, 8, width), F32),
        grid=(n_layers, width // tn),
        in_specs=[pl.BlockSpec((8, d), lambda l, j: (0, 0)),
                  pl.BlockSpec((None, d, tn), lambda l, j: (l, 0, j)),
                  pl.BlockSpec((None, 1, tn), lambda l, j: (l, 0, j))],
        out_specs=pl.BlockSpec((None, 8, tn), lambda l, j: (l, 0, j)),
        compiler_params=_params(("arbitrary", "arbitrary"), blocks, temps=[((d, tn), BF16)]),
        name="modulation",
    )(c8, w_mod, b_mod.reshape(n_layers, 1, width))


def _mod_row_index(i, rows_per_tile, seq, n_lat_rows, n_batch):
    lat_tiles = n_lat_rows // rows_per_tile
    return jnp.where(i < lat_tiles, (i * rows_per_tile) // seq, n_batch)


def _prenorm_kernel(x_ref, g_ref, mod_ref, o_ref):
    y = _rms(x_ref[...], g_ref[...])
    o_ref[...] = (y * (1.0 + mod_ref[1:2, :]) + mod_ref[0:1, :]).astype(BF16)


def _prenorm(xall, gain, mod, seq, n_lat_rows, n_batch):
    t, d = xall.shape
    tr = EW_ROWS
    ridx = functools.partial(_mod_row_index, rows_per_tile=tr, seq=seq, n_lat_rows=n_lat_rows,
                             n_batch=n_batch)
    blocks = [((tr, d), F32), ((1, d), F32), ((N_MOD, d), F32), ((tr, d), BF16)]
    return pl.pallas_call(
        _prenorm_kernel,
        out_shape=jax.ShapeDtypeStruct((t, d), BF16),
        grid=(t // tr,),
        in_specs=[pl.BlockSpec((tr, d), lambda i: (i, 0)),
                  pl.BlockSpec((1, d), lambda i: (0, 0)),
                  pl.BlockSpec((None, N_MOD, d), lambda i: (ridx(i), 0, 0))],
        out_specs=pl.BlockSpec((tr, d), lambda i: (i, 0)),
        compiler_params=_params(("arbitrary",), blocks, temps=[((tr, d), F32)] * 2),
        name="prenorm",
    )(xall, gain.reshape(1, d), mod)


def _postmix_kernel(x_ref, y_ref, gpost_ref, gpre_ref, mod_ref, x1_ref, h2_ref):
    x1 = x_ref[...] + mod_ref[2:3, :] * _rms(y_ref[...], gpost_ref[...])
    x1_ref[...] = x1
    h2 = _rms(x1, gpre_ref[...]) * (1.0 + mod_ref[4:5, :]) + mod_ref[3:4, :]
    h2_ref[...] = h2.astype(BF16)


def _postmix(xall, y, g_post, g_pre, mod, n_rows, seq, n_lat_rows, n_batch):
    t, d = xall.shape
    tr = EW_ROWS
    ridx = functools.partial(_mod_row_index, rows_per_tile=tr, seq=seq, n_lat_rows=n_lat_rows,
                             n_batch=n_batch)
    row = pl.BlockSpec((tr, d), lambda i: (i, 0))
    vec = pl.BlockSpec((1, d), lambda i: (0, 0))
    blocks = [((tr, d), F32)] * 3 + [((tr, d), BF16), ((N_MOD, d), F32)]
    return pl.pallas_call(
        _postmix_kernel,
        out_shape=(jax.ShapeDtypeStruct((t, d), F32), jax.ShapeDtypeStruct((t, d), BF16)),
        grid=(n_rows // tr,),
        in_specs=[row, row, vec, vec, pl.BlockSpec((None, N_MOD, d), lambda i: (ridx(i), 0, 0))],
        out_specs=(row, row),
        compiler_params=_params(("arbitrary",), blocks, temps=[((tr, d), F32)] * 3),
        name="postmix",
    )(xall, y, g_post.reshape(1, d), g_pre.reshape(1, d), mod)


def _final_kernel(x_ref, f_ref, g_ref, mod_ref, o_ref):
    o_ref[...] = x_ref[...] + mod_ref[5:6, :] * _rms(f_ref[...], g_ref[...])


def _final(x1, f, g_post, mod, n_rows, seq, n_lat_rows, n_batch):
    d = x1.shape[1]
    tr = EW_ROWS
    ridx = functools.partial(_mod_row_index, rows_per_tile=tr, seq=seq, n_lat_rows=n_lat_rows,
                             n_batch=n_batch)
    row = pl.BlockSpec((tr, d), lambda i: (i, 0))
    blocks = [((tr, d), F32)] * 3 + [((N_MOD, d), F32)]
    return pl.pallas_call(
        _final_kernel,
        out_shape=jax.ShapeDtypeStruct((n_rows, d), F32),
        grid=(n_rows // tr,),
        in_specs=[row, row, pl.BlockSpec((1, d), lambda i: (0, 0)),
                  pl.BlockSpec((None, N_MOD, d), lambda i: (ridx(i), 0, 0))],
        out_specs=row,
        compiler_params=_params(("arbitrary",), blocks, temps=[((tr, d), F32)] * 2),
        name="final_residual",
    )(x1, f, g_post.reshape(1, d), mod)


def _swap_pairs(x):
    lane = lax.broadcasted_iota(jnp.int32, x.shape, 1)
    quarter = HEAD_DIM // 4
    first = (lane & quarter) == 0
    return jnp.where(first, pltpu.roll(x, HEAD_DIM - quarter, axis=1), pltpu.roll(x, quarter, axis=1))


def _inproj_kernel(*refs, mode, n_q_tiles, q_scale):
    if mode in ("rope", "normrope"):
        h_ref, w_ref, cos_ref, sin_ref, gq_ref, gk_ref, o_ref, wbf_ref = refs
    else:
        h_ref, w_ref, o_ref, wbf_ref = refs
    j = pl.program_id(0)

    @pl.when(pl.program_id(1) == 0)
    def _():
        _cast_weight(w_ref, wbf_ref)

    acc = _dot(h_ref[...], wbf_ref[...])
    if mode == "plain":
        o_ref[...] = acc.astype(BF16)
    elif mode == "gelu":
        o_ref[...] = (0.5 * acc * (1.0 + lax.erf(acc * (2.0 ** -0.5)))).astype(BF16)
    else:
        is_q = j < n_q_tiles
        scale = jnp.where(is_q, q_scale, 1.0).astype(F32)
        cos = cos_ref[...]
        sin = sin_ref[...]
        gain = jnp.where(is_q, gq_ref[...], gk_ref[...])
        for hh in range(acc.shape[1] // HEAD_DIM):
            sl = slice(hh * HEAD_DIM, (hh + 1) * HEAD_DIM)
            xh = acc[:, sl]
            if mode == "normrope":
                xh = _rms(xh, gain)
            xh = xh * cos + _swap_pairs(xh) * sin
            o_ref[:, sl] = (xh * scale).astype(BF16)


def _inproj(h, w_in, layer, col_start, width, mode, rope=None, n_q_tiles=0):
    t, d = h.shape
    tm, tn = ROW_TILE, COL_TILE
    assert col_start % tn == 0 and width % tn == 0 and t % tm == 0
    j0 = col_start // tn
    in_specs = [pl.BlockSpec((tm, d), lambda j, i: (i, 0)),
                pl.BlockSpec((None, d, tn), lambda j, i: (layer, 0, j0 + j))]
    args = [h, w_in]
    blocks = [((tm, d), BF16), ((d, tn), F32), ((tm, tn), BF16)]
    if mode in ("rope", "normrope"):
        cos, sin, rope_idx, gq, gk = rope
        tab = pl.BlockSpec((tm, HEAD_DIM), lambda j, i: (rope_idx(i), 0))
        vec = pl.BlockSpec((1, HEAD_DIM), lambda j, i: (0, 0))
        in_specs += [tab, tab, vec, vec]
        args += [cos, sin, gq, gk]
        blocks += [((tm, HEAD_DIM), F32)] * 2
    kern = functools.partial(_inproj_kernel, mode=mode, n_q_tiles=n_q_tiles,
                             q_scale=HEAD_DIM ** -0.5)
    return pl.pallas_call(
        kern,
        out_shape=jax.ShapeDtypeStruct((t, width), BF16),
        grid=(width // tn, t // tm),
        in_specs=in_specs,
        out_specs=pl.BlockSpec((tm, tn), lambda j, i: (i, j)),
        scratch_shapes=[pltpu.VMEM((d, tn), BF16)],
        compiler_params=_params(("arbitrary", "arbitrary"), blocks, scratch=[((d, tn), BF16)],
                                temps=[((tm, tn), F32)] * 2),
        name="inproj_" + mode,
    )(*args)


def _window_attn_kernel(sink_ref, q_ref, kp_ref, kc_ref, kn_ref, kx_ref, vp_ref, vc_ref, vn_ref,
                        vx_ref, o_ref, *, n_blocks, group):
    n = pl.program_id(1)
    n_ctx = kx_ref.shape[0]
    n_keys = 3 * BLOCK + n_ctx
    rows = group * BLOCK
    assert BLOCK & (BLOCK - 1) == 0
    shift = BLOCK.bit_length() - 1
    r = lax.broadcasted_iota(jnp.int32, (rows, n_keys), 0) & (BLOCK - 1)
    c = lax.broadcasted_iota(jnp.int32, (rows, n_keys), 1)
    first_visible = r + jnp.where(n > 0, 0, BLOCK)
    last_visible = r + 2 * BLOCK - jnp.where(n < n_blocks - 1, 0, BLOCK)
    hidden = ((c < BLOCK) & (c < first_visible)) | (
        (c >= 2 * BLOCK) & (c < 3 * BLOCK) & (c > last_visible))
    valid = jnp.logical_not(hidden)
    rid = lax.broadcasted_iota(jnp.int32, (rows, 1), 0) >> shift
    for hk in range(A_KV_HEADS):
        ksl = slice(hk * HEAD_DIM, (hk + 1) * HEAD_DIM)
        k_all = jnp.concatenate([kp_ref[:, ksl], kc_ref[:, ksl], kn_ref[:, ksl], kx_ref[:, ksl]],
                                axis=0)
        v_all = jnp.concatenate([vp_ref[:, ksl], vc_ref[:, ksl], vn_ref[:, ksl], vx_ref[:, ksl]],
                                axis=0)
        q3 = jnp.concatenate(
            [q_ref[:, (hk * group + g) * HEAD_DIM:(hk * group + g + 1) * HEAD_DIM]
             for g in range(group)], axis=0)
        sink = jnp.zeros((rows, 1), F32)
        for g in range(group):
            sink = jnp.where(rid == g, sink_ref[hk * group + g], sink)
        s = jnp.where(valid, _dot_nt(q3, k_all), -jnp.inf)
        m = jnp.maximum(jnp.max(s, axis=-1, keepdims=True), sink)
        p = jnp.exp(s - m)
        denom = jnp.sum(p, axis=-1, keepdims=True) + jnp.exp(sink - m)
        o = _dot(p.astype(BF16), v_all) / denom
        for g in range(group):
            osl = slice((hk * group + g) * HEAD_DIM, (hk * group + g + 1) * HEAD_DIM)
            o_ref[:, osl] = o[g * BLOCK:(g + 1) * BLOCK, :].astype(BF16)


def _window_attn(za, av, sink, n_batch, seq, n_ctx):
    t = za.shape[0]
    n_heads = sink.shape[0]
    group = n_heads // A_KV_HEADS
    qw = n_heads * HEAD_DIM
    kw = A_KV_HEADS * HEAD_DIM
    assert qw % kw == 0 and seq % BLOCK == 0 and (n_batch * seq) % n_ctx == 0 and WINDOW == BLOCK
    nb = seq // BLOCK
    kcol = qw // kw
    ctx0 = (n_batch * seq) // n_ctx

    def blk(shift):
        return lambda b, n: (b * nb + jnp.clip(n + shift, 0, nb - 1), kcol)

    def vblk(shift):
        return lambda b, n: (b * nb + jnp.clip(n + shift, 0, nb - 1), 0)

    kspec = [pl.BlockSpec((BLOCK, kw), blk(s)) for s in (-1, 0, 1)]
    vspec = [pl.BlockSpec((BLOCK, kw), vblk(s)) for s in (-1, 0, 1)]
    blocks = ([((BLOCK, qw), BF16)] * 2 + [((BLOCK, kw), BF16)] * 6 + [((n_ctx, kw), BF16)] * 2)
    n_keys = 3 * BLOCK + n_ctx
    kern = functools.partial(_window_attn_kernel, n_blocks=nb, group=group)
    return pl.pallas_call(
        kern,
        out_shape=jax.ShapeDtypeStruct((t, qw), BF16),
        grid=(n_batch, nb),
        in_specs=[pl.BlockSpec(memory_space=pltpu.SMEM),
                  pl.BlockSpec((BLOCK, qw), lambda b, n: (b * nb + n, 0))]
                 + kspec + [pl.BlockSpec((n_ctx, kw), lambda b, n: (ctx0 + b, kcol))]
                 + vspec + [pl.BlockSpec((n_ctx, kw), lambda b, n: (ctx0 + b, 0))],
        out_specs=pl.BlockSpec((BLOCK, qw), lambda b, n: (b * nb + n, 0)),
        compiler_params=_params(("arbitrary", "arbitrary"), blocks,
                                temps=[((group * BLOCK, n_keys), F32)] * 4),
        name="window_attn",
    )(sink, za, za, za, za, za, av, av, av, av)


def _ctx_attn_kernel(*refs, group, use_sink):
    if use_sink:
        sink_ref, q_ref, k_ref, v_ref, prev_ref, o_ref = refs
    else:
        q_ref, k_ref, v_ref, prev_ref, o_ref = refs
    del prev_ref
    hk = pl.program_id(1)
    n = q_ref.shape[0]
    q3 = jnp.concatenate([q_ref[:, g * HEAD_DIM:(g + 1) * HEAD_DIM] for g in range(group)], axis=0)
    s = _dot_nt(q3, k_ref[...])
    m = jnp.max(s, axis=-1, keepdims=True)
    if use_sink:
        row = lax.broadcasted_iota(jnp.int32, (group * n, 1), 0)
        sink = jnp.zeros((group * n, 1), F32)
        for g in range(group):
            sink = jnp.where(row >= g * n, sink_ref[hk * group + g], sink)
        m = jnp.maximum(m, sink)
    p = jnp.exp(s - m)
    denom = jnp.sum(p, axis=-1, keepdims=True)
    if use_sink:
        denom = denom + jnp.exp(sink - m)
    o = _dot(p.astype(BF16), v_ref[...]) / denom
    for g in range(group):
        o_ref[:, g * HEAD_DIM:(g + 1) * HEAD_DIM] = o[g * n:(g + 1) * n, :].astype(BF16)


def _ctx_attn(zq, v, out_prev, sink, n_batch, n_lat_rows, n_ctx, n_heads):
    kv_heads = A_KV_HEADS
    group = n_heads // kv_heads
    gw = group * HEAD_DIM
    kcol0 = n_heads
    ctx0 = n_lat_rows // n_ctx
    use_sink = sink is not None
    in_specs = [pl.BlockSpec((n_ctx, gw), lambda b, hk: (ctx0 + b, hk)),
                pl.BlockSpec((n_ctx, HEAD_DIM), lambda b, hk: (ctx0 + b, kcol0 + hk)),
                pl.BlockSpec((n_ctx, HEAD_DIM), lambda b, hk: (ctx0 + b, hk)),
                pl.BlockSpec(memory_space=pl.ANY)]
    args = [zq, zq, v, out_prev]
    if use_sink:
        in_specs = [pl.BlockSpec(memory_space=pltpu.SMEM)] + in_specs
        args = [sink] + args
    blocks = [((n_ctx, gw), BF16)] * 2 + [((n_ctx, HEAD_DIM), BF16)] * 2
    return pl.pallas_call(
        functools.partial(_ctx_attn_kernel, group=group, use_sink=use_sink),
        out_shape=jax.ShapeDtypeStruct(out_prev.shape, out_prev.dtype),
        grid=(n_batch, kv_heads),
        in_specs=in_specs,
        out_specs=pl.BlockSpec((n_ctx, gw), lambda b, hk: (ctx0 + b, hk)),
        input_output_aliases={len(args) - 1: 0},
        compiler_params=_params(("arbitrary", "arbitrary"), blocks,
                                temps=[((group * n_ctx, n_ctx), F32)] * 4),
        name="ctx_attn_sink" if use_sink else "ctx_attn",
    )(*args)


def _gmlp_kernel(z_ref, g_ref, b_ref, ws_ref, bst_ref, o_ref):
    width = g_ref.shape[1]
    u = z_ref[:, :width].astype(F32)
    v = z_ref[:, width:].astype(F32)
    mu = jnp.mean(v, axis=-1, keepdims=True)
    vc = v - mu
    var = jnp.mean(vc * vc, axis=-1, keepdims=True)
    vn = (vc * lax.rsqrt(var + EPS) * g_ref[...] + b_ref[...]).astype(BF16)
    gd = width // B_GROUPS
    for g in range(B_GROUPS):
        sl = slice(g * gd, (g + 1) * gd)
        mixed = _dot(ws_ref[g].astype(BF16), vn[:, sl]) + bst_ref[:, g:g + 1]
        o_ref[:, sl] = (u[:, sl] * mixed).astype(BF16)


def _gmlp(zb, ln_g, ln_b, ws, bs, n_rows):
    t, w2 = zb.shape
    width = w2 // 2
    blocks = [((CHUNK, w2), BF16), ((1, width), F32), ((1, width), F32),
              (ws.shape, F32), ((CHUNK, B_GROUPS), F32), ((CHUNK, width), BF16)]
    return pl.pallas_call(
        _gmlp_kernel,
        out_shape=jax.ShapeDtypeStruct((t, width), BF16),
        grid=(n_rows // CHUNK,),
        in_specs=[pl.BlockSpec((CHUNK, w2), lambda i: (i, 0)),
                  pl.BlockSpec((1, width), lambda i: (0, 0)),
                  pl.BlockSpec((1, width), lambda i: (0, 0)),
                  pl.BlockSpec(ws.shape, lambda i: (0, 0, 0)),
                  pl.BlockSpec((CHUNK, B_GROUPS), lambda i: (0, 0))],
        out_specs=pl.BlockSpec((CHUNK, width), lambda i: (i, 0)),
        compiler_params=_params(("arbitrary",), blocks, temps=[((CHUNK, w2), F32)] * 2),
        name="chunk_gmlp",
    )(zb, ln_g.reshape(1, width), ln_b.reshape(1, width), ws, bs.T)


def _global_attn_kernel(q_ref, kx_ref, vx_ref, k_ref, v_ref, o_ref, m_ref, l_ref, acc_ref, *, group):
    tq = q_ref.shape[0]
    q3 = jnp.concatenate([q_ref[:, g * HEAD_DIM:(g + 1) * HEAD_DIM] for g in range(group)], axis=0)

    def update(kc, vc):
        s = _dot_nt(q3, kc)
        m_old = m_ref[...]
        m_new = jnp.maximum(m_old, jnp.max(s, axis=-1, keepdims=True))
        alpha = jnp.exp(m_old - m_new)
        p = jnp.exp(s - m_new)
        l_ref[...] = alpha * l_ref[...] + jnp.sum(p, axis=-1, keepdims=True)
        acc_ref[...] = alpha * acc_ref[...] + _dot(p.astype(BF16), vc)
        m_ref[...] = m_new

    m_ref[...] = jnp.full(m_ref.shape, -jnp.inf, F32)
    l_ref[...] = jnp.zeros(l_ref.shape, F32)
    acc_ref[...] = jnp.zeros(acc_ref.shape, F32)
    update(kx_ref[...], vx_ref[...])

    def body(c, carry):
        sl = pl.ds(pl.multiple_of(c * K_TILE, K_TILE), K_TILE)
        update(k_ref[sl, :], v_ref[sl, :])
        return carry

    lax.fori_loop(0, k_ref.shape[0] // K_TILE, body, 0)
    o = acc_ref[...] / l_ref[...]
    for g in range(group):
        o_ref[:, g * HEAD_DIM:(g + 1) * HEAD_DIM] = o[g * tq:(g + 1) * tq, :].astype(BF16)


def _global_attn(zc, cv, n_batch, seq, n_ctx, n_heads):
    t = zc.shape[0]
    group = n_heads // C_KV_HEADS
    gw = group * HEAD_DIM
    tq = Q_TILE
    assert seq % tq == 0 and seq % K_TILE == 0 and (n_batch * seq) % n_ctx == 0
    nq = seq // tq
    ctx0 = (n_batch * seq) // n_ctx
    rows = group * tq
    blocks = [((tq, gw), BF16)] * 2 + [((n_ctx, HEAD_DIM), BF16)] * 2 + [((seq, HEAD_DIM), BF16)] * 2
    scratch = [((rows, 1), F32), ((rows, 1), F32), ((rows, HEAD_DIM), F32)]
    lane_padded = [((rows, V7X_LANES), F32)] * 3
    return pl.pallas_call(
        functools.partial(_global_attn_kernel, group=group),
        out_shape=jax.ShapeDtypeStruct((t, n_heads * HEAD_DIM), BF16),
        grid=(n_batch, C_KV_HEADS, nq),
        in_specs=[pl.BlockSpec((tq, gw), lambda b, hk, i: (b * nq + i, hk)),
                  pl.BlockSpec((n_ctx, HEAD_DIM), lambda b, hk, i: (ctx0 + b, n_heads + hk)),
                  pl.BlockSpec((n_ctx, HEAD_DIM), lambda b, hk, i: (ctx0 + b, hk)),
                  pl.BlockSpec((seq, HEAD_DIM), lambda b, hk, i: (b, n_heads + hk)),
                  pl.BlockSpec((seq, HEAD_DIM), lambda b, hk, i: (b, hk))],
        out_specs=pl.BlockSpec((tq, gw), lambda b, hk, i: (b * nq + i, hk)),
        scratch_shapes=[pltpu.VMEM(s, d) for s, d in scratch],
        compiler_params=_params(("arbitrary", "arbitrary", "arbitrary"), blocks, scratch=lane_padded,
                                temps=[((rows, K_TILE), F32)] * 4),
        name="global_attn",
    )(zc, zc, cv, zc, cv)


def _outproj_kernel(a_ref, b_ref, c_ref, w_ref, o_ref, wbf_ref):
    @pl.when(pl.program_id(1) == 0)
    def _():
        _cast_weight(w_ref, wbf_ref)

    ka, kb = a_ref.shape[1], b_ref.shape[1]
    acc = _dot(a_ref[...], wbf_ref[0:ka, :])
    acc += _dot(b_ref[...], wbf_ref[ka:ka + kb, :])
    acc += _dot(c_ref[...], wbf_ref[ka + kb:, :])
    o_ref[...] = acc


def _outproj(oa, ob, oc, w_out, layer, n_rows):
    t = oa.shape[0]
    _, k, d = w_out.shape
    tm, tn = ROW_TILE, COL_TILE
    assert oa.shape[1] + ob.shape[1] + oc.shape[1] == k and d % tn == 0 and n_rows % tm == 0
    blocks = [((tm, k), BF16), ((k, tn), F32), ((tm, tn), F32)]
    return pl.pallas_call(
        _outproj_kernel,
        out_shape=jax.ShapeDtypeStruct((t, d), F32),
        grid=(d // tn, n_rows // tm),
        in_specs=[pl.BlockSpec((tm, oa.shape[1]), lambda j, i: (i, 0)),
                  pl.BlockSpec((tm, ob.shape[1]), lambda j, i: (i, 0)),
                  pl.BlockSpec((tm, oc.shape[1]), lambda j, i: (i, 0)),
                  pl.BlockSpec((None, k, tn), lambda j, i: (layer, 0, j))],
        out_specs=pl.BlockSpec((tm, tn), lambda j, i: (i, j)),
        scratch_shapes=[pltpu.VMEM((k, tn), BF16)],
        compiler_params=_params(("arbitrary", "arbitrary"), blocks, scratch=[((k, tn), BF16)],
                                temps=[((tm, tn), F32)] * 2),
        name="outproj",
    )(oa, ob, oc, w_out)


def _ffn_up_kernel(h_ref, wg_ref, wu_ref, o_ref, wgbf_ref, wubf_ref):
    @pl.when(pl.program_id(1) == 0)
    def _():
        _cast_weight(wg_ref, wgbf_ref)
        _cast_weight(wu_ref, wubf_ref)

    h = h_ref[...]
    gate = _dot(h, wgbf_ref[...])
    up = _dot(h, wubf_ref[...])
    o_ref[...] = (jax.nn.silu(gate) * up).astype(BF16)


def _ffn_up(h2, w_gate, w_up, sel, n_rows):
    t, d = h2.shape
    f = w_gate.shape[-1]
    tm, tn = ROW_TILE, FFN_COL_TILE
    assert f % tn == 0 and n_rows % tm == 0
    lead = (None,) * len(sel)
    wspec = pl.BlockSpec(lead + (d, tn), lambda j, i: sel + (0, j))
    blocks = [((tm, d), BF16), ((d, tn), F32), ((d, tn), F32), ((tm, tn), BF16)]
    return pl.pallas_call(
        _ffn_up_kernel,
        out_shape=jax.ShapeDtypeStruct((t, f), BF16),
        grid=(f // tn, n_rows // tm),
        in_specs=[pl.BlockSpec((tm, d), lambda j, i: (i, 0)), wspec, wspec],
        out_specs=pl.BlockSpec((tm, tn), lambda j, i: (i, j)),
        scratch_shapes=[pltpu.VMEM((d, tn), BF16)] * 2,
        compiler_params=_params(("arbitrary", "arbitrary"), blocks, scratch=[((d, tn), BF16)] * 2,
                                temps=[((tm, tn), F32)] * 3),
        name="ffn_up",
    )(h2, w_gate, w_up)


def _ffn_down_kernel(*refs, has_prev, gate_col):
    refs = list(refs)
    a_ref, w_ref = refs[0], refs[1]
    pos = 2
    gates_ref = prev_ref = None
    if gate_col is not None:
        gates_ref = refs[pos]
        pos += 1
    if has_prev:
        prev_ref = refs[pos]
        pos += 1
    o_ref, wbf_ref = refs[pos], refs[pos + 1]

    @pl.when(pl.program_id(1) == 0)
    def _():
        _cast_weight(w_ref, wbf_ref)

    acc = _dot(a_ref[...], wbf_ref[...])
    if gates_ref is not None:
        acc = gates_ref[:, gate_col:gate_col + 1] * acc
    if prev_ref is not None:
        acc = prev_ref[...] + acc
    o_ref[...] = acc


def _ffn_down(a, w_down, sel, k_start, k_size, n_rows, prev=None, gates=None, gate_col=None):
    t = a.shape[0]
    d = w_down.shape[-1]
    tm, tn = ROW_TILE, COL_TILE
    assert k_start % k_size == 0 and d % tn == 0 and n_rows % tm == 0
    kb = k_start // k_size
    lead = (None,) * len(sel)
    in_specs = [pl.BlockSpec((tm, k_size), lambda j, i: (i, kb)),
                pl.BlockSpec(lead + (k_size, tn), lambda j, i: sel + (kb, j))]
    args = [a, w_down]
    blocks = [((tm, k_size), BF16), ((k_size, tn), F32), ((tm, tn), F32)]
    if gates is not None:
        in_specs.append(pl.BlockSpec((tm, gates.shape[1]), lambda j, i: (i, 0)))
        args.append(gates)
        blocks.append(((tm, gates.shape[1]), F32))
    aliases = {}
    if prev is not None:
        in_specs.append(pl.BlockSpec((tm, tn), lambda j, i: (i, j)))
        aliases = {len(args): 0}
        args.append(prev)
        blocks.append(((tm, tn), F32))
    kern = functools.partial(_ffn_down_kernel, has_prev=prev is not None,
                             gate_col=gate_col if gates is not None else None)
    return pl.pallas_call(
        kern,
        out_shape=jax.ShapeDtypeStruct((t, d), F32),
        grid=(d // tn, n_rows // tm),
        in_specs=in_specs,
        out_specs=pl.BlockSpec((tm, tn), lambda j, i: (i, j)),
        scratch_shapes=[pltpu.VMEM((k_size, tn), BF16)],
        input_output_aliases=aliases,
        compiler_params=_params(("arbitrary", "arbitrary"), blocks, scratch=[((k_size, tn), BF16)],
                                temps=[((tm, tn), F32)] * 2),
        name="ffn_down",
    )(*args)


def _router_kernel(h_ref, w_ref, o_ref, *, n_experts):
    logits = jnp.dot(h_ref[...].astype(F32), w_ref[...], preferred_element_type=F32,
                     precision=lax.Precision.HIGHEST)
    lane = lax.broadcasted_iota(jnp.int32, logits.shape, 1).astype(F32)
    neg = -jnp.inf
    logits = jnp.where(lane < n_experts, logits, neg)
    gates = jnp.zeros(logits.shape, F32)
    picked = []
    remaining = logits
    for _ in range(TOP_K):
        top = jnp.max(remaining, axis=-1, keepdims=True)
        idx = jnp.min(jnp.where(remaining == top, lane, float(logits.shape[1])), axis=-1,
                      keepdims=True)
        picked.append((top, idx))
        remaining = jnp.where(lane == idx, neg, remaining)
    top0 = picked[0][0]
    denom = sum(jnp.exp(tv - top0) for tv, _ in picked)
    for tv, idx in picked:
        gates = gates + jnp.where(lane == idx, jnp.exp(tv - top0) / denom, 0.0)
    o_ref[...] = gates


def _router(h2, w_router, n_rows):
    t, d = h2.shape
    n_experts = w_router.shape[1]
    assert n_experts <= V7X_LANES
    wpad = jnp.pad(w_router, ((0, 0), (0, V7X_LANES - n_experts)))
    tr = EW_ROWS
    blocks = [((tr, d), BF16), ((d, V7X_LANES), F32), ((tr, V7X_LANES), F32)]
    return pl.pallas_call(
        functools.partial(_router_kernel, n_experts=n_experts),
        out_shape=jax.ShapeDtypeStruct((t, V7X_LANES), F32),
        grid=(n_rows // tr,),
        in_specs=[pl.BlockSpec((tr, d), lambda i: (i, 0)),
                  pl.BlockSpec((d, V7X_LANES), lambda i: (0, 0))],
        out_specs=pl.BlockSpec((tr, V7X_LANES), lambda i: (i, 0)),
        compiler_params=_params(("arbitrary",), blocks, temps=[((tr, d), F32)] * 3),
        name="moe_router",
    )(h2, wpad)


def _rope_tables(seq, pad_rows):
    n = jnp.arange(seq)
    pos_r = (n // GRID_W).astype(F32)
    pos_w = (n % GRID_W).astype(F32)
    n_freq = HEAD_DIM // 4
    inv_freq = ROPE_THETA ** (-jnp.arange(n_freq, dtype=F32) / n_freq)
    ar = pos_r[:, None] * inv_freq
    aw = pos_w[:, None] * inv_freq
    cos = jnp.concatenate([jnp.cos(ar), jnp.cos(ar), jnp.cos(aw), jnp.cos(aw)], axis=-1)
    sin = jnp.concatenate([-jnp.sin(ar), jnp.sin(ar), -jnp.sin(aw), jnp.sin(aw)], axis=-1)
    cos = jnp.concatenate([cos, jnp.ones((pad_rows, HEAD_DIM), F32)], axis=0)
    sin = jnp.concatenate([sin, jnp.zeros((pad_rows, HEAD_DIM), F32)], axis=0)
    return cos, sin


def kernel(x, c, ctx, c_ctx, w_mod, b_mod, g_pre_mix, g_post_mix, g_pre_ffn, g_post_ffn, w_in, w_out,
           sink_a, qn_c, kn_c, gm_ln_g, gm_ln_b, gm_ws, gm_bs, ffn_w_gate, ffn_w_up, ffn_w_down,
           moe_router, moe_w_gate, moe_w_up, moe_w_down):
    n_batch, seq, d = x.shape
    n_ctx = ctx.shape[1]
    depth = w_mod.shape[0]
    n_lat = n_batch * seq
    n_ctx_rows = n_batch * n_ctx
    t = n_lat + n_ctx_rows
    a_heads = sink_a.shape[1]
    a_w = a_heads * HEAD_DIM
    akv_w = A_KV_HEADS * HEAD_DIM
    b_w = gm_ln_g.shape[1]
    ckv_w = C_KV_HEADS * HEAD_DIM
    c_w = w_in.shape[2] - a_w - 2 * akv_w - 2 * b_w - 2 * ckv_w
    c_heads = c_w // HEAD_DIM
    assert n_ctx_rows == ROW_TILE and seq % ROW_TILE == 0 and n_batch + 1 <= 8
    assert a_w == c_w and akv_w == COL_TILE and ckv_w == COL_TILE

    xall = jnp.concatenate([x.reshape(n_lat, d), ctx.reshape(n_ctx_rows, d)], axis=0)
    c8 = jnp.concatenate([c, c_ctx[None, :], jnp.zeros((8 - n_batch - 1, d), F32)], axis=0)
    mod_all = _modulation(c8, w_mod, b_mod).reshape(depth, 8, N_MOD, d)

    cos, sin = _rope_tables(seq, ROW_TILE)
    lat_tiles = n_lat // ROW_TILE
    seq_tiles = seq // ROW_TILE

    def rope_idx(i):
        return jnp.where(i < lat_tiles, i % seq_tiles, seq_tiles)

    for l in range(depth):
        need_ctx = l < depth - 1
        n_rows = t if need_ctx else n_lat
        mod = mod_all[l]
        geo = dict(seq=seq, n_lat_rows=n_lat, n_batch=n_batch)

        h = _prenorm(xall, g_pre_mix[l], mod, **geo)
        gq, gk = qn_c[l].reshape(1, HEAD_DIM), kn_c[l].reshape(1, HEAD_DIM)
        rope = (cos, sin, rope_idx, gq, gk)
        col = 0
        za = _inproj(h, w_in, l, col, a_w + akv_w, "rope", rope, n_q_tiles=a_w // COL_TILE)
        col += a_w + akv_w
        av = _inproj(h, w_in, l, col, akv_w, "plain")
        col += akv_w
        zb = _inproj(h, w_in, l, col, 2 * b_w, "gelu")
        col += 2 * b_w
        zc = _inproj(h, w_in, l, col, c_w + ckv_w, "normrope", rope, n_q_tiles=c_w // COL_TILE)
        col += c_w + ckv_w
        cv = _inproj(h, w_in, l, col, ckv_w, "plain")

        oa = _window_attn(za, av, sink_a[l], n_batch, seq, n_ctx)
        ob = _gmlp(zb, gm_ln_g[l], gm_ln_b[l], gm_ws[l], gm_bs[l], n_rows)
        oc = _global_attn(zc, cv, n_batch, seq, n_ctx, c_heads)
        if need_ctx:
            oa = _ctx_attn(za, av, oa, sink_a[l], n_batch, n_lat, n_ctx, a_heads)
            oc = _ctx_attn(zc, cv, oc, None, n_batch, n_lat, n_ctx, c_heads)

        y = _outproj(oa, ob, oc, w_out, l, n_rows)
        x1, h2 = _postmix(xall, y, g_post_mix[l], g_pre_ffn[l], mod, n_rows, **geo)

        i = l // 2
        if l % 2 == 0:
            f_dim = ffn_w_gate.shape[-1]
            hmid = _ffn_up(h2, ffn_w_gate, ffn_w_up, (i,), n_rows)
            half = f_dim // 2
            f = _ffn_down(hmid, ffn_w_down, (i,), 0, half, n_rows)
            f = _ffn_down(hmid, ffn_w_down, (i,), half, half, n_rows, prev=f)
        else:
            gates = _router(h2, moe_router[i], n_rows)
            f = None
            for e in range(moe_w_gate.shape[1]):
                hmid = _ffn_up(h2, moe_w_gate, moe_w_up, (i, e), n_rows)
                f = _ffn_down(hmid, moe_w_down, (i, e), 0, moe_w_down.shape[2], n_rows, prev=f,
                              gates=gates, gate_col=e)
        if need_ctx:
            xall = _final(x1, f, g_post_ffn[l], mod, t, **geo)
        else:
            xall = _final(x1, f, g_post_ffn[l], mod, n_lat, **geo)
    return xall[:n_lat].reshape(n_batch, seq, d)
```

```python
import functools
import math

import jax
import jax.numpy as jnp
from jax import lax
from jax.experimental import pallas as pl
from jax.experimental.pallas import tpu as pltpu

F32 = jnp.float32
BF16 = jnp.bfloat16

GRID_W = 64
HEAD_DIM = 128
BLOCK = 128
WINDOW = 128
A_KV_HEADS = 4
C_KV_HEADS = 4
B_GROUPS = 8
CHUNK = 128
N_MOD = 6
TOP_K = 2
ROPE_THETA = 10000.0
EPS = 1e-6
LOG2E = math.log2(math.e)

V7X_LANES = 128
V7X_VMEM_SCOPED_CAP = 60000 * 1024

ROW_TILE = 512
COL_TILE = 512
FFN_COL_TILE = 256
EW_ROWS = 256
CAST_ROWS = 256
Q_TILE = 256
K_TILE = 512
GATHER_UNROLL = 8


def _nbytes(shape, dtype):
    return math.prod(shape) * jnp.dtype(dtype).itemsize


def _params(semantics, blocks, scratch=(), temps=()):
    need = 2 * sum(_nbytes(s, d) for s, d in blocks)
    need += sum(_nbytes(s, d) for s, d in scratch)
    need += sum(_nbytes(s, d) for s, d in temps)
    limit = min(V7X_VMEM_SCOPED_CAP, max(need + need // 4, 16 * 1024 * 1024))
    return pltpu.CompilerParams(dimension_semantics=semantics, vmem_limit_bytes=limit)


def _cast_weight(w_ref, wbf_ref):
    rows = w_ref.shape[0]
    step = CAST_ROWS if rows % CAST_ROWS == 0 else V7X_LANES
    assert rows % step == 0

    def body(r, carry):
        sl = pl.ds(pl.multiple_of(r * step, step), step)
        wbf_ref[sl, :] = w_ref[sl, :].astype(BF16)
        return carry

    lax.fori_loop(0, rows // step, body, 0)


def _dot(a, b):
    return jnp.dot(a, b, preferred_element_type=F32)


def _dot_nt(a, b):
    return lax.dot_general(a, b, (((1,), (1,)), ((), ())), preferred_element_type=F32)


def _rms(x, gain):
    return x * lax.rsqrt(jnp.mean(x * x, axis=-1, keepdims=True) + EPS) * gain


def _mod_kernel(c_ref, w_ref, b_ref, o_ref):
    s = jax.nn.silu(c_ref[...])
    o_ref[...] = _dot(s.astype(BF16), w_ref[...].astype(BF16)) + b_ref[...]


def _modulation(c8, w_mod, b_mod):
    n_layers, d, width = w_mod.shape
    tn = COL_TILE
    assert width % tn == 0
    blocks = [((8, d), F32), ((d, tn), F32), ((1, tn), F32), ((8, tn), F32)]
    return pl.pallas_call(
        _mod_kernel,
        out_shape=jax.ShapeDtypeStruct((n_layers, 8, width), F32),
        grid=(n_layers, width // tn),
        in_specs=[pl.BlockSpec((8, d), lambda l, j: (0, 0)),
                  pl.BlockSpec((None, d, tn), lambda l, j: (l, 0, j)),
                  pl.BlockSpec((None, 1, tn), lambda l, j: (l, 0, j))],
        out_specs=pl.BlockSpec((None, 8, tn), lambda l, j: (l, 0, j)),
        compiler_params=_params(("arbitrary", "arbitrary"), blocks, temps=[((d, tn), BF16)]),
        name="modulation",
    )(c8, w_mod, b_mod.reshape(n_layers, 1, width))


def _mod_row_index(i, rows_per_tile, seq, n_lat_rows, n_batch):
    lat_tiles = n_lat_rows // rows_per_tile
    return jnp.where(i < lat_tiles, (i * rows_per_tile) // seq, n_batch)


def _prenorm_kernel(x_ref, g_ref, mod_ref, o_ref):
    y = _rms(x_ref[...], g_ref[...])
    o_ref[...] = (y * (1.0 + mod_ref[1:2, :]) + mod_ref[0:1, :]).astype(BF16)


def _prenorm(xall, gain, mod, seq, n_lat_rows, n_batch):
    t, d = xall.shape
    tr = EW_ROWS
    ridx = functools.partial(_mod_row_index, rows_per_tile=tr, seq=seq, n_lat_rows=n_lat_rows,
                             n_batch=n_batch)
    blocks = [((tr, d), F32), ((1, d), F32), ((N_MOD, d), F32), ((tr, d), BF16)]
    return pl.pallas_call(
        _prenorm_kernel,
        out_shape=jax.ShapeDtypeStruct((t, d), BF16),
        grid=(t // tr,),
        in_specs=[pl.BlockSpec((tr, d), lambda i: (i, 0)),
                  pl.BlockSpec((1, d), lambda i: (0, 0)),
                  pl.BlockSpec((None, N_MOD, d), lambda i: (ridx(i), 0, 0))],
        out_specs=pl.BlockSpec((tr, d), lambda i: (i, 0)),
        compiler_params=_params(("arbitrary",), blocks, temps=[((tr, d), F32)] * 2),
        name="prenorm",
    )(xall, gain.reshape(1, d), mod)


def _postmix_kernel(x_ref, y_ref, gpost_ref, gpre_ref, mod_ref, x1_ref, h2_ref, *h2f_ref):
    x1 = x_ref[...] + mod_ref[2:3, :] * _rms(y_ref[...], gpost_ref[...])
    x1_ref[...] = x1
    h2 = _rms(x1, gpre_ref[...]) * (1.0 + mod_ref[4:5, :]) + mod_ref[3:4, :]
    h2_ref[...] = h2.astype(BF16)
    for ref in h2f_ref:
        ref[...] = h2


def _postmix(xall, y, g_post, g_pre, mod, n_rows, want_f32, seq, n_lat_rows, n_batch):
    d = xall.shape[1]
    tr = EW_ROWS
    ridx = functools.partial(_mod_row_index, rows_per_tile=tr, seq=seq, n_lat_rows=n_lat_rows,
                             n_batch=n_batch)
    row = pl.BlockSpec((tr, d), lambda i: (i, 0))
    vec = pl.BlockSpec((1, d), lambda i: (0, 0))
    blocks = [((tr, d), F32)] * (3 + want_f32) + [((tr, d), BF16), ((N_MOD, d), F32)]
    out_shape = [jax.ShapeDtypeStruct((n_rows, d), F32), jax.ShapeDtypeStruct((n_rows, d), BF16)]
    out_shape += [jax.ShapeDtypeStruct((n_rows, d), F32)] * want_f32
    return pl.pallas_call(
        _postmix_kernel,
        out_shape=tuple(out_shape),
        grid=(n_rows // tr,),
        in_specs=[row, row, vec, vec, pl.BlockSpec((None, N_MOD, d), lambda i: (ridx(i), 0, 0))],
        out_specs=(row,) * len(out_shape),
        compiler_params=_params(("arbitrary",), blocks, temps=[((tr, d), F32)] * 3),
        name="postmix",
    )(xall, y, g_post.reshape(1, d), g_pre.reshape(1, d), mod)


def _final_kernel(x_ref, f_ref, g_ref, mod_ref, o_ref):
    o_ref[...] = x_ref[...] + mod_ref[5:6, :] * _rms(f_ref[...], g_ref[...])


def _final(x1, f, g_post, mod, n_rows, seq, n_lat_rows, n_batch):
    d = x1.shape[1]
    tr = EW_ROWS
    ridx = functools.partial(_mod_row_index, rows_per_tile=tr, seq=seq, n_lat_rows=n_lat_rows,
                             n_batch=n_batch)
    row = pl.BlockSpec((tr, d), lambda i: (i, 0))
    blocks = [((tr, d), F32)] * 3 + [((N_MOD, d), F32)]
    return pl.pallas_call(
        _final_kernel,
        out_shape=jax.ShapeDtypeStruct((n_rows, d), F32),
        grid=(n_rows // tr,),
        in_specs=[row, row, pl.BlockSpec((1, d), lambda i: (0, 0)),
                  pl.BlockSpec((None, N_MOD, d), lambda i: (ridx(i), 0, 0))],
        out_specs=row,
        compiler_params=_params(("arbitrary",), blocks, temps=[((tr, d), F32)] * 2),
        name="final_residual",
    )(x1, f, g_post.reshape(1, d), mod)


def _swap_pairs(x):
    lane = lax.broadcasted_iota(jnp.int32, x.shape, 1)
    quarter = HEAD_DIM // 4
    first = (lane & quarter) == 0
    return jnp.where(first, pltpu.roll(x, HEAD_DIM - quarter, axis=1), pltpu.roll(x, quarter, axis=1))


def _inproj_kernel(*refs, mode, n_q_tiles, q_scale):
    if mode in ("rope", "normrope"):
        h_ref, w_ref, cos_ref, sin_ref, gq_ref, gk_ref, o_ref, wbf_ref = refs
    else:
        h_ref, w_ref, o_ref, wbf_ref = refs
    j = pl.program_id(0)

    @pl.when(pl.program_id(1) == 0)
    def _():
        _cast_weight(w_ref, wbf_ref)

    acc = _dot(h_ref[...], wbf_ref[...])
    if mode == "plain":
        o_ref[...] = acc.astype(BF16)
    elif mode == "gelu":
        o_ref[...] = (0.5 * acc * (1.0 + lax.erf(acc * (2.0 ** -0.5)))).astype(BF16)
    else:
        is_q = j < n_q_tiles
        scale = jnp.where(is_q, q_scale, 1.0).astype(F32)
        cos = cos_ref[...]
        sin = sin_ref[...]
        gain = jnp.where(is_q, gq_ref[...], gk_ref[...])
        for hh in range(acc.shape[1] // HEAD_DIM):
            sl = slice(hh * HEAD_DIM, (hh + 1) * HEAD_DIM)
            xh = acc[:, sl]
            if mode == "normrope":
                xh = _rms(xh, gain)
            xh = xh * cos + _swap_pairs(xh) * sin
            o_ref[:, sl] = (xh * scale).astype(BF16)


def _inproj(h, w_in, layer, col_start, width, mode, rope=None, n_q_tiles=0):
    t, d = h.shape
    tm, tn = ROW_TILE, COL_TILE
    assert col_start % tn == 0 and width % tn == 0 and t % tm == 0
    j0 = col_start // tn
    in_specs = [pl.BlockSpec((tm, d), lambda j, i: (i, 0)),
                pl.BlockSpec((None, d, tn), lambda j, i: (layer, 0, j0 + j))]
    args = [h, w_in]
    blocks = [((tm, d), BF16), ((d, tn), F32), ((tm, tn), BF16)]
    if mode in ("rope", "normrope"):
        cos, sin, rope_idx, gq, gk = rope
        tab = pl.BlockSpec((tm, HEAD_DIM), lambda j, i: (rope_idx(i), 0))
        vec = pl.BlockSpec((1, HEAD_DIM), lambda j, i: (0, 0))
        in_specs += [tab, tab, vec, vec]
        args += [cos, sin, gq, gk]
        blocks += [((tm, HEAD_DIM), F32)] * 2
    kern = functools.partial(_inproj_kernel, mode=mode, n_q_tiles=n_q_tiles,
                             q_scale=HEAD_DIM ** -0.5 * LOG2E)
    return pl.pallas_call(
        kern,
        out_shape=jax.ShapeDtypeStruct((t, width), BF16),
        grid=(width // tn, t // tm),
        in_specs=in_specs,
        out_specs=pl.BlockSpec((tm, tn), lambda j, i: (i, j)),
        scratch_shapes=[pltpu.VMEM((d, tn), BF16)],
        compiler_params=_params(("arbitrary", "arbitrary"), blocks, scratch=[((d, tn), BF16)],
                                temps=[((tm, tn), F32)] * 2),
        name="inproj_" + mode,
    )(*args)


def _window_attn_kernel(sink_ref, q_ref, kp_ref, kc_ref, kn_ref, kx_ref, vp_ref, vc_ref, vn_ref,
                        vx_ref, o_ref, *, n_blocks, group):
    n = pl.program_id(1)
    n_ctx = kx_ref.shape[0]
    n_keys = 3 * BLOCK + n_ctx
    rows = group * BLOCK
    assert BLOCK & (BLOCK - 1) == 0
    shift = BLOCK.bit_length() - 1
    r = lax.broadcasted_iota(jnp.int32, (rows, n_keys), 0) & (BLOCK - 1)
    c = lax.broadcasted_iota(jnp.int32, (rows, n_keys), 1)
    is_lat = n < n_blocks
    first_visible = r + jnp.where((n > 0) & is_lat, 0, BLOCK)
    last_visible = r + 2 * BLOCK - jnp.where(n < n_blocks - 1, 0, BLOCK)
    current_hidden_end = jnp.where(is_lat, BLOCK, 2 * BLOCK)
    hidden = ((c < BLOCK) & (c < first_visible)) | ((c >= BLOCK) & (c < current_hidden_end)) | (
        (c >= 2 * BLOCK) & (c < 3 * BLOCK) & (c > last_visible))
    valid = jnp.logical_not(hidden)
    rid = lax.broadcasted_iota(jnp.int32, (rows, 1), 0) >> shift
    for hk in range(A_KV_HEADS):
        ksl = slice(hk * HEAD_DIM, (hk + 1) * HEAD_DIM)
        k_all = jnp.concatenate([kp_ref[:, ksl], kc_ref[:, ksl], kn_ref[:, ksl], kx_ref[:, ksl]],
                                axis=0)
        v_all = jnp.concatenate([vp_ref[:, ksl], vc_ref[:, ksl], vn_ref[:, ksl], vx_ref[:, ksl]],
                                axis=0)
        q3 = jnp.concatenate(
            [q_ref[:, (hk * group + g) * HEAD_DIM:(hk * group + g + 1) * HEAD_DIM]
             for g in range(group)], axis=0)
        sink = jnp.zeros((rows, 1), F32)
        for g in range(group):
            sink = jnp.where(rid == g, sink_ref[hk * group + g] * LOG2E, sink)
        s = jnp.where(valid, _dot_nt(q3, k_all), -jnp.inf)
        m = jnp.maximum(jnp.max(s, axis=-1, keepdims=True), sink)
        p = jnp.exp2(s - m)
        denom = jnp.sum(p, axis=-1, keepdims=True) + jnp.exp2(sink - m)
        o = _dot(p.astype(BF16), v_all) / denom
        for g in range(group):
            osl = slice((hk * group + g) * HEAD_DIM, (hk * group + g + 1) * HEAD_DIM)
            o_ref[:, osl] = o[g * BLOCK:(g + 1) * BLOCK, :].astype(BF16)


def _window_attn(za, av, sink, n_batch, seq, n_ctx, with_ctx):
    n_heads = sink.shape[0]
    group = n_heads // A_KV_HEADS
    qw = n_heads * HEAD_DIM
    kw = A_KV_HEADS * HEAD_DIM
    assert qw % kw == 0 and seq % BLOCK == 0 and (n_batch * seq) % n_ctx == 0 and WINDOW == BLOCK
    assert n_ctx % BLOCK == 0
    nb = seq // BLOCK
    ncb = n_ctx // BLOCK if with_ctx else 0
    kcol = qw // kw
    ctx0 = (n_batch * seq) // n_ctx

    def qblk(b, n):
        return (jnp.where(n < nb, b * nb + n, n_batch * nb + b * ncb + n - nb), 0)

    def blk(shift):
        return lambda b, n: (b * nb + jnp.clip(n + shift, 0, nb - 1), kcol)

    def vblk(shift):
        return lambda b, n: (b * nb + jnp.clip(n + shift, 0, nb - 1), 0)

    kspec = [pl.BlockSpec((BLOCK, kw), blk(s)) for s in (-1, 0, 1)]
    vspec = [pl.BlockSpec((BLOCK, kw), vblk(s)) for s in (-1, 0, 1)]
    blocks = ([((BLOCK, qw), BF16)] * 2 + [((BLOCK, kw), BF16)] * 6 + [((n_ctx, kw), BF16)] * 2)
    n_keys = 3 * BLOCK + n_ctx
    kern = functools.partial(_window_attn_kernel, n_blocks=nb, group=group)
    return pl.pallas_call(
        kern,
        out_shape=jax.ShapeDtypeStruct((n_batch * (nb + ncb) * BLOCK, qw), BF16),
        grid=(n_batch, nb + ncb),
        in_specs=[pl.BlockSpec(memory_space=pltpu.SMEM), pl.BlockSpec((BLOCK, qw), qblk)]
                 + kspec + [pl.BlockSpec((n_ctx, kw), lambda b, n: (ctx0 + b, kcol))]
                 + vspec + [pl.BlockSpec((n_ctx, kw), lambda b, n: (ctx0 + b, 0))],
        out_specs=pl.BlockSpec((BLOCK, qw), qblk),
        compiler_params=_params(("arbitrary", "arbitrary"), blocks,
                                temps=[((group * BLOCK, n_keys), F32)] * 4),
        name="window_attn",
    )(sink, za, za, za, za, za, av, av, av, av)


def _gmlp_kernel(z_ref, g_ref, b_ref, ws_ref, bst_ref, o_ref):
    width = g_ref.shape[1]
    u = z_ref[:, :width].astype(F32)
    v = z_ref[:, width:].astype(F32)
    mu = jnp.mean(v, axis=-1, keepdims=True)
    vc = v - mu
    var = jnp.mean(vc * vc, axis=-1, keepdims=True)
    vn = (vc * lax.rsqrt(var + EPS) * g_ref[...] + b_ref[...]).astype(BF16)
    gd = width // B_GROUPS
    for g in range(B_GROUPS):
        sl = slice(g * gd, (g + 1) * gd)
        mixed = _dot(ws_ref[g].astype(BF16), vn[:, sl]) + bst_ref[:, g:g + 1]
        o_ref[:, sl] = (u[:, sl] * mixed).astype(BF16)


def _gmlp(zb, ln_g, ln_b, ws, bs, n_rows):
    w2 = zb.shape[1]
    width = w2 // 2
    blocks = [((CHUNK, w2), BF16), ((1, width), F32), ((1, width), F32),
              (ws.shape, F32), ((CHUNK, B_GROUPS), F32), ((CHUNK, width), BF16)]
    return pl.pallas_call(
        _gmlp_kernel,
        out_shape=jax.ShapeDtypeStruct((n_rows, width), BF16),
        grid=(n_rows // CHUNK,),
        in_specs=[pl.BlockSpec((CHUNK, w2), lambda i: (i, 0)),
                  pl.BlockSpec((1, width), lambda i: (0, 0)),
                  pl.BlockSpec((1, width), lambda i: (0, 0)),
                  pl.BlockSpec(ws.shape, lambda i: (0, 0, 0)),
                  pl.BlockSpec((CHUNK, B_GROUPS), lambda i: (0, 0))],
        out_specs=pl.BlockSpec((CHUNK, width), lambda i: (i, 0)),
        compiler_params=_params(("arbitrary",), blocks, temps=[((CHUNK, w2), F32)] * 2),
        name="chunk_gmlp",
    )(zb, ln_g.reshape(1, width), ln_b.reshape(1, width), ws, bs.T)


def _global_attn_kernel(q_ref, kx_ref, vx_ref, k_ref, v_ref, o_ref, vext_ref, *, group,
                        n_lat_tiles, has_ctx_tiles):
    tq = q_ref.shape[0]
    n_ctx = kx_ref.shape[0]
    n_lat = k_ref.shape[0]
    rows = group * tq

    @pl.when(pl.program_id(2) == 0)
    def _():
        vext_ref[0:n_ctx, 0:HEAD_DIM] = vx_ref[...]
        vext_ref[n_ctx:, 0:HEAD_DIM] = v_ref[...]
        vext_ref[:, HEAD_DIM:] = jnp.ones((n_ctx + n_lat, HEAD_DIM), BF16)

    def attend(chunks):
        q3 = jnp.concatenate([q_ref[:, g * HEAD_DIM:(g + 1) * HEAD_DIM] for g in range(group)],
                             axis=0)
        m = jnp.full((rows, HEAD_DIM), -jnp.inf, F32)
        acc = jnp.zeros((rows, 2 * HEAD_DIM), F32)
        for k_chunk, v0, nk in chunks:
            s = _dot_nt(q3, k_chunk())
            m_new = jnp.maximum(m, jnp.broadcast_to(jnp.max(s, axis=-1, keepdims=True), m.shape))
            alpha = jnp.exp2(m - m_new)
            p = jnp.exp2(s - jnp.tile(m_new, (1, nk // HEAD_DIM)))
            acc = jnp.tile(alpha, (1, 2)) * acc + _dot(p.astype(BF16), vext_ref[v0:v0 + nk, :])
            m = m_new
        o = acc[:, :HEAD_DIM] / acc[:, HEAD_DIM:]
        for g in range(group):
            o_ref[:, g * HEAD_DIM:(g + 1) * HEAD_DIM] = o[g * tq:(g + 1) * tq, :].astype(BF16)

    ctx_chunk = [(lambda: kx_ref[...], 0, n_ctx)]
    lat_chunks = [(lambda c=c: k_ref[c:c + K_TILE, :], n_ctx + c, K_TILE)
                  for c in range(0, n_lat, K_TILE)]
    is_lat = pl.program_id(2) < n_lat_tiles
    pl.when(is_lat)(lambda: attend(ctx_chunk + lat_chunks))
    if has_ctx_tiles:
        pl.when(jnp.logical_not(is_lat))(lambda: attend(ctx_chunk))


def _global_attn(zc, cv, n_batch, seq, n_ctx, n_heads, with_ctx):
    group = n_heads // C_KV_HEADS
    gw = group * HEAD_DIM
    tq = Q_TILE
    assert seq % tq == 0 and seq % K_TILE == 0 and (n_batch * seq) % n_ctx == 0 and n_ctx % tq == 0
    nq = seq // tq
    ncq = n_ctx // tq if with_ctx else 0
    ctx0 = (n_batch * seq) // n_ctx
    rows = group * tq

    def qblk(b, hk, i):
        return (jnp.where(i < nq, b * nq + i, n_batch * nq + b * ncq + i - nq), hk)

    blocks = [((tq, gw), BF16)] * 2 + [((n_ctx, HEAD_DIM), BF16)] * 2 + [((seq, HEAD_DIM), BF16)] * 2
    scratch = [((n_ctx + seq, 2 * HEAD_DIM), BF16)]
    return pl.pallas_call(
        functools.partial(_global_attn_kernel, group=group, n_lat_tiles=nq, has_ctx_tiles=ncq > 0),
        out_shape=jax.ShapeDtypeStruct((n_batch * (nq + ncq) * tq, n_heads * HEAD_DIM), BF16),
        grid=(n_batch, C_KV_HEADS, nq + ncq),
        in_specs=[pl.BlockSpec((tq, gw), qblk),
                  pl.BlockSpec((n_ctx, HEAD_DIM), lambda b, hk, i: (ctx0 + b, n_heads + hk)),
                  pl.BlockSpec((n_ctx, HEAD_DIM), lambda b, hk, i: (ctx0 + b, hk)),
                  pl.BlockSpec((seq, HEAD_DIM), lambda b, hk, i: (b, n_heads + hk)),
                  pl.BlockSpec((seq, HEAD_DIM), lambda b, hk, i: (b, hk))],
        out_specs=pl.BlockSpec((tq, gw), qblk),
        scratch_shapes=[pltpu.VMEM(s, d) for s, d in scratch],
        compiler_params=_params(("arbitrary", "arbitrary", "arbitrary"), blocks, scratch=scratch,
                                temps=[((rows, K_TILE), F32)] * 6),
        name="global_attn",
    )(zc, zc, cv, zc, cv)


def _outproj_kernel(a_ref, b_ref, c_ref, w_ref, o_ref, wbf_ref):
    @pl.when(pl.program_id(1) == 0)
    def _():
        _cast_weight(w_ref, wbf_ref)

    ka, kb = a_ref.shape[1], b_ref.shape[1]
    acc = _dot(a_ref[...], wbf_ref[0:ka, :])
    acc += _dot(b_ref[...], wbf_ref[ka:ka + kb, :])
    acc += _dot(c_ref[...], wbf_ref[ka + kb:, :])
    o_ref[...] = acc


def _outproj(oa, ob, oc, w_out, layer, n_rows):
    t = oa.shape[0]
    _, k, d = w_out.shape
    tm, tn = ROW_TILE, COL_TILE
    assert oa.shape[1] + ob.shape[1] + oc.shape[1] == k and d % tn == 0 and n_rows % tm == 0
    blocks = [((tm, k), BF16), ((k, tn), F32), ((tm, tn), F32)]
    return pl.pallas_call(
        _outproj_kernel,
        out_shape=jax.ShapeDtypeStruct((t, d), F32),
        grid=(d // tn, n_rows // tm),
        in_specs=[pl.BlockSpec((tm, oa.shape[1]), lambda j, i: (i, 0)),
                  pl.BlockSpec((tm, ob.shape[1]), lambda j, i: (i, 0)),
                  pl.BlockSpec((tm, oc.shape[1]), lambda j, i: (i, 0)),
                  pl.BlockSpec((None, k, tn), lambda j, i: (layer, 0, j))],
        out_specs=pl.BlockSpec((tm, tn), lambda j, i: (i, j)),
        scratch_shapes=[pltpu.VMEM((k, tn), BF16)],
        compiler_params=_params(("arbitrary", "arbitrary"), blocks, scratch=[((k, tn), BF16)],
                                temps=[((tm, tn), F32)] * 2),
        name="outproj",
    )(oa, ob, oc, w_out)


def _ffn_up_kernel(h_ref, wg_ref, wu_ref, o_ref, wgbf_ref, wubf_ref):
    @pl.when(pl.program_id(1) == 0)
    def _():
        _cast_weight(wg_ref, wgbf_ref)
        _cast_weight(wu_ref, wubf_ref)

    h = h_ref[...]
    gate = _dot(h, wgbf_ref[...])
    up = _dot(h, wubf_ref[...])
    o_ref[...] = (jax.nn.silu(gate) * up).astype(BF16)


def _ffn_up(h2, w_gate, w_up, sel, n_rows):
    t, d = h2.shape
    f = w_gate.shape[-1]
    tm, tn = ROW_TILE, FFN_COL_TILE
    assert f % tn == 0 and n_rows % tm == 0
    lead = (None,) * len(sel)
    wspec = pl.BlockSpec(lead + (d, tn), lambda j, i: sel + (0, j))
    blocks = [((tm, d), BF16), ((d, tn), F32), ((d, tn), F32), ((tm, tn), BF16)]
    return pl.pallas_call(
        _ffn_up_kernel,
        out_shape=jax.ShapeDtypeStruct((t, f), BF16),
        grid=(f // tn, n_rows // tm),
        in_specs=[pl.BlockSpec((tm, d), lambda j, i: (i, 0)), wspec, wspec],
        out_specs=pl.BlockSpec((tm, tn), lambda j, i: (i, j)),
        scratch_shapes=[pltpu.VMEM((d, tn), BF16)] * 2,
        compiler_params=_params(("arbitrary", "arbitrary"), blocks, scratch=[((d, tn), BF16)] * 2,
                                temps=[((tm, tn), F32)] * 3),
        name="ffn_up",
    )(h2, w_gate, w_up)


def _ffn_down_kernel(*refs, has_prev):
    if has_prev:
        a_ref, w_ref, prev_ref, o_ref, wbf_ref = refs
    else:
        a_ref, w_ref, o_ref, wbf_ref = refs

    @pl.when(pl.program_id(1) == 0)
    def _():
        _cast_weight(w_ref, wbf_ref)

    acc = _dot(a_ref[...], wbf_ref[...])
    if has_prev:
        acc = prev_ref[...] + acc
    o_ref[...] = acc


def _ffn_down(a, w_down, sel, k_start, k_size, n_rows, prev=None):
    t = a.shape[0]
    d = w_down.shape[-1]
    tm, tn = ROW_TILE, COL_TILE
    assert k_start % k_size == 0 and d % tn == 0 and n_rows % tm == 0
    kb = k_start // k_size
    lead = (None,) * len(sel)
    in_specs = [pl.BlockSpec((tm, k_size), lambda j, i: (i, kb)),
                pl.BlockSpec(lead + (k_size, tn), lambda j, i: sel + (kb, j))]
    args = [a, w_down]
    blocks = [((tm, k_size), BF16), ((k_size, tn), F32), ((tm, tn), F32)]
    aliases = {}
    if prev is not None:
        in_specs.append(pl.BlockSpec((tm, tn), lambda j, i: (i, j)))
        aliases = {len(args): 0}
        args.append(prev)
        blocks.append(((tm, tn), F32))
    return pl.pallas_call(
        functools.partial(_ffn_down_kernel, has_prev=prev is not None),
        out_shape=jax.ShapeDtypeStruct((t, d), F32),
        grid=(d // tn, n_rows // tm),
        in_specs=in_specs,
        out_specs=pl.BlockSpec((tm, tn), lambda j, i: (i, j)),
        scratch_shapes=[pltpu.VMEM((k_size, tn), BF16)],
        input_output_aliases=aliases,
        compiler_params=_params(("arbitrary", "arbitrary"), blocks, scratch=[((k_size, tn), BF16)],
                                temps=[((tm, tn), F32)] * 2),
        name="ffn_down",
    )(*args)


def _router_kernel(h_ref, w_ref, o_ref, *, n_experts):
    logits = jnp.dot(h_ref[...].astype(F32), w_ref[...], preferred_element_type=F32,
                     precision=lax.Precision.HIGHEST)
    lane = lax.broadcasted_iota(jnp.int32, logits.shape, 1).astype(F32)
    neg = -jnp.inf
    logits = jnp.where(lane < n_experts, logits, neg)
    picked = []
    remaining = logits
    for _ in range(TOP_K):
        top = jnp.max(remaining, axis=-1, keepdims=True)
        idx = jnp.min(jnp.where(remaining == top, lane, float(logits.shape[1])), axis=-1,
                      keepdims=True)
        picked.append((top, idx))
        remaining = jnp.where(lane == idx, neg, remaining)
    top0 = picked[0][0]
    denom = sum(jnp.exp(tv - top0) for tv, _ in picked)
    route = jnp.zeros(logits.shape, F32)
    for k, (tv, idx) in enumerate(picked):
        route = jnp.where(lane == k, idx, route)
        route = jnp.where(lane == TOP_K + k, jnp.exp(tv - top0) / denom, route)
    o_ref[...] = route


def _router(h2, w_router, n_rows):
    t, d = h2.shape
    n_experts = w_router.shape[1]
    assert n_experts <= V7X_LANES
    wpad = jnp.pad(w_router, ((0, 0), (0, V7X_LANES - n_experts)))
    tr = EW_ROWS
    blocks = [((tr, d), BF16), ((d, V7X_LANES), F32), ((tr, V7X_LANES), F32)]
    return pl.pallas_call(
        functools.partial(_router_kernel, n_experts=n_experts),
        out_shape=jax.ShapeDtypeStruct((t, V7X_LANES), F32),
        grid=(n_rows // tr,),
        in_specs=[pl.BlockSpec((tr, d), lambda i: (i, 0)),
                  pl.BlockSpec((d, V7X_LANES), lambda i: (0, 0))],
        out_specs=pl.BlockSpec((tr, V7X_LANES), lambda i: (i, 0)),
        compiler_params=_params(("arbitrary",), blocks, temps=[((tr, d), F32)] * 3),
        name="moe_router",
    )(h2, wpad)


def _route_plan(route, n_experts):
    tile = ROW_TILE
    n_tok = route.shape[0]
    n_asg = n_tok * TOP_K
    e_flat = route[:, :TOP_K].astype(jnp.int32).reshape(n_asg)
    w_flat = route[:, TOP_K:2 * TOP_K].reshape(n_asg)
    onehot = (e_flat[:, None] == jnp.arange(n_experts, dtype=jnp.int32)[None, :]).astype(jnp.int32)
    csum = jnp.cumsum(onehot, axis=0)
    rank = jnp.take_along_axis(csum, e_flat[:, None], axis=1)[:, 0] - 1
    counts = csum[-1]
    tiles_per = (counts + tile - 1) // tile
    tile_end = jnp.cumsum(tiles_per)
    tile_start = tile_end - tiles_per
    dest = tile_start[e_flat] * tile + rank

    n_tiles = (n_asg + n_experts * (tile - 1)) // tile
    tile_ids = jnp.arange(n_tiles, dtype=jnp.int32)
    n_used = tile_end[-1]
    valid = tile_ids < n_used
    expert_raw = jnp.sum((tile_end[None, :] <= tile_ids[:, None]).astype(jnp.int32), axis=1)
    tile_expert = jnp.where(valid, expert_raw, expert_raw[n_used - 1])
    tile_first = valid & (tile_ids == tile_start[tile_expert])
    tile_row = jnp.where(valid, tile_ids, n_used - 1)

    n_rows = n_tiles * tile
    tok_of_row = jnp.zeros((n_rows,), jnp.int32).at[dest].set(
        jnp.arange(n_asg, dtype=jnp.int32) // TOP_K, unique_indices=True)
    gate_of_row = jnp.zeros((n_rows,), F32).at[dest].set(w_flat, unique_indices=True)[:, None]
    tiles = (tile_expert, tile_first.astype(jnp.int32), valid.astype(jnp.int32), tile_row)
    return tok_of_row, gate_of_row, dest, tiles


def _row_copy(src_hbm, dst_vmem, sem, src_row, dst_row):
    return pltpu.make_async_copy(src_hbm.at[pl.ds(src_row, 1), :], dst_vmem.at[pl.ds(dst_row, 1), :],
                                 sem)


def _gather_rows_kernel(idx_ref, src_ref, o_ref, buf_ref, sem):
    tile = buf_ref.shape[0]
    base = pl.program_id(0) * tile

    def issue(r, carry):
        _row_copy(src_ref, buf_ref, sem, idx_ref[base + r], r).start()
        return carry

    def drain(r, carry):
        _row_copy(src_ref, buf_ref, sem, 0, r).wait()
        return carry

    lax.fori_loop(0, tile, issue, 0, unroll=GATHER_UNROLL)
    lax.fori_loop(0, tile, drain, 0, unroll=GATHER_UNROLL)
    o_ref[...] = buf_ref[...].astype(BF16)


def _gather_rows(src, idx):
    d = src.shape[1]
    n = idx.shape[0]
    tg = EW_ROWS
    assert n % tg == 0
    blocks = [((tg, d), BF16)]
    return pl.pallas_call(
        _gather_rows_kernel,
        out_shape=jax.ShapeDtypeStruct((n, d), BF16),
        grid_spec=pltpu.PrefetchScalarGridSpec(
            num_scalar_prefetch=1,
            grid=(n // tg,),
            in_specs=[pl.BlockSpec(memory_space=pl.ANY)],
            out_specs=pl.BlockSpec((tg, d), lambda i, idx_ref: (i, 0)),
            scratch_shapes=[pltpu.VMEM((tg, d), F32), pltpu.SemaphoreType.DMA]),
        compiler_params=_params(("arbitrary",), blocks, scratch=[((tg, d), F32)],
                                temps=[((tg, d), F32)]),
        name="moe_dispatch",
    )(idx, src)


def _moe_up_kernel(te_ref, tf_ref, tv_ref, tr_ref, x_ref, wg_ref, wu_ref, o_ref, wgbf_ref, wubf_ref):
    del te_ref, tr_ref
    i = pl.program_id(1)

    @pl.when(tf_ref[i] == 1)
    def _():
        _cast_weight(wg_ref, wgbf_ref)
        _cast_weight(wu_ref, wubf_ref)

    @pl.when(tv_ref[i] == 1)
    def _():
        x = x_ref[...]
        o_ref[...] = (jax.nn.silu(_dot(x, wgbf_ref[...])) * _dot(x, wubf_ref[...])).astype(BF16)

    @pl.when(tv_ref[i] == 0)
    def _():
        o_ref[...] = jnp.zeros(o_ref.shape, BF16)


def _moe_up(xs, w_gate, w_up, layer, tiles):
    r, d = xs.shape
    f = w_gate.shape[-1]
    tm, tn = ROW_TILE, FFN_COL_TILE
    assert f % tn == 0 and r % tm == 0
    wspec = pl.BlockSpec((None, None, d, tn), lambda j, i, te, tf, tv, tr: (layer, te[i], 0, j))
    blocks = [((tm, d), BF16), ((d, tn), F32), ((d, tn), F32), ((tm, tn), BF16)]
    return pl.pallas_call(
        _moe_up_kernel,
        out_shape=jax.ShapeDtypeStruct((r, f), BF16),
        grid_spec=pltpu.PrefetchScalarGridSpec(
            num_scalar_prefetch=4,
            grid=(f // tn, r // tm),
            in_specs=[pl.BlockSpec((tm, d), lambda j, i, te, tf, tv, tr: (tr[i], 0)), wspec, wspec],
            out_specs=pl.BlockSpec((tm, tn), lambda j, i, te, tf, tv, tr: (i, j)),
            scratch_shapes=[pltpu.VMEM((d, tn), BF16)] * 2),
        compiler_params=_params(("arbitrary", "arbitrary"), blocks, scratch=[((d, tn), BF16)] * 2,
                                temps=[((tm, tn), F32)] * 3),
        name="moe_up",
    )(*tiles, xs, w_gate, w_up)


def _moe_down_kernel(te_ref, tf_ref, tv_ref, tr_ref, a_ref, w_ref, g_ref, o_ref, wbf_ref):
    del te_ref, tr_ref
    i = pl.program_id(1)

    @pl.when(tf_ref[i] == 1)
    def _():
        _cast_weight(w_ref, wbf_ref)

    @pl.when(tv_ref[i] == 1)
    def _():
        o_ref[...] = g_ref[...] * _dot(a_ref[...], wbf_ref[...])

    @pl.when(tv_ref[i] == 0)
    def _():
        o_ref[...] = jnp.zeros(o_ref.shape, F32)


def _moe_down(hmid, w_down, gate_of_row, layer, tiles):
    r, f = hmid.shape
    d = w_down.shape[-1]
    tm, tn = ROW_TILE, COL_TILE
    assert d % tn == 0 and r % tm == 0
    blocks = [((tm, f), BF16), ((f, tn), F32), ((tm, V7X_LANES), F32), ((tm, tn), F32)]
    return pl.pallas_call(
        _moe_down_kernel,
        out_shape=jax.ShapeDtypeStruct((r, d), F32),
        grid_spec=pltpu.PrefetchScalarGridSpec(
            num_scalar_prefetch=4,
            grid=(d // tn, r // tm),
            in_specs=[pl.BlockSpec((tm, f), lambda j, i, te, tf, tv, tr: (tr[i], 0)),
                      pl.BlockSpec((None, None, f, tn),
                                   lambda j, i, te, tf, tv, tr: (layer, te[i], 0, j)),
                      pl.BlockSpec((tm, 1), lambda j, i, te, tf, tv, tr: (i, 0))],
            out_specs=pl.BlockSpec((tm, tn), lambda j, i, te, tf, tv, tr: (i, j)),
            scratch_shapes=[pltpu.VMEM((f, tn), BF16)]),
        compiler_params=_params(("arbitrary", "arbitrary"), blocks, scratch=[((f, tn), BF16)],
                                temps=[((tm, tn), F32)] * 2),
        name="moe_down",
    )(*tiles, hmid, w_down, gate_of_row)


def _moe_final_kernel(dest_ref, x_ref, y_ref, g_ref, mod_ref, o_ref, buf_ref, sems):
    tile = x_ref.shape[0]
    base = pl.program_id(0) * tile

    def issue(r, carry):
        for k in range(TOP_K):
            _row_copy(y_ref, buf_ref.at[k], sems.at[k], dest_ref[(base + r) * TOP_K + k], r).start()
        return carry

    def drain(r, carry):
        for k in range(TOP_K):
            _row_copy(y_ref, buf_ref.at[k], sems.at[k], 0, r).wait()
        return carry

    lax.fori_loop(0, tile, issue, 0, unroll=GATHER_UNROLL)
    lax.fori_loop(0, tile, drain, 0, unroll=GATHER_UNROLL)
    f = buf_ref[0]
    for k in range(1, TOP_K):
        f = f + buf_ref[k]
    o_ref[...] = x_ref[...] + mod_ref[5:6, :] * _rms(f, g_ref[...])


def _moe_final(x1, y, dest, g_post, mod, n_rows, seq, n_lat_rows, n_batch):
    d = x1.shape[1]
    tr = EW_ROWS
    ridx = functools.partial(_mod_row_index, rows_per_tile=tr, seq=seq, n_lat_rows=n_lat_rows,
                             n_batch=n_batch)
    row = pl.BlockSpec((tr, d), lambda i, dest_ref: (i, 0))
    blocks = [((tr, d), F32)] * 2 + [((N_MOD, d), F32)]
    return pl.pallas_call(
        _moe_final_kernel,
        out_shape=jax.ShapeDtypeStruct((n_rows, d), F32),
        grid_spec=pltpu.PrefetchScalarGridSpec(
            num_scalar_prefetch=1,
            grid=(n_rows // tr,),
            in_specs=[row, pl.BlockSpec(memory_space=pl.ANY),
                      pl.BlockSpec((1, d), lambda i, dest_ref: (0, 0)),
                      pl.BlockSpec((None, N_MOD, d), lambda i, dest_ref: (ridx(i), 0, 0))],
            out_specs=row,
            scratch_shapes=[pltpu.VMEM((TOP_K, tr, d), F32), pltpu.SemaphoreType.DMA((TOP_K,))]),
        compiler_params=_params(("arbitrary",), blocks, scratch=[((TOP_K, tr, d), F32)],
                                temps=[((tr, d), F32)] * 2),
        name="moe_combine_final",
    )(dest, x1, y, g_post.reshape(1, d), mod)


def _rope_tables(seq, pad_rows):
    n = jnp.arange(seq)
    pos_r = (n // GRID_W).astype(F32)
    pos_w = (n % GRID_W).astype(F32)
    n_freq = HEAD_DIM // 4
    inv_freq = ROPE_THETA ** (-jnp.arange(n_freq, dtype=F32) / n_freq)
    ar = pos_r[:, None] * inv_freq
    aw = pos_w[:, None] * inv_freq
    cos = jnp.concatenate([jnp.cos(ar), jnp.cos(ar), jnp.cos(aw), jnp.cos(aw)], axis=-1)
    sin = jnp.concatenate([-jnp.sin(ar), jnp.sin(ar), -jnp.sin(aw), jnp.sin(aw)], axis=-1)
    cos = jnp.concatenate([cos, jnp.ones((pad_rows, HEAD_DIM), F32)], axis=0)
    sin = jnp.concatenate([sin, jnp.zeros((pad_rows, HEAD_DIM), F32)], axis=0)
    return cos, sin


def kernel(x, c, ctx, c_ctx, w_mod, b_mod, g_pre_mix, g_post_mix, g_pre_ffn, g_post_ffn, w_in, w_out,
           sink_a, qn_c, kn_c, gm_ln_g, gm_ln_b, gm_ws, gm_bs, ffn_w_gate, ffn_w_up, ffn_w_down,
           moe_router, moe_w_gate, moe_w_up, moe_w_down):
    n_batch, seq, d = x.shape
    n_ctx = ctx.shape[1]
    depth = w_mod.shape[0]
    n_lat = n_batch * seq
    n_ctx_rows = n_batch * n_ctx
    t = n_lat + n_ctx_rows
    a_heads = sink_a.shape[1]
    a_w = a_heads * HEAD_DIM
    akv_w = A_KV_HEADS * HEAD_DIM
    b_w = gm_ln_g.shape[1]
    ckv_w = C_KV_HEADS * HEAD_DIM
    c_w = w_in.shape[2] - a_w - 2 * akv_w - 2 * b_w - 2 * ckv_w
    c_heads = c_w // HEAD_DIM
    assert n_ctx_rows == ROW_TILE and seq % ROW_TILE == 0 and n_batch + 1 <= 8
    assert a_w == c_w and akv_w == COL_TILE and ckv_w == COL_TILE

    xall = jnp.concatenate([x.reshape(n_lat, d), ctx.reshape(n_ctx_rows, d)], axis=0)
    c8 = jnp.concatenate([c, c_ctx[None, :], jnp.zeros((8 - n_batch - 1, d), F32)], axis=0)
    mod_all = _modulation(c8, w_mod, b_mod).reshape(depth, 8, N_MOD, d)

    cos, sin = _rope_tables(seq, ROW_TILE)
    lat_tiles = n_lat // ROW_TILE
    seq_tiles = seq // ROW_TILE

    def rope_idx(i):
        return jnp.where(i < lat_tiles, i % seq_tiles, seq_tiles)

    for l in range(depth):
        need_ctx = l < depth - 1
        n_rows = t if need_ctx else n_lat
        mod = mod_all[l]
        geo = dict(seq=seq, n_lat_rows=n_lat, n_batch=n_batch)

        h = _prenorm(xall, g_pre_mix[l], mod, **geo)
        gq, gk = qn_c[l].reshape(1, HEAD_DIM), kn_c[l].reshape(1, HEAD_DIM)
        rope = (cos, sin, rope_idx, gq, gk)
        col = 0
        za = _inproj(h, w_in, l, col, a_w + akv_w, "rope", rope, n_q_tiles=a_w // COL_TILE)
        col += a_w + akv_w
        av = _inproj(h, w_in, l, col, akv_w, "plain")
        col += akv_w
        zb = _inproj(h, w_in, l, col, 2 * b_w, "gelu")
        col += 2 * b_w
        zc = _inproj(h, w_in, l, col, c_w + ckv_w, "normrope", rope, n_q_tiles=c_w // COL_TILE)
        col += c_w + ckv_w
        cv = _inproj(h, w_in, l, col, ckv_w, "plain")

        oa = _window_attn(za, av, sink_a[l], n_batch, seq, n_ctx, need_ctx)
        ob = _gmlp(zb, gm_ln_g[l], gm_ln_b[l], gm_ws[l], gm_bs[l], n_rows)
        oc = _global_attn(zc, cv, n_batch, seq, n_ctx, c_heads, need_ctx)

        y = _outproj(oa, ob, oc, w_out, l, n_rows)
        i = l // 2
        if l % 2 == 0:
            x1, h2 = _postmix(xall, y, g_post_mix[l], g_pre_ffn[l], mod, n_rows, 0, **geo)
            f_dim = ffn_w_gate.shape[-1]
            hmid = _ffn_up(h2, ffn_w_gate, ffn_w_up, (i,), n_rows)
            half = f_dim // 2
            f = _ffn_down(hmid, ffn_w_down, (i,), 0, half, n_rows)
            f = _ffn_down(hmid, ffn_w_down, (i,), half, half, n_rows, prev=f)
            xall = _final(x1, f, g_post_ffn[l], mod, n_rows, **geo)
        else:
            x1, h2, h2f = _postmix(xall, y, g_post_mix[l], g_pre_ffn[l], mod, n_rows, 1, **geo)
            route = _router(h2, moe_router[i], n_rows)
            tok_of_row, gate_of_row, dest, tiles = _route_plan(route[:n_rows], moe_router.shape[-1])
            xs = _gather_rows(h2f, tok_of_row)
            hmid = _moe_up(xs, moe_w_gate, moe_w_up, i, tiles)
            ys = _moe_down(hmid, moe_w_down, gate_of_row, i, tiles)
            xall = _moe_final(x1, ys, dest, g_post_ffn[l], mod, n_rows, **geo)
    return xall[:n_lat].reshape(n_batch, seq, d)
```

```python
import functools
import math

import jax
import jax.numpy as jnp
from jax import lax
from jax.experimental import pallas as pl
from jax.experimental.pallas import tpu as pltpu

F32 = jnp.float32
BF16 = jnp.bfloat16

GRID_W = 64
HEAD_DIM = 128
BLOCK = 128
WINDOW = 128
A_KV_HEADS = 4
C_KV_HEADS = 4
B_GROUPS = 8
CHUNK = 128
N_MOD = 6
TOP_K = 2
ROPE_THETA = 10000.0
EPS = 1e-6
LOG2E = math.log2(math.e)

V7X_LANES = 128
V7X_VMEM_SCOPED_CAP = 60000 * 1024

ROW_TILE = 512
MAX_ROW_TILE = 1088
ROW_ALIGN = 16
COL_TILE = 512
FFN_COL_TILE = 256
EW_ROWS = 256
CAST_ROWS = 256
Q_TILE = 256
K_TILE = 512
GATHER_UNROLL = 8
ROUTER_PIECES = 3
SLAB_ROWS = 8


def _nbytes(shape, dtype):
    return math.prod(shape) * jnp.dtype(dtype).itemsize


def _params(semantics, blocks, scratch=(), temps=()):
    need = 2 * sum(_nbytes(s, d) for s, d in blocks)
    need += sum(_nbytes(s, d) for s, d in scratch)
    need += sum(_nbytes(s, d) for s, d in temps)
    limit = min(V7X_VMEM_SCOPED_CAP, max(need + need // 4, 16 * 1024 * 1024))
    return pltpu.CompilerParams(dimension_semantics=semantics, vmem_limit_bytes=limit)


def _row_tile(n_rows):
    for tm in range(MAX_ROW_TILE - MAX_ROW_TILE % ROW_ALIGN, 0, -ROW_ALIGN):
        if n_rows % tm == 0:
            return tm
    raise ValueError(f"no row tile for {n_rows} rows")


def _cast_weight(w_ref, wbf_ref):
    rows = w_ref.shape[0]
    step = CAST_ROWS if rows % CAST_ROWS == 0 else V7X_LANES
    assert rows % step == 0

    def body(r, carry):
        sl = pl.ds(pl.multiple_of(r * step, step), step)
        wbf_ref[sl, :] = w_ref[sl, :].astype(BF16)
        return carry

    lax.fori_loop(0, rows // step, body, 0)


def _dot(a, b):
    return jnp.dot(a, b, preferred_element_type=F32)


def _dot_nt(a, b):
    return lax.dot_general(a, b, (((1,), (1,)), ((), ())), preferred_element_type=F32)


def _rms(x, gain):
    return x * lax.rsqrt(jnp.mean(x * x, axis=-1, keepdims=True) + EPS) * gain


def _to_slabs(x, slab_ref):
    w = slab_ref.shape[2]
    for s in range(slab_ref.shape[1]):
        slab_ref[:, s, :] = x[:, s * w:(s + 1) * w]


def _from_slabs(slab_ref):
    return jnp.concatenate([slab_ref[:, s, :] for s in range(slab_ref.shape[1])], axis=1)


def _mod_kernel(c_ref, w_ref, b_ref, o_ref):
    s = jax.nn.silu(c_ref[...])
    o_ref[...] = _dot(s.astype(BF16), w_ref[...].astype(BF16)) + b_ref[...]


def _modulation(c8, w_mod, b_mod):
    n_layers, d, width = w_mod.shape
    tn = COL_TILE
    assert width % tn == 0
    blocks = [((8, d), F32), ((d, tn), F32), ((1, tn), F32), ((8, tn), F32)]
    return pl.pallas_call(
        _mod_kernel,
        out_shape=jax.ShapeDtypeStruct((n_layers, 8, width), F32),
        grid=(n_layers, width // tn),
        in_specs=[pl.BlockSpec((8, d), lambda l, j: (0, 0)),
                  pl.BlockSpec((None, d, tn), lambda l, j: (l, 0, j)),
                  pl.BlockSpec((None, 1, tn), lambda l, j: (l, 0, j))],
        out_specs=pl.BlockSpec((None, 8, tn), lambda l, j: (l, 0, j)),
        compiler_params=_params(("arbitrary", "arbitrary"), blocks, temps=[((d, tn), BF16)]),
        name="modulation",
    )(c8, w_mod, b_mod.reshape(n_layers, 1, width))


def _mod_row_index(i, rows_per_tile, seq, n_lat_rows, n_batch):
    lat_tiles = n_lat_rows // rows_per_tile
    return jnp.where(i < lat_tiles, (i * rows_per_tile) // seq, n_batch)


def _prenorm_kernel(x_ref, g_ref, mod_ref, o_ref):
    y = _rms(x_ref[...], g_ref[...])
    o_ref[...] = (y * (1.0 + mod_ref[1:2, :]) + mod_ref[0:1, :]).astype(BF16)


def _prenorm(xall, gain, mod, seq, n_lat_rows, n_batch):
    t, d = xall.shape
    tr = EW_ROWS
    ridx = functools.partial(_mod_row_index, rows_per_tile=tr, seq=seq, n_lat_rows=n_lat_rows,
                             n_batch=n_batch)
    blocks = [((tr, d), F32), ((1, d), F32), ((N_MOD, d), F32), ((tr, d), BF16)]
    return pl.pallas_call(
        _prenorm_kernel,
        out_shape=jax.ShapeDtypeStruct((t, d), BF16),
        grid=(t // tr,),
        in_specs=[pl.BlockSpec((tr, d), lambda i: (i, 0)),
                  pl.BlockSpec((1, d), lambda i: (0, 0)),
                  pl.BlockSpec((None, N_MOD, d), lambda i: (ridx(i), 0, 0))],
        out_specs=pl.BlockSpec((tr, d), lambda i: (i, 0)),
        compiler_params=_params(("arbitrary",), blocks, temps=[((tr, d), F32)] * 2),
        name="prenorm",
    )(xall, gain.reshape(1, d), mod)


def _postmix_kernel(x_ref, y_ref, gpost_ref, gpre_ref, mod_ref, x1_ref, h2_ref, *h2f_ref):
    x1 = x_ref[...] + mod_ref[2:3, :] * _rms(y_ref[...], gpost_ref[...])
    x1_ref[...] = x1
    h2 = _rms(x1, gpre_ref[...]) * (1.0 + mod_ref[4:5, :]) + mod_ref[3:4, :]
    h2_ref[...] = h2.astype(BF16)
    for ref in h2f_ref:
        _to_slabs(h2, ref)


def _postmix(xall, y, g_post, g_pre, mod, n_rows, want_slabs, seq, n_lat_rows, n_batch):
    d = xall.shape[1]
    tr = EW_ROWS
    assert d % (SLAB_ROWS * V7X_LANES) == 0
    slab = (SLAB_ROWS, d // SLAB_ROWS)
    ridx = functools.partial(_mod_row_index, rows_per_tile=tr, seq=seq, n_lat_rows=n_lat_rows,
                             n_batch=n_batch)
    row = pl.BlockSpec((tr, d), lambda i: (i, 0))
    vec = pl.BlockSpec((1, d), lambda i: (0, 0))
    blocks = [((tr, d), F32)] * (3 + want_slabs) + [((tr, d), BF16), ((N_MOD, d), F32)]
    out_shape = [jax.ShapeDtypeStruct((n_rows, d), F32), jax.ShapeDtypeStruct((n_rows, d), BF16)]
    out_specs = [row, row]
    if want_slabs:
        out_shape.append(jax.ShapeDtypeStruct((n_rows,) + slab, F32))
        out_specs.append(pl.BlockSpec((tr,) + slab, lambda i: (i, 0, 0)))
    return pl.pallas_call(
        _postmix_kernel,
        out_shape=tuple(out_shape),
        grid=(n_rows // tr,),
        in_specs=[row, row, vec, vec, pl.BlockSpec((None, N_MOD, d), lambda i: (ridx(i), 0, 0))],
        out_specs=tuple(out_specs),
        compiler_params=_params(("arbitrary",), blocks, temps=[((tr, d), F32)] * 3),
        name="postmix",
    )(xall, y, g_post.reshape(1, d), g_pre.reshape(1, d), mod)


def _final_kernel(x_ref, f_ref, g_ref, mod_ref, o_ref):
    o_ref[...] = x_ref[...] + mod_ref[5:6, :] * _rms(f_ref[...], g_ref[...])


def _final(x1, f, g_post, mod, n_rows, seq, n_lat_rows, n_batch):
    d = x1.shape[1]
    tr = EW_ROWS
    ridx = functools.partial(_mod_row_index, rows_per_tile=tr, seq=seq, n_lat_rows=n_lat_rows,
                             n_batch=n_batch)
    row = pl.BlockSpec((tr, d), lambda i: (i, 0))
    blocks = [((tr, d), F32)] * 3 + [((N_MOD, d), F32)]
    return pl.pallas_call(
        _final_kernel,
        out_shape=jax.ShapeDtypeStruct((n_rows, d), F32),
        grid=(n_rows // tr,),
        in_specs=[row, row, pl.BlockSpec((1, d), lambda i: (0, 0)),
                  pl.BlockSpec((None, N_MOD, d), lambda i: (ridx(i), 0, 0))],
        out_specs=row,
        compiler_params=_params(("arbitrary",), blocks, temps=[((tr, d), F32)] * 2),
        name="final_residual",
    )(x1, f, g_post.reshape(1, d), mod)


def _swap_pairs(x):
    lane = lax.broadcasted_iota(jnp.int32, x.shape, 1)
    quarter = HEAD_DIM // 4
    first = (lane & quarter) == 0
    return jnp.where(first, pltpu.roll(x, HEAD_DIM - quarter, axis=1), pltpu.roll(x, quarter, axis=1))


def _inproj_kernel(*refs, mode, n_q_tiles, q_scale):
    if mode in ("rope", "normrope"):
        h_ref, w_ref, cos_ref, sin_ref, gq_ref, gk_ref, o_ref, wbf_ref = refs
    else:
        h_ref, w_ref, o_ref, wbf_ref = refs
    j = pl.program_id(0)

    @pl.when(pl.program_id(1) == 0)
    def _():
        _cast_weight(w_ref, wbf_ref)

    acc = _dot(h_ref[...], wbf_ref[...])
    if mode == "plain":
        o_ref[...] = acc.astype(BF16)
    elif mode == "gelu":
        o_ref[...] = (0.5 * acc * (1.0 + lax.erf(acc * (2.0 ** -0.5)))).astype(BF16)
    else:
        is_q = j < n_q_tiles
        scale = jnp.where(is_q, q_scale, 1.0).astype(F32)
        cos = cos_ref[...]
        sin = sin_ref[...]
        gain = jnp.where(is_q, gq_ref[...], gk_ref[...])
        for hh in range(acc.shape[1] // HEAD_DIM):
            sl = slice(hh * HEAD_DIM, (hh + 1) * HEAD_DIM)
            xh = acc[:, sl]
            if mode == "normrope":
                xh = _rms(xh, gain)
            xh = xh * cos + _swap_pairs(xh) * sin
            o_ref[:, sl] = (xh * scale).astype(BF16)


def _inproj(h, w_in, layer, col_start, width, mode, rope=None, n_q_tiles=0):
    t, d = h.shape
    tm, tn = _row_tile(t), COL_TILE
    assert col_start % tn == 0 and width % tn == 0
    j0 = col_start // tn
    in_specs = [pl.BlockSpec((tm, d), lambda j, i: (i, 0)),
                pl.BlockSpec((None, d, tn), lambda j, i: (layer, 0, j0 + j))]
    args = [h, w_in]
    blocks = [((tm, d), BF16), ((d, tn), F32), ((tm, tn), BF16)]
    if mode in ("rope", "normrope"):
        cos, sin, gq, gk = rope
        tab = pl.BlockSpec((tm, HEAD_DIM), lambda j, i: (i, 0))
        vec = pl.BlockSpec((1, HEAD_DIM), lambda j, i: (0, 0))
        in_specs += [tab, tab, vec, vec]
        args += [cos, sin, gq, gk]
        blocks += [((tm, HEAD_DIM), F32)] * 2
    kern = functools.partial(_inproj_kernel, mode=mode, n_q_tiles=n_q_tiles,
                             q_scale=HEAD_DIM ** -0.5 * LOG2E)
    return pl.pallas_call(
        kern,
        out_shape=jax.ShapeDtypeStruct((t, width), BF16),
        grid=(width // tn, t // tm),
        in_specs=in_specs,
        out_specs=pl.BlockSpec((tm, tn), lambda j, i: (i, j)),
        scratch_shapes=[pltpu.VMEM((d, tn), BF16)],
        compiler_params=_params(("arbitrary", "arbitrary"), blocks, scratch=[((d, tn), BF16)],
                                temps=[((tm, tn), F32)] * 2),
        name="inproj_" + mode,
    )(*args)


def _window_attn_kernel(sink_ref, q_ref, kp_ref, kc_ref, kn_ref, kx_ref, vp_ref, vc_ref, vn_ref,
                        vx_ref, o_ref, *, n_blocks, group):
    n = pl.program_id(1)
    n_ctx = kx_ref.shape[0]
    n_keys = 3 * BLOCK + n_ctx
    rows = group * BLOCK
    assert BLOCK & (BLOCK - 1) == 0
    shift = BLOCK.bit_length() - 1
    r = lax.broadcasted_iota(jnp.int32, (rows, n_keys), 0) & (BLOCK - 1)
    c = lax.broadcasted_iota(jnp.int32, (rows, n_keys), 1)
    is_lat = n < n_blocks
    first_visible = r + jnp.where((n > 0) & is_lat, 0, BLOCK)
    last_visible = r + 2 * BLOCK - jnp.where(n < n_blocks - 1, 0, BLOCK)
    current_hidden_end = jnp.where(is_lat, BLOCK, 2 * BLOCK)
    hidden = ((c < BLOCK) & (c < first_visible)) | ((c >= BLOCK) & (c < current_hidden_end)) | (
        (c >= 2 * BLOCK) & (c < 3 * BLOCK) & (c > last_visible))
    valid = jnp.logical_not(hidden)
    rid = lax.broadcasted_iota(jnp.int32, (rows, 1), 0) >> shift
    for hk in range(A_KV_HEADS):
        ksl = slice(hk * HEAD_DIM, (hk + 1) * HEAD_DIM)
        k_all = jnp.concatenate([kp_ref[:, ksl], kc_ref[:, ksl], kn_ref[:, ksl], kx_ref[:, ksl]],
                                axis=0)
        v_all = jnp.concatenate([vp_ref[:, ksl], vc_ref[:, ksl], vn_ref[:, ksl], vx_ref[:, ksl]],
                                axis=0)
        q3 = jnp.concatenate(
            [q_ref[:, (hk * group + g) * HEAD_DIM:(hk * group + g + 1) * HEAD_DIM]
             for g in range(group)], axis=0)
        sink = jnp.zeros((rows, 1), F32)
        for g in range(group):
            sink = jnp.where(rid == g, sink_ref[hk * group + g] * LOG2E, sink)
        s = jnp.where(valid, _dot_nt(q3, k_all), -jnp.inf)
        m = jnp.maximum(jnp.max(s, axis=-1, keepdims=True), sink)
        p = jnp.exp2(s - m)
        denom = jnp.sum(p, axis=-1, keepdims=True) + jnp.exp2(sink - m)
        o = _dot(p.astype(BF16), v_all) / denom
        for g in range(group):
            osl = slice((hk * group + g) * HEAD_DIM, (hk * group + g + 1) * HEAD_DIM)
            o_ref[:, osl] = o[g * BLOCK:(g + 1) * BLOCK, :].astype(BF16)


def _window_attn(za, av, sink, n_batch, seq, n_ctx, with_ctx):
    n_heads = sink.shape[0]
    group = n_heads // A_KV_HEADS
    qw = n_heads * HEAD_DIM
    kw = A_KV_HEADS * HEAD_DIM
    assert qw % kw == 0 and seq % BLOCK == 0 and (n_batch * seq) % n_ctx == 0 and WINDOW == BLOCK
    assert n_ctx % BLOCK == 0
    nb = seq // BLOCK
    ncb = n_ctx // BLOCK if with_ctx else 0
    kcol = qw // kw
    ctx0 = (n_batch * seq) // n_ctx

    def qblk(b, n):
        return (jnp.where(n < nb, b * nb + n, n_batch * nb + b * ncb + n - nb), 0)

    def blk(shift):
        return lambda b, n: (b * nb + jnp.clip(n + shift, 0, nb - 1), kcol)

    def vblk(shift):
        return lambda b, n: (b * nb + jnp.clip(n + shift, 0, nb - 1), 0)

    kspec = [pl.BlockSpec((BLOCK, kw), blk(s)) for s in (-1, 0, 1)]
    vspec = [pl.BlockSpec((BLOCK, kw), vblk(s)) for s in (-1, 0, 1)]
    blocks = ([((BLOCK, qw), BF16)] * 2 + [((BLOCK, kw), BF16)] * 6 + [((n_ctx, kw), BF16)] * 2)
    n_keys = 3 * BLOCK + n_ctx
    kern = functools.partial(_window_attn_kernel, n_blocks=nb, group=group)
    return pl.pallas_call(
        kern,
        out_shape=jax.ShapeDtypeStruct((n_batch * (nb + ncb) * BLOCK, qw), BF16),
        grid=(n_batch, nb + ncb),
        in_specs=[pl.BlockSpec(memory_space=pltpu.SMEM), pl.BlockSpec((BLOCK, qw), qblk)]
                 + kspec + [pl.BlockSpec((n_ctx, kw), lambda b, n: (ctx0 + b, kcol))]
                 + vspec + [pl.BlockSpec((n_ctx, kw), lambda b, n: (ctx0 + b, 0))],
        out_specs=pl.BlockSpec((BLOCK, qw), qblk),
        compiler_params=_params(("arbitrary", "arbitrary"), blocks,
                                temps=[((group * BLOCK, n_keys), F32)] * 4),
        name="window_attn",
    )(sink, za, za, za, za, za, av, av, av, av)


def _gmlp_kernel(z_ref, g_ref, b_ref, ws_ref, bst_ref, o_ref):
    width = g_ref.shape[1]
    u = z_ref[:, :width].astype(F32)
    v = z_ref[:, width:].astype(F32)
    mu = jnp.mean(v, axis=-1, keepdims=True)
    vc = v - mu
    var = jnp.mean(vc * vc, axis=-1, keepdims=True)
    vn = (vc * lax.rsqrt(var + EPS) * g_ref[...] + b_ref[...]).astype(BF16)
    gd = width // B_GROUPS
    for g in range(B_GROUPS):
        sl = slice(g * gd, (g + 1) * gd)
        mixed = _dot(ws_ref[g].astype(BF16), vn[:, sl]) + bst_ref[:, g:g + 1]
        o_ref[:, sl] = (u[:, sl] * mixed).astype(BF16)


def _gmlp(zb, ln_g, ln_b, ws, bs, n_rows):
    w2 = zb.shape[1]
    width = w2 // 2
    blocks = [((CHUNK, w2), BF16), ((1, width), F32), ((1, width), F32),
              (ws.shape, F32), ((CHUNK, B_GROUPS), F32), ((CHUNK, width), BF16)]
    return pl.pallas_call(
        _gmlp_kernel,
        out_shape=jax.ShapeDtypeStruct((n_rows, width), BF16),
        grid=(n_rows // CHUNK,),
        in_specs=[pl.BlockSpec((CHUNK, w2), lambda i: (i, 0)),
                  pl.BlockSpec((1, width), lambda i: (0, 0)),
                  pl.BlockSpec((1, width), lambda i: (0, 0)),
                  pl.BlockSpec(ws.shape, lambda i: (0, 0, 0)),
                  pl.BlockSpec((CHUNK, B_GROUPS), lambda i: (0, 0))],
        out_specs=pl.BlockSpec((CHUNK, width), lambda i: (i, 0)),
        compiler_params=_params(("arbitrary",), blocks, temps=[((CHUNK, w2), F32)] * 2),
        name="chunk_gmlp",
    )(zb, ln_g.reshape(1, width), ln_b.reshape(1, width), ws, bs.T)


def _global_attn_kernel(q_ref, kx_ref, vx_ref, k_ref, v_ref, o_ref, vext_ref, *, group,
                        n_lat_tiles, has_ctx_tiles):
    tq = q_ref.shape[0]
    n_ctx = kx_ref.shape[0]
    n_lat = k_ref.shape[0]
    rows = group * tq

    @pl.when(pl.program_id(2) == 0)
    def _():
        vext_ref[0:n_ctx, 0:HEAD_DIM] = vx_ref[...]
        vext_ref[n_ctx:, 0:HEAD_DIM] = v_ref[...]
        vext_ref[:, HEAD_DIM:] = jnp.ones((n_ctx + n_lat, HEAD_DIM), BF16)

    def attend(chunks):
        q3 = jnp.concatenate([q_ref[:, g * HEAD_DIM:(g + 1) * HEAD_DIM] for g in range(group)],
                             axis=0)
        m = jnp.full((rows, HEAD_DIM), -jnp.inf, F32)
        acc = jnp.zeros((rows, 2 * HEAD_DIM), F32)
        for k_chunk, v0, nk in chunks:
            s = _dot_nt(q3, k_chunk())
            m_new = jnp.maximum(m, jnp.broadcast_to(jnp.max(s, axis=-1, keepdims=True), m.shape))
            alpha = jnp.exp2(m - m_new)
            p = jnp.exp2(s - jnp.tile(m_new, (1, nk // HEAD_DIM)))
            acc = jnp.tile(alpha, (1, 2)) * acc + _dot(p.astype(BF16), vext_ref[v0:v0 + nk, :])
            m = m_new
        o = acc[:, :HEAD_DIM] / acc[:, HEAD_DIM:]
        for g in range(group):
            o_ref[:, g * HEAD_DIM:(g + 1) * HEAD_DIM] = o[g * tq:(g + 1) * tq, :].astype(BF16)

    ctx_chunk = [(lambda: kx_ref[...], 0, n_ctx)]
    lat_chunks = [(lambda c=c: k_ref[c:c + K_TILE, :], n_ctx + c, K_TILE)
                  for c in range(0, n_lat, K_TILE)]
    is_lat = pl.program_id(2) < n_lat_tiles
    pl.when(is_lat)(lambda: attend(ctx_chunk + lat_chunks))
    if has_ctx_tiles:
        pl.when(jnp.logical_not(is_lat))(lambda: attend(ctx_chunk))


def _global_attn(zc, cv, n_batch, seq, n_ctx, n_heads, with_ctx):
    group = n_heads // C_KV_HEADS
    gw = group * HEAD_DIM
    tq = Q_TILE
    assert seq % tq == 0 and seq % K_TILE == 0 and (n_batch * seq) % n_ctx == 0 and n_ctx % tq == 0
    nq = seq // tq
    ncq = n_ctx // tq if with_ctx else 0
    ctx0 = (n_batch * seq) // n_ctx
    rows = group * tq

    def qblk(b, hk, i):
        return (jnp.where(i < nq, b * nq + i, n_batch * nq + b * ncq + i - nq), hk)

    blocks = [((tq, gw), BF16)] * 2 + [((n_ctx, HEAD_DIM), BF16)] * 2 + [((seq, HEAD_DIM), BF16)] * 2
    scratch = [((n_ctx + seq, 2 * HEAD_DIM), BF16)]
    return pl.pallas_call(
        functools.partial(_global_attn_kernel, group=group, n_lat_tiles=nq, has_ctx_tiles=ncq > 0),
        out_shape=jax.ShapeDtypeStruct((n_batch * (nq + ncq) * tq, n_heads * HEAD_DIM), BF16),
        grid=(n_batch, C_KV_HEADS, nq + ncq),
        in_specs=[pl.BlockSpec((tq, gw), qblk),
                  pl.BlockSpec((n_ctx, HEAD_DIM), lambda b, hk, i: (ctx0 + b, n_heads + hk)),
                  pl.BlockSpec((n_ctx, HEAD_DIM), lambda b, hk, i: (ctx0 + b, hk)),
                  pl.BlockSpec((seq, HEAD_DIM), lambda b, hk, i: (b, n_heads + hk)),
                  pl.BlockSpec((seq, HEAD_DIM), lambda b, hk, i: (b, hk))],
        out_specs=pl.BlockSpec((tq, gw), qblk),
        scratch_shapes=[pltpu.VMEM(s, d) for s, d in scratch],
        compiler_params=_params(("arbitrary", "arbitrary", "arbitrary"), blocks, scratch=scratch,
                                temps=[((rows, K_TILE), F32)] * 6),
        name="global_attn",
    )(zc, zc, cv, zc, cv)


def _outproj_kernel(a_ref, b_ref, c_ref, w_ref, o_ref, wbf_ref):
    @pl.when(pl.program_id(1) == 0)
    def _():
        _cast_weight(w_ref, wbf_ref)

    ka, kb = a_ref.shape[1], b_ref.shape[1]
    acc = _dot(a_ref[...], wbf_ref[0:ka, :])
    acc += _dot(b_ref[...], wbf_ref[ka:ka + kb, :])
    acc += _dot(c_ref[...], wbf_ref[ka + kb:, :])
    o_ref[...] = acc


def _outproj(oa, ob, oc, w_out, layer, n_rows):
    t = oa.shape[0]
    _, k, d = w_out.shape
    tm, tn = _row_tile(n_rows), COL_TILE
    assert oa.shape[1] + ob.shape[1] + oc.shape[1] == k and d % tn == 0
    blocks = [((tm, k), BF16), ((k, tn), F32), ((tm, tn), F32)]
    return pl.pallas_call(
        _outproj_kernel,
        out_shape=jax.ShapeDtypeStruct((t, d), F32),
        grid=(d // tn, n_rows // tm),
        in_specs=[pl.BlockSpec((tm, oa.shape[1]), lambda j, i: (i, 0)),
                  pl.BlockSpec((tm, ob.shape[1]), lambda j, i: (i, 0)),
                  pl.BlockSpec((tm, oc.shape[1]), lambda j, i: (i, 0)),
                  pl.BlockSpec((None, k, tn), lambda j, i: (layer, 0, j))],
        out_specs=pl.BlockSpec((tm, tn), lambda j, i: (i, j)),
        scratch_shapes=[pltpu.VMEM((k, tn), BF16)],
        compiler_params=_params(("arbitrary", "arbitrary"), blocks, scratch=[((k, tn), BF16)],
                                temps=[((tm, tn), F32)] * 2),
        name="outproj",
    )(oa, ob, oc, w_out)


def _ffn_up_kernel(h_ref, wg_ref, wu_ref, o_ref, wgbf_ref, wubf_ref):
    @pl.when(pl.program_id(1) == 0)
    def _():
        _cast_weight(wg_ref, wgbf_ref)
        _cast_weight(wu_ref, wubf_ref)

    h = h_ref[...]
    gate = _dot(h, wgbf_ref[...])
    up = _dot(h, wubf_ref[...])
    o_ref[...] = (jax.nn.silu(gate) * up).astype(BF16)


def _ffn_up(h2, w_gate, w_up, sel, n_rows):
    t, d = h2.shape
    f = w_gate.shape[-1]
    tm, tn = _row_tile(n_rows), FFN_COL_TILE
    assert f % tn == 0
    lead = (None,) * len(sel)
    wspec = pl.BlockSpec(lead + (d, tn), lambda j, i: sel + (0, j))
    blocks = [((tm, d), BF16), ((d, tn), F32), ((d, tn), F32), ((tm, tn), BF16)]
    return pl.pallas_call(
        _ffn_up_kernel,
        out_shape=jax.ShapeDtypeStruct((t, f), BF16),
        grid=(f // tn, n_rows // tm),
        in_specs=[pl.BlockSpec((tm, d), lambda j, i: (i, 0)), wspec, wspec],
        out_specs=pl.BlockSpec((tm, tn), lambda j, i: (i, j)),
        scratch_shapes=[pltpu.VMEM((d, tn), BF16)] * 2,
        compiler_params=_params(("arbitrary", "arbitrary"), blocks, scratch=[((d, tn), BF16)] * 2,
                                temps=[((tm, tn), F32)] * 3),
        name="ffn_up",
    )(h2, w_gate, w_up)


def _ffn_down_kernel(*refs, has_prev):
    if has_prev:
        a_ref, w_ref, prev_ref, o_ref, wbf_ref = refs
    else:
        a_ref, w_ref, o_ref, wbf_ref = refs

    @pl.when(pl.program_id(1) == 0)
    def _():
        _cast_weight(w_ref, wbf_ref)

    acc = _dot(a_ref[...], wbf_ref[...])
    if has_prev:
        acc = prev_ref[...] + acc
    o_ref[...] = acc


def _ffn_down(a, w_down, sel, k_start, k_size, n_rows, prev=None):
    t = a.shape[0]
    d = w_down.shape[-1]
    tm, tn = ROW_TILE, COL_TILE
    assert k_start % k_size == 0 and d % tn == 0 and n_rows % tm == 0
    kb = k_start // k_size
    lead = (None,) * len(sel)
    in_specs = [pl.BlockSpec((tm, k_size), lambda j, i: (i, kb)),
                pl.BlockSpec(lead + (k_size, tn), lambda j, i: sel + (kb, j))]
    args = [a, w_down]
    blocks = [((tm, k_size), BF16), ((k_size, tn), F32), ((tm, tn), F32)]
    aliases = {}
    if prev is not None:
        in_specs.append(pl.BlockSpec((tm, tn), lambda j, i: (i, j)))
        aliases = {len(args): 0}
        args.append(prev)
        blocks.append(((tm, tn), F32))
    return pl.pallas_call(
        functools.partial(_ffn_down_kernel, has_prev=prev is not None),
        out_shape=jax.ShapeDtypeStruct((t, d), F32),
        grid=(d // tn, n_rows // tm),
        in_specs=in_specs,
        out_specs=pl.BlockSpec((tm, tn), lambda j, i: (i, j)),
        scratch_shapes=[pltpu.VMEM((k_size, tn), BF16)],
        input_output_aliases=aliases,
        compiler_params=_params(("arbitrary", "arbitrary"), blocks, scratch=[((k_size, tn), BF16)],
                                temps=[((tm, tn), F32)] * 2),
        name="ffn_down",
    )(*args)


def _router_kernel(h_ref, w_ref, o_ref, *, n_experts):
    pieces = _dot(h_ref[...], w_ref[...])
    logits = pieces
    for k in range(1, ROUTER_PIECES):
        logits = logits + pltpu.roll(pieces, V7X_LANES - k * n_experts, axis=1)
    lane = lax.broadcasted_iota(jnp.int32, logits.shape, 1).astype(F32)
    neg = -jnp.inf
    logits = jnp.where(lane < n_experts, logits, neg)
    picked = []
    remaining = logits
    for _ in range(TOP_K):
        top = jnp.max(remaining, axis=-1, keepdims=True)
        idx = jnp.min(jnp.where(remaining == top, lane, float(logits.shape[1])), axis=-1,
                      keepdims=True)
        picked.append((top, idx))
        remaining = jnp.where(lane == idx, neg, remaining)
    top0 = picked[0][0]
    denom = sum(jnp.exp(tv - top0) for tv, _ in picked)
    route = jnp.zeros(logits.shape, F32)
    for k, (tv, idx) in enumerate(picked):
        route = jnp.where(lane == k, idx, route)
        route = jnp.where(lane == TOP_K + k, jnp.exp(tv - top0) / denom, route)
    o_ref[...] = route


def _router(h2, w_router, n_rows):
    t, d = h2.shape
    n_experts = w_router.shape[1]
    assert ROUTER_PIECES * n_experts <= V7X_LANES
    pieces, rest = [], w_router
    for _ in range(ROUTER_PIECES):
        pieces.append(rest.astype(BF16))
        rest = rest - pieces[-1].astype(F32)
    wpad = jnp.pad(jnp.concatenate(pieces, axis=1),
                   ((0, 0), (0, V7X_LANES - ROUTER_PIECES * n_experts)))
    tr = EW_ROWS
    blocks = [((tr, d), BF16), ((d, V7X_LANES), BF16), ((tr, V7X_LANES), F32)]
    return pl.pallas_call(
        functools.partial(_router_kernel, n_experts=n_experts),
        out_shape=jax.ShapeDtypeStruct((t, V7X_LANES), F32),
        grid=(n_rows // tr,),
        in_specs=[pl.BlockSpec((tr, d), lambda i: (i, 0)),
                  pl.BlockSpec((d, V7X_LANES), lambda i: (0, 0))],
        out_specs=pl.BlockSpec((tr, V7X_LANES), lambda i: (i, 0)),
        compiler_params=_params(("arbitrary",), blocks, temps=[((tr, d), F32)] * 3),
        name="moe_router",
    )(h2, wpad)


def _route_plan(route, n_experts):
    tile = ROW_TILE
    n_tok = route.shape[0]
    n_asg = n_tok * TOP_K
    e_flat = route[:, :TOP_K].astype(jnp.int32).reshape(n_asg)
    onehot = (e_flat[:, None] == jnp.arange(n_experts, dtype=jnp.int32)[None, :]).astype(jnp.int32)
    csum = jnp.cumsum(onehot, axis=0)
    rank = jnp.take_along_axis(csum, e_flat[:, None], axis=1)[:, 0] - 1
    counts = csum[-1]
    tiles_per = (counts + tile - 1) // tile
    tile_end = jnp.cumsum(tiles_per)
    tile_start = tile_end - tiles_per
    dest = tile_start[e_flat] * tile + rank

    n_tiles = (n_asg + n_experts * (tile - 1)) // tile
    tile_ids = jnp.arange(n_tiles, dtype=jnp.int32)
    n_used = tile_end[-1]
    valid = tile_ids < n_used
    expert_raw = jnp.sum((tile_end[None, :] <= tile_ids[:, None]).astype(jnp.int32), axis=1)
    tile_expert = jnp.where(valid, expert_raw, expert_raw[n_used - 1])
    tile_first = valid & (tile_ids == tile_start[tile_expert])
    tile_row = jnp.where(valid, tile_ids, n_used - 1)

    tok_of_row = jnp.zeros((n_tiles * tile,), jnp.int32).at[dest].set(
        jnp.arange(n_asg, dtype=jnp.int32) // TOP_K, unique_indices=True)
    tiles = (tile_expert, tile_first.astype(jnp.int32), valid.astype(jnp.int32), tile_row)
    return tok_of_row, dest, tiles


def _row_copy(src_hbm, dst_vmem, sem, src_row, dst_row):
    return pltpu.make_async_copy(src_hbm.at[pl.ds(src_row, 1)], dst_vmem.at[pl.ds(dst_row, 1)], sem)


def _gather_rows_kernel(idx_ref, src_ref, o_ref, buf_ref, sem):
    tile = buf_ref.shape[0]
    base = pl.program_id(0) * tile

    def issue(r, carry):
        _row_copy(src_ref, buf_ref, sem, idx_ref[base + r], r).start()
        return carry

    def drain(r, carry):
        _row_copy(src_ref, buf_ref, sem, 0, r).wait()
        return carry

    lax.fori_loop(0, tile, issue, 0, unroll=GATHER_UNROLL)
    lax.fori_loop(0, tile, drain, 0, unroll=GATHER_UNROLL)
    o_ref[...] = _from_slabs(buf_ref).astype(BF16)


def _gather_rows(src, idx):
    slab = src.shape[1:]
    d = slab[0] * slab[1]
    n = idx.shape[0]
    tg = EW_ROWS
    assert n % tg == 0
    blocks = [((tg, d), BF16)]
    return pl.pallas_call(
        _gather_rows_kernel,
        out_shape=jax.ShapeDtypeStruct((n, d), BF16),
        grid_spec=pltpu.PrefetchScalarGridSpec(
            num_scalar_prefetch=1,
            grid=(n // tg,),
            in_specs=[pl.BlockSpec(memory_space=pl.ANY)],
            out_specs=pl.BlockSpec((tg, d), lambda i, idx_ref: (i, 0)),
            scratch_shapes=[pltpu.VMEM((tg,) + slab, F32), pltpu.SemaphoreType.DMA]),
        compiler_params=_params(("arbitrary",), blocks, scratch=[((tg, d), F32)],
                                temps=[((tg, d), F32)]),
        name="moe_dispatch",
    )(idx, src)


def _moe_up_kernel(te_ref, tf_ref, tv_ref, tr_ref, x_ref, wg_ref, wu_ref, o_ref, wgbf_ref, wubf_ref):
    del te_ref, tr_ref
    i = pl.program_id(1)

    @pl.when(tf_ref[i] == 1)
    def _():
        _cast_weight(wg_ref, wgbf_ref)
        _cast_weight(wu_ref, wubf_ref)

    @pl.when(tv_ref[i] == 1)
    def _():
        x = x_ref[...]
        o_ref[...] = (jax.nn.silu(_dot(x, wgbf_ref[...])) * _dot(x, wubf_ref[...])).astype(BF16)

    @pl.when(tv_ref[i] == 0)
    def _():
        o_ref[...] = jnp.zeros(o_ref.shape, BF16)


def _moe_up(xs, w_gate, w_up, layer, tiles):
    r, d = xs.shape
    f = w_gate.shape[-1]
    tm, tn = ROW_TILE, FFN_COL_TILE
    assert f % tn == 0 and r % tm == 0
    wspec = pl.BlockSpec((None, None, d, tn), lambda j, i, te, tf, tv, tr: (layer, te[i], 0, j))
    blocks = [((tm, d), BF16), ((d, tn), F32), ((d, tn), F32), ((tm, tn), BF16)]
    return pl.pallas_call(
        _moe_up_kernel,
        out_shape=jax.ShapeDtypeStruct((r, f), BF16),
        grid_spec=pltpu.PrefetchScalarGridSpec(
            num_scalar_prefetch=4,
            grid=(f // tn, r // tm),
            in_specs=[pl.BlockSpec((tm, d), lambda j, i, te, tf, tv, tr: (tr[i], 0)), wspec, wspec],
            out_specs=pl.BlockSpec((tm, tn), lambda j, i, te, tf, tv, tr: (i, j)),
            scratch_shapes=[pltpu.VMEM((d, tn), BF16)] * 2),
        compiler_params=_params(("arbitrary", "arbitrary"), blocks, scratch=[((d, tn), BF16)] * 2,
                                temps=[((tm, tn), F32)] * 3),
        name="moe_up",
    )(*tiles, xs, w_gate, w_up)


def _moe_down_kernel(te_ref, tf_ref, tv_ref, tr_ref, a_ref, w_ref, o_ref, wbf_ref):
    del te_ref, tr_ref
    i = pl.program_id(1)

    @pl.when(tf_ref[i] == 1)
    def _():
        _cast_weight(w_ref, wbf_ref)

    @pl.when(tv_ref[i] == 1)
    def _():
        o_ref[...] = _dot(a_ref[...], wbf_ref[...])

    @pl.when(tv_ref[i] == 0)
    def _():
        o_ref[...] = jnp.zeros(o_ref.shape, F32)


def _moe_down(hmid, w_down, layer, tiles):
    r, f = hmid.shape
    d = w_down.shape[-1]
    tm, tn = ROW_TILE, COL_TILE
    assert d % tn == 0 and r % tm == 0
    blocks = [((tm, f), BF16), ((f, tn), F32), ((tm, tn), F32)]
    return pl.pallas_call(
        _moe_down_kernel,
        out_shape=jax.ShapeDtypeStruct((r, d), F32),
        grid_spec=pltpu.PrefetchScalarGridSpec(
            num_scalar_prefetch=4,
            grid=(d // tn, r // tm),
            in_specs=[pl.BlockSpec((tm, f), lambda j, i, te, tf, tv, tr: (tr[i], 0)),
                      pl.BlockSpec((None, None, f, tn),
                                   lambda j, i, te, tf, tv, tr: (layer, te[i], 0, j))],
            out_specs=pl.BlockSpec((tm, tn), lambda j, i, te, tf, tv, tr: (i, j)),
            scratch_shapes=[pltpu.VMEM((f, tn), BF16)]),
        compiler_params=_params(("arbitrary", "arbitrary"), blocks, scratch=[((f, tn), BF16)],
                                temps=[((tm, tn), F32)] * 2),
        name="moe_down",
    )(*tiles, hmid, w_down)


def _moe_final_kernel(dest_ref, x_ref, y_ref, route_ref, g_ref, mod_ref, o_ref, buf_ref, sems):
    tile = x_ref.shape[0]
    base = pl.program_id(0) * tile

    def issue(r, carry):
        for k in range(TOP_K):
            _row_copy(y_ref, buf_ref.at[k], sems.at[k], dest_ref[(base + r) * TOP_K + k], r).start()
        return carry

    def drain(r, carry):
        for k in range(TOP_K):
            _row_copy(y_ref, buf_ref.at[k], sems.at[k], 0, r).wait()
        return carry

    lax.fori_loop(0, tile, issue, 0, unroll=GATHER_UNROLL)
    lax.fori_loop(0, tile, drain, 0, unroll=GATHER_UNROLL)
    f = route_ref[:, TOP_K:TOP_K + 1] * buf_ref[0]
    for k in range(1, TOP_K):
        f = f + route_ref[:, TOP_K + k:TOP_K + k + 1] * buf_ref[k]
    o_ref[...] = x_ref[...] + mod_ref[5:6, :] * _rms(f, g_ref[...])


def _moe_final(x1, y, dest, route, g_post, mod, n_rows, seq, n_lat_rows, n_batch):
    d = x1.shape[1]
    tr = EW_ROWS
    ridx = functools.partial(_mod_row_index, rows_per_tile=tr, seq=seq, n_lat_rows=n_lat_rows,
                             n_batch=n_batch)
    row = pl.BlockSpec((tr, d), lambda i, dest_ref: (i, 0))
    blocks = [((tr, d), F32)] * 2 + [((N_MOD, d), F32)]
    return pl.pallas_call(
        _moe_final_kernel,
        out_shape=jax.ShapeDtypeStruct((n_rows, d), F32),
        grid_spec=pltpu.PrefetchScalarGridSpec(
            num_scalar_prefetch=1,
            grid=(n_rows // tr,),
            in_specs=[row, pl.BlockSpec(memory_space=pl.ANY),
                      pl.BlockSpec((tr, V7X_LANES), lambda i, dest_ref: (i, 0)),
                      pl.BlockSpec((1, d), lambda i, dest_ref: (0, 0)),
                      pl.BlockSpec((None, N_MOD, d), lambda i, dest_ref: (ridx(i), 0, 0))],
            out_specs=row,
            scratch_shapes=[pltpu.VMEM((TOP_K, tr, d), F32), pltpu.SemaphoreType.DMA((TOP_K,))]),
        compiler_params=_params(("arbitrary",), blocks, scratch=[((TOP_K, tr, d), F32)],
                                temps=[((tr, d), F32)] * 2),
        name="moe_combine_final",
    )(dest, x1, y, route, g_post.reshape(1, d), mod)


def _rope_tables(seq, n_batch, ctx_rows):
    n = jnp.arange(seq)
    pos_r = (n // GRID_W).astype(F32)
    pos_w = (n % GRID_W).astype(F32)
    n_freq = HEAD_DIM // 4
    inv_freq = ROPE_THETA ** (-jnp.arange(n_freq, dtype=F32) / n_freq)
    ar = pos_r[:, None] * inv_freq
    aw = pos_w[:, None] * inv_freq
    cos = jnp.concatenate([jnp.cos(ar), jnp.cos(ar), jnp.cos(aw), jnp.cos(aw)], axis=-1)
    sin = jnp.concatenate([-jnp.sin(ar), jnp.sin(ar), -jnp.sin(aw), jnp.sin(aw)], axis=-1)
    cos = jnp.concatenate([cos] * n_batch + [jnp.ones((ctx_rows, HEAD_DIM), F32)], axis=0)
    sin = jnp.concatenate([sin] * n_batch + [jnp.zeros((ctx_rows, HEAD_DIM), F32)], axis=0)
    return cos, sin


def kernel(x, c, ctx, c_ctx, w_mod, b_mod, g_pre_mix, g_post_mix, g_pre_ffn, g_post_ffn, w_in, w_out,
           sink_a, qn_c, kn_c, gm_ln_g, gm_ln_b, gm_ws, gm_bs, ffn_w_gate, ffn_w_up, ffn_w_down,
           moe_router, moe_w_gate, moe_w_up, moe_w_down):
    n_batch, seq, d = x.shape
    n_ctx = ctx.shape[1]
    depth = w_mod.shape[0]
    n_lat = n_batch * seq
    n_ctx_rows = n_batch * n_ctx
    t = n_lat + n_ctx_rows
    a_heads = sink_a.shape[1]
    a_w = a_heads * HEAD_DIM
    akv_w = A_KV_HEADS * HEAD_DIM
    b_w = gm_ln_g.shape[1]
    ckv_w = C_KV_HEADS * HEAD_DIM
    c_w = w_in.shape[2] - a_w - 2 * akv_w - 2 * b_w - 2 * ckv_w
    c_heads = c_w // HEAD_DIM
    assert n_ctx_rows % EW_ROWS == 0 and seq % EW_ROWS == 0 and n_batch + 1 <= 8
    assert a_w == c_w and akv_w == COL_TILE and ckv_w == COL_TILE

    xall = jnp.concatenate([x.reshape(n_lat, d), ctx.reshape(n_ctx_rows, d)], axis=0)
    c8 = jnp.concatenate([c, c_ctx[None, :], jnp.zeros((8 - n_batch - 1, d), F32)], axis=0)
    mod_all = _modulation(c8, w_mod, b_mod).reshape(depth, 8, N_MOD, d)

    cos, sin = _rope_tables(seq, n_batch, n_ctx_rows)

    for l in range(depth):
        need_ctx = l < depth - 1
        n_rows = t if need_ctx else n_lat
        mod = mod_all[l]
        geo = dict(seq=seq, n_lat_rows=n_lat, n_batch=n_batch)

        h = _prenorm(xall, g_pre_mix[l], mod, **geo)
        gq, gk = qn_c[l].reshape(1, HEAD_DIM), kn_c[l].reshape(1, HEAD_DIM)
        rope = (cos, sin, gq, gk)
        col = 0
        za = _inproj(h, w_in, l, col, a_w + akv_w, "rope", rope, n_q_tiles=a_w // COL_TILE)
        col += a_w + akv_w
        av = _inproj(h, w_in, l, col, akv_w, "plain")
        col += akv_w
        zb = _inproj(h, w_in, l, col, 2 * b_w, "gelu")
        col += 2 * b_w
        zc = _inproj(h, w_in, l, col, c_w + ckv_w, "normrope", rope, n_q_tiles=c_w // COL_TILE)
        col += c_w + ckv_w
        cv = _inproj(h, w_in, l, col, ckv_w, "plain")

        oa = _window_attn(za, av, sink_a[l], n_batch, seq, n_ctx, need_ctx)
        ob = _gmlp(zb, gm_ln_g[l], gm_ln_b[l], gm_ws[l], gm_bs[l], n_rows)
        oc = _global_attn(zc, cv, n_batch, seq, n_ctx, c_heads, need_ctx)

        y = _outproj(oa, ob, oc, w_out, l, n_rows)
        i = l // 2
        if l % 2 == 0:
            x1, h2 = _postmix(xall, y, g_post_mix[l], g_pre_ffn[l], mod, n_rows, 0, **geo)
            f_dim = ffn_w_gate.shape[-1]
            hmid = _ffn_up(h2, ffn_w_gate, ffn_w_up, (i,), n_rows)
            half = f_dim // 2
            f = _ffn_down(hmid, ffn_w_down, (i,), 0, half, n_rows)
            f = _ffn_down(hmid, ffn_w_down, (i,), half, half, n_rows, prev=f)
            xall = _final(x1, f, g_post_ffn[l], mod, n_rows, **geo)
        else:
            x1, h2, h2f = _postmix(xall, y, g_post_mix[l], g_pre_ffn[l], mod, n_rows, 1, **geo)
            route = _router(h2, moe_router[i], n_rows)
            tok_of_row, dest, tiles = _route_plan(route, moe_router.shape[-1])
            xs = _gather_rows(h2f, tok_of_row)
            hmid = _moe_up(xs, moe_w_gate, moe_w_up, i, tiles)
            ys = _moe_down(hmid, moe_w_down, i, tiles)
            xall = _moe_final(x1, ys, dest, route, g_post_ffn[l], mod, n_rows, **geo)
    return xall[:n_lat].reshape(n_batch, seq, d)
```

```python
import functools
import math

import jax
import jax.numpy as jnp
from jax import lax
from jax.experimental import pallas as pl
from jax.experimental.pallas import tpu as pltpu

F32 = jnp.float32
BF16 = jnp.bfloat16

GRID_W = 64
HEAD_DIM = 128
BLOCK = 128
WINDOW = 128
A_KV_HEADS = 4
C_KV_HEADS = 4
B_GROUPS = 8
CHUNK = 128
N_MOD = 6
TOP_K = 2
ROPE_THETA = 10000.0
EPS = 1e-6
LOG2E = math.log2(math.e)

V7X_LANES = 128
V7X_VMEM_SCOPED_CAP = 60000 * 1024

ROW_TILE = 512
MAX_ROW_TILE = 1088
ROW_ALIGN = 16
COL_TILE = 512
FFN_COL_TILE = 256
EW_ROWS = 256
CAST_ROWS = 256
Q_TILE = 256
K_TILE = 512
GATHER_UNROLL = 8
ROUTER_PIECES = 3
EPILOGUE_CHUNKS = 4
SLAB_ROWS = 8


def _nbytes(shape, dtype):
    return math.prod(shape) * jnp.dtype(dtype).itemsize


def _params(semantics, blocks, scratch=(), temps=()):
    need = 2 * sum(_nbytes(s, d) for s, d in blocks)
    need += sum(_nbytes(s, d) for s, d in scratch)
    need += sum(_nbytes(s, d) for s, d in temps)
    limit = min(V7X_VMEM_SCOPED_CAP, max(need + need // 4, 16 * 1024 * 1024))
    return pltpu.CompilerParams(dimension_semantics=semantics, vmem_limit_bytes=limit)


def _row_tile(n_rows):
    for tm in range(MAX_ROW_TILE - MAX_ROW_TILE % ROW_ALIGN, 0, -ROW_ALIGN):
        if n_rows % tm == 0:
            return tm
    raise ValueError(f"no row tile for {n_rows} rows")


def _cast_weight(w_ref, wbf_ref):
    rows = w_ref.shape[0]
    step = CAST_ROWS if rows % CAST_ROWS == 0 else V7X_LANES
    assert rows % step == 0

    def body(r, carry):
        sl = pl.ds(pl.multiple_of(r * step, step), step)
        wbf_ref[sl, :] = w_ref[sl, :].astype(BF16)
        return carry

    lax.fori_loop(0, rows // step, body, 0)


def _dot(a, b):
    return jnp.dot(a, b, preferred_element_type=F32)


def _dot_nt(a, b):
    return lax.dot_general(a, b, (((1,), (1,)), ((), ())), preferred_element_type=F32)


def _rms(x, gain):
    return x * lax.rsqrt(jnp.mean(x * x, axis=-1, keepdims=True) + EPS) * gain


def _to_slabs(x, slab_ref):
    w = slab_ref.shape[2]
    for s in range(slab_ref.shape[1]):
        slab_ref[:, s, :] = x[:, s * w:(s + 1) * w]


def _from_slabs(slab_ref):
    return jnp.concatenate([slab_ref[:, s, :] for s in range(slab_ref.shape[1])], axis=1)


def _mod_kernel(c_ref, w_ref, b_ref, o_ref):
    s = jax.nn.silu(c_ref[...])
    o_ref[...] = _dot(s.astype(BF16), w_ref[...].astype(BF16)) + b_ref[...]


def _modulation(c8, w_mod, b_mod):
    n_layers, d, width = w_mod.shape
    tn = COL_TILE
    assert width % tn == 0
    blocks = [((8, d), F32), ((d, tn), F32), ((1, tn), F32), ((8, tn), F32)]
    return pl.pallas_call(
        _mod_kernel,
        out_shape=jax.ShapeDtypeStruct((n_layers, 8, width), F32),
        grid=(n_layers, width // tn),
        in_specs=[pl.BlockSpec((8, d), lambda l, j: (0, 0)),
                  pl.BlockSpec((None, d, tn), lambda l, j: (l, 0, j)),
                  pl.BlockSpec((None, 1, tn), lambda l, j: (l, 0, j))],
        out_specs=pl.BlockSpec((None, 8, tn), lambda l, j: (l, 0, j)),
        compiler_params=_params(("arbitrary", "arbitrary"), blocks, temps=[((d, tn), BF16)]),
        name="modulation",
    )(c8, w_mod, b_mod.reshape(n_layers, 1, width))


def _mod_row_index(i, rows_per_tile, seq, n_lat_rows, n_batch):
    lat_tiles = n_lat_rows // rows_per_tile
    return jnp.where(i < lat_tiles, (i * rows_per_tile) // seq, n_batch)


def _prenorm_kernel(x_ref, g_ref, mod_ref, o_ref):
    y = _rms(x_ref[...], g_ref[...])
    o_ref[...] = (y * (1.0 + mod_ref[1:2, :]) + mod_ref[0:1, :]).astype(BF16)


def _prenorm(xall, gain, mod, seq, n_lat_rows, n_batch):
    t, d = xall.shape
    tr = EW_ROWS
    ridx = functools.partial(_mod_row_index, rows_per_tile=tr, seq=seq, n_lat_rows=n_lat_rows,
                             n_batch=n_batch)
    blocks = [((tr, d), F32), ((1, d), F32), ((N_MOD, d), F32), ((tr, d), BF16)]
    return pl.pallas_call(
        _prenorm_kernel,
        out_shape=jax.ShapeDtypeStruct((t, d), BF16),
        grid=(t // tr,),
        in_specs=[pl.BlockSpec((tr, d), lambda i: (i, 0)),
                  pl.BlockSpec((1, d), lambda i: (0, 0)),
                  pl.BlockSpec((None, N_MOD, d), lambda i: (ridx(i), 0, 0))],
        out_specs=pl.BlockSpec((tr, d), lambda i: (i, 0)),
        compiler_params=_params(("arbitrary",), blocks, temps=[((tr, d), F32)] * 2),
        name="prenorm",
    )(xall, gain.reshape(1, d), mod)


def _postmix_kernel(x_ref, y_ref, gpost_ref, gpre_ref, mod_ref, x1_ref, h2_ref, *h2f_ref):
    x1 = x_ref[...] + mod_ref[2:3, :] * _rms(y_ref[...], gpost_ref[...])
    x1_ref[...] = x1
    h2 = _rms(x1, gpre_ref[...]) * (1.0 + mod_ref[4:5, :]) + mod_ref[3:4, :]
    h2_ref[...] = h2.astype(BF16)
    for ref in h2f_ref:
        _to_slabs(h2, ref)


def _postmix(xall, y, g_post, g_pre, mod, n_rows, want_slabs, seq, n_lat_rows, n_batch):
    d = xall.shape[1]
    tr = EW_ROWS
    assert d % (SLAB_ROWS * V7X_LANES) == 0
    slab = (SLAB_ROWS, d // SLAB_ROWS)
    ridx = functools.partial(_mod_row_index, rows_per_tile=tr, seq=seq, n_lat_rows=n_lat_rows,
                             n_batch=n_batch)
    row = pl.BlockSpec((tr, d), lambda i: (i, 0))
    vec = pl.BlockSpec((1, d), lambda i: (0, 0))
    blocks = [((tr, d), F32)] * (3 + want_slabs) + [((tr, d), BF16), ((N_MOD, d), F32)]
    out_shape = [jax.ShapeDtypeStruct((n_rows, d), F32), jax.ShapeDtypeStruct((n_rows, d), BF16)]
    out_specs = [row, row]
    if want_slabs:
        out_shape.append(jax.ShapeDtypeStruct((n_rows,) + slab, F32))
        out_specs.append(pl.BlockSpec((tr,) + slab, lambda i: (i, 0, 0)))
    return pl.pallas_call(
        _postmix_kernel,
        out_shape=tuple(out_shape),
        grid=(n_rows // tr,),
        in_specs=[row, row, vec, vec, pl.BlockSpec((None, N_MOD, d), lambda i: (ridx(i), 0, 0))],
        out_specs=tuple(out_specs),
        compiler_params=_params(("arbitrary",), blocks, temps=[((tr, d), F32)] * 3),
        name="postmix",
    )(xall, y, g_post.reshape(1, d), g_pre.reshape(1, d), mod)


def _final_kernel(x_ref, f_ref, g_ref, mod_ref, o_ref):
    o_ref[...] = x_ref[...] + mod_ref[5:6, :] * _rms(f_ref[...], g_ref[...])


def _final(x1, f, g_post, mod, n_rows, seq, n_lat_rows, n_batch):
    d = x1.shape[1]
    tr = EW_ROWS
    ridx = functools.partial(_mod_row_index, rows_per_tile=tr, seq=seq, n_lat_rows=n_lat_rows,
                             n_batch=n_batch)
    row = pl.BlockSpec((tr, d), lambda i: (i, 0))
    blocks = [((tr, d), F32)] * 3 + [((N_MOD, d), F32)]
    return pl.pallas_call(
        _final_kernel,
        out_shape=jax.ShapeDtypeStruct((n_rows, d), F32),
        grid=(n_rows // tr,),
        in_specs=[row, row, pl.BlockSpec((1, d), lambda i: (0, 0)),
                  pl.BlockSpec((None, N_MOD, d), lambda i: (ridx(i), 0, 0))],
        out_specs=row,
        compiler_params=_params(("arbitrary",), blocks, temps=[((tr, d), F32)] * 2),
        name="final_residual",
    )(x1, f, g_post.reshape(1, d), mod)


def _swap_pairs(x):
    lane = lax.broadcasted_iota(jnp.int32, x.shape, 1)
    quarter = HEAD_DIM // 4
    first = (lane & quarter) == 0
    return jnp.where(first, pltpu.roll(x, HEAD_DIM - quarter, axis=1), pltpu.roll(x, quarter, axis=1))


def _inproj_kernel(*refs, mode, n_q_tiles, q_scale):
    if mode in ("rope", "normrope"):
        h_ref, w_ref, cos_ref, sin_ref, gq_ref, gk_ref, o_ref, wbf_ref = refs
    else:
        h_ref, w_ref, o_ref, wbf_ref = refs
    j = pl.program_id(0)

    @pl.when(pl.program_id(1) == 0)
    def _():
        _cast_weight(w_ref, wbf_ref)

    tm = h_ref.shape[0]
    n_chunks = EPILOGUE_CHUNKS if tm % (EPILOGUE_CHUNKS * ROW_ALIGN) == 0 else 1
    rc = tm // n_chunks
    for c in range(n_chunks):
        rows = slice(c * rc, (c + 1) * rc)
        acc = _dot(h_ref[rows, :], wbf_ref[...])
        if mode == "plain":
            o_ref[rows, :] = acc.astype(BF16)
        elif mode == "gelu":
            o_ref[rows, :] = (0.5 * acc * (1.0 + lax.erf(acc * (2.0 ** -0.5)))).astype(BF16)
        else:
            is_q = j < n_q_tiles
            scale = jnp.where(is_q, q_scale, 1.0).astype(F32)
            cos = cos_ref[rows, :]
            sin = sin_ref[rows, :]
            gain = jnp.where(is_q, gq_ref[...], gk_ref[...])
            for hh in range(acc.shape[1] // HEAD_DIM):
                sl = slice(hh * HEAD_DIM, (hh + 1) * HEAD_DIM)
                xh = acc[:, sl]
                if mode == "normrope":
                    xh = _rms(xh, gain)
                xh = xh * cos + _swap_pairs(xh) * sin
                o_ref[rows, sl] = (xh * scale).astype(BF16)


def _inproj(h, w_in, layer, col_start, width, mode, rope=None, n_q_tiles=0):
    t, d = h.shape
    tm, tn = _row_tile(t), COL_TILE
    assert col_start % tn == 0 and width % tn == 0
    j0 = col_start // tn
    in_specs = [pl.BlockSpec((tm, d), lambda j, i: (i, 0)),
                pl.BlockSpec((None, d, tn), lambda j, i: (layer, 0, j0 + j))]
    args = [h, w_in]
    blocks = [((tm, d), BF16), ((d, tn), F32), ((tm, tn), BF16)]
    if mode in ("rope", "normrope"):
        cos, sin, gq, gk = rope
        tab = pl.BlockSpec((tm, HEAD_DIM), lambda j, i: (i, 0))
        vec = pl.BlockSpec((1, HEAD_DIM), lambda j, i: (0, 0))
        in_specs += [tab, tab, vec, vec]
        args += [cos, sin, gq, gk]
        blocks += [((tm, HEAD_DIM), F32)] * 2
    kern = functools.partial(_inproj_kernel, mode=mode, n_q_tiles=n_q_tiles,
                             q_scale=HEAD_DIM ** -0.5 * LOG2E)
    return pl.pallas_call(
        kern,
        out_shape=jax.ShapeDtypeStruct((t, width), BF16),
        grid=(width // tn, t // tm),
        in_specs=in_specs,
        out_specs=pl.BlockSpec((tm, tn), lambda j, i: (i, j)),
        scratch_shapes=[pltpu.VMEM((d, tn), BF16)],
        compiler_params=_params(("arbitrary", "arbitrary"), blocks, scratch=[((d, tn), BF16)],
                                temps=[((tm, tn), F32)] * 2),
        name="inproj_" + mode,
    )(*args)


def _window_attn_kernel(sink_ref, q_ref, kp_ref, kc_ref, kn_ref, kx_ref, vp_ref, vc_ref, vn_ref,
                        vx_ref, o_ref, *, n_blocks, group):
    n = pl.program_id(1)
    n_ctx = kx_ref.shape[0]
    n_keys = 3 * BLOCK + n_ctx
    rows = group * BLOCK
    assert BLOCK & (BLOCK - 1) == 0
    shift = BLOCK.bit_length() - 1
    r = lax.broadcasted_iota(jnp.int32, (rows, n_keys), 0) & (BLOCK - 1)
    c = lax.broadcasted_iota(jnp.int32, (rows, n_keys), 1)
    is_lat = n < n_blocks
    first_visible = r + jnp.where((n > 0) & is_lat, 0, BLOCK)
    last_visible = r + 2 * BLOCK - jnp.where(n < n_blocks - 1, 0, BLOCK)
    current_hidden_end = jnp.where(is_lat, BLOCK, 2 * BLOCK)
    hidden = ((c < BLOCK) & (c < first_visible)) | ((c >= BLOCK) & (c < current_hidden_end)) | (
        (c >= 2 * BLOCK) & (c < 3 * BLOCK) & (c > last_visible))
    valid = jnp.logical_not(hidden)
    rid = lax.broadcasted_iota(jnp.int32, (rows, 1), 0) >> shift
    for hk in range(A_KV_HEADS):
        ksl = slice(hk * HEAD_DIM, (hk + 1) * HEAD_DIM)
        k_all = jnp.concatenate([kp_ref[:, ksl], kc_ref[:, ksl], kn_ref[:, ksl], kx_ref[:, ksl]],
                                axis=0)
        v_all = jnp.concatenate([vp_ref[:, ksl], vc_ref[:, ksl], vn_ref[:, ksl], vx_ref[:, ksl]],
                                axis=0)
        q3 = jnp.concatenate(
            [q_ref[:, (hk * group + g) * HEAD_DIM:(hk * group + g + 1) * HEAD_DIM]
             for g in range(group)], axis=0)
        sink = jnp.zeros((rows, 1), F32)
        for g in range(group):
            sink = jnp.where(rid == g, sink_ref[hk * group + g] * LOG2E, sink)
        s = jnp.where(valid, _dot_nt(q3, k_all), -jnp.inf)
        m = jnp.maximum(jnp.max(s, axis=-1, keepdims=True), sink)
        p = jnp.exp2(s - m)
        denom = jnp.sum(p, axis=-1, keepdims=True) + jnp.exp2(sink - m)
        o = _dot(p.astype(BF16), v_all) / denom
        for g in range(group):
            osl = slice((hk * group + g) * HEAD_DIM, (hk * group + g + 1) * HEAD_DIM)
            o_ref[:, osl] = o[g * BLOCK:(g + 1) * BLOCK, :].astype(BF16)


def _window_attn(za, av, sink, n_batch, seq, n_ctx, with_ctx):
    n_heads = sink.shape[0]
    group = n_heads // A_KV_HEADS
    qw = n_heads * HEAD_DIM
    kw = A_KV_HEADS * HEAD_DIM
    assert qw % kw == 0 and seq % BLOCK == 0 and (n_batch * seq) % n_ctx == 0 and WINDOW == BLOCK
    assert n_ctx % BLOCK == 0
    nb = seq // BLOCK
    ncb = n_ctx // BLOCK if with_ctx else 0
    kcol = qw // kw
    ctx0 = (n_batch * seq) // n_ctx

    def qblk(b, n):
        return (jnp.where(n < nb, b * nb + n, n_batch * nb + b * ncb + n - nb), 0)

    def blk(shift):
        return lambda b, n: (b * nb + jnp.clip(n + shift, 0, nb - 1), kcol)

    def vblk(shift):
        return lambda b, n: (b * nb + jnp.clip(n + shift, 0, nb - 1), 0)

    kspec = [pl.BlockSpec((BLOCK, kw), blk(s)) for s in (-1, 0, 1)]
    vspec = [pl.BlockSpec((BLOCK, kw), vblk(s)) for s in (-1, 0, 1)]
    blocks = ([((BLOCK, qw), BF16)] * 2 + [((BLOCK, kw), BF16)] * 6 + [((n_ctx, kw), BF16)] * 2)
    n_keys = 3 * BLOCK + n_ctx
    kern = functools.partial(_window_attn_kernel, n_blocks=nb, group=group)
    return pl.pallas_call(
        kern,
        out_shape=jax.ShapeDtypeStruct((n_batch * (nb + ncb) * BLOCK, qw), BF16),
        grid=(n_batch, nb + ncb),
        in_specs=[pl.BlockSpec(memory_space=pltpu.SMEM), pl.BlockSpec((BLOCK, qw), qblk)]
                 + kspec + [pl.BlockSpec((n_ctx, kw), lambda b, n: (ctx0 + b, kcol))]
                 + vspec + [pl.BlockSpec((n_ctx, kw), lambda b, n: (ctx0 + b, 0))],
        out_specs=pl.BlockSpec((BLOCK, qw), qblk),
        compiler_params=_params(("arbitrary", "arbitrary"), blocks,
                                temps=[((group * BLOCK, n_keys), F32)] * 4),
        name="window_attn",
    )(sink, za, za, za, za, za, av, av, av, av)


def _gmlp_kernel(z_ref, g_ref, b_ref, ws_ref, bst_ref, o_ref):
    width = g_ref.shape[1]
    u = z_ref[:, :width].astype(F32)
    v = z_ref[:, width:].astype(F32)
    mu = jnp.mean(v, axis=-1, keepdims=True)
    vc = v - mu
    var = jnp.mean(vc * vc, axis=-1, keepdims=True)
    vn = (vc * lax.rsqrt(var + EPS) * g_ref[...] + b_ref[...]).astype(BF16)
    gd = width // B_GROUPS
    for g in range(B_GROUPS):
        sl = slice(g * gd, (g + 1) * gd)
        mixed = _dot(ws_ref[g].astype(BF16), vn[:, sl]) + bst_ref[:, g:g + 1]
        o_ref[:, sl] = (u[:, sl] * mixed).astype(BF16)


def _gmlp(zb, ln_g, ln_b, ws, bs, n_rows):
    w2 = zb.shape[1]
    width = w2 // 2
    blocks = [((CHUNK, w2), BF16), ((1, width), F32), ((1, width), F32),
              (ws.shape, F32), ((CHUNK, B_GROUPS), F32), ((CHUNK, width), BF16)]
    return pl.pallas_call(
        _gmlp_kernel,
        out_shape=jax.ShapeDtypeStruct((n_rows, width), BF16),
        grid=(n_rows // CHUNK,),
        in_specs=[pl.BlockSpec((CHUNK, w2), lambda i: (i, 0)),
                  pl.BlockSpec((1, width), lambda i: (0, 0)),
                  pl.BlockSpec((1, width), lambda i: (0, 0)),
                  pl.BlockSpec(ws.shape, lambda i: (0, 0, 0)),
                  pl.BlockSpec((CHUNK, B_GROUPS), lambda i: (0, 0))],
        out_specs=pl.BlockSpec((CHUNK, width), lambda i: (i, 0)),
        compiler_params=_params(("arbitrary",), blocks, temps=[((CHUNK, w2), F32)] * 2),
        name="chunk_gmlp",
    )(zb, ln_g.reshape(1, width), ln_b.reshape(1, width), ws, bs.T)


def _global_attn_kernel(q_ref, kx_ref, vx_ref, k_ref, v_ref, o_ref, vext_ref, *, group,
                        n_lat_tiles, has_ctx_tiles):
    tq = q_ref.shape[0]
    n_ctx = kx_ref.shape[0]
    n_lat = k_ref.shape[0]
    rows = group * tq

    @pl.when(pl.program_id(2) == 0)
    def _():
        vext_ref[0:n_ctx, 0:HEAD_DIM] = vx_ref[...]
        vext_ref[n_ctx:, 0:HEAD_DIM] = v_ref[...]
        vext_ref[:, HEAD_DIM:] = jnp.ones((n_ctx + n_lat, HEAD_DIM), BF16)

    def attend(chunks):
        q3 = jnp.concatenate([q_ref[:, g * HEAD_DIM:(g + 1) * HEAD_DIM] for g in range(group)],
                             axis=0)
        m = jnp.full((rows, HEAD_DIM), -jnp.inf, F32)
        acc = jnp.zeros((rows, 2 * HEAD_DIM), F32)
        for k_chunk, v0, nk in chunks:
            s = _dot_nt(q3, k_chunk())
            m_new = jnp.maximum(m, jnp.broadcast_to(jnp.max(s, axis=-1, keepdims=True), m.shape))
            alpha = jnp.exp2(m - m_new)
            p = jnp.exp2(s - jnp.tile(m_new, (1, nk // HEAD_DIM)))
            acc = jnp.tile(alpha, (1, 2)) * acc + _dot(p.astype(BF16), vext_ref[v0:v0 + nk, :])
            m = m_new
        o = acc[:, :HEAD_DIM] / acc[:, HEAD_DIM:]
        for g in range(group):
            o_ref[:, g * HEAD_DIM:(g + 1) * HEAD_DIM] = o[g * tq:(g + 1) * tq, :].astype(BF16)

    ctx_chunk = [(lambda: kx_ref[...], 0, n_ctx)]
    lat_chunks = [(lambda c=c: k_ref[c:c + K_TILE, :], n_ctx + c, K_TILE)
                  for c in range(0, n_lat, K_TILE)]
    is_lat = pl.program_id(2) < n_lat_tiles
    pl.when(is_lat)(lambda: attend(ctx_chunk + lat_chunks))
    if has_ctx_tiles:
        pl.when(jnp.logical_not(is_lat))(lambda: attend(ctx_chunk))


def _global_attn(zc, cv, n_batch, seq, n_ctx, n_heads, with_ctx):
    group = n_heads // C_KV_HEADS
    gw = group * HEAD_DIM
    tq = Q_TILE
    assert seq % tq == 0 and seq % K_TILE == 0 and (n_batch * seq) % n_ctx == 0 and n_ctx % tq == 0
    nq = seq // tq
    ncq = n_ctx // tq if with_ctx else 0
    ctx0 = (n_batch * seq) // n_ctx
    rows = group * tq

    def qblk(b, hk, i):
        return (jnp.where(i < nq, b * nq + i, n_batch * nq + b * ncq + i - nq), hk)

    blocks = [((tq, gw), BF16)] * 2 + [((n_ctx, HEAD_DIM), BF16)] * 2 + [((seq, HEAD_DIM), BF16)] * 2
    scratch = [((n_ctx + seq, 2 * HEAD_DIM), BF16)]
    return pl.pallas_call(
        functools.partial(_global_attn_kernel, group=group, n_lat_tiles=nq, has_ctx_tiles=ncq > 0),
        out_shape=jax.ShapeDtypeStruct((n_batch * (nq + ncq) * tq, n_heads * HEAD_DIM), BF16),
        grid=(n_batch, C_KV_HEADS, nq + ncq),
        in_specs=[pl.BlockSpec((tq, gw), qblk),
                  pl.BlockSpec((n_ctx, HEAD_DIM), lambda b, hk, i: (ctx0 + b, n_heads + hk)),
                  pl.BlockSpec((n_ctx, HEAD_DIM), lambda b, hk, i: (ctx0 + b, hk)),
                  pl.BlockSpec((seq, HEAD_DIM), lambda b, hk, i: (b, n_heads + hk)),
                  pl.BlockSpec((seq, HEAD_DIM), lambda b, hk, i: (b, hk))],
        out_specs=pl.BlockSpec((tq, gw), qblk),
        scratch_shapes=[pltpu.VMEM(s, d) for s, d in scratch],
        compiler_params=_params(("arbitrary", "arbitrary", "arbitrary"), blocks, scratch=scratch,
                                temps=[((rows, K_TILE), F32)] * 6),
        name="global_attn",
    )(zc, zc, cv, zc, cv)


def _outproj_kernel(a_ref, b_ref, c_ref, w_ref, o_ref, wbf_ref):
    @pl.when(pl.program_id(1) == 0)
    def _():
        _cast_weight(w_ref, wbf_ref)

    ka, kb = a_ref.shape[1], b_ref.shape[1]
    acc = _dot(a_ref[...], wbf_ref[0:ka, :])
    acc += _dot(b_ref[...], wbf_ref[ka:ka + kb, :])
    acc += _dot(c_ref[...], wbf_ref[ka + kb:, :])
    o_ref[...] = acc


def _outproj(oa, ob, oc, w_out, layer, n_rows):
    t = oa.shape[0]
    _, k, d = w_out.shape
    tm, tn = _row_tile(n_rows), COL_TILE
    assert oa.shape[1] + ob.shape[1] + oc.shape[1] == k and d % tn == 0
    blocks = [((tm, k), BF16), ((k, tn), F32), ((tm, tn), F32)]
    return pl.pallas_call(
        _outproj_kernel,
        out_shape=jax.ShapeDtypeStruct((t, d), F32),
        grid=(d // tn, n_rows // tm),
        in_specs=[pl.BlockSpec((tm, oa.shape[1]), lambda j, i: (i, 0)),
                  pl.BlockSpec((tm, ob.shape[1]), lambda j, i: (i, 0)),
                  pl.BlockSpec((tm, oc.shape[1]), lambda j, i: (i, 0)),
                  pl.BlockSpec((None, k, tn), lambda j, i: (layer, 0, j))],
        out_specs=pl.BlockSpec((tm, tn), lambda j, i: (i, j)),
        scratch_shapes=[pltpu.VMEM((k, tn), BF16)],
        compiler_params=_params(("arbitrary", "arbitrary"), blocks, scratch=[((k, tn), BF16)],
                                temps=[((tm, tn), F32)] * 2),
        name="outproj",
    )(oa, ob, oc, w_out)


def _ffn_up_kernel(h_ref, wg_ref, wu_ref, o_ref, wgbf_ref, wubf_ref):
    @pl.when(pl.program_id(1) == 0)
    def _():
        _cast_weight(wg_ref, wgbf_ref)
        _cast_weight(wu_ref, wubf_ref)

    h = h_ref[...]
    gate = _dot(h, wgbf_ref[...])
    up = _dot(h, wubf_ref[...])
    o_ref[...] = (jax.nn.silu(gate) * up).astype(BF16)


def _ffn_up(h2, w_gate, w_up, sel, n_rows):
    t, d = h2.shape
    f = w_gate.shape[-1]
    tm, tn = _row_tile(n_rows), FFN_COL_TILE
    assert f % tn == 0
    lead = (None,) * len(sel)
    wspec = pl.BlockSpec(lead + (d, tn), lambda j, i: sel + (0, j))
    blocks = [((tm, d), BF16), ((d, tn), F32), ((d, tn), F32), ((tm, tn), BF16)]
    return pl.pallas_call(
        _ffn_up_kernel,
        out_shape=jax.ShapeDtypeStruct((t, f), BF16),
        grid=(f // tn, n_rows // tm),
        in_specs=[pl.BlockSpec((tm, d), lambda j, i: (i, 0)), wspec, wspec],
        out_specs=pl.BlockSpec((tm, tn), lambda j, i: (i, j)),
        scratch_shapes=[pltpu.VMEM((d, tn), BF16)] * 2,
        compiler_params=_params(("arbitrary", "arbitrary"), blocks, scratch=[((d, tn), BF16)] * 2,
                                temps=[((tm, tn), F32)] * 3),
        name="ffn_up",
    )(h2, w_gate, w_up)


def _ffn_down_kernel(*refs, has_prev):
    if has_prev:
        a_ref, w_ref, prev_ref, o_ref, wbf_ref = refs
    else:
        a_ref, w_ref, o_ref, wbf_ref = refs

    @pl.when(pl.program_id(1) == 0)
    def _():
        _cast_weight(w_ref, wbf_ref)

    acc = _dot(a_ref[...], wbf_ref[...])
    if has_prev:
        acc = prev_ref[...] + acc
    o_ref[...] = acc


def _ffn_down(a, w_down, sel, k_start, k_size, n_rows, prev=None):
    t = a.shape[0]
    d = w_down.shape[-1]
    tm, tn = ROW_TILE, COL_TILE
    assert k_start % k_size == 0 and d % tn == 0 and n_rows % tm == 0
    kb = k_start // k_size
    lead = (None,) * len(sel)
    in_specs = [pl.BlockSpec((tm, k_size), lambda j, i: (i, kb)),
                pl.BlockSpec(lead + (k_size, tn), lambda j, i: sel + (kb, j))]
    args = [a, w_down]
    blocks = [((tm, k_size), BF16), ((k_size, tn), F32), ((tm, tn), F32)]
    aliases = {}
    if prev is not None:
        in_specs.append(pl.BlockSpec((tm, tn), lambda j, i: (i, j)))
        aliases = {len(args): 0}
        args.append(prev)
        blocks.append(((tm, tn), F32))
    return pl.pallas_call(
        functools.partial(_ffn_down_kernel, has_prev=prev is not None),
        out_shape=jax.ShapeDtypeStruct((t, d), F32),
        grid=(d // tn, n_rows // tm),
        in_specs=in_specs,
        out_specs=pl.BlockSpec((tm, tn), lambda j, i: (i, j)),
        scratch_shapes=[pltpu.VMEM((k_size, tn), BF16)],
        input_output_aliases=aliases,
        compiler_params=_params(("arbitrary", "arbitrary"), blocks, scratch=[((k_size, tn), BF16)],
                                temps=[((tm, tn), F32)] * 2),
        name="ffn_down",
    )(*args)


def _router_kernel(h_ref, w_ref, o_ref, *, n_experts):
    pieces = _dot(h_ref[...], w_ref[...])
    logits = pieces
    for k in range(1, ROUTER_PIECES):
        logits = logits + pltpu.roll(pieces, V7X_LANES - k * n_experts, axis=1)
    lane = lax.broadcasted_iota(jnp.int32, logits.shape, 1).astype(F32)
    neg = -jnp.inf
    logits = jnp.where(lane < n_experts, logits, neg)
    picked = []
    remaining = logits
    for _ in range(TOP_K):
        top = jnp.max(remaining, axis=-1, keepdims=True)
        idx = jnp.min(jnp.where(remaining == top, lane, float(logits.shape[1])), axis=-1,
                      keepdims=True)
        picked.append((top, idx))
        remaining = jnp.where(lane == idx, neg, remaining)
    top0 = picked[0][0]
    denom = sum(jnp.exp(tv - top0) for tv, _ in picked)
    route = jnp.zeros(logits.shape, F32)
    for k, (tv, idx) in enumerate(picked):
        route = jnp.where(lane == k, idx, route)
        route = jnp.where(lane == TOP_K + k, jnp.exp(tv - top0) / denom, route)
    o_ref[...] = route


def _router(h2, w_router, n_rows):
    t, d = h2.shape
    n_experts = w_router.shape[1]
    assert ROUTER_PIECES * n_experts <= V7X_LANES
    pieces, rest = [], w_router
    for _ in range(ROUTER_PIECES):
        pieces.append(rest.astype(BF16))
        rest = rest - pieces[-1].astype(F32)
    wpad = jnp.pad(jnp.concatenate(pieces, axis=1),
                   ((0, 0), (0, V7X_LANES - ROUTER_PIECES * n_experts)))
    tr = EW_ROWS
    blocks = [((tr, d), BF16), ((d, V7X_LANES), BF16), ((tr, V7X_LANES), F32)]
    return pl.pallas_call(
        functools.partial(_router_kernel, n_experts=n_experts),
        out_shape=jax.ShapeDtypeStruct((t, V7X_LANES), F32),
        grid=(n_rows // tr,),
        in_specs=[pl.BlockSpec((tr, d), lambda i: (i, 0)),
                  pl.BlockSpec((d, V7X_LANES), lambda i: (0, 0))],
        out_specs=pl.BlockSpec((tr, V7X_LANES), lambda i: (i, 0)),
        compiler_params=_params(("arbitrary",), blocks, temps=[((tr, d), F32)] * 3),
        name="moe_router",
    )(h2, wpad)


def _route_plan(route, n_experts):
    tile = ROW_TILE
    n_tok = route.shape[0]
    n_asg = n_tok * TOP_K
    e_flat = route[:, :TOP_K].astype(jnp.int32).reshape(n_asg)
    onehot = (e_flat[:, None] == jnp.arange(n_experts, dtype=jnp.int32)[None, :]).astype(jnp.int32)
    csum = jnp.cumsum(onehot, axis=0)
    rank = jnp.take_along_axis(csum, e_flat[:, None], axis=1)[:, 0] - 1
    counts = csum[-1]
    tiles_per = (counts + tile - 1) // tile
    tile_end = jnp.cumsum(tiles_per)
    tile_start = tile_end - tiles_per
    dest = tile_start[e_flat] * tile + rank

    n_tiles = (n_asg + n_experts * (tile - 1)) // tile
    tile_ids = jnp.arange(n_tiles, dtype=jnp.int32)
    n_used = tile_end[-1]
    valid = tile_ids < n_used
    expert_raw = jnp.sum((tile_end[None, :] <= tile_ids[:, None]).astype(jnp.int32), axis=1)
    tile_expert = jnp.where(valid, expert_raw, expert_raw[n_used - 1])
    tile_first = valid & (tile_ids == tile_start[tile_expert])
    tile_row = jnp.where(valid, tile_ids, n_used - 1)

    tok_of_row = jnp.zeros((n_tiles * tile,), jnp.int32).at[dest].set(
        jnp.arange(n_asg, dtype=jnp.int32) // TOP_K, unique_indices=True)
    tile_first = tile_first.astype(jnp.int32)
    tile_slot = (jnp.cumsum(tile_first) - 1) % 2
    tiles = (tile_expert, tile_first, valid.astype(jnp.int32), tile_row, tile_slot)
    return tok_of_row, dest, tiles


def _row_copy(src_hbm, dst_vmem, sem, src_row, dst_row):
    return pltpu.make_async_copy(src_hbm.at[pl.ds(src_row, 1)], dst_vmem.at[pl.ds(dst_row, 1)], sem)


def _gather_rows_kernel(idx_ref, src_ref, o_ref, buf_ref, sem):
    tile = buf_ref.shape[0]
    base = pl.program_id(0) * tile

    def issue(g, carry):
        for u in range(GATHER_UNROLL):
            r = g * GATHER_UNROLL + u
            _row_copy(src_ref, buf_ref, sem, idx_ref[base + r], r).start(priority=u % 2)
        return carry

    def drain(r, carry):
        _row_copy(src_ref, buf_ref, sem, 0, r).wait()
        return carry

    lax.fori_loop(0, tile // GATHER_UNROLL, issue, 0)
    lax.fori_loop(0, tile, drain, 0, unroll=GATHER_UNROLL)
    o_ref[...] = _from_slabs(buf_ref).astype(BF16)


def _gather_rows(src, idx):
    slab = src.shape[1:]
    d = slab[0] * slab[1]
    n = idx.shape[0]
    tg = EW_ROWS
    assert n % tg == 0
    blocks = [((tg, d), BF16)]
    return pl.pallas_call(
        _gather_rows_kernel,
        out_shape=jax.ShapeDtypeStruct((n, d), BF16),
        grid_spec=pltpu.PrefetchScalarGridSpec(
            num_scalar_prefetch=1,
            grid=(n // tg,),
            in_specs=[pl.BlockSpec(memory_space=pl.ANY)],
            out_specs=pl.BlockSpec((tg, d), lambda i, idx_ref: (i, 0)),
            scratch_shapes=[pltpu.VMEM((tg,) + slab, F32), pltpu.SemaphoreType.DMA]),
        compiler_params=_params(("arbitrary",), blocks, scratch=[((tg, d), F32)],
                                temps=[((tg, d), F32)]),
        name="moe_dispatch",
    )(idx, src)


def _held_tile(s, n_tiles):
    return jnp.minimum(s, n_tiles - 1)


def _done_tile(s):
    return jnp.maximum(s - 1, 0)


def _grouped_step(tables, w_refs, wbf_refs, compute, o_ref):
    _, tf_ref, tv_ref, _, ts_ref = tables
    n_tiles = tf_ref.shape[0]
    s = pl.program_id(1)
    held = _held_tile(s, n_tiles)
    done = _done_tile(s)

    @pl.when((s < n_tiles) & (tf_ref[held] == 1))
    def _():
        for w_ref, wbf_ref in zip(w_refs, wbf_refs):
            _cast_weight(w_ref, wbf_ref.at[ts_ref[held]])

    @pl.when((s > 0) & (tv_ref[done] == 1))
    def _():
        o_ref[...] = compute([wbf_ref[ts_ref[done]] for wbf_ref in wbf_refs])

    @pl.when((s > 0) & (tv_ref[done] == 0))
    def _():
        o_ref[...] = jnp.zeros(o_ref.shape, o_ref.dtype)


def _moe_up_kernel(te_ref, tf_ref, tv_ref, tr_ref, ts_ref, x_ref, wg_ref, wu_ref, o_ref, wgbf_ref,
                   wubf_ref):
    def compute(w):
        x = x_ref[...]
        return (jax.nn.silu(_dot(x, w[0])) * _dot(x, w[1])).astype(BF16)

    _grouped_step((te_ref, tf_ref, tv_ref, tr_ref, ts_ref), (wg_ref, wu_ref), (wgbf_ref, wubf_ref),
                  compute, o_ref)


def _moe_up(xs, w_gate, w_up, layer, tiles):
    r, d = xs.shape
    f = w_gate.shape[-1]
    tm, tn = ROW_TILE, FFN_COL_TILE
    assert f % tn == 0 and r % tm == 0
    nt = r // tm
    wspec = pl.BlockSpec((None, None, d, tn),
                         lambda j, s, te, tf, tv, tr, ts: (layer, te[_held_tile(s, nt)], 0, j))
    blocks = [((tm, d), BF16), ((d, tn), F32), ((d, tn), F32), ((tm, tn), BF16)]
    return pl.pallas_call(
        _moe_up_kernel,
        out_shape=jax.ShapeDtypeStruct((r, f), BF16),
        grid_spec=pltpu.PrefetchScalarGridSpec(
            num_scalar_prefetch=5,
            grid=(f // tn, nt + 1),
            in_specs=[pl.BlockSpec((tm, d), lambda j, s, te, tf, tv, tr, ts: (tr[_done_tile(s)], 0)),
                      wspec, wspec],
            out_specs=pl.BlockSpec((tm, tn), lambda j, s, te, tf, tv, tr, ts: (_done_tile(s), j)),
            scratch_shapes=[pltpu.VMEM((2, d, tn), BF16)] * 2),
        compiler_params=_params(("arbitrary", "arbitrary"), blocks,
                                scratch=[((2, d, tn), BF16)] * 2, temps=[((tm, tn), F32)] * 3),
        name="moe_up",
    )(*tiles, xs, w_gate, w_up)


def _moe_down_kernel(te_ref, tf_ref, tv_ref, tr_ref, ts_ref, a_ref, w_ref, o_ref, wbf_ref):
    _grouped_step((te_ref, tf_ref, tv_ref, tr_ref, ts_ref), (w_ref,), (wbf_ref,),
                  lambda w: _dot(a_ref[...], w[0]), o_ref)


def _moe_down(hmid, w_down, layer, tiles):
    r, f = hmid.shape
    d = w_down.shape[-1]
    tm, tn = ROW_TILE, COL_TILE
    assert d % tn == 0 and r % tm == 0
    nt = r // tm
    blocks = [((tm, f), BF16), ((f, tn), F32), ((tm, tn), F32)]
    return pl.pallas_call(
        _moe_down_kernel,
        out_shape=jax.ShapeDtypeStruct((r, d), F32),
        grid_spec=pltpu.PrefetchScalarGridSpec(
            num_scalar_prefetch=5,
            grid=(d // tn, nt + 1),
            in_specs=[pl.BlockSpec((tm, f), lambda j, s, te, tf, tv, tr, ts: (tr[_done_tile(s)], 0)),
                      pl.BlockSpec((None, None, f, tn),
                                   lambda j, s, te, tf, tv, tr, ts:
                                   (layer, te[_held_tile(s, nt)], 0, j))],
            out_specs=pl.BlockSpec((tm, tn), lambda j, s, te, tf, tv, tr, ts: (_done_tile(s), j)),
            scratch_shapes=[pltpu.VMEM((2, f, tn), BF16)]),
        compiler_params=_params(("arbitrary", "arbitrary"), blocks, scratch=[((2, f, tn), BF16)],
                                temps=[((tm, tn), F32)] * 2),
        name="moe_down",
    )(*tiles, hmid, w_down)


def _moe_final_kernel(dest_ref, x_ref, y_ref, route_ref, g_ref, mod_ref, o_ref, buf_ref, sems):
    tile = x_ref.shape[0]
    base = pl.program_id(0) * tile

    def issue(g, carry):
        for u in range(GATHER_UNROLL):
            r = g * GATHER_UNROLL + u
            for k in range(TOP_K):
                _row_copy(y_ref, buf_ref.at[k], sems.at[k], dest_ref[(base + r) * TOP_K + k],
                          r).start(priority=k % 2)
        return carry

    def drain(r, carry):
        for k in range(TOP_K):
            _row_copy(y_ref, buf_ref.at[k], sems.at[k], 0, r).wait()
        return carry

    lax.fori_loop(0, tile // GATHER_UNROLL, issue, 0)
    lax.fori_loop(0, tile, drain, 0, unroll=GATHER_UNROLL)
    f = route_ref[:, TOP_K:TOP_K + 1] * buf_ref[0]
    for k in range(1, TOP_K):
        f = f + route_ref[:, TOP_K + k:TOP_K + k + 1] * buf_ref[k]
    o_ref[...] = x_ref[...] + mod_ref[5:6, :] * _rms(f, g_ref[...])


def _moe_final(x1, y, dest, route, g_post, mod, n_rows, seq, n_lat_rows, n_batch):
    d = x1.shape[1]
    tr = EW_ROWS
    ridx = functools.partial(_mod_row_index, rows_per_tile=tr, seq=seq, n_lat_rows=n_lat_rows,
                             n_batch=n_batch)
    row = pl.BlockSpec((tr, d), lambda i, dest_ref: (i, 0))
    blocks = [((tr, d), F32)] * 2 + [((N_MOD, d), F32)]
    return pl.pallas_call(
        _moe_final_kernel,
        out_shape=jax.ShapeDtypeStruct((n_rows, d), F32),
        grid_spec=pltpu.PrefetchScalarGridSpec(
            num_scalar_prefetch=1,
            grid=(n_rows // tr,),
            in_specs=[row, pl.BlockSpec(memory_space=pl.ANY),
                      pl.BlockSpec((tr, V7X_LANES), lambda i, dest_ref: (i, 0)),
                      pl.BlockSpec((1, d), lambda i, dest_ref: (0, 0)),
                      pl.BlockSpec((None, N_MOD, d), lambda i, dest_ref: (ridx(i), 0, 0))],
            out_specs=row,
            scratch_shapes=[pltpu.VMEM((TOP_K, tr, d), F32), pltpu.SemaphoreType.DMA((TOP_K,))]),
        compiler_params=_params(("arbitrary",), blocks, scratch=[((TOP_K, tr, d), F32)],
                                temps=[((tr, d), F32)] * 2),
        name="moe_combine_final",
    )(dest, x1, y, route, g_post.reshape(1, d), mod)


def _rope_tables(seq, n_batch, ctx_rows):
    n = jnp.arange(seq)
    pos_r = (n // GRID_W).astype(F32)
    pos_w = (n % GRID_W).astype(F32)
    n_freq = HEAD_DIM // 4
    inv_freq = ROPE_THETA ** (-jnp.arange(n_freq, dtype=F32) / n_freq)
    ar = pos_r[:, None] * inv_freq
    aw = pos_w[:, None] * inv_freq
    cos = jnp.concatenate([jnp.cos(ar), jnp.cos(ar), jnp.cos(aw), jnp.cos(aw)], axis=-1)
    sin = jnp.concatenate([-jnp.sin(ar), jnp.sin(ar), -jnp.sin(aw), jnp.sin(aw)], axis=-1)
    cos = jnp.concatenate([cos] * n_batch + [jnp.ones((ctx_rows, HEAD_DIM), F32)], axis=0)
    sin = jnp.concatenate([sin] * n_batch + [jnp.zeros((ctx_rows, HEAD_DIM), F32)], axis=0)
    return cos, sin


def kernel(x, c, ctx, c_ctx, w_mod, b_mod, g_pre_mix, g_post_mix, g_pre_ffn, g_post_ffn, w_in, w_out,
           sink_a, qn_c, kn_c, gm_ln_g, gm_ln_b, gm_ws, gm_bs, ffn_w_gate, ffn_w_up, ffn_w_down,
           moe_router, moe_w_gate, moe_w_up, moe_w_down):
    n_batch, seq, d = x.shape
    n_ctx = ctx.shape[1]
    depth = w_mod.shape[0]
    n_lat = n_batch * seq
    n_ctx_rows = n_batch * n_ctx
    t = n_lat + n_ctx_rows
    a_heads = sink_a.shape[1]
    a_w = a_heads * HEAD_DIM
    akv_w = A_KV_HEADS * HEAD_DIM
    b_w = gm_ln_g.shape[1]
    ckv_w = C_KV_HEADS * HEAD_DIM
    c_w = w_in.shape[2] - a_w - 2 * akv_w - 2 * b_w - 2 * ckv_w
    c_heads = c_w // HEAD_DIM
    assert n_ctx_rows % EW_ROWS == 0 and seq % EW_ROWS == 0 and n_batch + 1 <= 8
    assert a_w == c_w and akv_w == COL_TILE and ckv_w == COL_TILE

    xall = jnp.concatenate([x.reshape(n_lat, d), ctx.reshape(n_ctx_rows, d)], axis=0)
    c8 = jnp.concatenate([c, c_ctx[None, :], jnp.zeros((8 - n_batch - 1, d), F32)], axis=0)
    mod_all = _modulation(c8, w_mod, b_mod).reshape(depth, 8, N_MOD, d)

    cos, sin = _rope_tables(seq, n_batch, n_ctx_rows)

    for l in range(depth):
        need_ctx = l < depth - 1
        n_rows = t if need_ctx else n_lat
        mod = mod_all[l]
        geo = dict(seq=seq, n_lat_rows=n_lat, n_batch=n_batch)

        h = _prenorm(xall, g_pre_mix[l], mod, **geo)
        gq, gk = qn_c[l].reshape(1, HEAD_DIM), kn_c[l].reshape(1, HEAD_DIM)
        rope = (cos, sin, gq, gk)
        col = 0
        za = _inproj(h, w_in, l, col, a_w + akv_w, "rope", rope, n_q_tiles=a_w // COL_TILE)
        col += a_w + akv_w
        av = _inproj(h, w_in, l, col, akv_w, "plain")
        col += akv_w
        zb = _inproj(h, w_in, l, col, 2 * b_w, "gelu")
        col += 2 * b_w
        zc = _inproj(h, w_in, l, col, c_w + ckv_w, "normrope", rope, n_q_tiles=c_w // COL_TILE)
        col += c_w + ckv_w
        cv = _inproj(h, w_in, l, col, ckv_w, "plain")

        oa = _window_attn(za, av, sink_a[l], n_batch, seq, n_ctx, need_ctx)
        ob = _gmlp(zb, gm_ln_g[l], gm_ln_b[l], gm_ws[l], gm_bs[l], n_rows)
        oc = _global_attn(zc, cv, n_batch, seq, n_ctx, c_heads, need_ctx)

        y = _outproj(oa, ob, oc, w_out, l, n_rows)
        i = l // 2
        if l % 2 == 0:
            x1, h2 = _postmix(xall, y, g_post_mix[l], g_pre_ffn[l], mod, n_rows, 0, **geo)
            f_dim = ffn_w_gate.shape[-1]
            hmid = _ffn_up(h2, ffn_w_gate, ffn_w_up, (i,), n_rows)
            half = f_dim // 2
            f = _ffn_down(hmid, ffn_w_down, (i,), 0, half, n_rows)
            f = _ffn_down(hmid, ffn_w_down, (i,), half, half, n_rows, prev=f)
            xall = _final(x1, f, g_post_ffn[l], mod, n_rows, **geo)
        else:
            x1, h2, h2f = _postmix(xall, y, g_post_mix[l], g_pre_ffn[l], mod, n_rows, 1, **geo)
            route = _router(h2, moe_router[i], n_rows)
            tok_of_row, dest, tiles = _route_plan(route, moe_router.shape[-1])
            xs = _gather_rows(h2f, tok_of_row)
            hmid = _moe_up(xs, moe_w_gate, moe_w_up, i, tiles)
            ys = _moe_down(hmid, moe_w_down, i, tiles)
            xall = _moe_final(x1, ys, dest, route, g_post_ffn[l], mod, n_rows, **geo)
    return xall[:n_lat].reshape(n_batch, seq, d)
```

```python
import functools
import math

import jax
import jax.numpy as jnp
from jax import lax
from jax.experimental import pallas as pl
from jax.experimental.pallas import tpu as pltpu

F32 = jnp.float32
BF16 = jnp.bfloat16

GRID_W = 64
HEAD_DIM = 128
BLOCK = 128
WINDOW = 128
A_KV_HEADS = 4
C_KV_HEADS = 4
B_GROUPS = 8
CHUNK = 128
N_MOD = 6
TOP_K = 2
ROPE_THETA = 10000.0
EPS = 1e-6
LOG2E = math.log2(math.e)

V7X_LANES = 128
V7X_VMEM_SCOPED_CAP = 60000 * 1024

ROW_TILE = 512
MAX_ROW_TILE = 1088
ROW_ALIGN = 16
COL_TILE = 512
FFN_COL_TILE = 256
EW_ROWS = 256
CAST_ROWS = 256
Q_TILE = 256
K_TILE = 512
GATHER_UNROLL = 8
ROUTER_PIECES = 3
EPILOGUE_CHUNKS = 4
SLAB_ROWS = 8


def _nbytes(shape, dtype):
    return math.prod(shape) * jnp.dtype(dtype).itemsize


def _params(semantics, blocks, scratch=(), temps=()):
    need = 2 * sum(_nbytes(s, d) for s, d in blocks)
    need += sum(_nbytes(s, d) for s, d in scratch)
    need += sum(_nbytes(s, d) for s, d in temps)
    limit = min(V7X_VMEM_SCOPED_CAP, max(need + need // 4, 16 * 1024 * 1024))
    return pltpu.CompilerParams(dimension_semantics=semantics, vmem_limit_bytes=limit)


def _row_tile(n_rows):
    for tm in range(MAX_ROW_TILE - MAX_ROW_TILE % ROW_ALIGN, 0, -ROW_ALIGN):
        if n_rows % tm == 0:
            return tm
    raise ValueError(f"no row tile for {n_rows} rows")


def _cast_weight(w_ref, wbf_ref):
    rows = w_ref.shape[0]
    step = CAST_ROWS if rows % CAST_ROWS == 0 else V7X_LANES
    assert rows % step == 0

    def body(r, carry):
        sl = pl.ds(pl.multiple_of(r * step, step), step)
        wbf_ref[sl, :] = w_ref[sl, :].astype(BF16)
        return carry

    lax.fori_loop(0, rows // step, body, 0)


def _dot(a, b):
    return jnp.dot(a, b, preferred_element_type=F32)


def _dot_nt(a, b):
    return lax.dot_general(a, b, (((1,), (1,)), ((), ())), preferred_element_type=F32)


def _rms(x, gain):
    return x * lax.rsqrt(jnp.mean(x * x, axis=-1, keepdims=True) + EPS) * gain


def _to_slabs(x, slab_ref):
    w = slab_ref.shape[2]
    for s in range(slab_ref.shape[1]):
        slab_ref[:, s, :] = x[:, s * w:(s + 1) * w]


def _from_slabs(slab_ref):
    return jnp.concatenate([slab_ref[:, s, :] for s in range(slab_ref.shape[1])], axis=1)


def _mod_kernel(c_ref, w_ref, b_ref, o_ref):
    s = jax.nn.silu(c_ref[...])
    o_ref[...] = _dot(s.astype(BF16), w_ref[...].astype(BF16)) + b_ref[...]


def _modulation(c8, w_mod, b_mod):
    n_layers, d, width = w_mod.shape
    tn = COL_TILE
    assert width % tn == 0
    blocks = [((8, d), F32), ((d, tn), F32), ((1, tn), F32), ((8, tn), F32)]
    return pl.pallas_call(
        _mod_kernel,
        out_shape=jax.ShapeDtypeStruct((n_layers, 8, width), F32),
        grid=(n_layers, width // tn),
        in_specs=[pl.BlockSpec((8, d), lambda l, j: (0, 0)),
                  pl.BlockSpec((None, d, tn), lambda l, j: (l, 0, j)),
                  pl.BlockSpec((None, 1, tn), lambda l, j: (l, 0, j))],
        out_specs=pl.BlockSpec((None, 8, tn), lambda l, j: (l, 0, j)),
        compiler_params=_params(("arbitrary", "arbitrary"), blocks, temps=[((d, tn), BF16)]),
        name="modulation",
    )(c8, w_mod, b_mod.reshape(n_layers, 1, width))


def _mod_row_index(i, rows_per_tile, seq, n_lat_rows, n_batch):
    lat_tiles = n_lat_rows // rows_per_tile
    return jnp.where(i < lat_tiles, (i * rows_per_tile) // seq, n_batch)


def _row_parts(parts, tr):
    specs, starts, start = [], [], 0
    for p in parts:
        n = p.shape[0] // tr
        assert n * tr == p.shape[0]
        specs.append(pl.BlockSpec((tr, p.shape[1]),
                                  lambda i, *_, start=start, n=n: (jnp.clip(i - start, 0, n - 1), 0)))
        starts.append(start)
        start += n
    return specs, tuple(starts)


def _pick_rows(i, x_refs, starts):
    x = x_refs[0][...]
    for ref, start in zip(x_refs[1:], starts[1:]):
        x = jnp.where(i >= start, ref[...], x)
    return x


def _pre_mix_norm(x, g_ref, mod_ref):
    return (_rms(x, g_ref[...]) * (1.0 + mod_ref[1:2, :]) + mod_ref[0:1, :]).astype(BF16)


def _prenorm_kernel(*refs, starts):
    *x_refs, g_ref, mod_ref, o_ref = refs
    o_ref[...] = _pre_mix_norm(_pick_rows(pl.program_id(0), x_refs, starts), g_ref, mod_ref)


def _prenorm(x_parts, gain, mod, seq, n_lat_rows, n_batch):
    d = x_parts[0].shape[1]
    t = sum(p.shape[0] for p in x_parts)
    tr = EW_ROWS
    ridx = functools.partial(_mod_row_index, rows_per_tile=tr, seq=seq, n_lat_rows=n_lat_rows,
                             n_batch=n_batch)
    x_specs, starts = _row_parts(x_parts, tr)
    blocks = [((tr, d), F32)] * len(x_parts) + [((1, d), F32), ((N_MOD, d), F32), ((tr, d), BF16)]
    return pl.pallas_call(
        functools.partial(_prenorm_kernel, starts=starts),
        out_shape=jax.ShapeDtypeStruct((t, d), BF16),
        grid=(t // tr,),
        in_specs=x_specs + [pl.BlockSpec((1, d), lambda i: (0, 0)),
                            pl.BlockSpec((None, N_MOD, d), lambda i: (ridx(i), 0, 0))],
        out_specs=pl.BlockSpec((tr, d), lambda i: (i, 0)),
        compiler_params=_params(("arbitrary",), blocks, temps=[((tr, d), F32)] * 2),
        name="prenorm",
    )(*x_parts, gain.reshape(1, d), mod)


def _postmix_kernel(*refs, starts, n_parts):
    x_refs = refs[:n_parts]
    y_ref, gpost_ref, gpre_ref, mod_ref, x1_ref, h2_ref, *h2f_ref = refs[n_parts:]
    x = _pick_rows(pl.program_id(0), x_refs, starts)
    x1 = x + mod_ref[2:3, :] * _rms(y_ref[...].astype(F32), gpost_ref[...])
    x1_ref[...] = x1
    h2 = _rms(x1, gpre_ref[...]) * (1.0 + mod_ref[4:5, :]) + mod_ref[3:4, :]
    h2_ref[...] = h2.astype(BF16)
    for ref in h2f_ref:
        _to_slabs(h2, ref)


def _postmix(x_parts, y, g_post, g_pre, mod, n_rows, want_slabs, seq, n_lat_rows, n_batch):
    d = x_parts[0].shape[1]
    tr = EW_ROWS
    x_specs, starts = _row_parts(x_parts, tr)
    assert d % (SLAB_ROWS * V7X_LANES) == 0
    slab = (SLAB_ROWS, d // SLAB_ROWS)
    ridx = functools.partial(_mod_row_index, rows_per_tile=tr, seq=seq, n_lat_rows=n_lat_rows,
                             n_batch=n_batch)
    row = pl.BlockSpec((tr, d), lambda i: (i, 0))
    vec = pl.BlockSpec((1, d), lambda i: (0, 0))
    blocks = ([((tr, d), F32)] * (len(x_parts) + 1 + want_slabs)
              + [((tr, d), BF16)] * 2 + [((N_MOD, d), F32)])
    out_shape = [jax.ShapeDtypeStruct((n_rows, d), F32), jax.ShapeDtypeStruct((n_rows, d), BF16)]
    out_specs = [row, row]
    if want_slabs:
        out_shape.append(jax.ShapeDtypeStruct((n_rows,) + slab, F32))
        out_specs.append(pl.BlockSpec((tr,) + slab, lambda i: (i, 0, 0)))
    return pl.pallas_call(
        functools.partial(_postmix_kernel, starts=starts, n_parts=len(x_parts)),
        out_shape=tuple(out_shape),
        grid=(n_rows // tr,),
        in_specs=x_specs + [row, vec, vec,
                            pl.BlockSpec((None, N_MOD, d), lambda i: (ridx(i), 0, 0))],
        out_specs=tuple(out_specs),
        compiler_params=_params(("arbitrary",), blocks, temps=[((tr, d), F32)] * 3),
        name="postmix",
    )(*x_parts, y, g_post.reshape(1, d), g_pre.reshape(1, d), mod)


def _final_kernel(x_ref, f_ref, g_ref, mod_ref, *rest):
    x = x_ref[...] + mod_ref[5:6, :] * _rms(f_ref[...], g_ref[...])
    if len(rest) == 1:
        (o_ref,) = rest
    else:
        gnext_ref, modnext_ref, o_ref, h_ref = rest
        h_ref[...] = _pre_mix_norm(x, gnext_ref, modnext_ref)
    o_ref[...] = x


def _final(x1, f, g_post, mod, n_rows, next_norm, seq, n_lat_rows, n_batch):
    d = x1.shape[1]
    tr = EW_ROWS
    ridx = functools.partial(_mod_row_index, rows_per_tile=tr, seq=seq, n_lat_rows=n_lat_rows,
                             n_batch=n_batch)
    row = pl.BlockSpec((tr, d), lambda i: (i, 0))
    vec = pl.BlockSpec((1, d), lambda i: (0, 0))
    modspec = pl.BlockSpec((None, N_MOD, d), lambda i: (ridx(i), 0, 0))
    blocks = [((tr, d), F32)] * 3 + [((N_MOD, d), F32)]
    in_specs, args = [row, row, vec, modspec], [x1, f, g_post.reshape(1, d), mod]
    out_shape, out_specs = jax.ShapeDtypeStruct((n_rows, d), F32), row
    if next_norm is not None:
        in_specs += [vec, modspec]
        args += [next_norm[0].reshape(1, d), next_norm[1]]
        out_shape = (out_shape, jax.ShapeDtypeStruct((n_rows, d), BF16))
        out_specs = (row, row)
        blocks += [((N_MOD, d), F32), ((tr, d), BF16)]
    return pl.pallas_call(
        _final_kernel,
        out_shape=out_shape,
        grid=(n_rows // tr,),
        in_specs=in_specs,
        out_specs=out_specs,
        compiler_params=_params(("arbitrary",), blocks, temps=[((tr, d), F32)] * 2),
        name="final_residual",
    )(*args)


def _swap_pairs(x):
    lane = lax.broadcasted_iota(jnp.int32, x.shape, 1)
    quarter = HEAD_DIM // 4
    first = (lane & quarter) == 0
    return jnp.where(first, pltpu.roll(x, HEAD_DIM - quarter, axis=1), pltpu.roll(x, quarter, axis=1))


def _inproj_kernel(*refs, mode, n_q_tiles, q_scale):
    if mode in ("rope", "normrope"):
        h_ref, w_ref, cos_ref, sin_ref, gq_ref, gk_ref, o_ref, wbf_ref = refs
    else:
        h_ref, w_ref, o_ref, wbf_ref = refs
    j = pl.program_id(0)

    @pl.when(pl.program_id(1) == 0)
    def _():
        _cast_weight(w_ref, wbf_ref)

    tm = h_ref.shape[0]
    n_chunks = EPILOGUE_CHUNKS if tm % (EPILOGUE_CHUNKS * ROW_ALIGN) == 0 else 1
    rc = tm // n_chunks
    for c in range(n_chunks):
        rows = slice(c * rc, (c + 1) * rc)
        acc = _dot(h_ref[rows, :], wbf_ref[...])
        if mode == "plain":
            o_ref[rows, :] = acc.astype(BF16)
        elif mode == "gelu":
            o_ref[rows, :] = (0.5 * acc * (1.0 + lax.erf(acc * (2.0 ** -0.5)))).astype(BF16)
        else:
            is_q = j < n_q_tiles
            scale = jnp.where(is_q, q_scale, 1.0).astype(F32)
            cos = cos_ref[rows, :]
            sin = sin_ref[rows, :]
            gain = jnp.where(is_q, gq_ref[...], gk_ref[...])
            for hh in range(acc.shape[1] // HEAD_DIM):
                sl = slice(hh * HEAD_DIM, (hh + 1) * HEAD_DIM)
                xh = acc[:, sl]
                if mode == "normrope":
                    xh = _rms(xh, gain)
                xh = xh * cos + _swap_pairs(xh) * sin
                o_ref[rows, sl] = (xh * scale).astype(BF16)


def _inproj(h, w_in, layer, col_start, width, mode, rope=None, n_q_tiles=0):
    t, d = h.shape
    tm, tn = _row_tile(t), COL_TILE
    assert col_start % tn == 0 and width % tn == 0
    j0 = col_start // tn
    in_specs = [pl.BlockSpec((tm, d), lambda j, i: (i, 0)),
                pl.BlockSpec((None, d, tn), lambda j, i: (layer, 0, j0 + j))]
    args = [h, w_in]
    blocks = [((tm, d), BF16), ((d, tn), F32), ((tm, tn), BF16)]
    if mode in ("rope", "normrope"):
        cos, sin, gq, gk = rope
        tab = pl.BlockSpec((tm, HEAD_DIM), lambda j, i: (i, 0))
        vec = pl.BlockSpec((1, HEAD_DIM), lambda j, i: (0, 0))
        in_specs += [tab, tab, vec, vec]
        args += [cos, sin, gq, gk]
        blocks += [((tm, HEAD_DIM), F32)] * 2
    kern = functools.partial(_inproj_kernel, mode=mode, n_q_tiles=n_q_tiles,
                             q_scale=HEAD_DIM ** -0.5 * LOG2E)
    return pl.pallas_call(
        kern,
        out_shape=jax.ShapeDtypeStruct((t, width), BF16),
        grid=(width // tn, t // tm),
        in_specs=in_specs,
        out_specs=pl.BlockSpec((tm, tn), lambda j, i: (i, j)),
        scratch_shapes=[pltpu.VMEM((d, tn), BF16)],
        compiler_params=_params(("arbitrary", "arbitrary"), blocks, scratch=[((d, tn), BF16)],
                                temps=[((tm, tn), F32)] * 2),
        name="inproj_" + mode,
    )(*args)


def _window_bias(group, n_ctx):
    r = jnp.arange(group * BLOCK)[:, None] % BLOCK
    c = jnp.arange(3 * BLOCK + n_ctx)[None, :]
    in_prev, in_cur = c < BLOCK, (c >= BLOCK) & (c < 2 * BLOCK)
    in_next = (c >= 2 * BLOCK) & (c < 3 * BLOCK)
    band_prev = in_prev & (c < r)
    band_next = in_next & (c - 2 * BLOCK > r)
    hidden = [band_prev | band_next, in_prev | band_next, band_prev | in_next, in_prev | in_next,
              in_prev | in_cur | in_next]
    shape = (group * BLOCK, 3 * BLOCK + n_ctx)
    return jnp.stack([jnp.where(jnp.broadcast_to(h, shape), -jnp.inf, 0.0).astype(F32)
                      for h in hidden])


def _window_attn_kernel(sink_ref, bias_ref, q_ref, kp_ref, kc_ref, kn_ref, kx_ref, vp_ref, vc_ref,
                        vn_ref, vx_ref, o_ref, *, group):
    rows = group * BLOCK
    assert BLOCK & (BLOCK - 1) == 0
    shift = BLOCK.bit_length() - 1
    bias = bias_ref[...]
    rid = lax.broadcasted_iota(jnp.int32, (rows, 1), 0) >> shift
    for hk in range(A_KV_HEADS):
        ksl = slice(hk * HEAD_DIM, (hk + 1) * HEAD_DIM)
        k_all = jnp.concatenate([kp_ref[:, ksl], kc_ref[:, ksl], kn_ref[:, ksl], kx_ref[:, ksl]],
                                axis=0)
        v_all = jnp.concatenate([vp_ref[:, ksl], vc_ref[:, ksl], vn_ref[:, ksl], vx_ref[:, ksl]],
                                axis=0)
        q3 = jnp.concatenate(
            [q_ref[:, (hk * group + g) * HEAD_DIM:(hk * group + g + 1) * HEAD_DIM]
             for g in range(group)], axis=0)
        sink = jnp.zeros((rows, 1), F32)
        for g in range(group):
            sink = jnp.where(rid == g, sink_ref[hk * group + g] * LOG2E, sink)
        s = _dot_nt(q3, k_all) + bias
        m = jnp.maximum(jnp.max(s, axis=-1, keepdims=True), sink)
        p = jnp.exp2(s - m)
        denom = jnp.sum(p, axis=-1, keepdims=True) + jnp.exp2(sink - m)
        o = _dot(p.astype(BF16), v_all) / denom
        for g in range(group):
            osl = slice((hk * group + g) * HEAD_DIM, (hk * group + g + 1) * HEAD_DIM)
            o_ref[:, osl] = o[g * BLOCK:(g + 1) * BLOCK, :].astype(BF16)


def _window_attn(za, av, sink, n_batch, seq, n_ctx, with_ctx):
    n_heads = sink.shape[0]
    group = n_heads // A_KV_HEADS
    qw = n_heads * HEAD_DIM
    kw = A_KV_HEADS * HEAD_DIM
    assert qw % kw == 0 and seq % BLOCK == 0 and (n_batch * seq) % n_ctx == 0 and WINDOW == BLOCK
    assert n_ctx % BLOCK == 0
    nb = seq // BLOCK
    ncb = n_ctx // BLOCK if with_ctx else 0
    kcol = qw // kw
    ctx0 = (n_batch * seq) // n_ctx

    def qblk(b, n):
        return (jnp.where(n < nb, b * nb + n, n_batch * nb + b * ncb + n - nb), 0)

    def blk(shift):
        return lambda b, n: (b * nb + jnp.clip(n + shift, 0, nb - 1), kcol)

    def vblk(shift):
        return lambda b, n: (b * nb + jnp.clip(n + shift, 0, nb - 1), 0)

    def bias_variant(b, n):
        lat = jnp.where(n == 0, 1, 0) + jnp.where(n == nb - 1, 2, 0)
        return (jnp.where(n < nb, lat, 4), 0, 0)

    kspec = [pl.BlockSpec((BLOCK, kw), blk(s)) for s in (-1, 0, 1)]
    vspec = [pl.BlockSpec((BLOCK, kw), vblk(s)) for s in (-1, 0, 1)]
    n_keys = 3 * BLOCK + n_ctx
    rows = group * BLOCK
    blocks = ([((BLOCK, qw), BF16)] * 2 + [((BLOCK, kw), BF16)] * 6 + [((n_ctx, kw), BF16)] * 2
              + [((rows, n_keys), F32)])
    return pl.pallas_call(
        functools.partial(_window_attn_kernel, group=group),
        out_shape=jax.ShapeDtypeStruct((n_batch * (nb + ncb) * BLOCK, qw), BF16),
        grid=(n_batch, nb + ncb),
        in_specs=[pl.BlockSpec(memory_space=pltpu.SMEM),
                  pl.BlockSpec((None, rows, n_keys), bias_variant),
                  pl.BlockSpec((BLOCK, qw), qblk)]
                 + kspec + [pl.BlockSpec((n_ctx, kw), lambda b, n: (ctx0 + b, kcol))]
                 + vspec + [pl.BlockSpec((n_ctx, kw), lambda b, n: (ctx0 + b, 0))],
        out_specs=pl.BlockSpec((BLOCK, qw), qblk),
        compiler_params=_params(("arbitrary", "arbitrary"), blocks,
                                temps=[((rows, n_keys), F32)] * 4),
        name="window_attn",
    )(sink, _window_bias(group, n_ctx), za, za, za, za, za, av, av, av, av)


def _gmlp_kernel(z_ref, g_ref, b_ref, ws_ref, bst_ref, o_ref):
    width = g_ref.shape[1]
    u = z_ref[:, :width].astype(F32)
    v = z_ref[:, width:].astype(F32)
    mu = jnp.mean(v, axis=-1, keepdims=True)
    vc = v - mu
    var = jnp.mean(vc * vc, axis=-1, keepdims=True)
    vn = (vc * lax.rsqrt(var + EPS) * g_ref[...] + b_ref[...]).astype(BF16)
    gd = width // B_GROUPS
    for g in range(B_GROUPS):
        sl = slice(g * gd, (g + 1) * gd)
        mixed = _dot(ws_ref[g].astype(BF16), vn[:, sl]) + bst_ref[:, g:g + 1]
        o_ref[:, sl] = (u[:, sl] * mixed).astype(BF16)


def _gmlp(zb, ln_g, ln_b, ws, bs, n_rows):
    w2 = zb.shape[1]
    width = w2 // 2
    blocks = [((CHUNK, w2), BF16), ((1, width), F32), ((1, width), F32),
              (ws.shape, F32), ((CHUNK, B_GROUPS), F32), ((CHUNK, width), BF16)]
    return pl.pallas_call(
        _gmlp_kernel,
        out_shape=jax.ShapeDtypeStruct((n_rows, width), BF16),
        grid=(n_rows // CHUNK,),
        in_specs=[pl.BlockSpec((CHUNK, w2), lambda i: (i, 0)),
                  pl.BlockSpec((1, width), lambda i: (0, 0)),
                  pl.BlockSpec((1, width), lambda i: (0, 0)),
                  pl.BlockSpec(ws.shape, lambda i: (0, 0, 0)),
                  pl.BlockSpec((CHUNK, B_GROUPS), lambda i: (0, 0))],
        out_specs=pl.BlockSpec((CHUNK, width), lambda i: (i, 0)),
        compiler_params=_params(("arbitrary",), blocks, temps=[((CHUNK, w2), F32)] * 2),
        name="chunk_gmlp",
    )(zb, ln_g.reshape(1, width), ln_b.reshape(1, width), ws, bs.T)


def _global_attn_kernel(q_ref, kx_ref, vx_ref, k_ref, v_ref, o_ref, vext_ref, *, group,
                        n_lat_tiles, has_ctx_tiles):
    tq = q_ref.shape[0]
    n_ctx = kx_ref.shape[0]
    n_lat = k_ref.shape[0]
    rows = group * tq

    @pl.when(pl.program_id(2) == 0)
    def _():
        vext_ref[0:n_ctx, 0:HEAD_DIM] = vx_ref[...]
        vext_ref[n_ctx:, 0:HEAD_DIM] = v_ref[...]
        vext_ref[:, HEAD_DIM:] = jnp.ones((n_ctx + n_lat, HEAD_DIM), BF16)

    def attend(chunks):
        q3 = jnp.concatenate([q_ref[:, g * HEAD_DIM:(g + 1) * HEAD_DIM] for g in range(group)],
                             axis=0)
        m = jnp.full((rows, HEAD_DIM), -jnp.inf, F32)
        acc = jnp.zeros((rows, 2 * HEAD_DIM), F32)
        for k_chunk, v0, nk in chunks:
            s = _dot_nt(q3, k_chunk())
            m_new = jnp.maximum(m, jnp.broadcast_to(jnp.max(s, axis=-1, keepdims=True), m.shape))
            alpha = jnp.exp2(m - m_new)
            p = jnp.exp2(s - jnp.tile(m_new, (1, nk // HEAD_DIM)))
            acc = jnp.tile(alpha, (1, 2)) * acc + _dot(p.astype(BF16), vext_ref[v0:v0 + nk, :])
            m = m_new
        o = acc[:, :HEAD_DIM] / acc[:, HEAD_DIM:]
        for g in range(group):
            o_ref[:, g * HEAD_DIM:(g + 1) * HEAD_DIM] = o[g * tq:(g + 1) * tq, :].astype(BF16)

    ctx_chunk = [(lambda: kx_ref[...], 0, n_ctx)]
    lat_chunks = [(lambda c=c: k_ref[c:c + K_TILE, :], n_ctx + c, K_TILE)
                  for c in range(0, n_lat, K_TILE)]
    is_lat = pl.program_id(2) < n_lat_tiles
    pl.when(is_lat)(lambda: attend(ctx_chunk + lat_chunks))
    if has_ctx_tiles:
        pl.when(jnp.logical_not(is_lat))(lambda: attend(ctx_chunk))


def _global_attn(zc, cv, n_batch, seq, n_ctx, n_heads, with_ctx):
    group = n_heads // C_KV_HEADS
    gw = group * HEAD_DIM
    tq = Q_TILE
    assert seq % tq == 0 and seq % K_TILE == 0 and (n_batch * seq) % n_ctx == 0 and n_ctx % tq == 0
    nq = seq // tq
    ncq = n_ctx // tq if with_ctx else 0
    ctx0 = (n_batch * seq) // n_ctx
    rows = group * tq

    def qblk(b, hk, i):
        return (jnp.where(i < nq, b * nq + i, n_batch * nq + b * ncq + i - nq), hk)

    blocks = [((tq, gw), BF16)] * 2 + [((n_ctx, HEAD_DIM), BF16)] * 2 + [((seq, HEAD_DIM), BF16)] * 2
    scratch = [((n_ctx + seq, 2 * HEAD_DIM), BF16)]
    return pl.pallas_call(
        functools.partial(_global_attn_kernel, group=group, n_lat_tiles=nq, has_ctx_tiles=ncq > 0),
        out_shape=jax.ShapeDtypeStruct((n_batch * (nq + ncq) * tq, n_heads * HEAD_DIM), BF16),
        grid=(n_batch, C_KV_HEADS, nq + ncq),
        in_specs=[pl.BlockSpec((tq, gw), qblk),
                  pl.BlockSpec((n_ctx, HEAD_DIM), lambda b, hk, i: (ctx0 + b, n_heads + hk)),
                  pl.BlockSpec((n_ctx, HEAD_DIM), lambda b, hk, i: (ctx0 + b, hk)),
                  pl.BlockSpec((seq, HEAD_DIM), lambda b, hk, i: (b, n_heads + hk)),
                  pl.BlockSpec((seq, HEAD_DIM), lambda b, hk, i: (b, hk))],
        out_specs=pl.BlockSpec((tq, gw), qblk),
        scratch_shapes=[pltpu.VMEM(s, d) for s, d in scratch],
        compiler_params=_params(("arbitrary", "arbitrary", "arbitrary"), blocks, scratch=scratch,
                                temps=[((rows, K_TILE), F32)] * 6),
        name="global_attn",
    )(zc, zc, cv, zc, cv)


def _outproj_kernel(a_ref, b_ref, c_ref, w_ref, o_ref, wbf_ref):
    @pl.when(pl.program_id(1) == 0)
    def _():
        _cast_weight(w_ref, wbf_ref)

    ka, kb = a_ref.shape[1], b_ref.shape[1]
    acc = _dot(a_ref[...], wbf_ref[0:ka, :])
    acc += _dot(b_ref[...], wbf_ref[ka:ka + kb, :])
    acc += _dot(c_ref[...], wbf_ref[ka + kb:, :])
    o_ref[...] = acc.astype(BF16)


def _outproj(oa, ob, oc, w_out, layer, n_rows):
    t = oa.shape[0]
    _, k, d = w_out.shape
    tm, tn = _row_tile(n_rows), COL_TILE
    assert oa.shape[1] + ob.shape[1] + oc.shape[1] == k and d % tn == 0
    blocks = [((tm, k), BF16), ((k, tn), F32), ((tm, tn), BF16)]
    return pl.pallas_call(
        _outproj_kernel,
        out_shape=jax.ShapeDtypeStruct((t, d), BF16),
        grid=(d // tn, n_rows // tm),
        in_specs=[pl.BlockSpec((tm, oa.shape[1]), lambda j, i: (i, 0)),
                  pl.BlockSpec((tm, ob.shape[1]), lambda j, i: (i, 0)),
                  pl.BlockSpec((tm, oc.shape[1]), lambda j, i: (i, 0)),
                  pl.BlockSpec((None, k, tn), lambda j, i: (layer, 0, j))],
        out_specs=pl.BlockSpec((tm, tn), lambda j, i: (i, j)),
        scratch_shapes=[pltpu.VMEM((k, tn), BF16)],
        compiler_params=_params(("arbitrary", "arbitrary"), blocks, scratch=[((k, tn), BF16)],
                                temps=[((tm, tn), F32)] * 2),
        name="outproj",
    )(oa, ob, oc, w_out)


def _ffn_up_kernel(h_ref, wg_ref, wu_ref, o_ref, wgbf_ref, wubf_ref):
    @pl.when(pl.program_id(1) == 0)
    def _():
        _cast_weight(wg_ref, wgbf_ref)
        _cast_weight(wu_ref, wubf_ref)

    h = h_ref[...]
    gate = _dot(h, wgbf_ref[...])
    up = _dot(h, wubf_ref[...])
    o_ref[...] = (jax.nn.silu(gate) * up).astype(BF16)


def _ffn_up(h2, w_gate, w_up, sel, n_rows):
    t, d = h2.shape
    f = w_gate.shape[-1]
    tm, tn = _row_tile(n_rows), FFN_COL_TILE
    assert f % tn == 0
    lead = (None,) * len(sel)
    wspec = pl.BlockSpec(lead + (d, tn), lambda j, i: sel + (0, j))
    blocks = [((tm, d), BF16), ((d, tn), F32), ((d, tn), F32), ((tm, tn), BF16)]
    return pl.pallas_call(
        _ffn_up_kernel,
        out_shape=jax.ShapeDtypeStruct((t, f), BF16),
        grid=(f // tn, n_rows // tm),
        in_specs=[pl.BlockSpec((tm, d), lambda j, i: (i, 0)), wspec, wspec],
        out_specs=pl.BlockSpec((tm, tn), lambda j, i: (i, j)),
        scratch_shapes=[pltpu.VMEM((d, tn), BF16)] * 2,
        compiler_params=_params(("arbitrary", "arbitrary"), blocks, scratch=[((d, tn), BF16)] * 2,
                                temps=[((tm, tn), F32)] * 3),
        name="ffn_up",
    )(h2, w_gate, w_up)


def _ffn_down_kernel(*refs, has_prev):
    if has_prev:
        a_ref, w_ref, prev_ref, o_ref, wbf_ref = refs
    else:
        a_ref, w_ref, o_ref, wbf_ref = refs

    @pl.when(pl.program_id(1) == 0)
    def _():
        _cast_weight(w_ref, wbf_ref)

    acc = _dot(a_ref[...], wbf_ref[...])
    if has_prev:
        acc = prev_ref[...] + acc
    o_ref[...] = acc


def _ffn_down(a, w_down, sel, k_start, k_size, n_rows, prev=None):
    t = a.shape[0]
    d = w_down.shape[-1]
    tm, tn = ROW_TILE, COL_TILE
    assert k_start % k_size == 0 and d % tn == 0 and n_rows % tm == 0
    kb = k_start // k_size
    lead = (None,) * len(sel)
    in_specs = [pl.BlockSpec((tm, k_size), lambda j, i: (i, kb)),
                pl.BlockSpec(lead + (k_size, tn), lambda j, i: sel + (kb, j))]
    args = [a, w_down]
    blocks = [((tm, k_size), BF16), ((k_size, tn), F32), ((tm, tn), F32)]
    aliases = {}
    if prev is not None:
        in_specs.append(pl.BlockSpec((tm, tn), lambda j, i: (i, j)))
        aliases = {len(args): 0}
        args.append(prev)
        blocks.append(((tm, tn), F32))
    return pl.pallas_call(
        functools.partial(_ffn_down_kernel, has_prev=prev is not None),
        out_shape=jax.ShapeDtypeStruct((t, d), F32),
        grid=(d // tn, n_rows // tm),
        in_specs=in_specs,
        out_specs=pl.BlockSpec((tm, tn), lambda j, i: (i, j)),
        scratch_shapes=[pltpu.VMEM((k_size, tn), BF16)],
        input_output_aliases=aliases,
        compiler_params=_params(("arbitrary", "arbitrary"), blocks, scratch=[((k_size, tn), BF16)],
                                temps=[((tm, tn), F32)] * 2),
        name="ffn_down",
    )(*args)


def _router_kernel(h_ref, w_ref, o_ref, *, n_experts):
    pieces = _dot(h_ref[...], w_ref[...])
    logits = pieces
    for k in range(1, ROUTER_PIECES):
        logits = logits + pltpu.roll(pieces, V7X_LANES - k * n_experts, axis=1)
    lane = lax.broadcasted_iota(jnp.int32, logits.shape, 1).astype(F32)
    neg = -jnp.inf
    logits = jnp.where(lane < n_experts, logits, neg)
    picked = []
    remaining = logits
    for _ in range(TOP_K):
        top = jnp.max(remaining, axis=-1, keepdims=True)
        idx = jnp.min(jnp.where(remaining == top, lane, float(logits.shape[1])), axis=-1,
                      keepdims=True)
        picked.append((top, idx))
        remaining = jnp.where(lane == idx, neg, remaining)
    top0 = picked[0][0]
    denom = sum(jnp.exp(tv - top0) for tv, _ in picked)
    route = jnp.zeros(logits.shape, F32)
    for k, (tv, idx) in enumerate(picked):
        route = jnp.where(lane == k, idx, route)
        route = jnp.where(lane == TOP_K + k, jnp.exp(tv - top0) / denom, route)
    o_ref[...] = route


def _router(h2, w_router, n_rows):
    t, d = h2.shape
    n_experts = w_router.shape[1]
    assert ROUTER_PIECES * n_experts <= V7X_LANES
    pieces, rest = [], w_router
    for _ in range(ROUTER_PIECES):
        pieces.append(rest.astype(BF16))
        rest = rest - pieces[-1].astype(F32)
    wpad = jnp.pad(jnp.concatenate(pieces, axis=1),
                   ((0, 0), (0, V7X_LANES - ROUTER_PIECES * n_experts)))
    tr = EW_ROWS
    blocks = [((tr, d), BF16), ((d, V7X_LANES), BF16), ((tr, V7X_LANES), F32)]
    return pl.pallas_call(
        functools.partial(_router_kernel, n_experts=n_experts),
        out_shape=jax.ShapeDtypeStruct((t, V7X_LANES), F32),
        grid=(n_rows // tr,),
        in_specs=[pl.BlockSpec((tr, d), lambda i: (i, 0)),
                  pl.BlockSpec((d, V7X_LANES), lambda i: (0, 0))],
        out_specs=pl.BlockSpec((tr, V7X_LANES), lambda i: (i, 0)),
        compiler_params=_params(("arbitrary",), blocks, temps=[((tr, d), F32)] * 3),
        name="moe_router",
    )(h2, wpad)


def _route_plan(route, n_experts):
    tile = ROW_TILE
    n_tok = route.shape[0]
    n_asg = n_tok * TOP_K
    e_flat = route[:, :TOP_K].astype(jnp.int32).reshape(n_asg)
    onehot = (e_flat[:, None] == jnp.arange(n_experts, dtype=jnp.int32)[None, :]).astype(jnp.int32)
    csum = jnp.cumsum(onehot, axis=0)
    rank = jnp.take_along_axis(csum, e_flat[:, None], axis=1)[:, 0] - 1
    counts = csum[-1]
    tiles_per = (counts + tile - 1) // tile
    tile_end = jnp.cumsum(tiles_per)
    tile_start = tile_end - tiles_per
    dest = tile_start[e_flat] * tile + rank

    n_tiles = (n_asg + n_experts * (tile - 1)) // tile
    tile_ids = jnp.arange(n_tiles, dtype=jnp.int32)
    n_used = tile_end[-1]
    valid = tile_ids < n_used
    expert_raw = jnp.sum((tile_end[None, :] <= tile_ids[:, None]).astype(jnp.int32), axis=1)
    tile_expert = jnp.where(valid, expert_raw, expert_raw[n_used - 1])
    tile_first = valid & (tile_ids == tile_start[tile_expert])
    tile_row = jnp.where(valid, tile_ids, n_used - 1)

    tok_of_row = jnp.zeros((n_tiles * tile,), jnp.int32).at[dest].set(
        jnp.arange(n_asg, dtype=jnp.int32) // TOP_K, unique_indices=True)
    tile_first = tile_first.astype(jnp.int32)
    tile_slot = (jnp.cumsum(tile_first) - 1) % 2
    tiles = (tile_expert, tile_first, valid.astype(jnp.int32), tile_row, tile_slot)
    return tok_of_row, dest, tiles


def _row_copy(src_hbm, dst_vmem, sem, src_row, dst_row):
    return pltpu.make_async_copy(src_hbm.at[pl.ds(src_row, 1)], dst_vmem.at[pl.ds(dst_row, 1)], sem)


def _gather_rows_kernel(idx_ref, src_ref, o_ref, buf_ref, sem):
    tile = buf_ref.shape[0]
    base = pl.program_id(0) * tile

    def issue(g, carry):
        for u in range(GATHER_UNROLL):
            r = g * GATHER_UNROLL + u
            _row_copy(src_ref, buf_ref, sem, idx_ref[base + r], r).start(priority=u % 2)
        return carry

    def drain(r, carry):
        _row_copy(src_ref, buf_ref, sem, 0, r).wait()
        return carry

    lax.fori_loop(0, tile // GATHER_UNROLL, issue, 0)
    lax.fori_loop(0, tile, drain, 0, unroll=GATHER_UNROLL)
    o_ref[...] = _from_slabs(buf_ref).astype(BF16)


def _gather_rows(src, idx):
    slab = src.shape[1:]
    d = slab[0] * slab[1]
    n = idx.shape[0]
    tg = EW_ROWS
    assert n % tg == 0
    blocks = [((tg, d), BF16)]
    return pl.pallas_call(
        _gather_rows_kernel,
        out_shape=jax.ShapeDtypeStruct((n, d), BF16),
        grid_spec=pltpu.PrefetchScalarGridSpec(
            num_scalar_prefetch=1,
            grid=(n // tg,),
            in_specs=[pl.BlockSpec(memory_space=pl.ANY)],
            out_specs=pl.BlockSpec((tg, d), lambda i, idx_ref: (i, 0)),
            scratch_shapes=[pltpu.VMEM((tg,) + slab, F32), pltpu.SemaphoreType.DMA]),
        compiler_params=_params(("arbitrary",), blocks, scratch=[((tg, d), F32)],
                                temps=[((tg, d), F32)]),
        name="moe_dispatch",
    )(idx, src)


def _held_tile(s, n_tiles):
    return jnp.minimum(s, n_tiles - 1)


def _done_tile(s):
    return jnp.maximum(s - 1, 0)


def _grouped_step(tables, w_refs, wbf_refs, compute, o_ref):
    _, tf_ref, tv_ref, _, ts_ref = tables
    n_tiles = tf_ref.shape[0]
    s = pl.program_id(1)
    held = _held_tile(s, n_tiles)
    done = _done_tile(s)

    @pl.when((s < n_tiles) & (tf_ref[held] == 1))
    def _():
        for w_ref, wbf_ref in zip(w_refs, wbf_refs):
            _cast_weight(w_ref, wbf_ref.at[ts_ref[held]])

    @pl.when((s > 0) & (tv_ref[done] == 1))
    def _():
        o_ref[...] = compute([wbf_ref[ts_ref[done]] for wbf_ref in wbf_refs])

    @pl.when((s > 0) & (tv_ref[done] == 0))
    def _():
        o_ref[...] = jnp.zeros(o_ref.shape, o_ref.dtype)


def _moe_up_kernel(te_ref, tf_ref, tv_ref, tr_ref, ts_ref, x_ref, wg_ref, wu_ref, o_ref, wgbf_ref,
                   wubf_ref):
    def compute(w):
        x = x_ref[...]
        return (jax.nn.silu(_dot(x, w[0])) * _dot(x, w[1])).astype(BF16)

    _grouped_step((te_ref, tf_ref, tv_ref, tr_ref, ts_ref), (wg_ref, wu_ref), (wgbf_ref, wubf_ref),
                  compute, o_ref)


def _moe_up(xs, w_gate, w_up, layer, tiles):
    r, d = xs.shape
    f = w_gate.shape[-1]
    tm, tn = ROW_TILE, FFN_COL_TILE
    assert f % tn == 0 and r % tm == 0
    nt = r // tm
    wspec = pl.BlockSpec((None, None, d, tn),
                         lambda j, s, te, tf, tv, tr, ts: (layer, te[_held_tile(s, nt)], 0, j))
    blocks = [((tm, d), BF16), ((d, tn), F32), ((d, tn), F32), ((tm, tn), BF16)]
    return pl.pallas_call(
        _moe_up_kernel,
        out_shape=jax.ShapeDtypeStruct((r, f), BF16),
        grid_spec=pltpu.PrefetchScalarGridSpec(
            num_scalar_prefetch=5,
            grid=(f // tn, nt + 1),
            in_specs=[pl.BlockSpec((tm, d), lambda j, s, te, tf, tv, tr, ts: (tr[_done_tile(s)], 0)),
                      wspec, wspec],
            out_specs=pl.BlockSpec((tm, tn), lambda j, s, te, tf, tv, tr, ts: (_done_tile(s), j)),
            scratch_shapes=[pltpu.VMEM((2, d, tn), BF16)] * 2),
        compiler_params=_params(("arbitrary", "arbitrary"), blocks,
                                scratch=[((2, d, tn), BF16)] * 2, temps=[((tm, tn), F32)] * 3),
        name="moe_up",
    )(*tiles, xs, w_gate, w_up)


def _moe_down_kernel(te_ref, tf_ref, tv_ref, tr_ref, ts_ref, a_ref, w_ref, o_ref, wbf_ref):
    _grouped_step((te_ref, tf_ref, tv_ref, tr_ref, ts_ref), (w_ref,), (wbf_ref,),
                  lambda w: _dot(a_ref[...], w[0]), o_ref)


def _moe_down(hmid, w_down, layer, tiles):
    r, f = hmid.shape
    d = w_down.shape[-1]
    tm, tn = ROW_TILE, COL_TILE
    assert d % tn == 0 and r % tm == 0
    nt = r // tm
    blocks = [((tm, f), BF16), ((f, tn), F32), ((tm, tn), F32)]
    return pl.pallas_call(
        _moe_down_kernel,
        out_shape=jax.ShapeDtypeStruct((r, d), F32),
        grid_spec=pltpu.PrefetchScalarGridSpec(
            num_scalar_prefetch=5,
            grid=(d // tn, nt + 1),
            in_specs=[pl.BlockSpec((tm, f), lambda j, s, te, tf, tv, tr, ts: (tr[_done_tile(s)], 0)),
                      pl.BlockSpec((None, None, f, tn),
                                   lambda j, s, te, tf, tv, tr, ts:
                                   (layer, te[_held_tile(s, nt)], 0, j))],
            out_specs=pl.BlockSpec((tm, tn), lambda j, s, te, tf, tv, tr, ts: (_done_tile(s), j)),
            scratch_shapes=[pltpu.VMEM((2, f, tn), BF16)]),
        compiler_params=_params(("arbitrary", "arbitrary"), blocks, scratch=[((2, f, tn), BF16)],
                                temps=[((tm, tn), F32)] * 2),
        name="moe_down",
    )(*tiles, hmid, w_down)


def _moe_final_kernel(dest_ref, x_ref, y_ref, route_ref, g_ref, mod_ref, o_ref, buf_ref, sems):
    tile = x_ref.shape[0]
    base = pl.program_id(0) * tile

    def issue(g, carry):
        for u in range(GATHER_UNROLL):
            r = g * GATHER_UNROLL + u
            for k in range(TOP_K):
                _row_copy(y_ref, buf_ref.at[k], sems.at[k], dest_ref[(base + r) * TOP_K + k],
                          r).start(priority=k % 2)
        return carry

    def drain(r, carry):
        for k in range(TOP_K):
            _row_copy(y_ref, buf_ref.at[k], sems.at[k], 0, r).wait()
        return carry

    lax.fori_loop(0, tile // GATHER_UNROLL, issue, 0)
    lax.fori_loop(0, tile, drain, 0, unroll=GATHER_UNROLL)
    f = route_ref[:, TOP_K:TOP_K + 1] * buf_ref[0]
    for k in range(1, TOP_K):
        f = f + route_ref[:, TOP_K + k:TOP_K + k + 1] * buf_ref[k]
    o_ref[...] = x_ref[...] + mod_ref[5:6, :] * _rms(f, g_ref[...])


def _moe_final(x1, y, dest, route, g_post, mod, n_rows, seq, n_lat_rows, n_batch):
    d = x1.shape[1]
    tr = EW_ROWS
    ridx = functools.partial(_mod_row_index, rows_per_tile=tr, seq=seq, n_lat_rows=n_lat_rows,
                             n_batch=n_batch)
    row = pl.BlockSpec((tr, d), lambda i, dest_ref: (i, 0))
    blocks = [((tr, d), F32)] * 2 + [((N_MOD, d), F32)]
    return pl.pallas_call(
        _moe_final_kernel,
        out_shape=jax.ShapeDtypeStruct((n_rows, d), F32),
        grid_spec=pltpu.PrefetchScalarGridSpec(
            num_scalar_prefetch=1,
            grid=(n_rows // tr,),
            in_specs=[row, pl.BlockSpec(memory_space=pl.ANY),
                      pl.BlockSpec((tr, V7X_LANES), lambda i, dest_ref: (i, 0)),
                      pl.BlockSpec((1, d), lambda i, dest_ref: (0, 0)),
                      pl.BlockSpec((None, N_MOD, d), lambda i, dest_ref: (ridx(i), 0, 0))],
            out_specs=row,
            scratch_shapes=[pltpu.VMEM((TOP_K, tr, d), F32), pltpu.SemaphoreType.DMA((TOP_K,))]),
        compiler_params=_params(("arbitrary",), blocks, scratch=[((TOP_K, tr, d), F32)],
                                temps=[((tr, d), F32)] * 2),
        name="moe_combine_final",
    )(dest, x1, y, route, g_post.reshape(1, d), mod)


def _rope_tables(seq, n_batch, ctx_rows):
    n = jnp.arange(seq)
    pos_r = (n // GRID_W).astype(F32)
    pos_w = (n % GRID_W).astype(F32)
    n_freq = HEAD_DIM // 4
    inv_freq = ROPE_THETA ** (-jnp.arange(n_freq, dtype=F32) / n_freq)
    ar = pos_r[:, None] * inv_freq
    aw = pos_w[:, None] * inv_freq
    cos = jnp.concatenate([jnp.cos(ar), jnp.cos(ar), jnp.cos(aw), jnp.cos(aw)], axis=-1)
    sin = jnp.concatenate([-jnp.sin(ar), jnp.sin(ar), -jnp.sin(aw), jnp.sin(aw)], axis=-1)
    cos = jnp.concatenate([cos] * n_batch + [jnp.ones((ctx_rows, HEAD_DIM), F32)], axis=0)
    sin = jnp.concatenate([sin] * n_batch + [jnp.zeros((ctx_rows, HEAD_DIM), F32)], axis=0)
    return cos, sin


def kernel(x, c, ctx, c_ctx, w_mod, b_mod, g_pre_mix, g_post_mix, g_pre_ffn, g_post_ffn, w_in, w_out,
           sink_a, qn_c, kn_c, gm_ln_g, gm_ln_b, gm_ws, gm_bs, ffn_w_gate, ffn_w_up, ffn_w_down,
           moe_router, moe_w_gate, moe_w_up, moe_w_down):
    n_batch, seq, d = x.shape
    n_ctx = ctx.shape[1]
    depth = w_mod.shape[0]
    n_lat = n_batch * seq
    n_ctx_rows = n_batch * n_ctx
    t = n_lat + n_ctx_rows
    a_heads = sink_a.shape[1]
    a_w = a_heads * HEAD_DIM
    akv_w = A_KV_HEADS * HEAD_DIM
    b_w = gm_ln_g.shape[1]
    ckv_w = C_KV_HEADS * HEAD_DIM
    c_w = w_in.shape[2] - a_w - 2 * akv_w - 2 * b_w - 2 * ckv_w
    c_heads = c_w // HEAD_DIM
    assert n_ctx_rows % EW_ROWS == 0 and seq % EW_ROWS == 0 and n_batch + 1 <= 8
    assert a_w == c_w and akv_w == COL_TILE and ckv_w == COL_TILE

    x_parts = (x.reshape(n_lat, d), ctx.reshape(n_ctx_rows, d))
    c8 = jnp.concatenate([c, c_ctx[None, :], jnp.zeros((8 - n_batch - 1, d), F32)], axis=0)
    mod_all = _modulation(c8, w_mod, b_mod).reshape(depth, 8, N_MOD, d)

    cos, sin = _rope_tables(seq, n_batch, n_ctx_rows)
    geo = dict(seq=seq, n_lat_rows=n_lat, n_batch=n_batch)
    h = _prenorm(x_parts, g_pre_mix[0], mod_all[0], **geo)

    for l in range(depth):
        need_ctx = l < depth - 1
        n_rows = t if need_ctx else n_lat
        mod = mod_all[l]

        gq, gk = qn_c[l].reshape(1, HEAD_DIM), kn_c[l].reshape(1, HEAD_DIM)
        rope = (cos, sin, gq, gk)
        col = 0
        za = _inproj(h, w_in, l, col, a_w + akv_w, "rope", rope, n_q_tiles=a_w // COL_TILE)
        col += a_w + akv_w
        av = _inproj(h, w_in, l, col, akv_w, "plain")
        col += akv_w
        zb = _inproj(h, w_in, l, col, 2 * b_w, "gelu")
        col += 2 * b_w
        zc = _inproj(h, w_in, l, col, c_w + ckv_w, "normrope", rope, n_q_tiles=c_w // COL_TILE)
        col += c_w + ckv_w
        cv = _inproj(h, w_in, l, col, ckv_w, "plain")

        oa = _window_attn(za, av, sink_a[l], n_batch, seq, n_ctx, need_ctx)
        ob = _gmlp(zb, gm_ln_g[l], gm_ln_b[l], gm_ws[l], gm_bs[l], n_rows)
        oc = _global_attn(zc, cv, n_batch, seq, n_ctx, c_heads, need_ctx)

        y = _outproj(oa, ob, oc, w_out, l, n_rows)
        i = l // 2
        next_norm = (g_pre_mix[l + 1], mod_all[l + 1]) if need_ctx else None
        if l % 2 == 0:
            x1, h2 = _postmix(x_parts, y, g_post_mix[l], g_pre_ffn[l], mod, n_rows, 0, **geo)
            f_dim = ffn_w_gate.shape[-1]
            hmid = _ffn_up(h2, ffn_w_gate, ffn_w_up, (i,), n_rows)
            half = f_dim // 2
            f = _ffn_down(hmid, ffn_w_down, (i,), 0, half, n_rows)
            f = _ffn_down(hmid, ffn_w_down, (i,), half, half, n_rows, prev=f)
            xall = _final(x1, f, g_post_ffn[l], mod, n_rows, next_norm, **geo)
            if need_ctx:
                xall, h = xall
        else:
            x1, h2, h2f = _postmix(x_parts, y, g_post_mix[l], g_pre_ffn[l], mod, n_rows, 1, **geo)
            route = _router(h2, moe_router[i], n_rows)
            tok_of_row, dest, tiles = _route_plan(route, moe_router.shape[-1])
            xs = _gather_rows(h2f, tok_of_row)
            hmid = _moe_up(xs, moe_w_gate, moe_w_up, i, tiles)
            ys = _moe_down(hmid, moe_w_down, i, tiles)
            xall = _moe_final(x1, ys, dest, route, g_post_ffn[l], mod, n_rows, **geo)
            if need_ctx:
                h = _prenorm((xall,), next_norm[0], next_norm[1], **geo)
        x_parts = (xall,)
    return xall[:n_lat].reshape(n_batch, seq, d)
```

```python
import functools
import math

import jax
import jax.numpy as jnp
from jax import lax
from jax.experimental import pallas as pl
from jax.experimental.pallas import tpu as pltpu

F32 = jnp.float32
BF16 = jnp.bfloat16

GRID_W = 64
HEAD_DIM = 128
BLOCK = 128
WINDOW = 128
A_KV_HEADS = 4
C_KV_HEADS = 4
B_GROUPS = 8
CHUNK = 128
N_MOD = 6
TOP_K = 2
ROPE_THETA = 10000.0
EPS = 1e-6
LOG2E = math.log2(math.e)

V7X_LANES = 128
V7X_VMEM_SCOPED_CAP = 60000 * 1024

ROW_TILE = 512
MAX_ROW_TILE = 1088
ROW_ALIGN = 16
COL_TILE = 512
FFN_COL_TILE = 256
EW_ROWS = 256
CAST_ROWS = 256
Q_TILE = 256
K_TILE = 512
GATHER_UNROLL = 8
GATHER_SPLIT = 2
MOE_ROW_STEP = 128
ROUTER_PIECES = 3
EPILOGUE_CHUNKS = 4
SLAB_ROWS = 8


def _nbytes(shape, dtype):
    return math.prod(shape) * jnp.dtype(dtype).itemsize


def _params(semantics, blocks, scratch=(), temps=()):
    need = 2 * sum(_nbytes(s, d) for s, d in blocks)
    need += sum(_nbytes(s, d) for s, d in scratch)
    need += sum(_nbytes(s, d) for s, d in temps)
    limit = min(V7X_VMEM_SCOPED_CAP, max(need + need // 4, 16 * 1024 * 1024))
    return pltpu.CompilerParams(dimension_semantics=semantics, vmem_limit_bytes=limit)


def _row_tile(n_rows):
    for tm in range(MAX_ROW_TILE - MAX_ROW_TILE % ROW_ALIGN, 0, -ROW_ALIGN):
        if n_rows % tm == 0:
            return tm
    raise ValueError(f"no row tile for {n_rows} rows")


def _cast_weight(w_ref, wbf_ref):
    rows = w_ref.shape[0]
    step = CAST_ROWS if rows % CAST_ROWS == 0 else V7X_LANES
    assert rows % step == 0

    def body(r, carry):
        sl = pl.ds(pl.multiple_of(r * step, step), step)
        wbf_ref[sl, :] = w_ref[sl, :].astype(BF16)
        return carry

    lax.fori_loop(0, rows // step, body, 0)


def _dot(a, b):
    return jnp.dot(a, b, preferred_element_type=F32)


def _dot_nt(a, b):
    return lax.dot_general(a, b, (((1,), (1,)), ((), ())), preferred_element_type=F32)


def _rms(x, gain):
    return x * lax.rsqrt(jnp.mean(x * x, axis=-1, keepdims=True) + EPS) * gain


def _to_slabs(x, slab_ref):
    w = slab_ref.shape[2]
    for s in range(slab_ref.shape[1]):
        slab_ref[:, s, :] = x[:, s * w:(s + 1) * w]


def _from_slabs(slab_ref):
    return jnp.concatenate([slab_ref[:, s, :] for s in range(slab_ref.shape[1])], axis=1)


def _mod_kernel(c_ref, w_ref, b_ref, o_ref):
    s = jax.nn.silu(c_ref[...])
    o_ref[...] = _dot(s.astype(BF16), w_ref[...].astype(BF16)) + b_ref[...]


def _modulation(c8, w_mod, b_mod):
    n_layers, d, width = w_mod.shape
    tn = COL_TILE
    assert width % tn == 0
    blocks = [((8, d), F32), ((d, tn), F32), ((1, tn), F32), ((8, tn), F32)]
    return pl.pallas_call(
        _mod_kernel,
        out_shape=jax.ShapeDtypeStruct((n_layers, 8, width), F32),
        grid=(n_layers, width // tn),
        in_specs=[pl.BlockSpec((8, d), lambda l, j: (0, 0)),
                  pl.BlockSpec((None, d, tn), lambda l, j: (l, 0, j)),
                  pl.BlockSpec((None, 1, tn), lambda l, j: (l, 0, j))],
        out_specs=pl.BlockSpec((None, 8, tn), lambda l, j: (l, 0, j)),
        compiler_params=_params(("arbitrary", "arbitrary"), blocks, temps=[((d, tn), BF16)]),
        name="modulation",
    )(c8, w_mod, b_mod.reshape(n_layers, 1, width))


def _mod_row_index(i, rows_per_tile, seq, n_lat_rows, n_batch):
    lat_tiles = n_lat_rows // rows_per_tile
    return jnp.where(i < lat_tiles, (i * rows_per_tile) // seq, n_batch)


def _row_parts(parts, tr):
    specs, starts, start = [], [], 0
    for p in parts:
        n = p.shape[0] // tr
        assert n * tr == p.shape[0]
        specs.append(pl.BlockSpec((tr, p.shape[1]),
                                  lambda i, *_, start=start, n=n: (jnp.clip(i - start, 0, n - 1), 0)))
        starts.append(start)
        start += n
    return specs, tuple(starts)


def _pick_rows(i, x_refs, starts):
    x = x_refs[0][...]
    for ref, start in zip(x_refs[1:], starts[1:]):
        x = jnp.where(i >= start, ref[...], x)
    return x


def _pre_mix_norm(x, g_ref, mod_ref):
    return (_rms(x, g_ref[...]) * (1.0 + mod_ref[1:2, :]) + mod_ref[0:1, :]).astype(BF16)


def _prenorm_kernel(*refs, starts):
    *x_refs, g_ref, mod_ref, o_ref = refs
    o_ref[...] = _pre_mix_norm(_pick_rows(pl.program_id(0), x_refs, starts), g_ref, mod_ref)


def _prenorm(x_parts, gain, mod, seq, n_lat_rows, n_batch):
    d = x_parts[0].shape[1]
    t = sum(p.shape[0] for p in x_parts)
    tr = EW_ROWS
    ridx = functools.partial(_mod_row_index, rows_per_tile=tr, seq=seq, n_lat_rows=n_lat_rows,
                             n_batch=n_batch)
    x_specs, starts = _row_parts(x_parts, tr)
    blocks = [((tr, d), F32)] * len(x_parts) + [((1, d), F32), ((N_MOD, d), F32), ((tr, d), BF16)]
    return pl.pallas_call(
        functools.partial(_prenorm_kernel, starts=starts),
        out_shape=jax.ShapeDtypeStruct((t, d), BF16),
        grid=(t // tr,),
        in_specs=x_specs + [pl.BlockSpec((1, d), lambda i: (0, 0)),
                            pl.BlockSpec((None, N_MOD, d), lambda i: (ridx(i), 0, 0))],
        out_specs=pl.BlockSpec((tr, d), lambda i: (i, 0)),
        compiler_params=_params(("arbitrary",), blocks, temps=[((tr, d), F32)] * 2),
        name="prenorm",
    )(*x_parts, gain.reshape(1, d), mod)


def _postmix_kernel(*refs, starts, n_parts):
    x_refs = refs[:n_parts]
    y_ref, gpost_ref, gpre_ref, mod_ref, x1_ref, h2_ref, *h2f_ref = refs[n_parts:]
    x = _pick_rows(pl.program_id(0), x_refs, starts)
    x1 = x + mod_ref[2:3, :] * _rms(y_ref[...].astype(F32), gpost_ref[...])
    x1_ref[...] = x1
    h2 = _rms(x1, gpre_ref[...]) * (1.0 + mod_ref[4:5, :]) + mod_ref[3:4, :]
    h2_ref[...] = h2.astype(BF16)
    for ref in h2f_ref:
        _to_slabs(h2, ref)


def _postmix(x_parts, y, g_post, g_pre, mod, n_rows, want_slabs, seq, n_lat_rows, n_batch):
    d = x_parts[0].shape[1]
    tr = EW_ROWS
    x_specs, starts = _row_parts(x_parts, tr)
    assert d % (SLAB_ROWS * V7X_LANES) == 0
    slab = (SLAB_ROWS, d // SLAB_ROWS)
    ridx = functools.partial(_mod_row_index, rows_per_tile=tr, seq=seq, n_lat_rows=n_lat_rows,
                             n_batch=n_batch)
    row = pl.BlockSpec((tr, d), lambda i: (i, 0))
    vec = pl.BlockSpec((1, d), lambda i: (0, 0))
    blocks = ([((tr, d), F32)] * (len(x_parts) + 1 + want_slabs)
              + [((tr, d), BF16)] * 2 + [((N_MOD, d), F32)])
    out_shape = [jax.ShapeDtypeStruct((n_rows, d), F32), jax.ShapeDtypeStruct((n_rows, d), BF16)]
    out_specs = [row, row]
    if want_slabs:
        out_shape.append(jax.ShapeDtypeStruct((n_rows,) + slab, F32))
        out_specs.append(pl.BlockSpec((tr,) + slab, lambda i: (i, 0, 0)))
    return pl.pallas_call(
        functools.partial(_postmix_kernel, starts=starts, n_parts=len(x_parts)),
        out_shape=tuple(out_shape),
        grid=(n_rows // tr,),
        in_specs=x_specs + [row, vec, vec,
                            pl.BlockSpec((None, N_MOD, d), lambda i: (ridx(i), 0, 0))],
        out_specs=tuple(out_specs),
        compiler_params=_params(("arbitrary",), blocks, temps=[((tr, d), F32)] * 3),
        name="postmix",
    )(*x_parts, y, g_post.reshape(1, d), g_pre.reshape(1, d), mod)


def _final_kernel(x_ref, f_ref, g_ref, mod_ref, *rest):
    x = x_ref[...] + mod_ref[5:6, :] * _rms(f_ref[...], g_ref[...])
    if len(rest) == 1:
        (o_ref,) = rest
    else:
        gnext_ref, modnext_ref, o_ref, h_ref = rest
        h_ref[...] = _pre_mix_norm(x, gnext_ref, modnext_ref)
    o_ref[...] = x


def _final(x1, f, g_post, mod, n_rows, next_norm, seq, n_lat_rows, n_batch):
    d = x1.shape[1]
    tr = EW_ROWS
    ridx = functools.partial(_mod_row_index, rows_per_tile=tr, seq=seq, n_lat_rows=n_lat_rows,
                             n_batch=n_batch)
    row = pl.BlockSpec((tr, d), lambda i: (i, 0))
    vec = pl.BlockSpec((1, d), lambda i: (0, 0))
    modspec = pl.BlockSpec((None, N_MOD, d), lambda i: (ridx(i), 0, 0))
    blocks = [((tr, d), F32)] * 3 + [((N_MOD, d), F32)]
    in_specs, args = [row, row, vec, modspec], [x1, f, g_post.reshape(1, d), mod]
    out_shape, out_specs = jax.ShapeDtypeStruct((n_rows, d), F32), row
    if next_norm is not None:
        in_specs += [vec, modspec]
        args += [next_norm[0].reshape(1, d), next_norm[1]]
        out_shape = (out_shape, jax.ShapeDtypeStruct((n_rows, d), BF16))
        out_specs = (row, row)
        blocks += [((N_MOD, d), F32), ((tr, d), BF16)]
    return pl.pallas_call(
        _final_kernel,
        out_shape=out_shape,
        grid=(n_rows // tr,),
        in_specs=in_specs,
        out_specs=out_specs,
        compiler_params=_params(("arbitrary",), blocks, temps=[((tr, d), F32)] * 2),
        name="final_residual",
    )(*args)


def _swap_pairs(x):
    lane = lax.broadcasted_iota(jnp.int32, x.shape, 1)
    quarter = HEAD_DIM // 4
    first = (lane & quarter) == 0
    return jnp.where(first, pltpu.roll(x, HEAD_DIM - quarter, axis=1), pltpu.roll(x, quarter, axis=1))


def _inproj_kernel(*refs, mode, n_q_tiles, q_scale):
    if mode in ("rope", "normrope"):
        h_ref, w_ref, cos_ref, sin_ref, gq_ref, gk_ref, o_ref, wbf_ref = refs
    else:
        h_ref, w_ref, o_ref, wbf_ref = refs
    j = pl.program_id(0)

    @pl.when(pl.program_id(1) == 0)
    def _():
        _cast_weight(w_ref, wbf_ref)

    tm = h_ref.shape[0]
    n_chunks = EPILOGUE_CHUNKS if tm % (EPILOGUE_CHUNKS * ROW_ALIGN) == 0 else 1
    rc = tm // n_chunks
    for c in range(n_chunks):
        rows = slice(c * rc, (c + 1) * rc)
        acc = _dot(h_ref[rows, :], wbf_ref[...])
        if mode == "plain":
            o_ref[rows, :] = acc.astype(BF16)
        elif mode == "gelu":
            o_ref[rows, :] = (0.5 * acc * (1.0 + lax.erf(acc * (2.0 ** -0.5)))).astype(BF16)
        else:
            is_q = j < n_q_tiles
            scale = jnp.where(is_q, q_scale, 1.0).astype(F32)
            cos = cos_ref[rows, :]
            sin = sin_ref[rows, :]
            gain = jnp.where(is_q, gq_ref[...], gk_ref[...])
            for hh in range(acc.shape[1] // HEAD_DIM):
                sl = slice(hh * HEAD_DIM, (hh + 1) * HEAD_DIM)
                xh = acc[:, sl]
                if mode == "normrope":
                    xh = _rms(xh, gain)
                xh = xh * cos + _swap_pairs(xh) * sin
                o_ref[rows, sl] = (xh * scale).astype(BF16)


def _inproj(h, w_in, layer, col_start, width, mode, rope=None, n_q_tiles=0):
    t, d = h.shape
    tm, tn = _row_tile(t), COL_TILE
    assert col_start % tn == 0 and width % tn == 0
    j0 = col_start // tn
    in_specs = [pl.BlockSpec((tm, d), lambda j, i: (i, 0)),
                pl.BlockSpec((None, d, tn), lambda j, i: (layer, 0, j0 + j))]
    args = [h, w_in]
    blocks = [((tm, d), BF16), ((d, tn), F32), ((tm, tn), BF16)]
    if mode in ("rope", "normrope"):
        cos, sin, gq, gk = rope
        tab = pl.BlockSpec((tm, HEAD_DIM), lambda j, i: (i, 0))
        vec = pl.BlockSpec((1, HEAD_DIM), lambda j, i: (0, 0))
        in_specs += [tab, tab, vec, vec]
        args += [cos, sin, gq, gk]
        blocks += [((tm, HEAD_DIM), F32)] * 2
    kern = functools.partial(_inproj_kernel, mode=mode, n_q_tiles=n_q_tiles,
                             q_scale=HEAD_DIM ** -0.5 * LOG2E)
    return pl.pallas_call(
        kern,
        out_shape=jax.ShapeDtypeStruct((t, width), BF16),
        grid=(width // tn, t // tm),
        in_specs=in_specs,
        out_specs=pl.BlockSpec((tm, tn), lambda j, i: (i, j)),
        scratch_shapes=[pltpu.VMEM((d, tn), BF16)],
        compiler_params=_params(("arbitrary", "arbitrary"), blocks, scratch=[((d, tn), BF16)],
                                temps=[((tm, tn), F32)] * 2),
        name="inproj_" + mode,
    )(*args)


def _window_bias(group, n_ctx):
    r = jnp.arange(group * BLOCK)[:, None] % BLOCK
    c = jnp.arange(3 * BLOCK + n_ctx)[None, :]
    in_prev, in_cur = c < BLOCK, (c >= BLOCK) & (c < 2 * BLOCK)
    in_next = (c >= 2 * BLOCK) & (c < 3 * BLOCK)
    band_prev = in_prev & (c < r)
    band_next = in_next & (c - 2 * BLOCK > r)
    hidden = [band_prev | band_next, in_prev | band_next, band_prev | in_next, in_prev | in_next,
              in_prev | in_cur | in_next]
    shape = (group * BLOCK, 3 * BLOCK + n_ctx)
    return jnp.stack([jnp.where(jnp.broadcast_to(h, shape), -jnp.inf, 0.0).astype(F32)
                      for h in hidden])


def _window_attn_kernel(sink_ref, bias_ref, q_ref, kp_ref, kc_ref, kn_ref, kx_ref, vp_ref, vc_ref,
                        vn_ref, vx_ref, o_ref, *, group):
    rows = group * BLOCK
    assert BLOCK & (BLOCK - 1) == 0
    shift = BLOCK.bit_length() - 1
    bias = bias_ref[...]
    rid = lax.broadcasted_iota(jnp.int32, (rows, 1), 0) >> shift
    for hk in range(A_KV_HEADS):
        ksl = slice(hk * HEAD_DIM, (hk + 1) * HEAD_DIM)
        k_all = jnp.concatenate([kp_ref[:, ksl], kc_ref[:, ksl], kn_ref[:, ksl], kx_ref[:, ksl]],
                                axis=0)
        v_all = jnp.concatenate([vp_ref[:, ksl], vc_ref[:, ksl], vn_ref[:, ksl], vx_ref[:, ksl]],
                                axis=0)
        q3 = jnp.concatenate(
            [q_ref[:, (hk * group + g) * HEAD_DIM:(hk * group + g + 1) * HEAD_DIM]
             for g in range(group)], axis=0)
        sink = jnp.zeros((rows, 1), F32)
        for g in range(group):
            sink = jnp.where(rid == g, sink_ref[hk * group + g] * LOG2E, sink)
        s = _dot_nt(q3, k_all) + bias
        m = jnp.maximum(jnp.max(s, axis=-1, keepdims=True), sink)
        p = jnp.exp2(s - m)
        denom = jnp.sum(p, axis=-1, keepdims=True) + jnp.exp2(sink - m)
        o = _dot(p.astype(BF16), v_all) / denom
        for g in range(group):
            osl = slice((hk * group + g) * HEAD_DIM, (hk * group + g + 1) * HEAD_DIM)
            o_ref[:, osl] = o[g * BLOCK:(g + 1) * BLOCK, :].astype(BF16)


def _window_attn(za, av, sink, n_batch, seq, n_ctx, with_ctx):
    n_heads = sink.shape[0]
    group = n_heads // A_KV_HEADS
    qw = n_heads * HEAD_DIM
    kw = A_KV_HEADS * HEAD_DIM
    assert qw % kw == 0 and seq % BLOCK == 0 and (n_batch * seq) % n_ctx == 0 and WINDOW == BLOCK
    assert n_ctx % BLOCK == 0
    nb = seq // BLOCK
    ncb = n_ctx // BLOCK if with_ctx else 0
    kcol = qw // kw
    ctx0 = (n_batch * seq) // n_ctx

    def qblk(b, n):
        return (jnp.where(n < nb, b * nb + n, n_batch * nb + b * ncb + n - nb), 0)

    def blk(shift):
        return lambda b, n: (b * nb + jnp.clip(n + shift, 0, nb - 1), kcol)

    def vblk(shift):
        return lambda b, n: (b * nb + jnp.clip(n + shift, 0, nb - 1), 0)

    def bias_variant(b, n):
        lat = jnp.where(n == 0, 1, 0) + jnp.where(n == nb - 1, 2, 0)
        return (jnp.where(n < nb, lat, 4), 0, 0)

    kspec = [pl.BlockSpec((BLOCK, kw), blk(s)) for s in (-1, 0, 1)]
    vspec = [pl.BlockSpec((BLOCK, kw), vblk(s)) for s in (-1, 0, 1)]
    n_keys = 3 * BLOCK + n_ctx
    rows = group * BLOCK
    blocks = ([((BLOCK, qw), BF16)] * 2 + [((BLOCK, kw), BF16)] * 6 + [((n_ctx, kw), BF16)] * 2
              + [((rows, n_keys), F32)])
    return pl.pallas_call(
        functools.partial(_window_attn_kernel, group=group),
        out_shape=jax.ShapeDtypeStruct((n_batch * (nb + ncb) * BLOCK, qw), BF16),
        grid=(n_batch, nb + ncb),
        in_specs=[pl.BlockSpec(memory_space=pltpu.SMEM),
                  pl.BlockSpec((None, rows, n_keys), bias_variant),
                  pl.BlockSpec((BLOCK, qw), qblk)]
                 + kspec + [pl.BlockSpec((n_ctx, kw), lambda b, n: (ctx0 + b, kcol))]
                 + vspec + [pl.BlockSpec((n_ctx, kw), lambda b, n: (ctx0 + b, 0))],
        out_specs=pl.BlockSpec((BLOCK, qw), qblk),
        compiler_params=_params(("arbitrary", "arbitrary"), blocks,
                                temps=[((rows, n_keys), F32)] * 4),
        name="window_attn",
    )(sink, _window_bias(group, n_ctx), za, za, za, za, za, av, av, av, av)


def _gmlp_kernel(z_ref, g_ref, b_ref, ws_ref, bst_ref, o_ref):
    width = g_ref.shape[1]
    u = z_ref[:, :width].astype(F32)
    v = z_ref[:, width:].astype(F32)
    mu = jnp.mean(v, axis=-1, keepdims=True)
    vc = v - mu
    var = jnp.mean(vc * vc, axis=-1, keepdims=True)
    vn = (vc * lax.rsqrt(var + EPS) * g_ref[...] + b_ref[...]).astype(BF16)
    gd = width // B_GROUPS
    for g in range(B_GROUPS):
        sl = slice(g * gd, (g + 1) * gd)
        mixed = _dot(ws_ref[g].astype(BF16), vn[:, sl]) + bst_ref[:, g:g + 1]
        o_ref[:, sl] = (u[:, sl] * mixed).astype(BF16)


def _gmlp(zb, ln_g, ln_b, ws, bs, n_rows):
    w2 = zb.shape[1]
    width = w2 // 2
    blocks = [((CHUNK, w2), BF16), ((1, width), F32), ((1, width), F32),
              (ws.shape, F32), ((CHUNK, B_GROUPS), F32), ((CHUNK, width), BF16)]
    return pl.pallas_call(
        _gmlp_kernel,
        out_shape=jax.ShapeDtypeStruct((n_rows, width), BF16),
        grid=(n_rows // CHUNK,),
        in_specs=[pl.BlockSpec((CHUNK, w2), lambda i: (i, 0)),
                  pl.BlockSpec((1, width), lambda i: (0, 0)),
                  pl.BlockSpec((1, width), lambda i: (0, 0)),
                  pl.BlockSpec(ws.shape, lambda i: (0, 0, 0)),
                  pl.BlockSpec((CHUNK, B_GROUPS), lambda i: (0, 0))],
        out_specs=pl.BlockSpec((CHUNK, width), lambda i: (i, 0)),
        compiler_params=_params(("arbitrary",), blocks, temps=[((CHUNK, w2), F32)] * 2),
        name="chunk_gmlp",
    )(zb, ln_g.reshape(1, width), ln_b.reshape(1, width), ws, bs.T)


def _global_attn_kernel(q_ref, kx_ref, vx_ref, k_ref, v_ref, o_ref, vext_ref, *, group,
                        n_lat_tiles, has_ctx_tiles):
    tq = q_ref.shape[0]
    n_ctx = kx_ref.shape[0]
    n_lat = k_ref.shape[0]
    rows = group * tq

    @pl.when(pl.program_id(2) == 0)
    def _():
        vext_ref[0:n_ctx, 0:HEAD_DIM] = vx_ref[...]
        vext_ref[n_ctx:, 0:HEAD_DIM] = v_ref[...]
        vext_ref[:, HEAD_DIM:] = jnp.ones((n_ctx + n_lat, HEAD_DIM), BF16)

    def attend(chunks):
        q3 = jnp.concatenate([q_ref[:, g * HEAD_DIM:(g + 1) * HEAD_DIM] for g in range(group)],
                             axis=0)
        m = jnp.full((rows, HEAD_DIM), -jnp.inf, F32)
        acc = jnp.zeros((rows, 2 * HEAD_DIM), F32)
        for k_chunk, v0, nk in chunks:
            s = _dot_nt(q3, k_chunk())
            m_new = jnp.maximum(m, jnp.broadcast_to(jnp.max(s, axis=-1, keepdims=True), m.shape))
            alpha = jnp.exp2(m - m_new)
            p = jnp.exp2(s - jnp.tile(m_new, (1, nk // HEAD_DIM)))
            acc = jnp.tile(alpha, (1, 2)) * acc + _dot(p.astype(BF16), vext_ref[v0:v0 + nk, :])
            m = m_new
        o = acc[:, :HEAD_DIM] / acc[:, HEAD_DIM:]
        for g in range(group):
            o_ref[:, g * HEAD_DIM:(g + 1) * HEAD_DIM] = o[g * tq:(g + 1) * tq, :].astype(BF16)

    ctx_chunk = [(lambda: kx_ref[...], 0, n_ctx)]
    lat_chunks = [(lambda c=c: k_ref[c:c + K_TILE, :], n_ctx + c, K_TILE)
                  for c in range(0, n_lat, K_TILE)]
    is_lat = pl.program_id(2) < n_lat_tiles
    pl.when(is_lat)(lambda: attend(ctx_chunk + lat_chunks))
    if has_ctx_tiles:
        pl.when(jnp.logical_not(is_lat))(lambda: attend(ctx_chunk))


def _global_attn(zc, cv, n_batch, seq, n_ctx, n_heads, with_ctx):
    group = n_heads // C_KV_HEADS
    gw = group * HEAD_DIM
    tq = Q_TILE
    assert seq % tq == 0 and seq % K_TILE == 0 and (n_batch * seq) % n_ctx == 0 and n_ctx % tq == 0
    nq = seq // tq
    ncq = n_ctx // tq if with_ctx else 0
    ctx0 = (n_batch * seq) // n_ctx
    rows = group * tq

    def qblk(b, hk, i):
        return (jnp.where(i < nq, b * nq + i, n_batch * nq + b * ncq + i - nq), hk)

    blocks = [((tq, gw), BF16)] * 2 + [((n_ctx, HEAD_DIM), BF16)] * 2 + [((seq, HEAD_DIM), BF16)] * 2
    scratch = [((n_ctx + seq, 2 * HEAD_DIM), BF16)]
    return pl.pallas_call(
        functools.partial(_global_attn_kernel, group=group, n_lat_tiles=nq, has_ctx_tiles=ncq > 0),
        out_shape=jax.ShapeDtypeStruct((n_batch * (nq + ncq) * tq, n_heads * HEAD_DIM), BF16),
        grid=(n_batch, C_KV_HEADS, nq + ncq),
        in_specs=[pl.BlockSpec((tq, gw), qblk),
                  pl.BlockSpec((n_ctx, HEAD_DIM), lambda b, hk, i: (ctx0 + b, n_heads + hk)),
                  pl.BlockSpec((n_ctx, HEAD_DIM), lambda b, hk, i: (ctx0 + b, hk)),
                  pl.BlockSpec((seq, HEAD_DIM), lambda b, hk, i: (b, n_heads + hk)),
                  pl.BlockSpec((seq, HEAD_DIM), lambda b, hk, i: (b, hk))],
        out_specs=pl.BlockSpec((tq, gw), qblk),
        scratch_shapes=[pltpu.VMEM(s, d) for s, d in scratch],
        compiler_params=_params(("arbitrary", "arbitrary", "arbitrary"), blocks, scratch=scratch,
                                temps=[((rows, K_TILE), F32)] * 6),
        name="global_attn",
    )(zc, zc, cv, zc, cv)


def _outproj_kernel(a_ref, b_ref, c_ref, w_ref, o_ref, wbf_ref):
    @pl.when(pl.program_id(1) == 0)
    def _():
        _cast_weight(w_ref, wbf_ref)

    ka, kb = a_ref.shape[1], b_ref.shape[1]
    acc = _dot(a_ref[...], wbf_ref[0:ka, :])
    acc += _dot(b_ref[...], wbf_ref[ka:ka + kb, :])
    acc += _dot(c_ref[...], wbf_ref[ka + kb:, :])
    o_ref[...] = acc.astype(BF16)


def _outproj(oa, ob, oc, w_out, layer, n_rows):
    t = oa.shape[0]
    _, k, d = w_out.shape
    tm, tn = _row_tile(n_rows), COL_TILE
    assert oa.shape[1] + ob.shape[1] + oc.shape[1] == k and d % tn == 0
    blocks = [((tm, k), BF16), ((k, tn), F32), ((tm, tn), BF16)]
    return pl.pallas_call(
        _outproj_kernel,
        out_shape=jax.ShapeDtypeStruct((t, d), BF16),
        grid=(d // tn, n_rows // tm),
        in_specs=[pl.BlockSpec((tm, oa.shape[1]), lambda j, i: (i, 0)),
                  pl.BlockSpec((tm, ob.shape[1]), lambda j, i: (i, 0)),
                  pl.BlockSpec((tm, oc.shape[1]), lambda j, i: (i, 0)),
                  pl.BlockSpec((None, k, tn), lambda j, i: (layer, 0, j))],
        out_specs=pl.BlockSpec((tm, tn), lambda j, i: (i, j)),
        scratch_shapes=[pltpu.VMEM((k, tn), BF16)],
        compiler_params=_params(("arbitrary", "arbitrary"), blocks, scratch=[((k, tn), BF16)],
                                temps=[((tm, tn), F32)] * 2),
        name="outproj",
    )(oa, ob, oc, w_out)


def _ffn_up_kernel(h_ref, wg_ref, wu_ref, o_ref, wgbf_ref, wubf_ref):
    @pl.when(pl.program_id(1) == 0)
    def _():
        _cast_weight(wg_ref, wgbf_ref)
        _cast_weight(wu_ref, wubf_ref)

    h = h_ref[...]
    gate = _dot(h, wgbf_ref[...])
    up = _dot(h, wubf_ref[...])
    o_ref[...] = (jax.nn.silu(gate) * up).astype(BF16)


def _ffn_up(h2, w_gate, w_up, sel, n_rows):
    t, d = h2.shape
    f = w_gate.shape[-1]
    tm, tn = _row_tile(n_rows), FFN_COL_TILE
    assert f % tn == 0
    lead = (None,) * len(sel)
    wspec = pl.BlockSpec(lead + (d, tn), lambda j, i: sel + (0, j))
    blocks = [((tm, d), BF16), ((d, tn), F32), ((d, tn), F32), ((tm, tn), BF16)]
    return pl.pallas_call(
        _ffn_up_kernel,
        out_shape=jax.ShapeDtypeStruct((t, f), BF16),
        grid=(f // tn, n_rows // tm),
        in_specs=[pl.BlockSpec((tm, d), lambda j, i: (i, 0)), wspec, wspec],
        out_specs=pl.BlockSpec((tm, tn), lambda j, i: (i, j)),
        scratch_shapes=[pltpu.VMEM((d, tn), BF16)] * 2,
        compiler_params=_params(("arbitrary", "arbitrary"), blocks, scratch=[((d, tn), BF16)] * 2,
                                temps=[((tm, tn), F32)] * 3),
        name="ffn_up",
    )(h2, w_gate, w_up)


def _ffn_down_kernel(*refs, has_prev):
    if has_prev:
        a_ref, w_ref, prev_ref, o_ref, wbf_ref = refs
    else:
        a_ref, w_ref, o_ref, wbf_ref = refs

    @pl.when(pl.program_id(1) == 0)
    def _():
        _cast_weight(w_ref, wbf_ref)

    acc = _dot(a_ref[...], wbf_ref[...])
    if has_prev:
        acc = prev_ref[...] + acc
    o_ref[...] = acc


def _ffn_down(a, w_down, sel, k_start, k_size, n_rows, prev=None):
    t = a.shape[0]
    d = w_down.shape[-1]
    tm, tn = ROW_TILE, COL_TILE
    assert k_start % k_size == 0 and d % tn == 0 and n_rows % tm == 0
    kb = k_start // k_size
    lead = (None,) * len(sel)
    in_specs = [pl.BlockSpec((tm, k_size), lambda j, i: (i, kb)),
                pl.BlockSpec(lead + (k_size, tn), lambda j, i: sel + (kb, j))]
    args = [a, w_down]
    blocks = [((tm, k_size), BF16), ((k_size, tn), F32), ((tm, tn), F32)]
    aliases = {}
    if prev is not None:
        in_specs.append(pl.BlockSpec((tm, tn), lambda j, i: (i, j)))
        aliases = {len(args): 0}
        args.append(prev)
        blocks.append(((tm, tn), F32))
    return pl.pallas_call(
        functools.partial(_ffn_down_kernel, has_prev=prev is not None),
        out_shape=jax.ShapeDtypeStruct((t, d), F32),
        grid=(d // tn, n_rows // tm),
        in_specs=in_specs,
        out_specs=pl.BlockSpec((tm, tn), lambda j, i: (i, j)),
        scratch_shapes=[pltpu.VMEM((k_size, tn), BF16)],
        input_output_aliases=aliases,
        compiler_params=_params(("arbitrary", "arbitrary"), blocks, scratch=[((k_size, tn), BF16)],
                                temps=[((tm, tn), F32)] * 2),
        name="ffn_down",
    )(*args)


def _router_kernel(h_ref, w_ref, o_ref, *, n_experts):
    pieces = _dot(h_ref[...], w_ref[...])
    logits = pieces
    for k in range(1, ROUTER_PIECES):
        logits = logits + pltpu.roll(pieces, V7X_LANES - k * n_experts, axis=1)
    lane = lax.broadcasted_iota(jnp.int32, logits.shape, 1).astype(F32)
    neg = -jnp.inf
    logits = jnp.where(lane < n_experts, logits, neg)
    picked = []
    remaining = logits
    for _ in range(TOP_K):
        top = jnp.max(remaining, axis=-1, keepdims=True)
        idx = jnp.min(jnp.where(remaining == top, lane, float(logits.shape[1])), axis=-1,
                      keepdims=True)
        picked.append((top, idx))
        remaining = jnp.where(lane == idx, neg, remaining)
    top0 = picked[0][0]
    denom = sum(jnp.exp(tv - top0) for tv, _ in picked)
    route = jnp.zeros(logits.shape, F32)
    for k, (tv, idx) in enumerate(picked):
        route = jnp.where(lane == k, idx, route)
        route = jnp.where(lane == TOP_K + k, jnp.exp(tv - top0) / denom, route)
    o_ref[...] = route


def _router(h2, w_router, n_rows):
    t, d = h2.shape
    n_experts = w_router.shape[1]
    assert ROUTER_PIECES * n_experts <= V7X_LANES
    pieces, rest = [], w_router
    for _ in range(ROUTER_PIECES):
        pieces.append(rest.astype(BF16))
        rest = rest - pieces[-1].astype(F32)
    wpad = jnp.pad(jnp.concatenate(pieces, axis=1),
                   ((0, 0), (0, V7X_LANES - ROUTER_PIECES * n_experts)))
    tr = EW_ROWS
    blocks = [((tr, d), BF16), ((d, V7X_LANES), BF16), ((tr, V7X_LANES), F32)]
    return pl.pallas_call(
        functools.partial(_router_kernel, n_experts=n_experts),
        out_shape=jax.ShapeDtypeStruct((t, V7X_LANES), F32),
        grid=(n_rows // tr,),
        in_specs=[pl.BlockSpec((tr, d), lambda i: (i, 0)),
                  pl.BlockSpec((d, V7X_LANES), lambda i: (0, 0))],
        out_specs=pl.BlockSpec((tr, V7X_LANES), lambda i: (i, 0)),
        compiler_params=_params(("arbitrary",), blocks, temps=[((tr, d), F32)] * 3),
        name="moe_router",
    )(h2, wpad)


def _route_plan(route, n_experts):
    tile = ROW_TILE
    n_tok = route.shape[0]
    n_asg = n_tok * TOP_K
    e_flat = route[:, :TOP_K].astype(jnp.int32).reshape(n_asg)
    onehot = (e_flat[:, None] == jnp.arange(n_experts, dtype=jnp.int32)[None, :]).astype(jnp.int32)
    csum = jnp.cumsum(onehot, axis=0)
    rank = jnp.take_along_axis(csum, e_flat[:, None], axis=1)[:, 0] - 1
    counts = csum[-1]
    tiles_per = (counts + tile - 1) // tile
    tile_end = jnp.cumsum(tiles_per)
    tile_start = tile_end - tiles_per
    dest = tile_start[e_flat] * tile + rank

    n_tiles = (n_asg + n_experts * (tile - 1)) // tile
    tile_ids = jnp.arange(n_tiles, dtype=jnp.int32)
    n_used = tile_end[-1]
    valid = tile_ids < n_used
    expert_raw = jnp.sum((tile_end[None, :] <= tile_ids[:, None]).astype(jnp.int32), axis=1)
    tile_expert = jnp.where(valid, expert_raw, expert_raw[n_used - 1])
    tile_first = valid & (tile_ids == tile_start[tile_expert])
    tile_row = jnp.where(valid, tile_ids, n_used - 1)

    tok_of_row = jnp.zeros((n_tiles * tile,), jnp.int32).at[dest].set(
        jnp.arange(n_asg, dtype=jnp.int32) // TOP_K, unique_indices=True)
    tile_first = tile_first.astype(jnp.int32)
    tile_slot = (jnp.cumsum(tile_first) - 1) % 2
    tile_fill = jnp.where(
        valid, jnp.clip(counts[tile_expert] - (tile_ids - tile_start[tile_expert]) * tile, 0, tile), 0)
    tiles = (tile_expert, tile_first, tile_fill, tile_row, tile_slot)
    return tok_of_row, dest, tiles


def _row_copy(src_hbm, dst_vmem, sem, src_row, dst_row):
    return pltpu.make_async_copy(src_hbm.at[pl.ds(src_row, 1)], dst_vmem.at[pl.ds(dst_row, 1)], sem)


def _slab_copies(src_hbm, dst_vmem, sem, src_row, dst_row):
    width = dst_vmem.shape[2]
    split = GATHER_SPLIT if width % (GATHER_SPLIT * V7X_LANES) == 0 else 1
    w = width // split
    return [pltpu.make_async_copy(src_hbm.at[pl.ds(src_row, 1), :, pl.ds(h * w, w)],
                                  dst_vmem.at[pl.ds(dst_row, 1), :, pl.ds(h * w, w)], sem)
            for h in range(split)]


def _gather_rows_kernel(idx_ref, src_ref, o_ref, buf_ref, sem):
    tile = buf_ref.shape[0]
    base = pl.program_id(0) * tile

    def issue(g, carry):
        for u in range(GATHER_UNROLL):
            r = g * GATHER_UNROLL + u
            for h, copy in enumerate(_slab_copies(src_ref, buf_ref, sem, idx_ref[base + r], r)):
                copy.start(priority=h % 2)
        return carry

    def drain(r, carry):
        for copy in _slab_copies(src_ref, buf_ref, sem, 0, r):
            copy.wait()
        return carry

    lax.fori_loop(0, tile // GATHER_UNROLL, issue, 0)
    lax.fori_loop(0, tile, drain, 0, unroll=GATHER_UNROLL)
    o_ref[...] = _from_slabs(buf_ref).astype(BF16)


def _gather_rows(src, idx):
    slab = src.shape[1:]
    d = slab[0] * slab[1]
    n = idx.shape[0]
    tg = EW_ROWS
    assert n % tg == 0
    blocks = [((tg, d), BF16)]
    return pl.pallas_call(
        _gather_rows_kernel,
        out_shape=jax.ShapeDtypeStruct((n, d), BF16),
        grid_spec=pltpu.PrefetchScalarGridSpec(
            num_scalar_prefetch=1,
            grid=(n // tg,),
            in_specs=[pl.BlockSpec(memory_space=pl.ANY)],
            out_specs=pl.BlockSpec((tg, d), lambda i, idx_ref: (i, 0)),
            scratch_shapes=[pltpu.VMEM((tg,) + slab, F32), pltpu.SemaphoreType.DMA]),
        compiler_params=_params(("arbitrary",), blocks, scratch=[((tg, d), F32)],
                                temps=[((tg, d), F32)]),
        name="moe_dispatch",
    )(idx, src)


def _held_tile(s, n_tiles):
    return jnp.minimum(s, n_tiles - 1)


def _done_tile(s):
    return jnp.maximum(s - 1, 0)


def _grouped_step(tables, w_refs, wbf_refs, compute, o_ref):
    _, tf_ref, tn_ref, _, ts_ref = tables
    n_tiles = tf_ref.shape[0]
    tm = o_ref.shape[0]
    s = pl.program_id(1)
    held = _held_tile(s, n_tiles)
    done = _done_tile(s)
    fill = jnp.where(s > 0, tn_ref[done], -MOE_ROW_STEP)

    @pl.when((s < n_tiles) & (tf_ref[held] == 1))
    def _():
        for w_ref, wbf_ref in zip(w_refs, wbf_refs):
            _cast_weight(w_ref, wbf_ref.at[ts_ref[held]])

    for n in range(0, tm + 1, MOE_ROW_STEP):
        @pl.when((fill > n - MOE_ROW_STEP) & (fill <= n))
        def _(n=n):
            if n > 0:
                o_ref[0:n, :] = compute([wbf_ref[ts_ref[done]] for wbf_ref in wbf_refs], n)
            if n < tm:
                o_ref[n:, :] = jnp.zeros((tm - n, o_ref.shape[1]), o_ref.dtype)


def _moe_up_kernel(te_ref, tf_ref, tn_ref, tr_ref, ts_ref, x_ref, wg_ref, wu_ref, o_ref, wgbf_ref,
                   wubf_ref):
    def compute(w, n):
        x = x_ref[0:n, :]
        return (jax.nn.silu(_dot(x, w[0])) * _dot(x, w[1])).astype(BF16)

    _grouped_step((te_ref, tf_ref, tn_ref, tr_ref, ts_ref), (wg_ref, wu_ref), (wgbf_ref, wubf_ref),
                  compute, o_ref)


def _moe_up(xs, w_gate, w_up, layer, tiles):
    r, d = xs.shape
    f = w_gate.shape[-1]
    tm, tn = ROW_TILE, FFN_COL_TILE
    assert f % tn == 0 and r % tm == 0
    nt = r // tm
    wspec = pl.BlockSpec((None, None, d, tn),
                         lambda j, s, te, tf, tv, tr, ts: (layer, te[_held_tile(s, nt)], 0, j))
    blocks = [((tm, d), BF16), ((d, tn), F32), ((d, tn), F32), ((tm, tn), BF16)]
    return pl.pallas_call(
        _moe_up_kernel,
        out_shape=jax.ShapeDtypeStruct((r, f), BF16),
        grid_spec=pltpu.PrefetchScalarGridSpec(
            num_scalar_prefetch=5,
            grid=(f // tn, nt + 1),
            in_specs=[pl.BlockSpec((tm, d), lambda j, s, te, tf, tv, tr, ts: (tr[_done_tile(s)], 0)),
                      wspec, wspec],
            out_specs=pl.BlockSpec((tm, tn), lambda j, s, te, tf, tv, tr, ts: (_done_tile(s), j)),
            scratch_shapes=[pltpu.VMEM((2, d, tn), BF16)] * 2),
        compiler_params=_params(("arbitrary", "arbitrary"), blocks,
                                scratch=[((2, d, tn), BF16)] * 2, temps=[((tm, tn), F32)] * 3),
        name="moe_up",
    )(*tiles, xs, w_gate, w_up)


def _moe_down_kernel(te_ref, tf_ref, tn_ref, tr_ref, ts_ref, a_ref, w_ref, o_ref, wbf_ref):
    _grouped_step((te_ref, tf_ref, tn_ref, tr_ref, ts_ref), (w_ref,), (wbf_ref,),
                  lambda w, n: _dot(a_ref[0:n, :], w[0]), o_ref)


def _moe_down(hmid, w_down, layer, tiles):
    r, f = hmid.shape
    d = w_down.shape[-1]
    tm, tn = ROW_TILE, COL_TILE
    assert d % tn == 0 and r % tm == 0
    nt = r // tm
    blocks = [((tm, f), BF16), ((f, tn), F32), ((tm, tn), F32)]
    return pl.pallas_call(
        _moe_down_kernel,
        out_shape=jax.ShapeDtypeStruct((r, d), F32),
        grid_spec=pltpu.PrefetchScalarGridSpec(
            num_scalar_prefetch=5,
            grid=(d // tn, nt + 1),
            in_specs=[pl.BlockSpec((tm, f), lambda j, s, te, tf, tv, tr, ts: (tr[_done_tile(s)], 0)),
                      pl.BlockSpec((None, None, f, tn),
                                   lambda j, s, te, tf, tv, tr, ts:
                                   (layer, te[_held_tile(s, nt)], 0, j))],
            out_specs=pl.BlockSpec((tm, tn), lambda j, s, te, tf, tv, tr, ts: (_done_tile(s), j)),
            scratch_shapes=[pltpu.VMEM((2, f, tn), BF16)]),
        compiler_params=_params(("arbitrary", "arbitrary"), blocks, scratch=[((2, f, tn), BF16)],
                                temps=[((tm, tn), F32)] * 2),
        name="moe_down",
    )(*tiles, hmid, w_down)


def _moe_final_kernel(dest_ref, x_ref, y_ref, route_ref, g_ref, mod_ref, o_ref, buf_ref, sems):
    tile = x_ref.shape[0]
    base = pl.program_id(0) * tile

    def issue(g, carry):
        for u in range(GATHER_UNROLL):
            r = g * GATHER_UNROLL + u
            for k in range(TOP_K):
                _row_copy(y_ref, buf_ref.at[k], sems.at[k], dest_ref[(base + r) * TOP_K + k],
                          r).start(priority=k % 2)
        return carry

    def drain(r, carry):
        for k in range(TOP_K):
            _row_copy(y_ref, buf_ref.at[k], sems.at[k], 0, r).wait()
        return carry

    lax.fori_loop(0, tile // GATHER_UNROLL, issue, 0)
    lax.fori_loop(0, tile, drain, 0, unroll=GATHER_UNROLL)
    f = route_ref[:, TOP_K:TOP_K + 1] * buf_ref[0]
    for k in range(1, TOP_K):
        f = f + route_ref[:, TOP_K + k:TOP_K + k + 1] * buf_ref[k]
    o_ref[...] = x_ref[...] + mod_ref[5:6, :] * _rms(f, g_ref[...])


def _moe_final(x1, y, dest, route, g_post, mod, n_rows, seq, n_lat_rows, n_batch):
    d = x1.shape[1]
    tr = EW_ROWS
    ridx = functools.partial(_mod_row_index, rows_per_tile=tr, seq=seq, n_lat_rows=n_lat_rows,
                             n_batch=n_batch)
    row = pl.BlockSpec((tr, d), lambda i, dest_ref: (i, 0))
    blocks = [((tr, d), F32)] * 2 + [((N_MOD, d), F32)]
    return pl.pallas_call(
        _moe_final_kernel,
        out_shape=jax.ShapeDtypeStruct((n_rows, d), F32),
        grid_spec=pltpu.PrefetchScalarGridSpec(
            num_scalar_prefetch=1,
            grid=(n_rows // tr,),
            in_specs=[row, pl.BlockSpec(memory_space=pl.ANY),
                      pl.BlockSpec((tr, V7X_LANES), lambda i, dest_ref: (i, 0)),
                      pl.BlockSpec((1, d), lambda i, dest_ref: (0, 0)),
                      pl.BlockSpec((None, N_MOD, d), lambda i, dest_ref: (ridx(i), 0, 0))],
            out_specs=row,
            scratch_shapes=[pltpu.VMEM((TOP_K, tr, d), F32), pltpu.SemaphoreType.DMA((TOP_K,))]),
        compiler_params=_params(("arbitrary",), blocks, scratch=[((TOP_K, tr, d), F32)],
                                temps=[((tr, d), F32)] * 2),
        name="moe_combine_final",
    )(dest, x1, y, route, g_post.reshape(1, d), mod)


def _rope_tables(seq, n_batch, ctx_rows):
    n = jnp.arange(seq)
    pos_r = (n // GRID_W).astype(F32)
    pos_w = (n % GRID_W).astype(F32)
    n_freq = HEAD_DIM // 4
    inv_freq = ROPE_THETA ** (-jnp.arange(n_freq, dtype=F32) / n_freq)
    ar = pos_r[:, None] * inv_freq
    aw = pos_w[:, None] * inv_freq
    cos = jnp.concatenate([jnp.cos(ar), jnp.cos(ar), jnp.cos(aw), jnp.cos(aw)], axis=-1)
    sin = jnp.concatenate([-jnp.sin(ar), jnp.sin(ar), -jnp.sin(aw), jnp.sin(aw)], axis=-1)
    cos = jnp.concatenate([cos] * n_batch + [jnp.ones((ctx_rows, HEAD_DIM), F32)], axis=0)
    sin = jnp.concatenate([sin] * n_batch + [jnp.zeros((ctx_rows, HEAD_DIM), F32)], axis=0)
    return cos, sin


def kernel(x, c, ctx, c_ctx, w_mod, b_mod, g_pre_mix, g_post_mix, g_pre_ffn, g_post_ffn, w_in, w_out,
           sink_a, qn_c, kn_c, gm_ln_g, gm_ln_b, gm_ws, gm_bs, ffn_w_gate, ffn_w_up, ffn_w_down,
           moe_router, moe_w_gate, moe_w_up, moe_w_down):
    n_batch, seq, d = x.shape
    n_ctx = ctx.shape[1]
    depth = w_mod.shape[0]
    n_lat = n_batch * seq
    n_ctx_rows = n_batch * n_ctx
    t = n_lat + n_ctx_rows
    a_heads = sink_a.shape[1]
    a_w = a_heads * HEAD_DIM
    akv_w = A_KV_HEADS * HEAD_DIM
    b_w = gm_ln_g.shape[1]
    ckv_w = C_KV_HEADS * HEAD_DIM
    c_w = w_in.shape[2] - a_w - 2 * akv_w - 2 * b_w - 2 * ckv_w
    c_heads = c_w // HEAD_DIM
    assert n_ctx_rows % EW_ROWS == 0 and seq % EW_ROWS == 0 and n_batch + 1 <= 8
    assert a_w == c_w and akv_w == COL_TILE and ckv_w == COL_TILE

    x_parts = (x.reshape(n_lat, d), ctx.reshape(n_ctx_rows, d))
    c8 = jnp.concatenate([c, c_ctx[None, :], jnp.zeros((8 - n_batch - 1, d), F32)], axis=0)
    mod_all = _modulation(c8, w_mod, b_mod).reshape(depth, 8, N_MOD, d)

    cos, sin = _rope_tables(seq, n_batch, n_ctx_rows)
    geo = dict(seq=seq, n_lat_rows=n_lat, n_batch=n_batch)
    h = _prenorm(x_parts, g_pre_mix[0], mod_all[0], **geo)

    for l in range(depth):
        need_ctx = l < depth - 1
        n_rows = t if need_ctx else n_lat
        mod = mod_all[l]

        gq, gk = qn_c[l].reshape(1, HEAD_DIM), kn_c[l].reshape(1, HEAD_DIM)
        rope = (cos, sin, gq, gk)
        col = 0
        za = _inproj(h, w_in, l, col, a_w + akv_w, "rope", rope, n_q_tiles=a_w // COL_TILE)
        col += a_w + akv_w
        av = _inproj(h, w_in, l, col, akv_w, "plain")
        col += akv_w
        zb = _inproj(h, w_in, l, col, 2 * b_w, "gelu")
        col += 2 * b_w
        zc = _inproj(h, w_in, l, col, c_w + ckv_w, "normrope", rope, n_q_tiles=c_w // COL_TILE)
        col += c_w + ckv_w
        cv = _inproj(h, w_in, l, col, ckv_w, "plain")

        oa = _window_attn(za, av, sink_a[l], n_batch, seq, n_ctx, need_ctx)
        ob = _gmlp(zb, gm_ln_g[l], gm_ln_b[l], gm_ws[l], gm_bs[l], n_rows)
        oc = _global_attn(zc, cv, n_batch, seq, n_ctx, c_heads, need_ctx)

        y = _outproj(oa, ob, oc, w_out, l, n_rows)
        i = l // 2
        next_norm = (g_pre_mix[l + 1], mod_all[l + 1]) if need_ctx else None
        if l % 2 == 0:
            x1, h2 = _postmix(x_parts, y, g_post_mix[l], g_pre_ffn[l], mod, n_rows, 0, **geo)
            f_dim = ffn_w_gate.shape[-1]
            hmid = _ffn_up(h2, ffn_w_gate, ffn_w_up, (i,), n_rows)
            half = f_dim // 2
            f = _ffn_down(hmid, ffn_w_down, (i,), 0, half, n_rows)
            f = _ffn_down(hmid, ffn_w_down, (i,), half, half, n_rows, prev=f)
            xall = _final(x1, f, g_post_ffn[l], mod, n_rows, next_norm, **geo)
            if need_ctx:
                xall, h = xall
        else:
            x1, h2, h2f = _postmix(x_parts, y, g_post_mix[l], g_pre_ffn[l], mod, n_rows, 1, **geo)
            route = _router(h2, moe_router[i], n_rows)
            tok_of_row, dest, tiles = _route_plan(route, moe_router.shape[-1])
            xs = _gather_rows(h2f, tok_of_row)
            hmid = _moe_up(xs, moe_w_gate, moe_w_up, i, tiles)
            ys = _moe_down(hmid, moe_w_down, i, tiles)
            xall = _moe_final(x1, ys, dest, route, g_post_ffn[l], mod, n_rows, **geo)
            if need_ctx:
                h = _prenorm((xall,), next_norm[0], next_norm[1], **geo)
        x_parts = (xall,)
    return xall[:n_lat].reshape(n_batch, seq, d)
```

```python
import functools
import math

import jax
import jax.numpy as jnp
from jax import lax
from jax.experimental import pallas as pl
from jax.experimental.pallas import tpu as pltpu

F32 = jnp.float32
BF16 = jnp.bfloat16

GRID_W = 64
HEAD_DIM = 128
BLOCK = 128
WINDOW = 128
A_KV_HEADS = 4
C_KV_HEADS = 4
B_GROUPS = 8
CHUNK = 128
N_MOD = 6
TOP_K = 2
ROPE_THETA = 10000.0
EPS = 1e-6
LOG2E = math.log2(math.e)

V7X_LANES = 128
V7X_VMEM_SCOPED_CAP = 60000 * 1024

ROW_TILE = 512
MAX_ROW_TILE = 1088
ROW_ALIGN = 16
COL_TILE = 512
FFN_COL_TILE = 256
EW_ROWS = 256
CAST_ROWS = 256
Q_TILE = 256
K_TILE = 512
GATHER_UNROLL = 8
MOE_ROW_TILE = 1024
MOE_ROW_STEP = 256
ROUTER_PIECES = 3
EPILOGUE_CHUNKS = 4
SLAB_ROWS = 8


def _nbytes(shape, dtype):
    return math.prod(shape) * jnp.dtype(dtype).itemsize


def _params(semantics, blocks, scratch=(), temps=()):
    need = 2 * sum(_nbytes(s, d) for s, d in blocks)
    need += sum(_nbytes(s, d) for s, d in scratch)
    need += sum(_nbytes(s, d) for s, d in temps)
    limit = min(V7X_VMEM_SCOPED_CAP, max(need + need // 4, 16 * 1024 * 1024))
    return pltpu.CompilerParams(dimension_semantics=semantics, vmem_limit_bytes=limit)


def _row_tile(n_rows):
    for tm in range(MAX_ROW_TILE - MAX_ROW_TILE % ROW_ALIGN, 0, -ROW_ALIGN):
        if n_rows % tm == 0:
            return tm
    raise ValueError(f"no row tile for {n_rows} rows")


def _cast_weight(w_ref, wbf_ref):
    rows = w_ref.shape[0]
    step = CAST_ROWS if rows % CAST_ROWS == 0 else V7X_LANES
    assert rows % step == 0

    def body(r, carry):
        sl = pl.ds(pl.multiple_of(r * step, step), step)
        wbf_ref[sl, :] = w_ref[sl, :].astype(BF16)
        return carry

    lax.fori_loop(0, rows // step, body, 0)


def _dot(a, b):
    return jnp.dot(a, b, preferred_element_type=F32)


def _dot_nt(a, b):
    return lax.dot_general(a, b, (((1,), (1,)), ((), ())), preferred_element_type=F32)


def _rms(x, gain):
    return x * lax.rsqrt(jnp.mean(x * x, axis=-1, keepdims=True) + EPS) * gain


def _to_slabs(x, slab_ref):
    w = slab_ref.shape[2]
    for s in range(slab_ref.shape[1]):
        slab_ref[:, s, :] = x[:, s * w:(s + 1) * w]


def _from_slabs(slab_ref):
    return jnp.concatenate([slab_ref[:, s, :] for s in range(slab_ref.shape[1])], axis=1)


def _mod_kernel(c_ref, w_ref, b_ref, o_ref):
    s = jax.nn.silu(c_ref[...])
    o_ref[...] = _dot(s.astype(BF16), w_ref[...].astype(BF16)) + b_ref[...]


def _modulation(c8, w_mod, b_mod):
    n_layers, d, width = w_mod.shape
    tn = COL_TILE
    assert width % tn == 0
    blocks = [((8, d), F32), ((d, tn), F32), ((1, tn), F32), ((8, tn), F32)]
    return pl.pallas_call(
        _mod_kernel,
        out_shape=jax.ShapeDtypeStruct((n_layers, 8, width), F32),
        grid=(n_layers, width // tn),
        in_specs=[pl.BlockSpec((8, d), lambda l, j: (0, 0)),
                  pl.BlockSpec((None, d, tn), lambda l, j: (l, 0, j)),
                  pl.BlockSpec((None, 1, tn), lambda l, j: (l, 0, j))],
        out_specs=pl.BlockSpec((None, 8, tn), lambda l, j: (l, 0, j)),
        compiler_params=_params(("arbitrary", "arbitrary"), blocks, temps=[((d, tn), BF16)]),
        name="modulation",
    )(c8, w_mod, b_mod.reshape(n_layers, 1, width))


def _mod_row_index(i, rows_per_tile, seq, n_lat_rows, n_batch):
    lat_tiles = n_lat_rows // rows_per_tile
    return jnp.where(i < lat_tiles, (i * rows_per_tile) // seq, n_batch)


def _row_parts(parts, tr):
    specs, starts, start = [], [], 0
    for p in parts:
        n = p.shape[0] // tr
        assert n * tr == p.shape[0]
        specs.append(pl.BlockSpec((tr, p.shape[1]),
                                  lambda i, *_, start=start, n=n: (jnp.clip(i - start, 0, n - 1), 0)))
        starts.append(start)
        start += n
    return specs, tuple(starts)


def _pick_rows(i, x_refs, starts):
    x = x_refs[0][...]
    for ref, start in zip(x_refs[1:], starts[1:]):
        x = jnp.where(i >= start, ref[...], x)
    return x


def _pre_mix_norm(x, g_ref, mod_ref):
    return (_rms(x, g_ref[...]) * (1.0 + mod_ref[1:2, :]) + mod_ref[0:1, :]).astype(BF16)


def _prenorm_kernel(*refs, starts):
    *x_refs, g_ref, mod_ref, o_ref = refs
    o_ref[...] = _pre_mix_norm(_pick_rows(pl.program_id(0), x_refs, starts), g_ref, mod_ref)


def _prenorm(x_parts, gain, mod, seq, n_lat_rows, n_batch):
    d = x_parts[0].shape[1]
    t = sum(p.shape[0] for p in x_parts)
    tr = EW_ROWS
    ridx = functools.partial(_mod_row_index, rows_per_tile=tr, seq=seq, n_lat_rows=n_lat_rows,
                             n_batch=n_batch)
    x_specs, starts = _row_parts(x_parts, tr)
    blocks = [((tr, d), F32)] * len(x_parts) + [((1, d), F32), ((N_MOD, d), F32), ((tr, d), BF16)]
    return pl.pallas_call(
        functools.partial(_prenorm_kernel, starts=starts),
        out_shape=jax.ShapeDtypeStruct((t, d), BF16),
        grid=(t // tr,),
        in_specs=x_specs + [pl.BlockSpec((1, d), lambda i: (0, 0)),
                            pl.BlockSpec((None, N_MOD, d), lambda i: (ridx(i), 0, 0))],
        out_specs=pl.BlockSpec((tr, d), lambda i: (i, 0)),
        compiler_params=_params(("arbitrary",), blocks, temps=[((tr, d), F32)] * 2),
        name="prenorm",
    )(*x_parts, gain.reshape(1, d), mod)


def _postmix_kernel(*refs, starts, n_parts):
    x_refs = refs[:n_parts]
    y_ref, gpost_ref, gpre_ref, mod_ref, x1_ref, h2_ref, *h2f_ref = refs[n_parts:]
    x = _pick_rows(pl.program_id(0), x_refs, starts)
    x1 = x + mod_ref[2:3, :] * _rms(y_ref[...].astype(F32), gpost_ref[...])
    x1_ref[...] = x1
    h2 = _rms(x1, gpre_ref[...]) * (1.0 + mod_ref[4:5, :]) + mod_ref[3:4, :]
    h2_ref[...] = h2.astype(BF16)
    for ref in h2f_ref:
        _to_slabs(h2, ref)


def _postmix(x_parts, y, g_post, g_pre, mod, n_rows, want_slabs, seq, n_lat_rows, n_batch):
    d = x_parts[0].shape[1]
    tr = EW_ROWS
    x_specs, starts = _row_parts(x_parts, tr)
    assert d % (SLAB_ROWS * V7X_LANES) == 0
    slab = (SLAB_ROWS, d // SLAB_ROWS)
    ridx = functools.partial(_mod_row_index, rows_per_tile=tr, seq=seq, n_lat_rows=n_lat_rows,
                             n_batch=n_batch)
    row = pl.BlockSpec((tr, d), lambda i: (i, 0))
    vec = pl.BlockSpec((1, d), lambda i: (0, 0))
    blocks = ([((tr, d), F32)] * (len(x_parts) + 1 + want_slabs)
              + [((tr, d), BF16)] * 2 + [((N_MOD, d), F32)])
    out_shape = [jax.ShapeDtypeStruct((n_rows, d), F32), jax.ShapeDtypeStruct((n_rows, d), BF16)]
    out_specs = [row, row]
    if want_slabs:
        out_shape.append(jax.ShapeDtypeStruct((n_rows,) + slab, F32))
        out_specs.append(pl.BlockSpec((tr,) + slab, lambda i: (i, 0, 0)))
    return pl.pallas_call(
        functools.partial(_postmix_kernel, starts=starts, n_parts=len(x_parts)),
        out_shape=tuple(out_shape),
        grid=(n_rows // tr,),
        in_specs=x_specs + [row, vec, vec,
                            pl.BlockSpec((None, N_MOD, d), lambda i: (ridx(i), 0, 0))],
        out_specs=tuple(out_specs),
        compiler_params=_params(("arbitrary",), blocks, temps=[((tr, d), F32)] * 3),
        name="postmix",
    )(*x_parts, y, g_post.reshape(1, d), g_pre.reshape(1, d), mod)


def _final_kernel(x_ref, f_ref, g_ref, mod_ref, *rest):
    x = x_ref[...] + mod_ref[5:6, :] * _rms(f_ref[...], g_ref[...])
    if len(rest) == 1:
        (o_ref,) = rest
    else:
        gnext_ref, modnext_ref, o_ref, h_ref = rest
        h_ref[...] = _pre_mix_norm(x, gnext_ref, modnext_ref)
    o_ref[...] = x


def _final(x1, f, g_post, mod, n_rows, next_norm, seq, n_lat_rows, n_batch):
    d = x1.shape[1]
    tr = EW_ROWS
    ridx = functools.partial(_mod_row_index, rows_per_tile=tr, seq=seq, n_lat_rows=n_lat_rows,
                             n_batch=n_batch)
    row = pl.BlockSpec((tr, d), lambda i: (i, 0))
    vec = pl.BlockSpec((1, d), lambda i: (0, 0))
    modspec = pl.BlockSpec((None, N_MOD, d), lambda i: (ridx(i), 0, 0))
    blocks = [((tr, d), F32)] * 3 + [((N_MOD, d), F32)]
    in_specs, args = [row, row, vec, modspec], [x1, f, g_post.reshape(1, d), mod]
    out_shape, out_specs = jax.ShapeDtypeStruct((n_rows, d), F32), row
    if next_norm is not None:
        in_specs += [vec, modspec]
        args += [next_norm[0].reshape(1, d), next_norm[1]]
        out_shape = (out_shape, jax.ShapeDtypeStruct((n_rows, d), BF16))
        out_specs = (row, row)
        blocks += [((N_MOD, d), F32), ((tr, d), BF16)]
    return pl.pallas_call(
        _final_kernel,
        out_shape=out_shape,
        grid=(n_rows // tr,),
        in_specs=in_specs,
        out_specs=out_specs,
        compiler_params=_params(("arbitrary",), blocks, temps=[((tr, d), F32)] * 2),
        name="final_residual",
    )(*args)


def _swap_pairs(x):
    lane = lax.broadcasted_iota(jnp.int32, x.shape, 1)
    quarter = HEAD_DIM // 4
    first = (lane & quarter) == 0
    return jnp.where(first, pltpu.roll(x, HEAD_DIM - quarter, axis=1), pltpu.roll(x, quarter, axis=1))


def _inproj_kernel(*refs, mode, n_q_tiles, q_scale):
    if mode in ("rope", "normrope"):
        h_ref, w_ref, cos_ref, sin_ref, gq_ref, gk_ref, o_ref, wbf_ref = refs
    else:
        h_ref, w_ref, o_ref, wbf_ref = refs
    j = pl.program_id(0)

    @pl.when(pl.program_id(1) == 0)
    def _():
        _cast_weight(w_ref, wbf_ref)

    tm = h_ref.shape[0]
    n_chunks = EPILOGUE_CHUNKS if tm % (EPILOGUE_CHUNKS * ROW_ALIGN) == 0 else 1
    rc = tm // n_chunks
    for c in range(n_chunks):
        rows = slice(c * rc, (c + 1) * rc)
        acc = _dot(h_ref[rows, :], wbf_ref[...])
        if mode == "plain":
            o_ref[rows, :] = acc.astype(BF16)
        elif mode == "gelu":
            o_ref[rows, :] = (0.5 * acc * (1.0 + lax.erf(acc * (2.0 ** -0.5)))).astype(BF16)
        else:
            is_q = j < n_q_tiles
            scale = jnp.where(is_q, q_scale, 1.0).astype(F32)
            cos = cos_ref[rows, :]
            sin = sin_ref[rows, :]
            gain = jnp.where(is_q, gq_ref[...], gk_ref[...])
            for hh in range(acc.shape[1] // HEAD_DIM):
                sl = slice(hh * HEAD_DIM, (hh + 1) * HEAD_DIM)
                xh = acc[:, sl]
                if mode == "normrope":
                    xh = _rms(xh, gain)
                xh = xh * cos + _swap_pairs(xh) * sin
                o_ref[rows, sl] = (xh * scale).astype(BF16)


def _inproj(h, w_in, layer, col_start, width, mode, rope=None, n_q_tiles=0):
    t, d = h.shape
    tm, tn = _row_tile(t), COL_TILE
    assert col_start % tn == 0 and width % tn == 0
    j0 = col_start // tn
    in_specs = [pl.BlockSpec((tm, d), lambda j, i: (i, 0)),
                pl.BlockSpec((None, d, tn), lambda j, i: (layer, 0, j0 + j))]
    args = [h, w_in]
    blocks = [((tm, d), BF16), ((d, tn), F32), ((tm, tn), BF16)]
    if mode in ("rope", "normrope"):
        cos, sin, gq, gk = rope
        tab = pl.BlockSpec((tm, HEAD_DIM), lambda j, i: (i, 0))
        vec = pl.BlockSpec((1, HEAD_DIM), lambda j, i: (0, 0))
        in_specs += [tab, tab, vec, vec]
        args += [cos, sin, gq, gk]
        blocks += [((tm, HEAD_DIM), F32)] * 2
    kern = functools.partial(_inproj_kernel, mode=mode, n_q_tiles=n_q_tiles,
                             q_scale=HEAD_DIM ** -0.5 * LOG2E)
    return pl.pallas_call(
        kern,
        out_shape=jax.ShapeDtypeStruct((t, width), BF16),
        grid=(width // tn, t // tm),
        in_specs=in_specs,
        out_specs=pl.BlockSpec((tm, tn), lambda j, i: (i, j)),
        scratch_shapes=[pltpu.VMEM((d, tn), BF16)],
        compiler_params=_params(("arbitrary", "arbitrary"), blocks, scratch=[((d, tn), BF16)],
                                temps=[((tm, tn), F32)] * 2),
        name="inproj_" + mode,
    )(*args)


def _window_bias(group, n_ctx):
    r = jnp.arange(group * BLOCK)[:, None] % BLOCK
    c = jnp.arange(3 * BLOCK + n_ctx)[None, :]
    in_prev, in_cur = c < BLOCK, (c >= BLOCK) & (c < 2 * BLOCK)
    in_next = (c >= 2 * BLOCK) & (c < 3 * BLOCK)
    band_prev = in_prev & (c < r)
    band_next = in_next & (c - 2 * BLOCK > r)
    hidden = [band_prev | band_next, in_prev | band_next, band_prev | in_next, in_prev | in_next,
              in_prev | in_cur | in_next]
    shape = (group * BLOCK, 3 * BLOCK + n_ctx)
    return jnp.stack([jnp.where(jnp.broadcast_to(h, shape), -jnp.inf, 0.0).astype(F32)
                      for h in hidden])


def _window_attn_kernel(sink_ref, bias_ref, q_ref, kp_ref, kc_ref, kn_ref, kx_ref, vp_ref, vc_ref,
                        vn_ref, vx_ref, o_ref, *, group):
    rows = group * BLOCK
    assert BLOCK & (BLOCK - 1) == 0
    shift = BLOCK.bit_length() - 1
    bias = bias_ref[...]
    rid = lax.broadcasted_iota(jnp.int32, (rows, 1), 0) >> shift
    for hk in range(A_KV_HEADS):
        ksl = slice(hk * HEAD_DIM, (hk + 1) * HEAD_DIM)
        k_all = jnp.concatenate([kp_ref[:, ksl], kc_ref[:, ksl], kn_ref[:, ksl], kx_ref[:, ksl]],
                                axis=0)
        v_all = jnp.concatenate([vp_ref[:, ksl], vc_ref[:, ksl], vn_ref[:, ksl], vx_ref[:, ksl]],
                                axis=0)
        q3 = jnp.concatenate(
            [q_ref[:, (hk * group + g) * HEAD_DIM:(hk * group + g + 1) * HEAD_DIM]
             for g in range(group)], axis=0)
        sink = jnp.zeros((rows, 1), F32)
        for g in range(group):
            sink = jnp.where(rid == g, sink_ref[hk * group + g] * LOG2E, sink)
        s = _dot_nt(q3, k_all) + bias
        m = jnp.maximum(jnp.max(s, axis=-1, keepdims=True), sink)
        p = jnp.exp2(s - m)
        denom = jnp.sum(p, axis=-1, keepdims=True) + jnp.exp2(sink - m)
        o = _dot(p.astype(BF16), v_all) / denom
        for g in range(group):
            osl = slice((hk * group + g) * HEAD_DIM, (hk * group + g + 1) * HEAD_DIM)
            o_ref[:, osl] = o[g * BLOCK:(g + 1) * BLOCK, :].astype(BF16)


def _window_attn(za, av, sink, n_batch, seq, n_ctx, with_ctx):
    n_heads = sink.shape[0]
    group = n_heads // A_KV_HEADS
    qw = n_heads * HEAD_DIM
    kw = A_KV_HEADS * HEAD_DIM
    assert qw % kw == 0 and seq % BLOCK == 0 and (n_batch * seq) % n_ctx == 0 and WINDOW == BLOCK
    assert n_ctx % BLOCK == 0
    nb = seq // BLOCK
    ncb = n_ctx // BLOCK if with_ctx else 0
    kcol = qw // kw
    ctx0 = (n_batch * seq) // n_ctx

    def qblk(b, n):
        return (jnp.where(n < nb, b * nb + n, n_batch * nb + b * ncb + n - nb), 0)

    def blk(shift):
        return lambda b, n: (b * nb + jnp.clip(n + shift, 0, nb - 1), kcol)

    def vblk(shift):
        return lambda b, n: (b * nb + jnp.clip(n + shift, 0, nb - 1), 0)

    def bias_variant(b, n):
        lat = jnp.where(n == 0, 1, 0) + jnp.where(n == nb - 1, 2, 0)
        return (jnp.where(n < nb, lat, 4), 0, 0)

    kspec = [pl.BlockSpec((BLOCK, kw), blk(s)) for s in (-1, 0, 1)]
    vspec = [pl.BlockSpec((BLOCK, kw), vblk(s)) for s in (-1, 0, 1)]
    n_keys = 3 * BLOCK + n_ctx
    rows = group * BLOCK
    blocks = ([((BLOCK, qw), BF16)] * 2 + [((BLOCK, kw), BF16)] * 6 + [((n_ctx, kw), BF16)] * 2
              + [((rows, n_keys), F32)])
    return pl.pallas_call(
        functools.partial(_window_attn_kernel, group=group),
        out_shape=jax.ShapeDtypeStruct((n_batch * (nb + ncb) * BLOCK, qw), BF16),
        grid=(n_batch, nb + ncb),
        in_specs=[pl.BlockSpec(memory_space=pltpu.SMEM),
                  pl.BlockSpec((None, rows, n_keys), bias_variant),
                  pl.BlockSpec((BLOCK, qw), qblk)]
                 + kspec + [pl.BlockSpec((n_ctx, kw), lambda b, n: (ctx0 + b, kcol))]
                 + vspec + [pl.BlockSpec((n_ctx, kw), lambda b, n: (ctx0 + b, 0))],
        out_specs=pl.BlockSpec((BLOCK, qw), qblk),
        compiler_params=_params(("arbitrary", "arbitrary"), blocks,
                                temps=[((rows, n_keys), F32)] * 4),
        name="window_attn",
    )(sink, _window_bias(group, n_ctx), za, za, za, za, za, av, av, av, av)


def _gmlp_kernel(z_ref, g_ref, b_ref, ws_ref, bst_ref, o_ref):
    width = g_ref.shape[1]
    u = z_ref[:, :width].astype(F32)
    v = z_ref[:, width:].astype(F32)
    mu = jnp.mean(v, axis=-1, keepdims=True)
    vc = v - mu
    var = jnp.mean(vc * vc, axis=-1, keepdims=True)
    vn = (vc * lax.rsqrt(var + EPS) * g_ref[...] + b_ref[...]).astype(BF16)
    gd = width // B_GROUPS
    for g in range(B_GROUPS):
        sl = slice(g * gd, (g + 1) * gd)
        mixed = _dot(ws_ref[g].astype(BF16), vn[:, sl]) + bst_ref[:, g:g + 1]
        o_ref[:, sl] = (u[:, sl] * mixed).astype(BF16)


def _gmlp(zb, ln_g, ln_b, ws, bs, n_rows):
    w2 = zb.shape[1]
    width = w2 // 2
    blocks = [((CHUNK, w2), BF16), ((1, width), F32), ((1, width), F32),
              (ws.shape, F32), ((CHUNK, B_GROUPS), F32), ((CHUNK, width), BF16)]
    return pl.pallas_call(
        _gmlp_kernel,
        out_shape=jax.ShapeDtypeStruct((n_rows, width), BF16),
        grid=(n_rows // CHUNK,),
        in_specs=[pl.BlockSpec((CHUNK, w2), lambda i: (i, 0)),
                  pl.BlockSpec((1, width), lambda i: (0, 0)),
                  pl.BlockSpec((1, width), lambda i: (0, 0)),
                  pl.BlockSpec(ws.shape, lambda i: (0, 0, 0)),
                  pl.BlockSpec((CHUNK, B_GROUPS), lambda i: (0, 0))],
        out_specs=pl.BlockSpec((CHUNK, width), lambda i: (i, 0)),
        compiler_params=_params(("arbitrary",), blocks, temps=[((CHUNK, w2), F32)] * 2),
        name="chunk_gmlp",
    )(zb, ln_g.reshape(1, width), ln_b.reshape(1, width), ws, bs.T)


def _global_attn_kernel(q_ref, kx_ref, vx_ref, k_ref, v_ref, o_ref, vext_ref, *, group,
                        n_lat_tiles, has_ctx_tiles):
    tq = q_ref.shape[0]
    n_ctx = kx_ref.shape[0]
    n_lat = k_ref.shape[0]
    rows = group * tq

    @pl.when(pl.program_id(2) == 0)
    def _():
        vext_ref[0:n_ctx, 0:HEAD_DIM] = vx_ref[...]
        vext_ref[n_ctx:, 0:HEAD_DIM] = v_ref[...]
        vext_ref[:, HEAD_DIM:] = jnp.ones((n_ctx + n_lat, HEAD_DIM), BF16)

    def attend(chunks):
        q3 = jnp.concatenate([q_ref[:, g * HEAD_DIM:(g + 1) * HEAD_DIM] for g in range(group)],
                             axis=0)
        m = jnp.full((rows, HEAD_DIM), -jnp.inf, F32)
        acc = jnp.zeros((rows, 2 * HEAD_DIM), F32)
        for k_chunk, v0, nk in chunks:
            s = _dot_nt(q3, k_chunk())
            m_new = jnp.maximum(m, jnp.broadcast_to(jnp.max(s, axis=-1, keepdims=True), m.shape))
            alpha = jnp.exp2(m - m_new)
            p = jnp.exp2(s - jnp.tile(m_new, (1, nk // HEAD_DIM)))
            acc = jnp.tile(alpha, (1, 2)) * acc + _dot(p.astype(BF16), vext_ref[v0:v0 + nk, :])
            m = m_new
        o = acc[:, :HEAD_DIM] / acc[:, HEAD_DIM:]
        for g in range(group):
            o_ref[:, g * HEAD_DIM:(g + 1) * HEAD_DIM] = o[g * tq:(g + 1) * tq, :].astype(BF16)

    ctx_chunk = [(lambda: kx_ref[...], 0, n_ctx)]
    lat_chunks = [(lambda c=c: k_ref[c:c + K_TILE, :], n_ctx + c, K_TILE)
                  for c in range(0, n_lat, K_TILE)]
    is_lat = pl.program_id(2) < n_lat_tiles
    pl.when(is_lat)(lambda: attend(ctx_chunk + lat_chunks))
    if has_ctx_tiles:
        pl.when(jnp.logical_not(is_lat))(lambda: attend(ctx_chunk))


def _global_attn(zc, cv, n_batch, seq, n_ctx, n_heads, with_ctx):
    group = n_heads // C_KV_HEADS
    gw = group * HEAD_DIM
    tq = Q_TILE
    assert seq % tq == 0 and seq % K_TILE == 0 and (n_batch * seq) % n_ctx == 0 and n_ctx % tq == 0
    nq = seq // tq
    ncq = n_ctx // tq if with_ctx else 0
    ctx0 = (n_batch * seq) // n_ctx
    rows = group * tq

    def qblk(b, hk, i):
        return (jnp.where(i < nq, b * nq + i, n_batch * nq + b * ncq + i - nq), hk)

    blocks = [((tq, gw), BF16)] * 2 + [((n_ctx, HEAD_DIM), BF16)] * 2 + [((seq, HEAD_DIM), BF16)] * 2
    scratch = [((n_ctx + seq, 2 * HEAD_DIM), BF16)]
    return pl.pallas_call(
        functools.partial(_global_attn_kernel, group=group, n_lat_tiles=nq, has_ctx_tiles=ncq > 0),
        out_shape=jax.ShapeDtypeStruct((n_batch * (nq + ncq) * tq, n_heads * HEAD_DIM), BF16),
        grid=(n_batch, C_KV_HEADS, nq + ncq),
        in_specs=[pl.BlockSpec((tq, gw), qblk),
                  pl.BlockSpec((n_ctx, HEAD_DIM), lambda b, hk, i: (ctx0 + b, n_heads + hk)),
                  pl.BlockSpec((n_ctx, HEAD_DIM), lambda b, hk, i: (ctx0 + b, hk)),
                  pl.BlockSpec((seq, HEAD_DIM), lambda b, hk, i: (b, n_heads + hk)),
                  pl.BlockSpec((seq, HEAD_DIM), lambda b, hk, i: (b, hk))],
        out_specs=pl.BlockSpec((tq, gw), qblk),
        scratch_shapes=[pltpu.VMEM(s, d) for s, d in scratch],
        compiler_params=_params(("arbitrary", "arbitrary", "arbitrary"), blocks, scratch=scratch,
                                temps=[((rows, K_TILE), F32)] * 6),
        name="global_attn",
    )(zc, zc, cv, zc, cv)


def _outproj_kernel(a_ref, b_ref, c_ref, w_ref, o_ref, wbf_ref):
    @pl.when(pl.program_id(1) == 0)
    def _():
        _cast_weight(w_ref, wbf_ref)

    ka, kb = a_ref.shape[1], b_ref.shape[1]
    acc = _dot(a_ref[...], wbf_ref[0:ka, :])
    acc += _dot(b_ref[...], wbf_ref[ka:ka + kb, :])
    acc += _dot(c_ref[...], wbf_ref[ka + kb:, :])
    o_ref[...] = acc.astype(BF16)


def _outproj(oa, ob, oc, w_out, layer, n_rows):
    t = oa.shape[0]
    _, k, d = w_out.shape
    tm, tn = _row_tile(n_rows), COL_TILE
    assert oa.shape[1] + ob.shape[1] + oc.shape[1] == k and d % tn == 0
    blocks = [((tm, k), BF16), ((k, tn), F32), ((tm, tn), BF16)]
    return pl.pallas_call(
        _outproj_kernel,
        out_shape=jax.ShapeDtypeStruct((t, d), BF16),
        grid=(d // tn, n_rows // tm),
        in_specs=[pl.BlockSpec((tm, oa.shape[1]), lambda j, i: (i, 0)),
                  pl.BlockSpec((tm, ob.shape[1]), lambda j, i: (i, 0)),
                  pl.BlockSpec((tm, oc.shape[1]), lambda j, i: (i, 0)),
                  pl.BlockSpec((None, k, tn), lambda j, i: (layer, 0, j))],
        out_specs=pl.BlockSpec((tm, tn), lambda j, i: (i, j)),
        scratch_shapes=[pltpu.VMEM((k, tn), BF16)],
        compiler_params=_params(("arbitrary", "arbitrary"), blocks, scratch=[((k, tn), BF16)],
                                temps=[((tm, tn), F32)] * 2),
        name="outproj",
    )(oa, ob, oc, w_out)


def _ffn_up_kernel(h_ref, wg_ref, wu_ref, o_ref, wgbf_ref, wubf_ref):
    @pl.when(pl.program_id(1) == 0)
    def _():
        _cast_weight(wg_ref, wgbf_ref)
        _cast_weight(wu_ref, wubf_ref)

    h = h_ref[...]
    gate = _dot(h, wgbf_ref[...])
    up = _dot(h, wubf_ref[...])
    o_ref[...] = (jax.nn.silu(gate) * up).astype(BF16)


def _ffn_up(h2, w_gate, w_up, sel, n_rows):
    t, d = h2.shape
    f = w_gate.shape[-1]
    tm, tn = _row_tile(n_rows), FFN_COL_TILE
    assert f % tn == 0
    lead = (None,) * len(sel)
    wspec = pl.BlockSpec(lead + (d, tn), lambda j, i: sel + (0, j))
    blocks = [((tm, d), BF16), ((d, tn), F32), ((d, tn), F32), ((tm, tn), BF16)]
    return pl.pallas_call(
        _ffn_up_kernel,
        out_shape=jax.ShapeDtypeStruct((t, f), BF16),
        grid=(f // tn, n_rows // tm),
        in_specs=[pl.BlockSpec((tm, d), lambda j, i: (i, 0)), wspec, wspec],
        out_specs=pl.BlockSpec((tm, tn), lambda j, i: (i, j)),
        scratch_shapes=[pltpu.VMEM((d, tn), BF16)] * 2,
        compiler_params=_params(("arbitrary", "arbitrary"), blocks, scratch=[((d, tn), BF16)] * 2,
                                temps=[((tm, tn), F32)] * 3),
        name="ffn_up",
    )(h2, w_gate, w_up)


def _ffn_down_kernel(*refs, has_prev):
    if has_prev:
        a_ref, w_ref, prev_ref, o_ref, wbf_ref = refs
    else:
        a_ref, w_ref, o_ref, wbf_ref = refs

    @pl.when(pl.program_id(1) == 0)
    def _():
        _cast_weight(w_ref, wbf_ref)

    acc = _dot(a_ref[...], wbf_ref[...])
    if has_prev:
        acc = prev_ref[...] + acc
    o_ref[...] = acc


def _ffn_down(a, w_down, sel, k_start, k_size, n_rows, prev=None):
    t = a.shape[0]
    d = w_down.shape[-1]
    tm, tn = ROW_TILE, COL_TILE
    assert k_start % k_size == 0 and d % tn == 0 and n_rows % tm == 0
    kb = k_start // k_size
    lead = (None,) * len(sel)
    in_specs = [pl.BlockSpec((tm, k_size), lambda j, i: (i, kb)),
                pl.BlockSpec(lead + (k_size, tn), lambda j, i: sel + (kb, j))]
    args = [a, w_down]
    blocks = [((tm, k_size), BF16), ((k_size, tn), F32), ((tm, tn), F32)]
    aliases = {}
    if prev is not None:
        in_specs.append(pl.BlockSpec((tm, tn), lambda j, i: (i, j)))
        aliases = {len(args): 0}
        args.append(prev)
        blocks.append(((tm, tn), F32))
    return pl.pallas_call(
        functools.partial(_ffn_down_kernel, has_prev=prev is not None),
        out_shape=jax.ShapeDtypeStruct((t, d), F32),
        grid=(d // tn, n_rows // tm),
        in_specs=in_specs,
        out_specs=pl.BlockSpec((tm, tn), lambda j, i: (i, j)),
        scratch_shapes=[pltpu.VMEM((k_size, tn), BF16)],
        input_output_aliases=aliases,
        compiler_params=_params(("arbitrary", "arbitrary"), blocks, scratch=[((k_size, tn), BF16)],
                                temps=[((tm, tn), F32)] * 2),
        name="ffn_down",
    )(*args)


def _router_kernel(h_ref, w_ref, o_ref, *, n_experts):
    pieces = _dot(h_ref[...], w_ref[...])
    logits = pieces
    for k in range(1, ROUTER_PIECES):
        logits = logits + pltpu.roll(pieces, V7X_LANES - k * n_experts, axis=1)
    lane = lax.broadcasted_iota(jnp.int32, logits.shape, 1).astype(F32)
    neg = -jnp.inf
    logits = jnp.where(lane < n_experts, logits, neg)
    picked = []
    remaining = logits
    for _ in range(TOP_K):
        top = jnp.max(remaining, axis=-1, keepdims=True)
        idx = jnp.min(jnp.where(remaining == top, lane, float(logits.shape[1])), axis=-1,
                      keepdims=True)
        picked.append((top, idx))
        remaining = jnp.where(lane == idx, neg, remaining)
    top0 = picked[0][0]
    denom = sum(jnp.exp(tv - top0) for tv, _ in picked)
    route = jnp.zeros(logits.shape, F32)
    for k, (tv, idx) in enumerate(picked):
        route = jnp.where(lane == k, idx, route)
        route = jnp.where(lane == TOP_K + k, jnp.exp(tv - top0) / denom, route)
    o_ref[...] = route


def _router(h2, w_router, n_rows):
    t, d = h2.shape
    n_experts = w_router.shape[1]
    assert ROUTER_PIECES * n_experts <= V7X_LANES
    pieces, rest = [], w_router
    for _ in range(ROUTER_PIECES):
        pieces.append(rest.astype(BF16))
        rest = rest - pieces[-1].astype(F32)
    wpad = jnp.pad(jnp.concatenate(pieces, axis=1),
                   ((0, 0), (0, V7X_LANES - ROUTER_PIECES * n_experts)))
    tr = EW_ROWS
    blocks = [((tr, d), BF16), ((d, V7X_LANES), BF16), ((tr, V7X_LANES), F32)]
    return pl.pallas_call(
        functools.partial(_router_kernel, n_experts=n_experts),
        out_shape=jax.ShapeDtypeStruct((t, V7X_LANES), F32),
        grid=(n_rows // tr,),
        in_specs=[pl.BlockSpec((tr, d), lambda i: (i, 0)),
                  pl.BlockSpec((d, V7X_LANES), lambda i: (0, 0))],
        out_specs=pl.BlockSpec((tr, V7X_LANES), lambda i: (i, 0)),
        compiler_params=_params(("arbitrary",), blocks, temps=[((tr, d), F32)] * 3),
        name="moe_router",
    )(h2, wpad)


def _route_plan(route, n_experts):
    tile = MOE_ROW_TILE
    n_tok = route.shape[0]
    n_asg = n_tok * TOP_K
    e_flat = route[:, :TOP_K].astype(jnp.int32).reshape(n_asg)
    onehot = (e_flat[:, None] == jnp.arange(n_experts, dtype=jnp.int32)[None, :]).astype(jnp.int32)
    csum = jnp.cumsum(onehot, axis=0)
    rank = jnp.take_along_axis(csum, e_flat[:, None], axis=1)[:, 0] - 1
    counts = csum[-1]
    tiles_per = (counts + tile - 1) // tile
    tile_end = jnp.cumsum(tiles_per)
    tile_start = tile_end - tiles_per
    dest = tile_start[e_flat] * tile + rank

    n_tiles = (n_asg + n_experts * (tile - 1)) // tile
    tile_ids = jnp.arange(n_tiles, dtype=jnp.int32)
    n_used = tile_end[-1]
    valid = tile_ids < n_used
    expert_raw = jnp.sum((tile_end[None, :] <= tile_ids[:, None]).astype(jnp.int32), axis=1)
    tile_expert = jnp.where(valid, expert_raw, expert_raw[n_used - 1])
    tile_first = valid & (tile_ids == tile_start[tile_expert])
    tile_row = jnp.where(valid, tile_ids, n_used - 1)

    tok_of_row = jnp.zeros((n_tiles * tile,), jnp.int32).at[dest].set(
        jnp.arange(n_asg, dtype=jnp.int32) // TOP_K, unique_indices=True)
    tile_first = tile_first.astype(jnp.int32)
    tile_slot = (jnp.cumsum(tile_first) - 1) % 2
    tile_fill = jnp.where(
        valid, jnp.clip(counts[tile_expert] - (tile_ids - tile_start[tile_expert]) * tile, 0, tile), 0)
    tiles = (tile_expert, tile_first, tile_fill, tile_row, tile_slot)
    assert tile % EW_ROWS == 0
    block_start = jnp.arange(n_tiles * tile // EW_ROWS, dtype=jnp.int32) * EW_ROWS
    live = (tile_fill[block_start // tile] > block_start % tile).astype(jnp.int32)
    return tok_of_row, live, dest, tiles


def _row_copy(src_hbm, dst_vmem, sem, src_row, dst_row):
    return pltpu.make_async_copy(src_hbm.at[pl.ds(src_row, 1)], dst_vmem.at[pl.ds(dst_row, 1)], sem)


def _gather_rows_kernel(idx_ref, live_ref, src_ref, o_ref, buf_ref, sem):
    tile = buf_ref.shape[0]
    g = pl.program_id(0)
    base = g * tile

    def issue(r, carry):
        _row_copy(src_ref, buf_ref, sem, idx_ref[base + r], r).start()
        return carry

    def drain(r, carry):
        _row_copy(src_ref, buf_ref, sem, 0, r).wait()
        return carry

    @pl.when(live_ref[g] == 1)
    def _():
        lax.fori_loop(0, tile, issue, 0, unroll=GATHER_UNROLL)
        lax.fori_loop(0, tile, drain, 0, unroll=GATHER_UNROLL)
        o_ref[...] = _from_slabs(buf_ref).astype(BF16)

    @pl.when(live_ref[g] == 0)
    def _():
        o_ref[...] = jnp.zeros(o_ref.shape, BF16)


def _gather_rows(src, idx, live):
    slab = src.shape[1:]
    d = slab[0] * slab[1]
    n = idx.shape[0]
    tg = EW_ROWS
    assert n % tg == 0 and live.shape[0] == n // tg
    blocks = [((tg, d), BF16)]
    return pl.pallas_call(
        _gather_rows_kernel,
        out_shape=jax.ShapeDtypeStruct((n, d), BF16),
        grid_spec=pltpu.PrefetchScalarGridSpec(
            num_scalar_prefetch=2,
            grid=(n // tg,),
            in_specs=[pl.BlockSpec(memory_space=pl.ANY)],
            out_specs=pl.BlockSpec((tg, d), lambda i, idx_ref, live_ref: (i, 0)),
            scratch_shapes=[pltpu.VMEM((tg,) + slab, F32), pltpu.SemaphoreType.DMA]),
        compiler_params=_params(("arbitrary",), blocks, scratch=[((tg, d), F32)],
                                temps=[((tg, d), F32)]),
        name="moe_dispatch",
    )(idx, live, src)


def _held_tile(s, n_tiles):
    return jnp.minimum(s, n_tiles - 1)


def _done_tile(s):
    return jnp.maximum(s - 1, 0)


def _grouped_step(tables, w_refs, wbf_refs, compute, o_ref):
    _, tf_ref, tn_ref, _, ts_ref = tables
    n_tiles = tf_ref.shape[0]
    tm = o_ref.shape[0]
    s = pl.program_id(1)
    held = _held_tile(s, n_tiles)
    done = _done_tile(s)
    fill = jnp.where(s > 0, tn_ref[done], -MOE_ROW_STEP)

    @pl.when((s < n_tiles) & (tf_ref[held] == 1))
    def _():
        for w_ref, wbf_ref in zip(w_refs, wbf_refs):
            _cast_weight(w_ref, wbf_ref.at[ts_ref[held]])

    for n in range(0, tm + 1, MOE_ROW_STEP):
        @pl.when((fill > n - MOE_ROW_STEP) & (fill <= n))
        def _(n=n):
            if n > 0:
                o_ref[0:n, :] = compute([wbf_ref[ts_ref[done]] for wbf_ref in wbf_refs], n)
            if n < tm:
                o_ref[n:, :] = jnp.zeros((tm - n, o_ref.shape[1]), o_ref.dtype)


def _moe_up_kernel(te_ref, tf_ref, tn_ref, tr_ref, ts_ref, x_ref, wg_ref, wu_ref, o_ref, wgbf_ref,
                   wubf_ref):
    def compute(w, n):
        x = x_ref[0:n, :]
        return (jax.nn.silu(_dot(x, w[0])) * _dot(x, w[1])).astype(BF16)

    _grouped_step((te_ref, tf_ref, tn_ref, tr_ref, ts_ref), (wg_ref, wu_ref), (wgbf_ref, wubf_ref),
                  compute, o_ref)


def _moe_up(xs, w_gate, w_up, layer, tiles):
    r, d = xs.shape
    f = w_gate.shape[-1]
    tm, tn = MOE_ROW_TILE, FFN_COL_TILE
    assert f % tn == 0 and r % tm == 0
    nt = r // tm
    wspec = pl.BlockSpec((None, None, d, tn),
                         lambda j, s, te, tf, tv, tr, ts: (layer, te[_held_tile(s, nt)], 0, j))
    blocks = [((tm, d), BF16), ((d, tn), F32), ((d, tn), F32), ((tm, tn), BF16)]
    return pl.pallas_call(
        _moe_up_kernel,
        out_shape=jax.ShapeDtypeStruct((r, f), BF16),
        grid_spec=pltpu.PrefetchScalarGridSpec(
            num_scalar_prefetch=5,
            grid=(f // tn, nt + 1),
            in_specs=[pl.BlockSpec((tm, d), lambda j, s, te, tf, tv, tr, ts: (tr[_done_tile(s)], 0)),
                      wspec, wspec],
            out_specs=pl.BlockSpec((tm, tn), lambda j, s, te, tf, tv, tr, ts: (_done_tile(s), j)),
            scratch_shapes=[pltpu.VMEM((2, d, tn), BF16)] * 2),
        compiler_params=_params(("arbitrary", "arbitrary"), blocks,
                                scratch=[((2, d, tn), BF16)] * 2, temps=[((tm, tn), F32)] * 3),
        name="moe_up",
    )(*tiles, xs, w_gate, w_up)


def _moe_down_kernel(te_ref, tf_ref, tn_ref, tr_ref, ts_ref, a_ref, w_ref, o_ref, wbf_ref):
    _grouped_step((te_ref, tf_ref, tn_ref, tr_ref, ts_ref), (w_ref,), (wbf_ref,),
                  lambda w, n: _dot(a_ref[0:n, :], w[0]), o_ref)


def _moe_down(hmid, w_down, layer, tiles):
    r, f = hmid.shape
    d = w_down.shape[-1]
    tm, tn = MOE_ROW_TILE, COL_TILE
    assert d % tn == 0 and r % tm == 0
    nt = r // tm
    blocks = [((tm, f), BF16), ((f, tn), F32), ((tm, tn), F32)]
    return pl.pallas_call(
        _moe_down_kernel,
        out_shape=jax.ShapeDtypeStruct((r, d), F32),
        grid_spec=pltpu.PrefetchScalarGridSpec(
            num_scalar_prefetch=5,
            grid=(d // tn, nt + 1),
            in_specs=[pl.BlockSpec((tm, f), lambda j, s, te, tf, tv, tr, ts: (tr[_done_tile(s)], 0)),
                      pl.BlockSpec((None, None, f, tn),
                                   lambda j, s, te, tf, tv, tr, ts:
                                   (layer, te[_held_tile(s, nt)], 0, j))],
            out_specs=pl.BlockSpec((tm, tn), lambda j, s, te, tf, tv, tr, ts: (_done_tile(s), j)),
            scratch_shapes=[pltpu.VMEM((2, f, tn), BF16)]),
        compiler_params=_params(("arbitrary", "arbitrary"), blocks, scratch=[((2, f, tn), BF16)],
                                temps=[((tm, tn), F32)] * 2),
        name="moe_down",
    )(*tiles, hmid, w_down)


def _moe_final_kernel(dest_ref, x_ref, y_ref, route_ref, g_ref, mod_ref, o_ref, buf_ref, sems):
    tile = x_ref.shape[0]
    base = pl.program_id(0) * tile

    def issue(r, carry):
        for k in range(TOP_K):
            _row_copy(y_ref, buf_ref.at[k], sems.at[k], dest_ref[(base + r) * TOP_K + k], r).start()
        return carry

    def drain(r, carry):
        for k in range(TOP_K):
            _row_copy(y_ref, buf_ref.at[k], sems.at[k], 0, r).wait()
        return carry

    lax.fori_loop(0, tile, issue, 0, unroll=GATHER_UNROLL)
    lax.fori_loop(0, tile, drain, 0, unroll=GATHER_UNROLL)
    f = route_ref[:, TOP_K:TOP_K + 1] * buf_ref[0]
    for k in range(1, TOP_K):
        f = f + route_ref[:, TOP_K + k:TOP_K + k + 1] * buf_ref[k]
    o_ref[...] = x_ref[...] + mod_ref[5:6, :] * _rms(f, g_ref[...])


def _moe_final(x1, y, dest, route, g_post, mod, n_rows, seq, n_lat_rows, n_batch):
    d = x1.shape[1]
    tr = EW_ROWS
    ridx = functools.partial(_mod_row_index, rows_per_tile=tr, seq=seq, n_lat_rows=n_lat_rows,
                             n_batch=n_batch)
    row = pl.BlockSpec((tr, d), lambda i, dest_ref: (i, 0))
    blocks = [((tr, d), F32)] * 2 + [((N_MOD, d), F32)]
    return pl.pallas_call(
        _moe_final_kernel,
        out_shape=jax.ShapeDtypeStruct((n_rows, d), F32),
        grid_spec=pltpu.PrefetchScalarGridSpec(
            num_scalar_prefetch=1,
            grid=(n_rows // tr,),
            in_specs=[row, pl.BlockSpec(memory_space=pl.ANY),
                      pl.BlockSpec((tr, V7X_LANES), lambda i, dest_ref: (i, 0)),
                      pl.BlockSpec((1, d), lambda i, dest_ref: (0, 0)),
                      pl.BlockSpec((None, N_MOD, d), lambda i, dest_ref: (ridx(i), 0, 0))],
            out_specs=row,
            scratch_shapes=[pltpu.VMEM((TOP_K, tr, d), F32), pltpu.SemaphoreType.DMA((TOP_K,))]),
        compiler_params=_params(("arbitrary",), blocks, scratch=[((TOP_K, tr, d), F32)],
                                temps=[((tr, d), F32)] * 2),
        name="moe_combine_final",
    )(dest, x1, y, route, g_post.reshape(1, d), mod)


def _rope_tables(seq, n_batch, ctx_rows):
    n = jnp.arange(seq)
    pos_r = (n // GRID_W).astype(F32)
    pos_w = (n % GRID_W).astype(F32)
    n_freq = HEAD_DIM // 4
    inv_freq = ROPE_THETA ** (-jnp.arange(n_freq, dtype=F32) / n_freq)
    ar = pos_r[:, None] * inv_freq
    aw = pos_w[:, None] * inv_freq
    cos = jnp.concatenate([jnp.cos(ar), jnp.cos(ar), jnp.cos(aw), jnp.cos(aw)], axis=-1)
    sin = jnp.concatenate([-jnp.sin(ar), jnp.sin(ar), -jnp.sin(aw), jnp.sin(aw)], axis=-1)
    cos = jnp.concatenate([cos] * n_batch + [jnp.ones((ctx_rows, HEAD_DIM), F32)], axis=0)
    sin = jnp.concatenate([sin] * n_batch + [jnp.zeros((ctx_rows, HEAD_DIM), F32)], axis=0)
    return cos, sin


def kernel(x, c, ctx, c_ctx, w_mod, b_mod, g_pre_mix, g_post_mix, g_pre_ffn, g_post_ffn, w_in, w_out,
           sink_a, qn_c, kn_c, gm_ln_g, gm_ln_b, gm_ws, gm_bs, ffn_w_gate, ffn_w_up, ffn_w_down,
           moe_router, moe_w_gate, moe_w_up, moe_w_down):
    n_batch, seq, d = x.shape
    n_ctx = ctx.shape[1]
    depth = w_mod.shape[0]
    n_lat = n_batch * seq
    n_ctx_rows = n_batch * n_ctx
    t = n_lat + n_ctx_rows
    a_heads = sink_a.shape[1]
    a_w = a_heads * HEAD_DIM
    akv_w = A_KV_HEADS * HEAD_DIM
    b_w = gm_ln_g.shape[1]
    ckv_w = C_KV_HEADS * HEAD_DIM
    c_w = w_in.shape[2] - a_w - 2 * akv_w - 2 * b_w - 2 * ckv_w
    c_heads = c_w // HEAD_DIM
    assert n_ctx_rows % EW_ROWS == 0 and seq % EW_ROWS == 0 and n_batch + 1 <= 8
    assert a_w == c_w and akv_w == COL_TILE and ckv_w == COL_TILE

    x_parts = (x.reshape(n_lat, d), ctx.reshape(n_ctx_rows, d))
    c8 = jnp.concatenate([c, c_ctx[None, :], jnp.zeros((8 - n_batch - 1, d), F32)], axis=0)
    mod_all = _modulation(c8, w_mod, b_mod).reshape(depth, 8, N_MOD, d)

    cos, sin = _rope_tables(seq, n_batch, n_ctx_rows)
    geo = dict(seq=seq, n_lat_rows=n_lat, n_batch=n_batch)
    h = _prenorm(x_parts, g_pre_mix[0], mod_all[0], **geo)

    for l in range(depth):
        need_ctx = l < depth - 1
        n_rows = t if need_ctx else n_lat
        mod = mod_all[l]

        gq, gk = qn_c[l].reshape(1, HEAD_DIM), kn_c[l].reshape(1, HEAD_DIM)
        rope = (cos, sin, gq, gk)
        col = 0
        za = _inproj(h, w_in, l, col, a_w + akv_w, "rope", rope, n_q_tiles=a_w // COL_TILE)
        col += a_w + akv_w
        av = _inproj(h, w_in, l, col, akv_w, "plain")
        col += akv_w
        zb = _inproj(h, w_in, l, col, 2 * b_w, "gelu")
        col += 2 * b_w
        zc = _inproj(h, w_in, l, col, c_w + ckv_w, "normrope", rope, n_q_tiles=c_w // COL_TILE)
        col += c_w + ckv_w
        cv = _inproj(h, w_in, l, col, ckv_w, "plain")

        oa = _window_attn(za, av, sink_a[l], n_batch, seq, n_ctx, need_ctx)
        ob = _gmlp(zb, gm_ln_g[l], gm_ln_b[l], gm_ws[l], gm_bs[l], n_rows)
        oc = _global_attn(zc, cv, n_batch, seq, n_ctx, c_heads, need_ctx)

        y = _outproj(oa, ob, oc, w_out, l, n_rows)
        i = l // 2
        next_norm = (g_pre_mix[l + 1], mod_all[l + 1]) if need_ctx else None
        if l % 2 == 0:
            x1, h2 = _postmix(x_parts, y, g_post_mix[l], g_pre_ffn[l], mod, n_rows, 0, **geo)
            f_dim = ffn_w_gate.shape[-1]
            hmid = _ffn_up(h2, ffn_w_gate, ffn_w_up, (i,), n_rows)
            half = f_dim // 2
            f = _ffn_down(hmid, ffn_w_down, (i,), 0, half, n_rows)
            f = _ffn_down(hmid, ffn_w_down, (i,), half, half, n_rows, prev=f)
            xall = _final(x1, f, g_post_ffn[l], mod, n_rows, next_norm, **geo)
            if need_ctx:
                xall, h = xall
        else:
            x1, h2, h2f = _postmix(x_parts, y, g_post_mix[l], g_pre_ffn[l], mod, n_rows, 1, **geo)
            route = _router(h2, moe_router[i], n_rows)
            tok_of_row, live, dest, tiles = _route_plan(route, moe_router.shape[-1])
            xs = _gather_rows(h2f, tok_of_row, live)
            hmid = _moe_up(xs, moe_w_gate, moe_w_up, i, tiles)
            ys = _moe_down(hmid, moe_w_down, i, tiles)
            xall = _moe_final(x1, ys, dest, route, g_post_ffn[l], mod, n_rows, **geo)
            if need_ctx:
                h = _prenorm((xall,), next_norm[0], next_norm[1], **geo)
        x_parts = (xall,)
    return xall[:n_lat].reshape(n_batch, seq, d)
```

```python
import functools
import math

import jax
import jax.numpy as jnp
from jax import lax
from jax.experimental import pallas as pl
from jax.experimental.pallas import tpu as pltpu

F32 = jnp.float32
BF16 = jnp.bfloat16

GRID_W = 64
HEAD_DIM = 128
BLOCK = 128
WINDOW = 128
A_KV_HEADS = 4
C_KV_HEADS = 4
B_GROUPS = 8
CHUNK = 128
N_MOD = 6
TOP_K = 2
ROPE_THETA = 10000.0
EPS = 1e-6
LOG2E = math.log2(math.e)

V7X_LANES = 128
V7X_VMEM_SCOPED_CAP = 60000 * 1024

ROW_TILE = 512
MAX_ROW_TILE = 1088
ROW_ALIGN = 16
COL_TILE = 512
FFN_COL_TILE = 256
EW_ROWS = 256
CAST_ROWS = 256
Q_TILE = 256
K_TILE = 512
GATHER_UNROLL = 8
MOE_ROW_TILE = 1024
MOE_UP_ROW_TILE = 512
MOE_ROW_STEP = 256
ROUTER_PIECES = 3
EPILOGUE_CHUNKS = 4
SLAB_ROWS = 8


def _nbytes(shape, dtype):
    return math.prod(shape) * jnp.dtype(dtype).itemsize


def _params(semantics, blocks, scratch=(), temps=()):
    need = 2 * sum(_nbytes(s, d) for s, d in blocks)
    need += sum(_nbytes(s, d) for s, d in scratch)
    need += sum(_nbytes(s, d) for s, d in temps)
    limit = min(V7X_VMEM_SCOPED_CAP, max(need + need // 4, 16 * 1024 * 1024))
    return pltpu.CompilerParams(dimension_semantics=semantics, vmem_limit_bytes=limit)


def _row_tile(n_rows):
    for tm in range(MAX_ROW_TILE - MAX_ROW_TILE % ROW_ALIGN, 0, -ROW_ALIGN):
        if n_rows % tm == 0:
            return tm
    raise ValueError(f"no row tile for {n_rows} rows")


def _cast_weight(w_ref, wbf_ref):
    rows = w_ref.shape[0]
    step = CAST_ROWS if rows % CAST_ROWS == 0 else V7X_LANES
    assert rows % step == 0

    def body(r, carry):
        sl = pl.ds(pl.multiple_of(r * step, step), step)
        wbf_ref[sl, :] = w_ref[sl, :].astype(BF16)
        return carry

    lax.fori_loop(0, rows // step, body, 0)


def _dot(a, b):
    return jnp.dot(a, b, preferred_element_type=F32)


def _dot_nt(a, b):
    return lax.dot_general(a, b, (((1,), (1,)), ((), ())), preferred_element_type=F32)


def _rms(x, gain):
    return x * lax.rsqrt(jnp.mean(x * x, axis=-1, keepdims=True) + EPS) * gain


def _to_slabs(x, slab_ref):
    w = slab_ref.shape[2]
    for s in range(slab_ref.shape[1]):
        slab_ref[:, s, :] = x[:, s * w:(s + 1) * w]


def _from_slabs(slab_ref):
    return jnp.concatenate([slab_ref[:, s, :] for s in range(slab_ref.shape[1])], axis=1)


def _mod_kernel(c_ref, w_ref, b_ref, o_ref):
    s = jax.nn.silu(c_ref[...])
    o_ref[...] = _dot(s.astype(BF16), w_ref[...].astype(BF16)) + b_ref[...]


def _modulation(c8, w_mod, b_mod):
    n_layers, d, width = w_mod.shape
    tn = COL_TILE
    assert width % tn == 0
    blocks = [((8, d), F32), ((d, tn), F32), ((1, tn), F32), ((8, tn), F32)]
    return pl.pallas_call(
        _mod_kernel,
        out_shape=jax.ShapeDtypeStruct((n_layers, 8, width), F32),
        grid=(n_layers, width // tn),
        in_specs=[pl.BlockSpec((8, d), lambda l, j: (0, 0)),
                  pl.BlockSpec((None, d, tn), lambda l, j: (l, 0, j)),
                  pl.BlockSpec((None, 1, tn), lambda l, j: (l, 0, j))],
        out_specs=pl.BlockSpec((None, 8, tn), lambda l, j: (l, 0, j)),
        compiler_params=_params(("arbitrary", "arbitrary"), blocks, temps=[((d, tn), BF16)]),
        name="modulation",
    )(c8, w_mod, b_mod.reshape(n_layers, 1, width))


def _mod_row_index(i, rows_per_tile, seq, n_lat_rows, n_batch):
    lat_tiles = n_lat_rows // rows_per_tile
    return jnp.where(i < lat_tiles, (i * rows_per_tile) // seq, n_batch)


def _row_parts(parts, tr):
    specs, starts, start = [], [], 0
    for p in parts:
        n = p.shape[0] // tr
        assert n * tr == p.shape[0]
        specs.append(pl.BlockSpec((tr, p.shape[1]),
                                  lambda i, *_, start=start, n=n: (jnp.clip(i - start, 0, n - 1), 0)))
        starts.append(start)
        start += n
    return specs, tuple(starts)


def _pick_rows(i, x_refs, starts):
    x = x_refs[0][...]
    for ref, start in zip(x_refs[1:], starts[1:]):
        x = jnp.where(i >= start, ref[...], x)
    return x


def _pre_mix_norm(x, g_ref, mod_ref):
    return (_rms(x, g_ref[...]) * (1.0 + mod_ref[1:2, :]) + mod_ref[0:1, :]).astype(BF16)


def _prenorm_kernel(*refs, starts):
    *x_refs, g_ref, mod_ref, o_ref = refs
    o_ref[...] = _pre_mix_norm(_pick_rows(pl.program_id(0), x_refs, starts), g_ref, mod_ref)


def _prenorm(x_parts, gain, mod, seq, n_lat_rows, n_batch):
    d = x_parts[0].shape[1]
    t = sum(p.shape[0] for p in x_parts)
    tr = EW_ROWS
    ridx = functools.partial(_mod_row_index, rows_per_tile=tr, seq=seq, n_lat_rows=n_lat_rows,
                             n_batch=n_batch)
    x_specs, starts = _row_parts(x_parts, tr)
    blocks = [((tr, d), F32)] * len(x_parts) + [((1, d), F32), ((N_MOD, d), F32), ((tr, d), BF16)]
    return pl.pallas_call(
        functools.partial(_prenorm_kernel, starts=starts),
        out_shape=jax.ShapeDtypeStruct((t, d), BF16),
        grid=(t // tr,),
        in_specs=x_specs + [pl.BlockSpec((1, d), lambda i: (0, 0)),
                            pl.BlockSpec((None, N_MOD, d), lambda i: (ridx(i), 0, 0))],
        out_specs=pl.BlockSpec((tr, d), lambda i: (i, 0)),
        compiler_params=_params(("arbitrary",), blocks, temps=[((tr, d), F32)] * 2),
        name="prenorm",
    )(*x_parts, gain.reshape(1, d), mod)


def _postmix_kernel(*refs, starts, n_parts):
    x_refs = refs[:n_parts]
    y_ref, gpost_ref, gpre_ref, mod_ref, x1_ref, h2_ref, *h2f_ref = refs[n_parts:]
    x = _pick_rows(pl.program_id(0), x_refs, starts)
    x1 = x + mod_ref[2:3, :] * _rms(y_ref[...].astype(F32), gpost_ref[...])
    x1_ref[...] = x1
    h2 = _rms(x1, gpre_ref[...]) * (1.0 + mod_ref[4:5, :]) + mod_ref[3:4, :]
    h2_ref[...] = h2.astype(BF16)
    for ref in h2f_ref:
        _to_slabs(h2, ref)


def _postmix(x_parts, y, g_post, g_pre, mod, n_rows, want_slabs, seq, n_lat_rows, n_batch):
    d = x_parts[0].shape[1]
    tr = EW_ROWS
    x_specs, starts = _row_parts(x_parts, tr)
    assert d % (SLAB_ROWS * V7X_LANES) == 0
    slab = (SLAB_ROWS, d // SLAB_ROWS)
    ridx = functools.partial(_mod_row_index, rows_per_tile=tr, seq=seq, n_lat_rows=n_lat_rows,
                             n_batch=n_batch)
    row = pl.BlockSpec((tr, d), lambda i: (i, 0))
    vec = pl.BlockSpec((1, d), lambda i: (0, 0))
    blocks = ([((tr, d), F32)] * (len(x_parts) + 1 + want_slabs)
              + [((tr, d), BF16)] * 2 + [((N_MOD, d), F32)])
    out_shape = [jax.ShapeDtypeStruct((n_rows, d), F32), jax.ShapeDtypeStruct((n_rows, d), BF16)]
    out_specs = [row, row]
    if want_slabs:
        out_shape.append(jax.ShapeDtypeStruct((n_rows,) + slab, F32))
        out_specs.append(pl.BlockSpec((tr,) + slab, lambda i: (i, 0, 0)))
    return pl.pallas_call(
        functools.partial(_postmix_kernel, starts=starts, n_parts=len(x_parts)),
        out_shape=tuple(out_shape),
        grid=(n_rows // tr,),
        in_specs=x_specs + [row, vec, vec,
                            pl.BlockSpec((None, N_MOD, d), lambda i: (ridx(i), 0, 0))],
        out_specs=tuple(out_specs),
        compiler_params=_params(("arbitrary",), blocks, temps=[((tr, d), F32)] * 3),
        name="postmix",
    )(*x_parts, y, g_post.reshape(1, d), g_pre.reshape(1, d), mod)


def _final_kernel(x_ref, f_ref, g_ref, mod_ref, *rest):
    x = x_ref[...] + mod_ref[5:6, :] * _rms(f_ref[...], g_ref[...])
    if len(rest) == 1:
        (o_ref,) = rest
    else:
        gnext_ref, modnext_ref, o_ref, h_ref = rest
        h_ref[...] = _pre_mix_norm(x, gnext_ref, modnext_ref)
    o_ref[...] = x


def _final(x1, f, g_post, mod, n_rows, next_norm, seq, n_lat_rows, n_batch):
    d = x1.shape[1]
    tr = EW_ROWS
    ridx = functools.partial(_mod_row_index, rows_per_tile=tr, seq=seq, n_lat_rows=n_lat_rows,
                             n_batch=n_batch)
    row = pl.BlockSpec((tr, d), lambda i: (i, 0))
    vec = pl.BlockSpec((1, d), lambda i: (0, 0))
    modspec = pl.BlockSpec((None, N_MOD, d), lambda i: (ridx(i), 0, 0))
    blocks = [((tr, d), F32)] * 3 + [((N_MOD, d), F32)]
    in_specs, args = [row, row, vec, modspec], [x1, f, g_post.reshape(1, d), mod]
    out_shape, out_specs = jax.ShapeDtypeStruct((n_rows, d), F32), row
    if next_norm is not None:
        in_specs += [vec, modspec]
        args += [next_norm[0].reshape(1, d), next_norm[1]]
        out_shape = (out_shape, jax.ShapeDtypeStruct((n_rows, d), BF16))
        out_specs = (row, row)
        blocks += [((N_MOD, d), F32), ((tr, d), BF16)]
    return pl.pallas_call(
        _final_kernel,
        out_shape=out_shape,
        grid=(n_rows // tr,),
        in_specs=in_specs,
        out_specs=out_specs,
        compiler_params=_params(("arbitrary",), blocks, temps=[((tr, d), F32)] * 2),
        name="final_residual",
    )(*args)


def _swap_pairs(x):
    lane = lax.broadcasted_iota(jnp.int32, x.shape, 1)
    quarter = HEAD_DIM // 4
    first = (lane & quarter) == 0
    return jnp.where(first, pltpu.roll(x, HEAD_DIM - quarter, axis=1), pltpu.roll(x, quarter, axis=1))


def _inproj_kernel(*refs, mode, n_q_tiles, q_scale):
    if mode in ("rope", "normrope"):
        h_ref, w_ref, cos_ref, sin_ref, gq_ref, gk_ref, o_ref, wbf_ref = refs
    else:
        h_ref, w_ref, o_ref, wbf_ref = refs
    j = pl.program_id(0)

    @pl.when(pl.program_id(1) == 0)
    def _():
        _cast_weight(w_ref, wbf_ref)

    tm = h_ref.shape[0]
    n_chunks = EPILOGUE_CHUNKS if tm % (EPILOGUE_CHUNKS * ROW_ALIGN) == 0 else 1
    rc = tm // n_chunks
    for c in range(n_chunks):
        rows = slice(c * rc, (c + 1) * rc)
        acc = _dot(h_ref[rows, :], wbf_ref[...])
        if mode == "plain":
            o_ref[rows, :] = acc.astype(BF16)
        elif mode == "gelu":
            o_ref[rows, :] = (0.5 * acc * (1.0 + lax.erf(acc * (2.0 ** -0.5)))).astype(BF16)
        else:
            is_q = j < n_q_tiles
            scale = jnp.where(is_q, q_scale, 1.0).astype(F32)
            cos = cos_ref[rows, :]
            sin = sin_ref[rows, :]
            gain = jnp.where(is_q, gq_ref[...], gk_ref[...])
            for hh in range(acc.shape[1] // HEAD_DIM):
                sl = slice(hh * HEAD_DIM, (hh + 1) * HEAD_DIM)
                xh = acc[:, sl]
                if mode == "normrope":
                    xh = _rms(xh, gain)
                xh = xh * cos + _swap_pairs(xh) * sin
                o_ref[rows, sl] = (xh * scale).astype(BF16)


def _inproj(h, w_in, layer, col_start, width, mode, rope=None, n_q_tiles=0):
    t, d = h.shape
    tm, tn = _row_tile(t), COL_TILE
    assert col_start % tn == 0 and width % tn == 0
    j0 = col_start // tn
    in_specs = [pl.BlockSpec((tm, d), lambda j, i: (i, 0)),
                pl.BlockSpec((None, d, tn), lambda j, i: (layer, 0, j0 + j))]
    args = [h, w_in]
    blocks = [((tm, d), BF16), ((d, tn), F32), ((tm, tn), BF16)]
    if mode in ("rope", "normrope"):
        cos, sin, gq, gk = rope
        tab = pl.BlockSpec((tm, HEAD_DIM), lambda j, i: (i, 0))
        vec = pl.BlockSpec((1, HEAD_DIM), lambda j, i: (0, 0))
        in_specs += [tab, tab, vec, vec]
        args += [cos, sin, gq, gk]
        blocks += [((tm, HEAD_DIM), F32)] * 2
    kern = functools.partial(_inproj_kernel, mode=mode, n_q_tiles=n_q_tiles,
                             q_scale=HEAD_DIM ** -0.5 * LOG2E)
    return pl.pallas_call(
        kern,
        out_shape=jax.ShapeDtypeStruct((t, width), BF16),
        grid=(width // tn, t // tm),
        in_specs=in_specs,
        out_specs=pl.BlockSpec((tm, tn), lambda j, i: (i, j)),
        scratch_shapes=[pltpu.VMEM((d, tn), BF16)],
        compiler_params=_params(("arbitrary", "arbitrary"), blocks, scratch=[((d, tn), BF16)],
                                temps=[((tm, tn), F32)] * 2),
        name="inproj_" + mode,
    )(*args)


def _window_bias(group, n_ctx):
    r = jnp.arange(group * BLOCK)[:, None] % BLOCK
    c = jnp.arange(3 * BLOCK + n_ctx)[None, :]
    in_prev, in_cur = c < BLOCK, (c >= BLOCK) & (c < 2 * BLOCK)
    in_next = (c >= 2 * BLOCK) & (c < 3 * BLOCK)
    band_prev = in_prev & (c < r)
    band_next = in_next & (c - 2 * BLOCK > r)
    hidden = [band_prev | band_next, in_prev | band_next, band_prev | in_next, in_prev | in_next,
              in_prev | in_cur | in_next]
    shape = (group * BLOCK, 3 * BLOCK + n_ctx)
    return jnp.stack([jnp.where(jnp.broadcast_to(h, shape), -jnp.inf, 0.0).astype(F32)
                      for h in hidden])


def _window_attn_kernel(sink_ref, bias_ref, q_ref, kp_ref, kc_ref, kn_ref, kx_ref, vp_ref, vc_ref,
                        vn_ref, vx_ref, o_ref, *, group):
    rows = group * BLOCK
    assert BLOCK & (BLOCK - 1) == 0
    shift = BLOCK.bit_length() - 1
    bias = bias_ref[...]
    rid = lax.broadcasted_iota(jnp.int32, (rows, 1), 0) >> shift
    for hk in range(A_KV_HEADS):
        ksl = slice(hk * HEAD_DIM, (hk + 1) * HEAD_DIM)
        k_all = jnp.concatenate([kp_ref[:, ksl], kc_ref[:, ksl], kn_ref[:, ksl], kx_ref[:, ksl]],
                                axis=0)
        v_all = jnp.concatenate([vp_ref[:, ksl], vc_ref[:, ksl], vn_ref[:, ksl], vx_ref[:, ksl]],
                                axis=0)
        q3 = jnp.concatenate(
            [q_ref[:, (hk * group + g) * HEAD_DIM:(hk * group + g + 1) * HEAD_DIM]
             for g in range(group)], axis=0)
        sink = jnp.zeros((rows, 1), F32)
        for g in range(group):
            sink = jnp.where(rid == g, sink_ref[hk * group + g] * LOG2E, sink)
        s = _dot_nt(q3, k_all) + bias
        m = jnp.maximum(jnp.max(s, axis=-1, keepdims=True), sink)
        p = jnp.exp2(s - m)
        denom = jnp.sum(p, axis=-1, keepdims=True) + jnp.exp2(sink - m)
        o = _dot(p.astype(BF16), v_all) / denom
        for g in range(group):
            osl = slice((hk * group + g) * HEAD_DIM, (hk * group + g + 1) * HEAD_DIM)
            o_ref[:, osl] = o[g * BLOCK:(g + 1) * BLOCK, :].astype(BF16)


def _window_attn(za, av, sink, n_batch, seq, n_ctx, with_ctx):
    n_heads = sink.shape[0]
    group = n_heads // A_KV_HEADS
    qw = n_heads * HEAD_DIM
    kw = A_KV_HEADS * HEAD_DIM
    assert qw % kw == 0 and seq % BLOCK == 0 and (n_batch * seq) % n_ctx == 0 and WINDOW == BLOCK
    assert n_ctx % BLOCK == 0
    nb = seq // BLOCK
    ncb = n_ctx // BLOCK if with_ctx else 0
    kcol = qw // kw
    ctx0 = (n_batch * seq) // n_ctx

    def qblk(b, n):
        return (jnp.where(n < nb, b * nb + n, n_batch * nb + b * ncb + n - nb), 0)

    def blk(shift):
        return lambda b, n: (b * nb + jnp.clip(n + shift, 0, nb - 1), kcol)

    def vblk(shift):
        return lambda b, n: (b * nb + jnp.clip(n + shift, 0, nb - 1), 0)

    def bias_variant(b, n):
        lat = jnp.where(n == 0, 1, 0) + jnp.where(n == nb - 1, 2, 0)
        return (jnp.where(n < nb, lat, 4), 0, 0)

    kspec = [pl.BlockSpec((BLOCK, kw), blk(s)) for s in (-1, 0, 1)]
    vspec = [pl.BlockSpec((BLOCK, kw), vblk(s)) for s in (-1, 0, 1)]
    n_keys = 3 * BLOCK + n_ctx
    rows = group * BLOCK
    blocks = ([((BLOCK, qw), BF16)] * 2 + [((BLOCK, kw), BF16)] * 6 + [((n_ctx, kw), BF16)] * 2
              + [((rows, n_keys), F32)])
    return pl.pallas_call(
        functools.partial(_window_attn_kernel, group=group),
        out_shape=jax.ShapeDtypeStruct((n_batch * (nb + ncb) * BLOCK, qw), BF16),
        grid=(n_batch, nb + ncb),
        in_specs=[pl.BlockSpec(memory_space=pltpu.SMEM),
                  pl.BlockSpec((None, rows, n_keys), bias_variant),
                  pl.BlockSpec((BLOCK, qw), qblk)]
                 + kspec + [pl.BlockSpec((n_ctx, kw), lambda b, n: (ctx0 + b, kcol))]
                 + vspec + [pl.BlockSpec((n_ctx, kw), lambda b, n: (ctx0 + b, 0))],
        out_specs=pl.BlockSpec((BLOCK, qw), qblk),
        compiler_params=_params(("arbitrary", "arbitrary"), blocks,
                                temps=[((rows, n_keys), F32)] * 4),
        name="window_attn",
    )(sink, _window_bias(group, n_ctx), za, za, za, za, za, av, av, av, av)


def _gmlp_kernel(z_ref, g_ref, b_ref, ws_ref, bst_ref, o_ref):
    width = g_ref.shape[1]
    u = z_ref[:, :width].astype(F32)
    v = z_ref[:, width:].astype(F32)
    mu = jnp.mean(v, axis=-1, keepdims=True)
    vc = v - mu
    var = jnp.mean(vc * vc, axis=-1, keepdims=True)
    vn = (vc * lax.rsqrt(var + EPS) * g_ref[...] + b_ref[...]).astype(BF16)
    gd = width // B_GROUPS
    for g in range(B_GROUPS):
        sl = slice(g * gd, (g + 1) * gd)
        mixed = _dot(ws_ref[g].astype(BF16), vn[:, sl]) + bst_ref[:, g:g + 1]
        o_ref[:, sl] = (u[:, sl] * mixed).astype(BF16)


def _gmlp(zb, ln_g, ln_b, ws, bs, n_rows):
    w2 = zb.shape[1]
    width = w2 // 2
    blocks = [((CHUNK, w2), BF16), ((1, width), F32), ((1, width), F32),
              (ws.shape, F32), ((CHUNK, B_GROUPS), F32), ((CHUNK, width), BF16)]
    return pl.pallas_call(
        _gmlp_kernel,
        out_shape=jax.ShapeDtypeStruct((n_rows, width), BF16),
        grid=(n_rows // CHUNK,),
        in_specs=[pl.BlockSpec((CHUNK, w2), lambda i: (i, 0)),
                  pl.BlockSpec((1, width), lambda i: (0, 0)),
                  pl.BlockSpec((1, width), lambda i: (0, 0)),
                  pl.BlockSpec(ws.shape, lambda i: (0, 0, 0)),
                  pl.BlockSpec((CHUNK, B_GROUPS), lambda i: (0, 0))],
        out_specs=pl.BlockSpec((CHUNK, width), lambda i: (i, 0)),
        compiler_params=_params(("arbitrary",), blocks, temps=[((CHUNK, w2), F32)] * 2),
        name="chunk_gmlp",
    )(zb, ln_g.reshape(1, width), ln_b.reshape(1, width), ws, bs.T)


def _global_attn_kernel(q_ref, kx_ref, vx_ref, k_ref, v_ref, o_ref, vext_ref, *, group,
                        n_lat_tiles, has_ctx_tiles):
    tq = q_ref.shape[0]
    n_ctx = kx_ref.shape[0]
    n_lat = k_ref.shape[0]
    rows = group * tq

    @pl.when(pl.program_id(2) == 0)
    def _():
        vext_ref[0:n_ctx, 0:HEAD_DIM] = vx_ref[...]
        vext_ref[n_ctx:, 0:HEAD_DIM] = v_ref[...]
        vext_ref[:, HEAD_DIM:] = jnp.ones((n_ctx + n_lat, HEAD_DIM), BF16)

    def attend(chunks):
        q3 = jnp.concatenate([q_ref[:, g * HEAD_DIM:(g + 1) * HEAD_DIM] for g in range(group)],
                             axis=0)
        m = jnp.full((rows, HEAD_DIM), -jnp.inf, F32)
        acc = jnp.zeros((rows, 2 * HEAD_DIM), F32)
        for k_chunk, v0, nk in chunks:
            s = _dot_nt(q3, k_chunk())
            m_new = jnp.maximum(m, jnp.broadcast_to(jnp.max(s, axis=-1, keepdims=True), m.shape))
            alpha = jnp.exp2(m - m_new)
            p = jnp.exp2(s - jnp.tile(m_new, (1, nk // HEAD_DIM)))
            acc = jnp.tile(alpha, (1, 2)) * acc + _dot(p.astype(BF16), vext_ref[v0:v0 + nk, :])
            m = m_new
        o = acc[:, :HEAD_DIM] / acc[:, HEAD_DIM:]
        for g in range(group):
            o_ref[:, g * HEAD_DIM:(g + 1) * HEAD_DIM] = o[g * tq:(g + 1) * tq, :].astype(BF16)

    ctx_chunk = [(lambda: kx_ref[...], 0, n_ctx)]
    lat_chunks = [(lambda c=c: k_ref[c:c + K_TILE, :], n_ctx + c, K_TILE)
                  for c in range(0, n_lat, K_TILE)]
    is_lat = pl.program_id(2) < n_lat_tiles
    pl.when(is_lat)(lambda: attend(ctx_chunk + lat_chunks))
    if has_ctx_tiles:
        pl.when(jnp.logical_not(is_lat))(lambda: attend(ctx_chunk))


def _global_attn(zc, cv, n_batch, seq, n_ctx, n_heads, with_ctx):
    group = n_heads // C_KV_HEADS
    gw = group * HEAD_DIM
    tq = Q_TILE
    assert seq % tq == 0 and seq % K_TILE == 0 and (n_batch * seq) % n_ctx == 0 and n_ctx % tq == 0
    nq = seq // tq
    ncq = n_ctx // tq if with_ctx else 0
    ctx0 = (n_batch * seq) // n_ctx
    rows = group * tq

    def qblk(b, hk, i):
        return (jnp.where(i < nq, b * nq + i, n_batch * nq + b * ncq + i - nq), hk)

    blocks = [((tq, gw), BF16)] * 2 + [((n_ctx, HEAD_DIM), BF16)] * 2 + [((seq, HEAD_DIM), BF16)] * 2
    scratch = [((n_ctx + seq, 2 * HEAD_DIM), BF16)]
    return pl.pallas_call(
        functools.partial(_global_attn_kernel, group=group, n_lat_tiles=nq, has_ctx_tiles=ncq > 0),
        out_shape=jax.ShapeDtypeStruct((n_batch * (nq + ncq) * tq, n_heads * HEAD_DIM), BF16),
        grid=(n_batch, C_KV_HEADS, nq + ncq),
        in_specs=[pl.BlockSpec((tq, gw), qblk),
                  pl.BlockSpec((n_ctx, HEAD_DIM), lambda b, hk, i: (ctx0 + b, n_heads + hk)),
                  pl.BlockSpec((n_ctx, HEAD_DIM), lambda b, hk, i: (ctx0 + b, hk)),
                  pl.BlockSpec((seq, HEAD_DIM), lambda b, hk, i: (b, n_heads + hk)),
                  pl.BlockSpec((seq, HEAD_DIM), lambda b, hk, i: (b, hk))],
        out_specs=pl.BlockSpec((tq, gw), qblk),
        scratch_shapes=[pltpu.VMEM(s, d) for s, d in scratch],
        compiler_params=_params(("arbitrary", "arbitrary", "arbitrary"), blocks, scratch=scratch,
                                temps=[((rows, K_TILE), F32)] * 6),
        name="global_attn",
    )(zc, zc, cv, zc, cv)


def _outproj_kernel(a_ref, b_ref, c_ref, w_ref, o_ref, wbf_ref):
    @pl.when(pl.program_id(1) == 0)
    def _():
        _cast_weight(w_ref, wbf_ref)

    ka, kb = a_ref.shape[1], b_ref.shape[1]
    acc = _dot(a_ref[...], wbf_ref[0:ka, :])
    acc += _dot(b_ref[...], wbf_ref[ka:ka + kb, :])
    acc += _dot(c_ref[...], wbf_ref[ka + kb:, :])
    o_ref[...] = acc.astype(BF16)


def _outproj(oa, ob, oc, w_out, layer, n_rows):
    t = oa.shape[0]
    _, k, d = w_out.shape
    tm, tn = _row_tile(n_rows), COL_TILE
    assert oa.shape[1] + ob.shape[1] + oc.shape[1] == k and d % tn == 0
    blocks = [((tm, k), BF16), ((k, tn), F32), ((tm, tn), BF16)]
    return pl.pallas_call(
        _outproj_kernel,
        out_shape=jax.ShapeDtypeStruct((t, d), BF16),
        grid=(d // tn, n_rows // tm),
        in_specs=[pl.BlockSpec((tm, oa.shape[1]), lambda j, i: (i, 0)),
                  pl.BlockSpec((tm, ob.shape[1]), lambda j, i: (i, 0)),
                  pl.BlockSpec((tm, oc.shape[1]), lambda j, i: (i, 0)),
                  pl.BlockSpec((None, k, tn), lambda j, i: (layer, 0, j))],
        out_specs=pl.BlockSpec((tm, tn), lambda j, i: (i, j)),
        scratch_shapes=[pltpu.VMEM((k, tn), BF16)],
        compiler_params=_params(("arbitrary", "arbitrary"), blocks, scratch=[((k, tn), BF16)],
                                temps=[((tm, tn), F32)] * 2),
        name="outproj",
    )(oa, ob, oc, w_out)


def _ffn_up_kernel(h_ref, wg_ref, wu_ref, o_ref, wgbf_ref, wubf_ref):
    @pl.when(pl.program_id(1) == 0)
    def _():
        _cast_weight(wg_ref, wgbf_ref)
        _cast_weight(wu_ref, wubf_ref)

    h = h_ref[...]
    gate = _dot(h, wgbf_ref[...])
    up = _dot(h, wubf_ref[...])
    o_ref[...] = (jax.nn.silu(gate) * up).astype(BF16)


def _ffn_up(h2, w_gate, w_up, sel, n_rows):
    t, d = h2.shape
    f = w_gate.shape[-1]
    tm, tn = _row_tile(n_rows), FFN_COL_TILE
    assert f % tn == 0
    lead = (None,) * len(sel)
    wspec = pl.BlockSpec(lead + (d, tn), lambda j, i: sel + (0, j))
    blocks = [((tm, d), BF16), ((d, tn), F32), ((d, tn), F32), ((tm, tn), BF16)]
    return pl.pallas_call(
        _ffn_up_kernel,
        out_shape=jax.ShapeDtypeStruct((t, f), BF16),
        grid=(f // tn, n_rows // tm),
        in_specs=[pl.BlockSpec((tm, d), lambda j, i: (i, 0)), wspec, wspec],
        out_specs=pl.BlockSpec((tm, tn), lambda j, i: (i, j)),
        scratch_shapes=[pltpu.VMEM((d, tn), BF16)] * 2,
        compiler_params=_params(("arbitrary", "arbitrary"), blocks, scratch=[((d, tn), BF16)] * 2,
                                temps=[((tm, tn), F32)] * 3),
        name="ffn_up",
    )(h2, w_gate, w_up)


def _ffn_down_kernel(*refs, has_prev):
    if has_prev:
        a_ref, w_ref, prev_ref, o_ref, wbf_ref = refs
    else:
        a_ref, w_ref, o_ref, wbf_ref = refs

    @pl.when(pl.program_id(1) == 0)
    def _():
        _cast_weight(w_ref, wbf_ref)

    acc = _dot(a_ref[...], wbf_ref[...])
    if has_prev:
        acc = prev_ref[...] + acc
    o_ref[...] = acc


def _ffn_down(a, w_down, sel, k_start, k_size, n_rows, prev=None):
    t = a.shape[0]
    d = w_down.shape[-1]
    tm, tn = ROW_TILE, COL_TILE
    assert k_start % k_size == 0 and d % tn == 0 and n_rows % tm == 0
    kb = k_start // k_size
    lead = (None,) * len(sel)
    in_specs = [pl.BlockSpec((tm, k_size), lambda j, i: (i, kb)),
                pl.BlockSpec(lead + (k_size, tn), lambda j, i: sel + (kb, j))]
    args = [a, w_down]
    blocks = [((tm, k_size), BF16), ((k_size, tn), F32), ((tm, tn), F32)]
    aliases = {}
    if prev is not None:
        in_specs.append(pl.BlockSpec((tm, tn), lambda j, i: (i, j)))
        aliases = {len(args): 0}
        args.append(prev)
        blocks.append(((tm, tn), F32))
    return pl.pallas_call(
        functools.partial(_ffn_down_kernel, has_prev=prev is not None),
        out_shape=jax.ShapeDtypeStruct((t, d), F32),
        grid=(d // tn, n_rows // tm),
        in_specs=in_specs,
        out_specs=pl.BlockSpec((tm, tn), lambda j, i: (i, j)),
        scratch_shapes=[pltpu.VMEM((k_size, tn), BF16)],
        input_output_aliases=aliases,
        compiler_params=_params(("arbitrary", "arbitrary"), blocks, scratch=[((k_size, tn), BF16)],
                                temps=[((tm, tn), F32)] * 2),
        name="ffn_down",
    )(*args)


def _router_kernel(h_ref, w_ref, o_ref, *, n_experts):
    pieces = _dot(h_ref[...], w_ref[...])
    logits = pieces
    for k in range(1, ROUTER_PIECES):
        logits = logits + pltpu.roll(pieces, V7X_LANES - k * n_experts, axis=1)
    lane = lax.broadcasted_iota(jnp.int32, logits.shape, 1).astype(F32)
    neg = -jnp.inf
    logits = jnp.where(lane < n_experts, logits, neg)
    picked = []
    remaining = logits
    for _ in range(TOP_K):
        top = jnp.max(remaining, axis=-1, keepdims=True)
        idx = jnp.min(jnp.where(remaining == top, lane, float(logits.shape[1])), axis=-1,
                      keepdims=True)
        picked.append((top, idx))
        remaining = jnp.where(lane == idx, neg, remaining)
    top0 = picked[0][0]
    denom = sum(jnp.exp(tv - top0) for tv, _ in picked)
    route = jnp.zeros(logits.shape, F32)
    for k, (tv, idx) in enumerate(picked):
        route = jnp.where(lane == k, idx, route)
        route = jnp.where(lane == TOP_K + k, jnp.exp(tv - top0) / denom, route)
    o_ref[...] = route


def _router(h2, w_router, n_rows):
    t, d = h2.shape
    n_experts = w_router.shape[1]
    assert ROUTER_PIECES * n_experts <= V7X_LANES
    pieces, rest = [], w_router
    for _ in range(ROUTER_PIECES):
        pieces.append(rest.astype(BF16))
        rest = rest - pieces[-1].astype(F32)
    wpad = jnp.pad(jnp.concatenate(pieces, axis=1),
                   ((0, 0), (0, V7X_LANES - ROUTER_PIECES * n_experts)))
    tr = EW_ROWS
    blocks = [((tr, d), BF16), ((d, V7X_LANES), BF16), ((tr, V7X_LANES), F32)]
    return pl.pallas_call(
        functools.partial(_router_kernel, n_experts=n_experts),
        out_shape=jax.ShapeDtypeStruct((t, V7X_LANES), F32),
        grid=(n_rows // tr,),
        in_specs=[pl.BlockSpec((tr, d), lambda i: (i, 0)),
                  pl.BlockSpec((d, V7X_LANES), lambda i: (0, 0))],
        out_specs=pl.BlockSpec((tr, V7X_LANES), lambda i: (i, 0)),
        compiler_params=_params(("arbitrary",), blocks, temps=[((tr, d), F32)] * 3),
        name="moe_router",
    )(h2, wpad)


def _route_plan(route, n_experts):
    tile = MOE_ROW_TILE
    n_tok = route.shape[0]
    n_asg = n_tok * TOP_K
    e_flat = route[:, :TOP_K].astype(jnp.int32).reshape(n_asg)
    onehot = (e_flat[:, None] == jnp.arange(n_experts, dtype=jnp.int32)[None, :]).astype(jnp.int32)
    csum = jnp.cumsum(onehot, axis=0)
    rank = jnp.take_along_axis(csum, e_flat[:, None], axis=1)[:, 0] - 1
    counts = csum[-1]
    tiles_per = (counts + tile - 1) // tile
    tile_end = jnp.cumsum(tiles_per)
    tile_start = tile_end - tiles_per
    dest = tile_start[e_flat] * tile + rank

    n_tiles = (n_asg + n_experts * (tile - 1)) // tile
    tile_ids = jnp.arange(n_tiles, dtype=jnp.int32)
    n_used = tile_end[-1]
    valid = tile_ids < n_used
    expert_raw = jnp.sum((tile_end[None, :] <= tile_ids[:, None]).astype(jnp.int32), axis=1)
    tile_expert = jnp.where(valid, expert_raw, expert_raw[n_used - 1])
    tile_first = valid & (tile_ids == tile_start[tile_expert])
    tile_row = jnp.where(valid, tile_ids, n_used - 1)

    tok_of_row = jnp.zeros((n_tiles * tile,), jnp.int32).at[dest].set(
        jnp.arange(n_asg, dtype=jnp.int32) // TOP_K, unique_indices=True)
    tile_first = tile_first.astype(jnp.int32)
    tile_slot = (jnp.cumsum(tile_first) - 1) % 2
    tile_fill = jnp.where(
        valid, jnp.clip(counts[tile_expert] - (tile_ids - tile_start[tile_expert]) * tile, 0, tile), 0)
    tiles = (tile_expert, tile_first, tile_fill, tile_row, tile_slot)
    assert tile % EW_ROWS == 0
    block_start = jnp.arange(n_tiles * tile // EW_ROWS, dtype=jnp.int32) * EW_ROWS
    live = (tile_fill[block_start // tile] > block_start % tile).astype(jnp.int32)
    return tok_of_row, live, dest, tiles


def _split_tiles(tiles, factor):
    tile_expert, tile_first, tile_fill, _, tile_slot = tiles
    n = tile_expert.shape[0] * factor
    sub = MOE_ROW_TILE // factor
    part = jnp.arange(n, dtype=jnp.int32) % factor
    fill = jnp.clip(jnp.repeat(tile_fill, factor) - part * sub, 0, sub)
    first = jnp.where(part == 0, jnp.repeat(tile_first, factor), 0)
    row = lax.cummax(jnp.where(fill > 0, jnp.arange(n, dtype=jnp.int32), 0))
    return jnp.repeat(tile_expert, factor), first, fill, row, jnp.repeat(tile_slot, factor)


def _row_copy(src_hbm, dst_vmem, sem, src_row, dst_row):
    return pltpu.make_async_copy(src_hbm.at[pl.ds(src_row, 1)], dst_vmem.at[pl.ds(dst_row, 1)], sem)


def _gather_rows_kernel(idx_ref, live_ref, src_ref, o_ref, buf_ref, sem):
    tile = buf_ref.shape[0]
    g = pl.program_id(0)
    base = g * tile

    def issue(r, carry):
        _row_copy(src_ref, buf_ref, sem, idx_ref[base + r], r).start()
        return carry

    def drain(r, carry):
        _row_copy(src_ref, buf_ref, sem, 0, r).wait()
        return carry

    @pl.when(live_ref[g] == 1)
    def _():
        lax.fori_loop(0, tile, issue, 0, unroll=GATHER_UNROLL)
        lax.fori_loop(0, tile, drain, 0, unroll=GATHER_UNROLL)
        o_ref[...] = _from_slabs(buf_ref).astype(BF16)

    @pl.when(live_ref[g] == 0)
    def _():
        o_ref[...] = jnp.zeros(o_ref.shape, BF16)


def _gather_rows(src, idx, live):
    slab = src.shape[1:]
    d = slab[0] * slab[1]
    n = idx.shape[0]
    tg = EW_ROWS
    assert n % tg == 0 and live.shape[0] == n // tg
    blocks = [((tg, d), BF16)]
    return pl.pallas_call(
        _gather_rows_kernel,
        out_shape=jax.ShapeDtypeStruct((n, d), BF16),
        grid_spec=pltpu.PrefetchScalarGridSpec(
            num_scalar_prefetch=2,
            grid=(n // tg,),
            in_specs=[pl.BlockSpec(memory_space=pl.ANY)],
            out_specs=pl.BlockSpec((tg, d), lambda i, idx_ref, live_ref: (i, 0)),
            scratch_shapes=[pltpu.VMEM((tg,) + slab, F32), pltpu.SemaphoreType.DMA]),
        compiler_params=_params(("arbitrary",), blocks, scratch=[((tg, d), F32)],
                                temps=[((tg, d), F32)]),
        name="moe_dispatch",
    )(idx, live, src)


def _held_tile(s, n_tiles):
    return jnp.minimum(s, n_tiles - 1)


def _done_tile(s):
    return jnp.maximum(s - 1, 0)


def _grouped_step(tables, w_refs, wbf_refs, compute, o_ref):
    _, tf_ref, tn_ref, _, ts_ref = tables
    n_tiles = tf_ref.shape[0]
    n_slots = wbf_refs[0].shape[0]
    tm = o_ref.shape[0]
    s = pl.program_id(1)
    held = _held_tile(s, n_tiles)
    done = _done_tile(s)
    fill = jnp.where(s > 0, tn_ref[done], -MOE_ROW_STEP)

    def slot(tile):
        return ts_ref[tile] % n_slots if n_slots > 1 else 0

    for n in range(0, tm + 1, MOE_ROW_STEP):
        @pl.when((fill > n - MOE_ROW_STEP) & (fill <= n))
        def _(n=n):
            if n > 0:
                o_ref[0:n, :] = compute([wbf_ref[slot(done)] for wbf_ref in wbf_refs], n)
            if n < tm:
                o_ref[n:, :] = jnp.zeros((tm - n, o_ref.shape[1]), o_ref.dtype)

    @pl.when((s < n_tiles) & (tf_ref[held] == 1))
    def _():
        for w_ref, wbf_ref in zip(w_refs, wbf_refs):
            _cast_weight(w_ref, wbf_ref.at[slot(held)])


def _moe_up_kernel(te_ref, tf_ref, tn_ref, tr_ref, ts_ref, x_ref, wg_ref, wu_ref, o_ref, wgbf_ref,
                   wubf_ref):
    def compute(w, n):
        x = x_ref[0:n, :]
        return (jax.nn.silu(_dot(x, w[0])) * _dot(x, w[1])).astype(BF16)

    _grouped_step((te_ref, tf_ref, tn_ref, tr_ref, ts_ref), (wg_ref, wu_ref), (wgbf_ref, wubf_ref),
                  compute, o_ref)


def _moe_up(xs, w_gate, w_up, layer, tiles):
    r, d = xs.shape
    f = w_gate.shape[-1]
    tm, tn = MOE_UP_ROW_TILE, COL_TILE
    assert f % tn == 0 and r % tm == 0
    tiles = _split_tiles(tiles, MOE_ROW_TILE // tm)
    nt = r // tm
    wspec = pl.BlockSpec((None, None, d, tn),
                         lambda j, s, te, tf, tv, tr, ts: (layer, te[_held_tile(s, nt)], 0, j))
    blocks = [((tm, d), BF16), ((d, tn), F32), ((d, tn), F32), ((tm, tn), BF16)]
    return pl.pallas_call(
        _moe_up_kernel,
        out_shape=jax.ShapeDtypeStruct((r, f), BF16),
        grid_spec=pltpu.PrefetchScalarGridSpec(
            num_scalar_prefetch=5,
            grid=(f // tn, nt + 1),
            in_specs=[pl.BlockSpec((tm, d), lambda j, s, te, tf, tv, tr, ts: (tr[_done_tile(s)], 0)),
                      wspec, wspec],
            out_specs=pl.BlockSpec((tm, tn), lambda j, s, te, tf, tv, tr, ts: (_done_tile(s), j)),
            scratch_shapes=[pltpu.VMEM((1, d, tn), BF16)] * 2),
        compiler_params=_params(("arbitrary", "arbitrary"), blocks,
                                scratch=[((1, d, tn), BF16)] * 2, temps=[((tm, tn), F32)] * 3),
        name="moe_up",
    )(*tiles, xs, w_gate, w_up)


def _moe_down_kernel(te_ref, tf_ref, tn_ref, tr_ref, ts_ref, a_ref, w_ref, o_ref, wbf_ref):
    _grouped_step((te_ref, tf_ref, tn_ref, tr_ref, ts_ref), (w_ref,), (wbf_ref,),
                  lambda w, n: _dot(a_ref[0:n, :], w[0]), o_ref)


def _moe_down(hmid, w_down, layer, tiles):
    r, f = hmid.shape
    d = w_down.shape[-1]
    tm, tn = MOE_ROW_TILE, COL_TILE
    assert d % tn == 0 and r % tm == 0
    nt = r // tm
    blocks = [((tm, f), BF16), ((f, tn), F32), ((tm, tn), F32)]
    return pl.pallas_call(
        _moe_down_kernel,
        out_shape=jax.ShapeDtypeStruct((r, d), F32),
        grid_spec=pltpu.PrefetchScalarGridSpec(
            num_scalar_prefetch=5,
            grid=(d // tn, nt + 1),
            in_specs=[pl.BlockSpec((tm, f), lambda j, s, te, tf, tv, tr, ts: (tr[_done_tile(s)], 0)),
                      pl.BlockSpec((None, None, f, tn),
                                   lambda j, s, te, tf, tv, tr, ts:
                                   (layer, te[_held_tile(s, nt)], 0, j))],
            out_specs=pl.BlockSpec((tm, tn), lambda j, s, te, tf, tv, tr, ts: (_done_tile(s), j)),
            scratch_shapes=[pltpu.VMEM((2, f, tn), BF16)]),
        compiler_params=_params(("arbitrary", "arbitrary"), blocks, scratch=[((2, f, tn), BF16)],
                                temps=[((tm, tn), F32)] * 2),
        name="moe_down",
    )(*tiles, hmid, w_down)


def _moe_final_kernel(dest_ref, x_ref, y_ref, route_ref, g_ref, mod_ref, o_ref, buf_ref, sems):
    tile = x_ref.shape[0]
    base = pl.program_id(0) * tile

    def issue(r, carry):
        for k in range(TOP_K):
            _row_copy(y_ref, buf_ref.at[k], sems.at[k], dest_ref[(base + r) * TOP_K + k], r).start()
        return carry

    def drain(r, carry):
        for k in range(TOP_K):
            _row_copy(y_ref, buf_ref.at[k], sems.at[k], 0, r).wait()
        return carry

    lax.fori_loop(0, tile, issue, 0, unroll=GATHER_UNROLL)
    lax.fori_loop(0, tile, drain, 0, unroll=GATHER_UNROLL)
    f = route_ref[:, TOP_K:TOP_K + 1] * buf_ref[0]
    for k in range(1, TOP_K):
        f = f + route_ref[:, TOP_K + k:TOP_K + k + 1] * buf_ref[k]
    o_ref[...] = x_ref[...] + mod_ref[5:6, :] * _rms(f, g_ref[...])


def _moe_final(x1, y, dest, route, g_post, mod, n_rows, seq, n_lat_rows, n_batch):
    d = x1.shape[1]
    tr = EW_ROWS
    ridx = functools.partial(_mod_row_index, rows_per_tile=tr, seq=seq, n_lat_rows=n_lat_rows,
                             n_batch=n_batch)
    row = pl.BlockSpec((tr, d), lambda i, dest_ref: (i, 0))
    blocks = [((tr, d), F32)] * 2 + [((N_MOD, d), F32)]
    return pl.pallas_call(
        _moe_final_kernel,
        out_shape=jax.ShapeDtypeStruct((n_rows, d), F32),
        grid_spec=pltpu.PrefetchScalarGridSpec(
            num_scalar_prefetch=1,
            grid=(n_rows // tr,),
            in_specs=[row, pl.BlockSpec(memory_space=pl.ANY),
                      pl.BlockSpec((tr, V7X_LANES), lambda i, dest_ref: (i, 0)),
                      pl.BlockSpec((1, d), lambda i, dest_ref: (0, 0)),
                      pl.BlockSpec((None, N_MOD, d), lambda i, dest_ref: (ridx(i), 0, 0))],
            out_specs=row,
            scratch_shapes=[pltpu.VMEM((TOP_K, tr, d), F32), pltpu.SemaphoreType.DMA((TOP_K,))]),
        compiler_params=_params(("arbitrary",), blocks, scratch=[((TOP_K, tr, d), F32)],
                                temps=[((tr, d), F32)] * 2),
        name="moe_combine_final",
    )(dest, x1, y, route, g_post.reshape(1, d), mod)


def _rope_tables(seq, n_batch, ctx_rows):
    n = jnp.arange(seq)
    pos_r = (n // GRID_W).astype(F32)
    pos_w = (n % GRID_W).astype(F32)
    n_freq = HEAD_DIM // 4
    inv_freq = ROPE_THETA ** (-jnp.arange(n_freq, dtype=F32) / n_freq)
    ar = pos_r[:, None] * inv_freq
    aw = pos_w[:, None] * inv_freq
    cos = jnp.concatenate([jnp.cos(ar), jnp.cos(ar), jnp.cos(aw), jnp.cos(aw)], axis=-1)
    sin = jnp.concatenate([-jnp.sin(ar), jnp.sin(ar), -jnp.sin(aw), jnp.sin(aw)], axis=-1)
    cos = jnp.concatenate([cos] * n_batch + [jnp.ones((ctx_rows, HEAD_DIM), F32)], axis=0)
    sin = jnp.concatenate([sin] * n_batch + [jnp.zeros((ctx_rows, HEAD_DIM), F32)], axis=0)
    return cos, sin


def kernel(x, c, ctx, c_ctx, w_mod, b_mod, g_pre_mix, g_post_mix, g_pre_ffn, g_post_ffn, w_in, w_out,
           sink_a, qn_c, kn_c, gm_ln_g, gm_ln_b, gm_ws, gm_bs, ffn_w_gate, ffn_w_up, ffn_w_down,
           moe_router, moe_w_gate, moe_w_up, moe_w_down):
    n_batch, seq, d = x.shape
    n_ctx = ctx.shape[1]
    depth = w_mod.shape[0]
    n_lat = n_batch * seq
    n_ctx_rows = n_batch * n_ctx
    t = n_lat + n_ctx_rows
    a_heads = sink_a.shape[1]
    a_w = a_heads * HEAD_DIM
    akv_w = A_KV_HEADS * HEAD_DIM
    b_w = gm_ln_g.shape[1]
    ckv_w = C_KV_HEADS * HEAD_DIM
    c_w = w_in.shape[2] - a_w - 2 * akv_w - 2 * b_w - 2 * ckv_w
    c_heads = c_w // HEAD_DIM
    assert n_ctx_rows % EW_ROWS == 0 and seq % EW_ROWS == 0 and n_batch + 1 <= 8
    assert a_w == c_w and akv_w == COL_TILE and ckv_w == COL_TILE

    x_parts = (x.reshape(n_lat, d), ctx.reshape(n_ctx_rows, d))
    c8 = jnp.concatenate([c, c_ctx[None, :], jnp.zeros((8 - n_batch - 1, d), F32)], axis=0)
    mod_all = _modulation(c8, w_mod, b_mod).reshape(depth, 8, N_MOD, d)

    cos, sin = _rope_tables(seq, n_batch, n_ctx_rows)
    geo = dict(seq=seq, n_lat_rows=n_lat, n_batch=n_batch)
    h = _prenorm(x_parts, g_pre_mix[0], mod_all[0], **geo)

    for l in range(depth):
        need_ctx = l < depth - 1
        n_rows = t if need_ctx else n_lat
        mod = mod_all[l]

        gq, gk = qn_c[l].reshape(1, HEAD_DIM), kn_c[l].reshape(1, HEAD_DIM)
        rope = (cos, sin, gq, gk)
        col = 0
        za = _inproj(h, w_in, l, col, a_w + akv_w, "rope", rope, n_q_tiles=a_w // COL_TILE)
        col += a_w + akv_w
        av = _inproj(h, w_in, l, col, akv_w, "plain")
        col += akv_w
        zb = _inproj(h, w_in, l, col, 2 * b_w, "gelu")
        col += 2 * b_w
        zc = _inproj(h, w_in, l, col, c_w + ckv_w, "normrope", rope, n_q_tiles=c_w // COL_TILE)
        col += c_w + ckv_w
        cv = _inproj(h, w_in, l, col, ckv_w, "plain")

        oa = _window_attn(za, av, sink_a[l], n_batch, seq, n_ctx, need_ctx)
        ob = _gmlp(zb, gm_ln_g[l], gm_ln_b[l], gm_ws[l], gm_bs[l], n_rows)
        oc = _global_attn(zc, cv, n_batch, seq, n_ctx, c_heads, need_ctx)

        y = _outproj(oa, ob, oc, w_out, l, n_rows)
        i = l // 2
        next_norm = (g_pre_mix[l + 1], mod_all[l + 1]) if need_ctx else None
        if l % 2 == 0:
            x1, h2 = _postmix(x_parts, y, g_post_mix[l], g_pre_ffn[l], mod, n_rows, 0, **geo)
            f_dim = ffn_w_gate.shape[-1]
            hmid = _ffn_up(h2, ffn_w_gate, ffn_w_up, (i,), n_rows)
            half = f_dim // 2
            f = _ffn_down(hmid, ffn_w_down, (i,), 0, half, n_rows)
            f = _ffn_down(hmid, ffn_w_down, (i,), half, half, n_rows, prev=f)
            xall = _final(x1, f, g_post_ffn[l], mod, n_rows, next_norm, **geo)
            if need_ctx:
                xall, h = xall
        else:
            x1, h2, h2f = _postmix(x_parts, y, g_post_mix[l], g_pre_ffn[l], mod, n_rows, 1, **geo)
            route = _router(h2, moe_router[i], n_rows)
            tok_of_row, live, dest, tiles = _route_plan(route, moe_router.shape[-1])
            xs = _gather_rows(h2f, tok_of_row, live)
            hmid = _moe_up(xs, moe_w_gate, moe_w_up, i, tiles)
            ys = _moe_down(hmid, moe_w_down, i, tiles)
            xall = _moe_final(x1, ys, dest, route, g_post_ffn[l], mod, n_rows, **geo)
            if need_ctx:
                h = _prenorm((xall,), next_norm[0], next_norm[1], **geo)
        x_parts = (xall,)
    return xall[:n_lat].reshape(n_batch, seq, d)
```

```python
import functools
import math

import jax
import jax.numpy as jnp
from jax import lax
from jax.experimental import pallas as pl
from jax.experimental.pallas import tpu as pltpu

F32 = jnp.float32
BF16 = jnp.bfloat16

GRID_W = 64
HEAD_DIM = 128
BLOCK = 128
WINDOW = 128
A_KV_HEADS = 4
C_KV_HEADS = 4
B_GROUPS = 8
CHUNK = 128
N_MOD = 6
TOP_K = 2
ROPE_THETA = 10000.0
EPS = 1e-6
LOG2E = math.log2(math.e)

V7X_LANES = 128
V7X_VMEM_SCOPED_CAP = 60000 * 1024

ROW_TILE = 512
MAX_ROW_TILE = 1088
ROW_ALIGN = 16
COL_TILE = 512
FFN_COL_TILE = 256
EW_ROWS = 256
CAST_ROWS = 256
Q_TILE = 256
K_TILE = 512
GATHER_UNROLL = 8
MOE_ROW_TILE = 1024
MOE_UP_ROW_TILE = 512
MOE_DOWN_COL_TILE = 1024
MOE_ROW_STEP = 256
ROUTER_PIECES = 3
EPILOGUE_CHUNKS = 4
SLAB_ROWS = 8


def _nbytes(shape, dtype):
    return math.prod(shape) * jnp.dtype(dtype).itemsize


def _params(semantics, blocks, scratch=(), temps=()):
    need = 2 * sum(_nbytes(s, d) for s, d in blocks)
    need += sum(_nbytes(s, d) for s, d in scratch)
    need += sum(_nbytes(s, d) for s, d in temps)
    limit = min(V7X_VMEM_SCOPED_CAP, max(need + need // 4, 16 * 1024 * 1024))
    return pltpu.CompilerParams(dimension_semantics=semantics, vmem_limit_bytes=limit)


def _row_tile(n_rows):
    for tm in range(MAX_ROW_TILE - MAX_ROW_TILE % ROW_ALIGN, 0, -ROW_ALIGN):
        if n_rows % tm == 0:
            return tm
    raise ValueError(f"no row tile for {n_rows} rows")


def _cast_weight(w_ref, wbf_ref):
    rows = w_ref.shape[0]
    step = CAST_ROWS if rows % CAST_ROWS == 0 else V7X_LANES
    assert rows % step == 0

    def body(r, carry):
        sl = pl.ds(pl.multiple_of(r * step, step), step)
        wbf_ref[sl, :] = w_ref[sl, :].astype(BF16)
        return carry

    lax.fori_loop(0, rows // step, body, 0)


def _dot(a, b):
    return jnp.dot(a, b, preferred_element_type=F32)


def _dot_nt(a, b):
    return lax.dot_general(a, b, (((1,), (1,)), ((), ())), preferred_element_type=F32)


def _rms(x, gain):
    return x * lax.rsqrt(jnp.mean(x * x, axis=-1, keepdims=True) + EPS) * gain


def _to_slabs(x, slab_ref):
    w = slab_ref.shape[2]
    for s in range(slab_ref.shape[1]):
        slab_ref[:, s, :] = x[:, s * w:(s + 1) * w]


def _from_slabs(slab_ref):
    return jnp.concatenate([slab_ref[:, s, :] for s in range(slab_ref.shape[1])], axis=1)


def _mod_kernel(c_ref, w_ref, b_ref, o_ref):
    s = jax.nn.silu(c_ref[...])
    o_ref[...] = _dot(s.astype(BF16), w_ref[...].astype(BF16)) + b_ref[...]


def _modulation(c8, w_mod, b_mod):
    n_layers, d, width = w_mod.shape
    tn = COL_TILE
    assert width % tn == 0
    blocks = [((8, d), F32), ((d, tn), F32), ((1, tn), F32), ((8, tn), F32)]
    return pl.pallas_call(
        _mod_kernel,
        out_shape=jax.ShapeDtypeStruct((n_layers, 8, width), F32),
        grid=(n_layers, width // tn),
        in_specs=[pl.BlockSpec((8, d), lambda l, j: (0, 0)),
                  pl.BlockSpec((None, d, tn), lambda l, j: (l, 0, j)),
                  pl.BlockSpec((None, 1, tn), lambda l, j: (l, 0, j))],
        out_specs=pl.BlockSpec((None, 8, tn), lambda l, j: (l, 0, j)),
        compiler_params=_params(("arbitrary", "arbitrary"), blocks, temps=[((d, tn), BF16)]),
        name="modulation",
    )(c8, w_mod, b_mod.reshape(n_layers, 1, width))


def _mod_row_index(i, rows_per_tile, seq, n_lat_rows, n_batch):
    lat_tiles = n_lat_rows // rows_per_tile
    return jnp.where(i < lat_tiles, (i * rows_per_tile) // seq, n_batch)


def _row_parts(parts, tr):
    specs, starts, start = [], [], 0
    for p in parts:
        n = p.shape[0] // tr
        assert n * tr == p.shape[0]
        specs.append(pl.BlockSpec((tr, p.shape[1]),
                                  lambda i, *_, start=start, n=n: (jnp.clip(i - start, 0, n - 1), 0)))
        starts.append(start)
        start += n
    return specs, tuple(starts)


def _pick_rows(i, x_refs, starts):
    x = x_refs[0][...]
    for ref, start in zip(x_refs[1:], starts[1:]):
        x = jnp.where(i >= start, ref[...], x)
    return x


def _pre_mix_norm(x, g_ref, mod_ref):
    return (_rms(x, g_ref[...]) * (1.0 + mod_ref[1:2, :]) + mod_ref[0:1, :]).astype(BF16)


def _prenorm_kernel(*refs, starts):
    *x_refs, g_ref, mod_ref, o_ref = refs
    o_ref[...] = _pre_mix_norm(_pick_rows(pl.program_id(0), x_refs, starts), g_ref, mod_ref)


def _prenorm(x_parts, gain, mod, seq, n_lat_rows, n_batch):
    d = x_parts[0].shape[1]
    t = sum(p.shape[0] for p in x_parts)
    tr = EW_ROWS
    ridx = functools.partial(_mod_row_index, rows_per_tile=tr, seq=seq, n_lat_rows=n_lat_rows,
                             n_batch=n_batch)
    x_specs, starts = _row_parts(x_parts, tr)
    blocks = [((tr, d), F32)] * len(x_parts) + [((1, d), F32), ((N_MOD, d), F32), ((tr, d), BF16)]
    return pl.pallas_call(
        functools.partial(_prenorm_kernel, starts=starts),
        out_shape=jax.ShapeDtypeStruct((t, d), BF16),
        grid=(t // tr,),
        in_specs=x_specs + [pl.BlockSpec((1, d), lambda i: (0, 0)),
                            pl.BlockSpec((None, N_MOD, d), lambda i: (ridx(i), 0, 0))],
        out_specs=pl.BlockSpec((tr, d), lambda i: (i, 0)),
        compiler_params=_params(("arbitrary",), blocks, temps=[((tr, d), F32)] * 2),
        name="prenorm",
    )(*x_parts, gain.reshape(1, d), mod)


def _postmix_kernel(*refs, starts, n_parts):
    x_refs = refs[:n_parts]
    y_ref, gpost_ref, gpre_ref, mod_ref, x1_ref, h2_ref, *h2f_ref = refs[n_parts:]
    x = _pick_rows(pl.program_id(0), x_refs, starts)
    x1 = x + mod_ref[2:3, :] * _rms(y_ref[...].astype(F32), gpost_ref[...])
    x1_ref[...] = x1
    h2 = _rms(x1, gpre_ref[...]) * (1.0 + mod_ref[4:5, :]) + mod_ref[3:4, :]
    h2_ref[...] = h2.astype(BF16)
    for ref in h2f_ref:
        _to_slabs(h2, ref)


def _postmix(x_parts, y, g_post, g_pre, mod, n_rows, want_slabs, seq, n_lat_rows, n_batch):
    d = x_parts[0].shape[1]
    tr = EW_ROWS
    x_specs, starts = _row_parts(x_parts, tr)
    assert d % (SLAB_ROWS * V7X_LANES) == 0
    slab = (SLAB_ROWS, d // SLAB_ROWS)
    ridx = functools.partial(_mod_row_index, rows_per_tile=tr, seq=seq, n_lat_rows=n_lat_rows,
                             n_batch=n_batch)
    row = pl.BlockSpec((tr, d), lambda i: (i, 0))
    vec = pl.BlockSpec((1, d), lambda i: (0, 0))
    blocks = ([((tr, d), F32)] * (len(x_parts) + 1 + want_slabs)
              + [((tr, d), BF16)] * 2 + [((N_MOD, d), F32)])
    out_shape = [jax.ShapeDtypeStruct((n_rows, d), F32), jax.ShapeDtypeStruct((n_rows, d), BF16)]
    out_specs = [row, row]
    if want_slabs:
        out_shape.append(jax.ShapeDtypeStruct((n_rows,) + slab, F32))
        out_specs.append(pl.BlockSpec((tr,) + slab, lambda i: (i, 0, 0)))
    return pl.pallas_call(
        functools.partial(_postmix_kernel, starts=starts, n_parts=len(x_parts)),
        out_shape=tuple(out_shape),
        grid=(n_rows // tr,),
        in_specs=x_specs + [row, vec, vec,
                            pl.BlockSpec((None, N_MOD, d), lambda i: (ridx(i), 0, 0))],
        out_specs=tuple(out_specs),
        compiler_params=_params(("arbitrary",), blocks, temps=[((tr, d), F32)] * 3),
        name="postmix",
    )(*x_parts, y, g_post.reshape(1, d), g_pre.reshape(1, d), mod)


def _final_kernel(x_ref, f_ref, g_ref, mod_ref, *rest):
    x = x_ref[...] + mod_ref[5:6, :] * _rms(f_ref[...], g_ref[...])
    if len(rest) == 1:
        (o_ref,) = rest
    else:
        gnext_ref, modnext_ref, o_ref, h_ref = rest
        h_ref[...] = _pre_mix_norm(x, gnext_ref, modnext_ref)
    o_ref[...] = x


def _final(x1, f, g_post, mod, n_rows, next_norm, seq, n_lat_rows, n_batch):
    d = x1.shape[1]
    tr = EW_ROWS
    ridx = functools.partial(_mod_row_index, rows_per_tile=tr, seq=seq, n_lat_rows=n_lat_rows,
                             n_batch=n_batch)
    row = pl.BlockSpec((tr, d), lambda i: (i, 0))
    vec = pl.BlockSpec((1, d), lambda i: (0, 0))
    modspec = pl.BlockSpec((None, N_MOD, d), lambda i: (ridx(i), 0, 0))
    blocks = [((tr, d), F32)] * 3 + [((N_MOD, d), F32)]
    in_specs, args = [row, row, vec, modspec], [x1, f, g_post.reshape(1, d), mod]
    out_shape, out_specs = jax.ShapeDtypeStruct((n_rows, d), F32), row
    if next_norm is not None:
        in_specs += [vec, modspec]
        args += [next_norm[0].reshape(1, d), next_norm[1]]
        out_shape = (out_shape, jax.ShapeDtypeStruct((n_rows, d), BF16))
        out_specs = (row, row)
        blocks += [((N_MOD, d), F32), ((tr, d), BF16)]
    return pl.pallas_call(
        _final_kernel,
        out_shape=out_shape,
        grid=(n_rows // tr,),
        in_specs=in_specs,
        out_specs=out_specs,
        compiler_params=_params(("arbitrary",), blocks, temps=[((tr, d), F32)] * 2),
        name="final_residual",
    )(*args)


def _swap_pairs(x):
    lane = lax.broadcasted_iota(jnp.int32, x.shape, 1)
    quarter = HEAD_DIM // 4
    first = (lane & quarter) == 0
    return jnp.where(first, pltpu.roll(x, HEAD_DIM - quarter, axis=1), pltpu.roll(x, quarter, axis=1))


def _inproj_kernel(*refs, mode, n_q_tiles, q_scale):
    if mode in ("rope", "normrope"):
        h_ref, w_ref, cos_ref, sin_ref, gq_ref, gk_ref, o_ref, wbf_ref = refs
    else:
        h_ref, w_ref, o_ref, wbf_ref = refs
    j = pl.program_id(0)

    @pl.when(pl.program_id(1) == 0)
    def _():
        _cast_weight(w_ref, wbf_ref)

    tm = h_ref.shape[0]
    n_chunks = EPILOGUE_CHUNKS if tm % (EPILOGUE_CHUNKS * ROW_ALIGN) == 0 else 1
    rc = tm // n_chunks
    for c in range(n_chunks):
        rows = slice(c * rc, (c + 1) * rc)
        acc = _dot(h_ref[rows, :], wbf_ref[...])
        if mode == "plain":
            o_ref[rows, :] = acc.astype(BF16)
        elif mode == "gelu":
            o_ref[rows, :] = (0.5 * acc * (1.0 + lax.erf(acc * (2.0 ** -0.5)))).astype(BF16)
        else:
            is_q = j < n_q_tiles
            scale = jnp.where(is_q, q_scale, 1.0).astype(F32)
            cos = cos_ref[rows, :]
            sin = sin_ref[rows, :]
            gain = jnp.where(is_q, gq_ref[...], gk_ref[...])
            for hh in range(acc.shape[1] // HEAD_DIM):
                sl = slice(hh * HEAD_DIM, (hh + 1) * HEAD_DIM)
                xh = acc[:, sl]
                if mode == "normrope":
                    xh = _rms(xh, gain)
                xh = xh * cos + _swap_pairs(xh) * sin
                o_ref[rows, sl] = (xh * scale).astype(BF16)


def _inproj(h, w_in, layer, col_start, width, mode, rope=None, n_q_tiles=0):
    t, d = h.shape
    tm, tn = _row_tile(t), COL_TILE
    assert col_start % tn == 0 and width % tn == 0
    j0 = col_start // tn
    in_specs = [pl.BlockSpec((tm, d), lambda j, i: (i, 0)),
                pl.BlockSpec((None, d, tn), lambda j, i: (layer, 0, j0 + j))]
    args = [h, w_in]
    blocks = [((tm, d), BF16), ((d, tn), F32), ((tm, tn), BF16)]
    if mode in ("rope", "normrope"):
        cos, sin, gq, gk = rope
        tab = pl.BlockSpec((tm, HEAD_DIM), lambda j, i: (i, 0))
        vec = pl.BlockSpec((1, HEAD_DIM), lambda j, i: (0, 0))
        in_specs += [tab, tab, vec, vec]
        args += [cos, sin, gq, gk]
        blocks += [((tm, HEAD_DIM), F32)] * 2
    kern = functools.partial(_inproj_kernel, mode=mode, n_q_tiles=n_q_tiles,
                             q_scale=HEAD_DIM ** -0.5 * LOG2E)
    return pl.pallas_call(
        kern,
        out_shape=jax.ShapeDtypeStruct((t, width), BF16),
        grid=(width // tn, t // tm),
        in_specs=in_specs,
        out_specs=pl.BlockSpec((tm, tn), lambda j, i: (i, j)),
        scratch_shapes=[pltpu.VMEM((d, tn), BF16)],
        compiler_params=_params(("arbitrary", "arbitrary"), blocks, scratch=[((d, tn), BF16)],
                                temps=[((tm, tn), F32)] * 2),
        name="inproj_" + mode,
    )(*args)


def _window_bias(group, n_ctx):
    r = jnp.arange(group * BLOCK)[:, None] % BLOCK
    c = jnp.arange(3 * BLOCK + n_ctx)[None, :]
    in_prev, in_cur = c < BLOCK, (c >= BLOCK) & (c < 2 * BLOCK)
    in_next = (c >= 2 * BLOCK) & (c < 3 * BLOCK)
    band_prev = in_prev & (c < r)
    band_next = in_next & (c - 2 * BLOCK > r)
    hidden = [band_prev | band_next, in_prev | band_next, band_prev | in_next, in_prev | in_next,
              in_prev | in_cur | in_next]
    shape = (group * BLOCK, 3 * BLOCK + n_ctx)
    return jnp.stack([jnp.where(jnp.broadcast_to(h, shape), -jnp.inf, 0.0).astype(F32)
                      for h in hidden])


def _window_attn_kernel(sink_ref, bias_ref, q_ref, kp_ref, kc_ref, kn_ref, kx_ref, vp_ref, vc_ref,
                        vn_ref, vx_ref, o_ref, *, group):
    rows = group * BLOCK
    assert BLOCK & (BLOCK - 1) == 0
    shift = BLOCK.bit_length() - 1
    bias = bias_ref[...]
    rid = lax.broadcasted_iota(jnp.int32, (rows, 1), 0) >> shift
    for hk in range(A_KV_HEADS):
        ksl = slice(hk * HEAD_DIM, (hk + 1) * HEAD_DIM)
        k_all = jnp.concatenate([kp_ref[:, ksl], kc_ref[:, ksl], kn_ref[:, ksl], kx_ref[:, ksl]],
                                axis=0)
        v_all = jnp.concatenate([vp_ref[:, ksl], vc_ref[:, ksl], vn_ref[:, ksl], vx_ref[:, ksl]],
                                axis=0)
        q3 = jnp.concatenate(
            [q_ref[:, (hk * group + g) * HEAD_DIM:(hk * group + g + 1) * HEAD_DIM]
             for g in range(group)], axis=0)
        sink = jnp.zeros((rows, 1), F32)
        for g in range(group):
            sink = jnp.where(rid == g, sink_ref[hk * group + g] * LOG2E, sink)
        s = _dot_nt(q3, k_all) + bias
        m = jnp.maximum(jnp.max(s, axis=-1, keepdims=True), sink)
        p = jnp.exp2(s - m)
        denom = jnp.sum(p, axis=-1, keepdims=True) + jnp.exp2(sink - m)
        o = _dot(p.astype(BF16), v_all) / denom
        for g in range(group):
            osl = slice((hk * group + g) * HEAD_DIM, (hk * group + g + 1) * HEAD_DIM)
            o_ref[:, osl] = o[g * BLOCK:(g + 1) * BLOCK, :].astype(BF16)


def _window_attn(za, av, sink, n_batch, seq, n_ctx, with_ctx):
    n_heads = sink.shape[0]
    group = n_heads // A_KV_HEADS
    qw = n_heads * HEAD_DIM
    kw = A_KV_HEADS * HEAD_DIM
    assert qw % kw == 0 and seq % BLOCK == 0 and (n_batch * seq) % n_ctx == 0 and WINDOW == BLOCK
    assert n_ctx % BLOCK == 0
    nb = seq // BLOCK
    ncb = n_ctx // BLOCK if with_ctx else 0
    kcol = qw // kw
    ctx0 = (n_batch * seq) // n_ctx

    def qblk(b, n):
        return (jnp.where(n < nb, b * nb + n, n_batch * nb + b * ncb + n - nb), 0)

    def blk(shift):
        return lambda b, n: (b * nb + jnp.clip(n + shift, 0, nb - 1), kcol)

    def vblk(shift):
        return lambda b, n: (b * nb + jnp.clip(n + shift, 0, nb - 1), 0)

    def bias_variant(b, n):
        lat = jnp.where(n == 0, 1, 0) + jnp.where(n == nb - 1, 2, 0)
        return (jnp.where(n < nb, lat, 4), 0, 0)

    kspec = [pl.BlockSpec((BLOCK, kw), blk(s)) for s in (-1, 0, 1)]
    vspec = [pl.BlockSpec((BLOCK, kw), vblk(s)) for s in (-1, 0, 1)]
    n_keys = 3 * BLOCK + n_ctx
    rows = group * BLOCK
    blocks = ([((BLOCK, qw), BF16)] * 2 + [((BLOCK, kw), BF16)] * 6 + [((n_ctx, kw), BF16)] * 2
              + [((rows, n_keys), F32)])
    return pl.pallas_call(
        functools.partial(_window_attn_kernel, group=group),
        out_shape=jax.ShapeDtypeStruct((n_batch * (nb + ncb) * BLOCK, qw), BF16),
        grid=(n_batch, nb + ncb),
        in_specs=[pl.BlockSpec(memory_space=pltpu.SMEM),
                  pl.BlockSpec((None, rows, n_keys), bias_variant),
                  pl.BlockSpec((BLOCK, qw), qblk)]
                 + kspec + [pl.BlockSpec((n_ctx, kw), lambda b, n: (ctx0 + b, kcol))]
                 + vspec + [pl.BlockSpec((n_ctx, kw), lambda b, n: (ctx0 + b, 0))],
        out_specs=pl.BlockSpec((BLOCK, qw), qblk),
        compiler_params=_params(("arbitrary", "arbitrary"), blocks,
                                temps=[((rows, n_keys), F32)] * 4),
        name="window_attn",
    )(sink, _window_bias(group, n_ctx), za, za, za, za, za, av, av, av, av)


def _gmlp_kernel(z_ref, g_ref, b_ref, ws_ref, bst_ref, o_ref):
    width = g_ref.shape[1]
    u = z_ref[:, :width].astype(F32)
    v = z_ref[:, width:].astype(F32)
    mu = jnp.mean(v, axis=-1, keepdims=True)
    vc = v - mu
    var = jnp.mean(vc * vc, axis=-1, keepdims=True)
    vn = (vc * lax.rsqrt(var + EPS) * g_ref[...] + b_ref[...]).astype(BF16)
    gd = width // B_GROUPS
    for g in range(B_GROUPS):
        sl = slice(g * gd, (g + 1) * gd)
        mixed = _dot(ws_ref[g].astype(BF16), vn[:, sl]) + bst_ref[:, g:g + 1]
        o_ref[:, sl] = (u[:, sl] * mixed).astype(BF16)


def _gmlp(zb, ln_g, ln_b, ws, bs, n_rows):
    w2 = zb.shape[1]
    width = w2 // 2
    blocks = [((CHUNK, w2), BF16), ((1, width), F32), ((1, width), F32),
              (ws.shape, F32), ((CHUNK, B_GROUPS), F32), ((CHUNK, width), BF16)]
    return pl.pallas_call(
        _gmlp_kernel,
        out_shape=jax.ShapeDtypeStruct((n_rows, width), BF16),
        grid=(n_rows // CHUNK,),
        in_specs=[pl.BlockSpec((CHUNK, w2), lambda i: (i, 0)),
                  pl.BlockSpec((1, width), lambda i: (0, 0)),
                  pl.BlockSpec((1, width), lambda i: (0, 0)),
                  pl.BlockSpec(ws.shape, lambda i: (0, 0, 0)),
                  pl.BlockSpec((CHUNK, B_GROUPS), lambda i: (0, 0))],
        out_specs=pl.BlockSpec((CHUNK, width), lambda i: (i, 0)),
        compiler_params=_params(("arbitrary",), blocks, temps=[((CHUNK, w2), F32)] * 2),
        name="chunk_gmlp",
    )(zb, ln_g.reshape(1, width), ln_b.reshape(1, width), ws, bs.T)


def _global_attn_kernel(q_ref, kx_ref, vx_ref, k_ref, v_ref, o_ref, vext_ref, *, group,
                        n_lat_tiles, has_ctx_tiles):
    tq = q_ref.shape[0]
    n_ctx = kx_ref.shape[0]
    n_lat = k_ref.shape[0]
    rows = group * tq

    @pl.when(pl.program_id(2) == 0)
    def _():
        vext_ref[0:n_ctx, 0:HEAD_DIM] = vx_ref[...]
        vext_ref[n_ctx:, 0:HEAD_DIM] = v_ref[...]
        vext_ref[:, HEAD_DIM:] = jnp.ones((n_ctx + n_lat, HEAD_DIM), BF16)

    def attend(chunks):
        q3 = jnp.concatenate([q_ref[:, g * HEAD_DIM:(g + 1) * HEAD_DIM] for g in range(group)],
                             axis=0)
        m = jnp.full((rows, HEAD_DIM), -jnp.inf, F32)
        acc = jnp.zeros((rows, 2 * HEAD_DIM), F32)
        for k_chunk, v0, nk in chunks:
            s = _dot_nt(q3, k_chunk())
            m_new = jnp.maximum(m, jnp.broadcast_to(jnp.max(s, axis=-1, keepdims=True), m.shape))
            alpha = jnp.exp2(m - m_new)
            p = jnp.exp2(s - jnp.tile(m_new, (1, nk // HEAD_DIM)))
            acc = jnp.tile(alpha, (1, 2)) * acc + _dot(p.astype(BF16), vext_ref[v0:v0 + nk, :])
            m = m_new
        o = acc[:, :HEAD_DIM] / acc[:, HEAD_DIM:]
        for g in range(group):
            o_ref[:, g * HEAD_DIM:(g + 1) * HEAD_DIM] = o[g * tq:(g + 1) * tq, :].astype(BF16)

    ctx_chunk = [(lambda: kx_ref[...], 0, n_ctx)]
    lat_chunks = [(lambda c=c: k_ref[c:c + K_TILE, :], n_ctx + c, K_TILE)
                  for c in range(0, n_lat, K_TILE)]
    is_lat = pl.program_id(2) < n_lat_tiles
    pl.when(is_lat)(lambda: attend(ctx_chunk + lat_chunks))
    if has_ctx_tiles:
        pl.when(jnp.logical_not(is_lat))(lambda: attend(ctx_chunk))


def _global_attn(zc, cv, n_batch, seq, n_ctx, n_heads, with_ctx):
    group = n_heads // C_KV_HEADS
    gw = group * HEAD_DIM
    tq = Q_TILE
    assert seq % tq == 0 and seq % K_TILE == 0 and (n_batch * seq) % n_ctx == 0 and n_ctx % tq == 0
    nq = seq // tq
    ncq = n_ctx // tq if with_ctx else 0
    ctx0 = (n_batch * seq) // n_ctx
    rows = group * tq

    def qblk(b, hk, i):
        return (jnp.where(i < nq, b * nq + i, n_batch * nq + b * ncq + i - nq), hk)

    blocks = [((tq, gw), BF16)] * 2 + [((n_ctx, HEAD_DIM), BF16)] * 2 + [((seq, HEAD_DIM), BF16)] * 2
    scratch = [((n_ctx + seq, 2 * HEAD_DIM), BF16)]
    return pl.pallas_call(
        functools.partial(_global_attn_kernel, group=group, n_lat_tiles=nq, has_ctx_tiles=ncq > 0),
        out_shape=jax.ShapeDtypeStruct((n_batch * (nq + ncq) * tq, n_heads * HEAD_DIM), BF16),
        grid=(n_batch, C_KV_HEADS, nq + ncq),
        in_specs=[pl.BlockSpec((tq, gw), qblk),
                  pl.BlockSpec((n_ctx, HEAD_DIM), lambda b, hk, i: (ctx0 + b, n_heads + hk)),
                  pl.BlockSpec((n_ctx, HEAD_DIM), lambda b, hk, i: (ctx0 + b, hk)),
                  pl.BlockSpec((seq, HEAD_DIM), lambda b, hk, i: (b, n_heads + hk)),
                  pl.BlockSpec((seq, HEAD_DIM), lambda b, hk, i: (b, hk))],
        out_specs=pl.BlockSpec((tq, gw), qblk),
        scratch_shapes=[pltpu.VMEM(s, d) for s, d in scratch],
        compiler_params=_params(("arbitrary", "arbitrary", "arbitrary"), blocks, scratch=scratch,
                                temps=[((rows, K_TILE), F32)] * 6),
        name="global_attn",
    )(zc, zc, cv, zc, cv)


def _outproj_kernel(a_ref, b_ref, c_ref, w_ref, o_ref, wbf_ref):
    @pl.when(pl.program_id(1) == 0)
    def _():
        _cast_weight(w_ref, wbf_ref)

    ka, kb = a_ref.shape[1], b_ref.shape[1]
    acc = _dot(a_ref[...], wbf_ref[0:ka, :])
    acc += _dot(b_ref[...], wbf_ref[ka:ka + kb, :])
    acc += _dot(c_ref[...], wbf_ref[ka + kb:, :])
    o_ref[...] = acc.astype(BF16)


def _outproj(oa, ob, oc, w_out, layer, n_rows):
    t = oa.shape[0]
    _, k, d = w_out.shape
    tm, tn = _row_tile(n_rows), COL_TILE
    assert oa.shape[1] + ob.shape[1] + oc.shape[1] == k and d % tn == 0
    blocks = [((tm, k), BF16), ((k, tn), F32), ((tm, tn), BF16)]
    return pl.pallas_call(
        _outproj_kernel,
        out_shape=jax.ShapeDtypeStruct((t, d), BF16),
        grid=(d // tn, n_rows // tm),
        in_specs=[pl.BlockSpec((tm, oa.shape[1]), lambda j, i: (i, 0)),
                  pl.BlockSpec((tm, ob.shape[1]), lambda j, i: (i, 0)),
                  pl.BlockSpec((tm, oc.shape[1]), lambda j, i: (i, 0)),
                  pl.BlockSpec((None, k, tn), lambda j, i: (layer, 0, j))],
        out_specs=pl.BlockSpec((tm, tn), lambda j, i: (i, j)),
        scratch_shapes=[pltpu.VMEM((k, tn), BF16)],
        compiler_params=_params(("arbitrary", "arbitrary"), blocks, scratch=[((k, tn), BF16)],
                                temps=[((tm, tn), F32)] * 2),
        name="outproj",
    )(oa, ob, oc, w_out)


def _ffn_up_kernel(h_ref, wg_ref, wu_ref, o_ref, wgbf_ref, wubf_ref):
    @pl.when(pl.program_id(1) == 0)
    def _():
        _cast_weight(wg_ref, wgbf_ref)
        _cast_weight(wu_ref, wubf_ref)

    h = h_ref[...]
    gate = _dot(h, wgbf_ref[...])
    up = _dot(h, wubf_ref[...])
    o_ref[...] = (jax.nn.silu(gate) * up).astype(BF16)


def _ffn_up(h2, w_gate, w_up, sel, n_rows):
    t, d = h2.shape
    f = w_gate.shape[-1]
    tm, tn = _row_tile(n_rows), FFN_COL_TILE
    assert f % tn == 0
    lead = (None,) * len(sel)
    wspec = pl.BlockSpec(lead + (d, tn), lambda j, i: sel + (0, j))
    blocks = [((tm, d), BF16), ((d, tn), F32), ((d, tn), F32), ((tm, tn), BF16)]
    return pl.pallas_call(
        _ffn_up_kernel,
        out_shape=jax.ShapeDtypeStruct((t, f), BF16),
        grid=(f // tn, n_rows // tm),
        in_specs=[pl.BlockSpec((tm, d), lambda j, i: (i, 0)), wspec, wspec],
        out_specs=pl.BlockSpec((tm, tn), lambda j, i: (i, j)),
        scratch_shapes=[pltpu.VMEM((d, tn), BF16)] * 2,
        compiler_params=_params(("arbitrary", "arbitrary"), blocks, scratch=[((d, tn), BF16)] * 2,
                                temps=[((tm, tn), F32)] * 3),
        name="ffn_up",
    )(h2, w_gate, w_up)


def _ffn_down_kernel(*refs, has_prev):
    if has_prev:
        a_ref, w_ref, prev_ref, o_ref, wbf_ref = refs
    else:
        a_ref, w_ref, o_ref, wbf_ref = refs

    @pl.when(pl.program_id(1) == 0)
    def _():
        _cast_weight(w_ref, wbf_ref)

    acc = _dot(a_ref[...], wbf_ref[...])
    if has_prev:
        acc = prev_ref[...] + acc
    o_ref[...] = acc


def _ffn_down(a, w_down, sel, k_start, k_size, n_rows, prev=None):
    t = a.shape[0]
    d = w_down.shape[-1]
    tm, tn = ROW_TILE, COL_TILE
    assert k_start % k_size == 0 and d % tn == 0 and n_rows % tm == 0
    kb = k_start // k_size
    lead = (None,) * len(sel)
    in_specs = [pl.BlockSpec((tm, k_size), lambda j, i: (i, kb)),
                pl.BlockSpec(lead + (k_size, tn), lambda j, i: sel + (kb, j))]
    args = [a, w_down]
    blocks = [((tm, k_size), BF16), ((k_size, tn), F32), ((tm, tn), F32)]
    aliases = {}
    if prev is not None:
        in_specs.append(pl.BlockSpec((tm, tn), lambda j, i: (i, j)))
        aliases = {len(args): 0}
        args.append(prev)
        blocks.append(((tm, tn), F32))
    return pl.pallas_call(
        functools.partial(_ffn_down_kernel, has_prev=prev is not None),
        out_shape=jax.ShapeDtypeStruct((t, d), F32),
        grid=(d // tn, n_rows // tm),
        in_specs=in_specs,
        out_specs=pl.BlockSpec((tm, tn), lambda j, i: (i, j)),
        scratch_shapes=[pltpu.VMEM((k_size, tn), BF16)],
        input_output_aliases=aliases,
        compiler_params=_params(("arbitrary", "arbitrary"), blocks, scratch=[((k_size, tn), BF16)],
                                temps=[((tm, tn), F32)] * 2),
        name="ffn_down",
    )(*args)


def _router_kernel(h_ref, w_ref, o_ref, *, n_experts):
    pieces = _dot(h_ref[...], w_ref[...])
    logits = pieces
    for k in range(1, ROUTER_PIECES):
        logits = logits + pltpu.roll(pieces, V7X_LANES - k * n_experts, axis=1)
    lane = lax.broadcasted_iota(jnp.int32, logits.shape, 1).astype(F32)
    neg = -jnp.inf
    logits = jnp.where(lane < n_experts, logits, neg)
    picked = []
    remaining = logits
    for _ in range(TOP_K):
        top = jnp.max(remaining, axis=-1, keepdims=True)
        idx = jnp.min(jnp.where(remaining == top, lane, float(logits.shape[1])), axis=-1,
                      keepdims=True)
        picked.append((top, idx))
        remaining = jnp.where(lane == idx, neg, remaining)
    top0 = picked[0][0]
    denom = sum(jnp.exp(tv - top0) for tv, _ in picked)
    route = jnp.zeros(logits.shape, F32)
    for k, (tv, idx) in enumerate(picked):
        route = jnp.where(lane == k, idx, route)
        route = jnp.where(lane == TOP_K + k, jnp.exp(tv - top0) / denom, route)
    o_ref[...] = route


def _router(h2, w_router, n_rows):
    t, d = h2.shape
    n_experts = w_router.shape[1]
    assert ROUTER_PIECES * n_experts <= V7X_LANES
    pieces, rest = [], w_router
    for _ in range(ROUTER_PIECES):
        pieces.append(rest.astype(BF16))
        rest = rest - pieces[-1].astype(F32)
    wpad = jnp.pad(jnp.concatenate(pieces, axis=1),
                   ((0, 0), (0, V7X_LANES - ROUTER_PIECES * n_experts)))
    tr = EW_ROWS
    blocks = [((tr, d), BF16), ((d, V7X_LANES), BF16), ((tr, V7X_LANES), F32)]
    return pl.pallas_call(
        functools.partial(_router_kernel, n_experts=n_experts),
        out_shape=jax.ShapeDtypeStruct((t, V7X_LANES), F32),
        grid=(n_rows // tr,),
        in_specs=[pl.BlockSpec((tr, d), lambda i: (i, 0)),
                  pl.BlockSpec((d, V7X_LANES), lambda i: (0, 0))],
        out_specs=pl.BlockSpec((tr, V7X_LANES), lambda i: (i, 0)),
        compiler_params=_params(("arbitrary",), blocks, temps=[((tr, d), F32)] * 3),
        name="moe_router",
    )(h2, wpad)


def _route_plan(route, n_experts):
    tile = MOE_ROW_TILE
    n_tok = route.shape[0]
    n_asg = n_tok * TOP_K
    e_flat = route[:, :TOP_K].astype(jnp.int32).reshape(n_asg)
    onehot = (e_flat[:, None] == jnp.arange(n_experts, dtype=jnp.int32)[None, :]).astype(jnp.int32)
    csum = jnp.cumsum(onehot, axis=0)
    rank = jnp.take_along_axis(csum, e_flat[:, None], axis=1)[:, 0] - 1
    counts = csum[-1]
    tiles_per = (counts + tile - 1) // tile
    tile_end = jnp.cumsum(tiles_per)
    tile_start = tile_end - tiles_per
    dest = tile_start[e_flat] * tile + rank

    n_tiles = (n_asg + n_experts * (tile - 1)) // tile
    tile_ids = jnp.arange(n_tiles, dtype=jnp.int32)
    n_used = tile_end[-1]
    valid = tile_ids < n_used
    expert_raw = jnp.sum((tile_end[None, :] <= tile_ids[:, None]).astype(jnp.int32), axis=1)
    tile_expert = jnp.where(valid, expert_raw, expert_raw[n_used - 1])
    tile_first = valid & (tile_ids == tile_start[tile_expert])
    tile_row = jnp.where(valid, tile_ids, n_used - 1)

    tok_of_row = jnp.zeros((n_tiles * tile,), jnp.int32).at[dest].set(
        jnp.arange(n_asg, dtype=jnp.int32) // TOP_K, unique_indices=True)
    tile_first = tile_first.astype(jnp.int32)
    tile_slot = (jnp.cumsum(tile_first) - 1) % 2
    tile_fill = jnp.where(
        valid, jnp.clip(counts[tile_expert] - (tile_ids - tile_start[tile_expert]) * tile, 0, tile), 0)
    tiles = (tile_expert, tile_first, tile_fill, tile_row, tile_slot)
    assert tile % EW_ROWS == 0
    block_start = jnp.arange(n_tiles * tile // EW_ROWS, dtype=jnp.int32) * EW_ROWS
    live = (tile_fill[block_start // tile] > block_start % tile).astype(jnp.int32)
    return tok_of_row, live, dest, tiles


def _split_tiles(tiles, factor):
    tile_expert, tile_first, tile_fill, _, tile_slot = tiles
    n = tile_expert.shape[0] * factor
    sub = MOE_ROW_TILE // factor
    part = jnp.arange(n, dtype=jnp.int32) % factor
    fill = jnp.clip(jnp.repeat(tile_fill, factor) - part * sub, 0, sub)
    first = jnp.where(part == 0, jnp.repeat(tile_first, factor), 0)
    row = lax.cummax(jnp.where(fill > 0, jnp.arange(n, dtype=jnp.int32), 0))
    return jnp.repeat(tile_expert, factor), first, fill, row, jnp.repeat(tile_slot, factor)


def _row_copy(src_hbm, dst_vmem, sem, src_row, dst_row):
    return pltpu.make_async_copy(src_hbm.at[pl.ds(src_row, 1)], dst_vmem.at[pl.ds(dst_row, 1)], sem)


def _gather_rows_kernel(idx_ref, live_ref, src_ref, o_ref, buf_ref, sem):
    tile = buf_ref.shape[0]
    g = pl.program_id(0)
    base = g * tile

    def issue(r, carry):
        _row_copy(src_ref, buf_ref, sem, idx_ref[base + r], r).start()
        return carry

    def drain(r, carry):
        _row_copy(src_ref, buf_ref, sem, 0, r).wait()
        return carry

    @pl.when(live_ref[g] == 1)
    def _():
        lax.fori_loop(0, tile, issue, 0, unroll=GATHER_UNROLL)
        lax.fori_loop(0, tile, drain, 0, unroll=GATHER_UNROLL)
        o_ref[...] = _from_slabs(buf_ref).astype(BF16)

    @pl.when(live_ref[g] == 0)
    def _():
        o_ref[...] = jnp.zeros(o_ref.shape, BF16)


def _gather_rows(src, idx, live):
    slab = src.shape[1:]
    d = slab[0] * slab[1]
    n = idx.shape[0]
    tg = EW_ROWS
    assert n % tg == 0 and live.shape[0] == n // tg
    blocks = [((tg, d), BF16)]
    return pl.pallas_call(
        _gather_rows_kernel,
        out_shape=jax.ShapeDtypeStruct((n, d), BF16),
        grid_spec=pltpu.PrefetchScalarGridSpec(
            num_scalar_prefetch=2,
            grid=(n // tg,),
            in_specs=[pl.BlockSpec(memory_space=pl.ANY)],
            out_specs=pl.BlockSpec((tg, d), lambda i, idx_ref, live_ref: (i, 0)),
            scratch_shapes=[pltpu.VMEM((tg,) + slab, F32), pltpu.SemaphoreType.DMA]),
        compiler_params=_params(("arbitrary",), blocks, scratch=[((tg, d), F32)],
                                temps=[((tg, d), F32)]),
        name="moe_dispatch",
    )(idx, live, src)


def _held_tile(s, n_tiles):
    return jnp.minimum(s, n_tiles - 1)


def _done_tile(s):
    return jnp.maximum(s - 1, 0)


def _grouped_step(tables, w_refs, wbf_refs, compute, o_ref):
    _, tf_ref, tn_ref, _, ts_ref = tables
    n_tiles = tf_ref.shape[0]
    n_slots = wbf_refs[0].shape[0]
    tm = o_ref.shape[0]
    s = pl.program_id(1)
    held = _held_tile(s, n_tiles)
    done = _done_tile(s)
    fill = jnp.where(s > 0, tn_ref[done], -MOE_ROW_STEP)

    def slot(tile):
        return ts_ref[tile] % n_slots if n_slots > 1 else 0

    for n in range(0, tm + 1, MOE_ROW_STEP):
        @pl.when((fill > n - MOE_ROW_STEP) & (fill <= n))
        def _(n=n):
            if n > 0:
                o_ref[0:n, :] = compute([wbf_ref[slot(done)] for wbf_ref in wbf_refs], n)
            if n < tm:
                o_ref[n:, :] = jnp.zeros((tm - n, o_ref.shape[1]), o_ref.dtype)

    @pl.when((s < n_tiles) & (tf_ref[held] == 1))
    def _():
        for w_ref, wbf_ref in zip(w_refs, wbf_refs):
            _cast_weight(w_ref, wbf_ref.at[slot(held)])


def _moe_up_kernel(te_ref, tf_ref, tn_ref, tr_ref, ts_ref, x_ref, wg_ref, wu_ref, o_ref, wgbf_ref,
                   wubf_ref):
    def compute(w, n):
        x = x_ref[0:n, :]
        return (jax.nn.silu(_dot(x, w[0])) * _dot(x, w[1])).astype(BF16)

    _grouped_step((te_ref, tf_ref, tn_ref, tr_ref, ts_ref), (wg_ref, wu_ref), (wgbf_ref, wubf_ref),
                  compute, o_ref)


def _moe_up(xs, w_gate, w_up, layer, tiles):
    r, d = xs.shape
    f = w_gate.shape[-1]
    tm, tn = MOE_UP_ROW_TILE, COL_TILE
    assert f % tn == 0 and r % tm == 0
    tiles = _split_tiles(tiles, MOE_ROW_TILE // tm)
    nt = r // tm
    wspec = pl.BlockSpec((None, None, d, tn),
                         lambda j, s, te, tf, tv, tr, ts: (layer, te[_held_tile(s, nt)], 0, j))
    blocks = [((tm, d), BF16), ((d, tn), F32), ((d, tn), F32), ((tm, tn), BF16)]
    return pl.pallas_call(
        _moe_up_kernel,
        out_shape=jax.ShapeDtypeStruct((r, f), BF16),
        grid_spec=pltpu.PrefetchScalarGridSpec(
            num_scalar_prefetch=5,
            grid=(f // tn, nt + 1),
            in_specs=[pl.BlockSpec((tm, d), lambda j, s, te, tf, tv, tr, ts: (tr[_done_tile(s)], 0)),
                      wspec, wspec],
            out_specs=pl.BlockSpec((tm, tn), lambda j, s, te, tf, tv, tr, ts: (_done_tile(s), j)),
            scratch_shapes=[pltpu.VMEM((1, d, tn), BF16)] * 2),
        compiler_params=_params(("arbitrary", "arbitrary"), blocks,
                                scratch=[((1, d, tn), BF16)] * 2, temps=[((tm, tn), F32)] * 3),
        name="moe_up",
    )(*tiles, xs, w_gate, w_up)


def _moe_down_kernel(te_ref, tf_ref, tn_ref, tr_ref, ts_ref, a_ref, w_ref, o_ref, wbf_ref):
    _grouped_step((te_ref, tf_ref, tn_ref, tr_ref, ts_ref), (w_ref,), (wbf_ref,),
                  lambda w, n: _dot(a_ref[0:n, :], w[0]), o_ref)


def _moe_down(hmid, w_down, layer, tiles):
    r, f = hmid.shape
    d = w_down.shape[-1]
    tm, tn = MOE_UP_ROW_TILE, MOE_DOWN_COL_TILE
    assert d % tn == 0 and r % tm == 0
    tiles = _split_tiles(tiles, MOE_ROW_TILE // tm)
    nt = r // tm
    blocks = [((tm, f), BF16), ((f, tn), F32), ((tm, tn), F32)]
    return pl.pallas_call(
        _moe_down_kernel,
        out_shape=jax.ShapeDtypeStruct((r, d), F32),
        grid_spec=pltpu.PrefetchScalarGridSpec(
            num_scalar_prefetch=5,
            grid=(d // tn, nt + 1),
            in_specs=[pl.BlockSpec((tm, f), lambda j, s, te, tf, tv, tr, ts: (tr[_done_tile(s)], 0)),
                      pl.BlockSpec((None, None, f, tn),
                                   lambda j, s, te, tf, tv, tr, ts:
                                   (layer, te[_held_tile(s, nt)], 0, j))],
            out_specs=pl.BlockSpec((tm, tn), lambda j, s, te, tf, tv, tr, ts: (_done_tile(s), j)),
            scratch_shapes=[pltpu.VMEM((1, f, tn), BF16)]),
        compiler_params=_params(("arbitrary", "arbitrary"), blocks, scratch=[((1, f, tn), BF16)],
                                temps=[((tm, tn), F32)] * 2),
        name="moe_down",
    )(*tiles, hmid, w_down)


def _moe_final_kernel(dest_ref, x_ref, y_ref, route_ref, g_ref, mod_ref, o_ref, buf_ref, sems):
    tile = x_ref.shape[0]
    base = pl.program_id(0) * tile

    def issue(r, carry):
        for k in range(TOP_K):
            _row_copy(y_ref, buf_ref.at[k], sems.at[k], dest_ref[(base + r) * TOP_K + k], r).start()
        return carry

    def drain(r, carry):
        for k in range(TOP_K):
            _row_copy(y_ref, buf_ref.at[k], sems.at[k], 0, r).wait()
        return carry

    lax.fori_loop(0, tile, issue, 0, unroll=GATHER_UNROLL)
    lax.fori_loop(0, tile, drain, 0, unroll=GATHER_UNROLL)
    f = route_ref[:, TOP_K:TOP_K + 1] * buf_ref[0]
    for k in range(1, TOP_K):
        f = f + route_ref[:, TOP_K + k:TOP_K + k + 1] * buf_ref[k]
    o_ref[...] = x_ref[...] + mod_ref[5:6, :] * _rms(f, g_ref[...])


def _moe_final(x1, y, dest, route, g_post, mod, n_rows, seq, n_lat_rows, n_batch):
    d = x1.shape[1]
    tr = EW_ROWS
    ridx = functools.partial(_mod_row_index, rows_per_tile=tr, seq=seq, n_lat_rows=n_lat_rows,
                             n_batch=n_batch)
    row = pl.BlockSpec((tr, d), lambda i, dest_ref: (i, 0))
    blocks = [((tr, d), F32)] * 2 + [((N_MOD, d), F32)]
    return pl.pallas_call(
        _moe_final_kernel,
        out_shape=jax.ShapeDtypeStruct((n_rows, d), F32),
        grid_spec=pltpu.PrefetchScalarGridSpec(
            num_scalar_prefetch=1,
            grid=(n_rows // tr,),
            in_specs=[row, pl.BlockSpec(memory_space=pl.ANY),
                      pl.BlockSpec((tr, V7X_LANES), lambda i, dest_ref: (i, 0)),
                      pl.BlockSpec((1, d), lambda i, dest_ref: (0, 0)),
                      pl.BlockSpec((None, N_MOD, d), lambda i, dest_ref: (ridx(i), 0, 0))],
            out_specs=row,
            scratch_shapes=[pltpu.VMEM((TOP_K, tr, d), F32), pltpu.SemaphoreType.DMA((TOP_K,))]),
        compiler_params=_params(("arbitrary",), blocks, scratch=[((TOP_K, tr, d), F32)],
                                temps=[((tr, d), F32)] * 2),
        name="moe_combine_final",
    )(dest, x1, y, route, g_post.reshape(1, d), mod)


def _rope_tables(seq, n_batch, ctx_rows):
    n = jnp.arange(seq)
    pos_r = (n // GRID_W).astype(F32)
    pos_w = (n % GRID_W).astype(F32)
    n_freq = HEAD_DIM // 4
    inv_freq = ROPE_THETA ** (-jnp.arange(n_freq, dtype=F32) / n_freq)
    ar = pos_r[:, None] * inv_freq
    aw = pos_w[:, None] * inv_freq
    cos = jnp.concatenate([jnp.cos(ar), jnp.cos(ar), jnp.cos(aw), jnp.cos(aw)], axis=-1)
    sin = jnp.concatenate([-jnp.sin(ar), jnp.sin(ar), -jnp.sin(aw), jnp.sin(aw)], axis=-1)
    cos = jnp.concatenate([cos] * n_batch + [jnp.ones((ctx_rows, HEAD_DIM), F32)], axis=0)
    sin = jnp.concatenate([sin] * n_batch + [jnp.zeros((ctx_rows, HEAD_DIM), F32)], axis=0)
    return cos, sin


def kernel(x, c, ctx, c_ctx, w_mod, b_mod, g_pre_mix, g_post_mix, g_pre_ffn, g_post_ffn, w_in, w_out,
           sink_a, qn_c, kn_c, gm_ln_g, gm_ln_b, gm_ws, gm_bs, ffn_w_gate, ffn_w_up, ffn_w_down,
           moe_router, moe_w_gate, moe_w_up, moe_w_down):
    n_batch, seq, d = x.shape
    n_ctx = ctx.shape[1]
    depth = w_mod.shape[0]
    n_lat = n_batch * seq
    n_ctx_rows = n_batch * n_ctx
    t = n_lat + n_ctx_rows
    a_heads = sink_a.shape[1]
    a_w = a_heads * HEAD_DIM
    akv_w = A_KV_HEADS * HEAD_DIM
    b_w = gm_ln_g.shape[1]
    ckv_w = C_KV_HEADS * HEAD_DIM
    c_w = w_in.shape[2] - a_w - 2 * akv_w - 2 * b_w - 2 * ckv_w
    c_heads = c_w // HEAD_DIM
    assert n_ctx_rows % EW_ROWS == 0 and seq % EW_ROWS == 0 and n_batch + 1 <= 8
    assert a_w == c_w and akv_w == COL_TILE and ckv_w == COL_TILE

    x_parts = (x.reshape(n_lat, d), ctx.reshape(n_ctx_rows, d))
    c8 = jnp.concatenate([c, c_ctx[None, :], jnp.zeros((8 - n_batch - 1, d), F32)], axis=0)
    mod_all = _modulation(c8, w_mod, b_mod).reshape(depth, 8, N_MOD, d)

    cos, sin = _rope_tables(seq, n_batch, n_ctx_rows)
    geo = dict(seq=seq, n_lat_rows=n_lat, n_batch=n_batch)
    h = _prenorm(x_parts, g_pre_mix[0], mod_all[0], **geo)

    for l in range(depth):
        need_ctx = l < depth - 1
        n_rows = t if need_ctx else n_lat
        mod = mod_all[l]

        gq, gk = qn_c[l].reshape(1, HEAD_DIM), kn_c[l].reshape(1, HEAD_DIM)
        rope = (cos, sin, gq, gk)
        col = 0
        za = _inproj(h, w_in, l, col, a_w + akv_w, "rope", rope, n_q_tiles=a_w // COL_TILE)
        col += a_w + akv_w
        av = _inproj(h, w_in, l, col, akv_w, "plain")
        col += akv_w
        zb = _inproj(h, w_in, l, col, 2 * b_w, "gelu")
        col += 2 * b_w
        zc = _inproj(h, w_in, l, col, c_w + ckv_w, "normrope", rope, n_q_tiles=c_w // COL_TILE)
        col += c_w + ckv_w
        cv = _inproj(h, w_in, l, col, ckv_w, "plain")

        oa = _window_attn(za, av, sink_a[l], n_batch, seq, n_ctx, need_ctx)
        ob = _gmlp(zb, gm_ln_g[l], gm_ln_b[l], gm_ws[l], gm_bs[l], n_rows)
        oc = _global_attn(zc, cv, n_batch, seq, n_ctx, c_heads, need_ctx)

        y = _outproj(oa, ob, oc, w_out, l, n_rows)
        i = l // 2
        next_norm = (g_pre_mix[l + 1], mod_all[l + 1]) if need_ctx else None
        if l % 2 == 0:
            x1, h2 = _postmix(x_parts, y, g_post_mix[l], g_pre_ffn[l], mod, n_rows, 0, **geo)
            f_dim = ffn_w_gate.shape[-1]
            hmid = _ffn_up(h2, ffn_w_gate, ffn_w_up, (i,), n_rows)
            half = f_dim // 2
            f = _ffn_down(hmid, ffn_w_down, (i,), 0, half, n_rows)
            f = _ffn_down(hmid, ffn_w_down, (i,), half, half, n_rows, prev=f)
            xall = _final(x1, f, g_post_ffn[l], mod, n_rows, next_norm, **geo)
            if need_ctx:
                xall, h = xall
        else:
            x1, h2, h2f = _postmix(x_parts, y, g_post_mix[l], g_pre_ffn[l], mod, n_rows, 1, **geo)
            route = _router(h2, moe_router[i], n_rows)
            tok_of_row, live, dest, tiles = _route_plan(route, moe_router.shape[-1])
            xs = _gather_rows(h2f, tok_of_row, live)
            hmid = _moe_up(xs, moe_w_gate, moe_w_up, i, tiles)
            ys = _moe_down(hmid, moe_w_down, i, tiles)
            xall = _moe_final(x1, ys, dest, route, g_post_ffn[l], mod, n_rows, **geo)
            if need_ctx:
                h = _prenorm((xall,), next_norm[0], next_norm[1], **geo)
        x_parts = (xall,)
    return xall[:n_lat].reshape(n_batch, seq, d)
```

```python
import functools
import math

import jax
import jax.numpy as jnp
from jax import lax
from jax.experimental import pallas as pl
from jax.experimental.pallas import tpu as pltpu

F32 = jnp.float32
BF16 = jnp.bfloat16

GRID_W = 64
HEAD_DIM = 128
BLOCK = 128
WINDOW = 128
A_KV_HEADS = 4
C_KV_HEADS = 4
B_GROUPS = 8
CHUNK = 128
N_MOD = 6
TOP_K = 2
ROPE_THETA = 10000.0
EPS = 1e-6
LOG2E = math.log2(math.e)

V7X_LANES = 128
V7X_VMEM_SCOPED_CAP = 60000 * 1024

ROW_TILE = 512
MAX_ROW_TILE = 1088
ROW_ALIGN = 16
COL_TILE = 512
FFN_COL_TILE = 256
EW_ROWS = 256
CAST_ROWS = 256
Q_TILE = 256
K_TILE = 512
GATHER_UNROLL = 8
MOE_ROW_TILE = 1024
MOE_UP_ROW_TILE = 512
MOE_DOWN_COL_TILE = 1024
MOE_ROW_STEP = 256
ROUTER_PIECES = 3
EPILOGUE_CHUNKS = 4
SLAB_ROWS = 8


def _nbytes(shape, dtype):
    return math.prod(shape) * jnp.dtype(dtype).itemsize


def _params(semantics, blocks, scratch=(), temps=()):
    need = 2 * sum(_nbytes(s, d) for s, d in blocks)
    need += sum(_nbytes(s, d) for s, d in scratch)
    need += sum(_nbytes(s, d) for s, d in temps)
    limit = min(V7X_VMEM_SCOPED_CAP, max(need + need // 4, 16 * 1024 * 1024))
    return pltpu.CompilerParams(dimension_semantics=semantics, vmem_limit_bytes=limit)


def _row_tile(n_rows):
    for tm in range(MAX_ROW_TILE - MAX_ROW_TILE % ROW_ALIGN, 0, -ROW_ALIGN):
        if n_rows % tm == 0:
            return tm
    raise ValueError(f"no row tile for {n_rows} rows")


def _cast_weight(w_ref, wbf_ref):
    rows = w_ref.shape[0]
    step = CAST_ROWS if rows % CAST_ROWS == 0 else V7X_LANES
    assert rows % step == 0

    def body(r, carry):
        sl = pl.ds(pl.multiple_of(r * step, step), step)
        wbf_ref[sl, :] = w_ref[sl, :].astype(BF16)
        return carry

    lax.fori_loop(0, rows // step, body, 0)


def _dot(a, b):
    return jnp.dot(a, b, preferred_element_type=F32)


def _dot_nt(a, b):
    return lax.dot_general(a, b, (((1,), (1,)), ((), ())), preferred_element_type=F32)


def _rms(x, gain):
    return x * lax.rsqrt(jnp.mean(x * x, axis=-1, keepdims=True) + EPS) * gain


def _to_slabs(x, slab_ref):
    w = slab_ref.shape[2]
    for s in range(slab_ref.shape[1]):
        slab_ref[:, s, :] = x[:, s * w:(s + 1) * w]


def _from_slabs(slab_ref):
    return jnp.concatenate([slab_ref[:, s, :] for s in range(slab_ref.shape[1])], axis=1)


def _mod_kernel(c_ref, w_ref, b_ref, o_ref):
    s = jax.nn.silu(c_ref[...])
    o_ref[...] = _dot(s.astype(BF16), w_ref[...].astype(BF16)) + b_ref[...]


def _modulation(c8, w_mod, b_mod):
    n_layers, d, width = w_mod.shape
    tn = COL_TILE
    assert width % tn == 0
    blocks = [((8, d), F32), ((d, tn), F32), ((1, tn), F32), ((8, tn), F32)]
    return pl.pallas_call(
        _mod_kernel,
        out_shape=jax.ShapeDtypeStruct((n_layers, 8, width), F32),
        grid=(n_layers, width // tn),
        in_specs=[pl.BlockSpec((8, d), lambda l, j: (0, 0)),
                  pl.BlockSpec((None, d, tn), lambda l, j: (l, 0, j)),
                  pl.BlockSpec((None, 1, tn), lambda l, j: (l, 0, j))],
        out_specs=pl.BlockSpec((None, 8, tn), lambda l, j: (l, 0, j)),
        compiler_params=_params(("arbitrary", "arbitrary"), blocks, temps=[((d, tn), BF16)]),
        name="modulation",
    )(c8, w_mod, b_mod.reshape(n_layers, 1, width))


def _mod_row_index(i, rows_per_tile, seq, n_lat_rows, n_batch):
    lat_tiles = n_lat_rows // rows_per_tile
    return jnp.where(i < lat_tiles, (i * rows_per_tile) // seq, n_batch)


def _row_parts(parts, tr):
    specs, starts, start = [], [], 0
    for p in parts:
        n = p.shape[0] // tr
        assert n * tr == p.shape[0]
        specs.append(pl.BlockSpec((tr, p.shape[1]),
                                  lambda i, *_, start=start, n=n: (jnp.clip(i - start, 0, n - 1), 0)))
        starts.append(start)
        start += n
    return specs, tuple(starts)


def _pick_rows(i, x_refs, starts):
    x = x_refs[0][...]
    for ref, start in zip(x_refs[1:], starts[1:]):
        x = jnp.where(i >= start, ref[...], x)
    return x


def _pre_mix_norm(x, g_ref, mod_ref):
    return (_rms(x, g_ref[...]) * (1.0 + mod_ref[1:2, :]) + mod_ref[0:1, :]).astype(BF16)


def _prenorm_kernel(*refs, starts):
    *x_refs, g_ref, mod_ref, o_ref = refs
    o_ref[...] = _pre_mix_norm(_pick_rows(pl.program_id(0), x_refs, starts), g_ref, mod_ref)


def _prenorm(x_parts, gain, mod, seq, n_lat_rows, n_batch):
    d = x_parts[0].shape[1]
    t = sum(p.shape[0] for p in x_parts)
    tr = EW_ROWS
    ridx = functools.partial(_mod_row_index, rows_per_tile=tr, seq=seq, n_lat_rows=n_lat_rows,
                             n_batch=n_batch)
    x_specs, starts = _row_parts(x_parts, tr)
    blocks = [((tr, d), F32)] * len(x_parts) + [((1, d), F32), ((N_MOD, d), F32), ((tr, d), BF16)]
    return pl.pallas_call(
        functools.partial(_prenorm_kernel, starts=starts),
        out_shape=jax.ShapeDtypeStruct((t, d), BF16),
        grid=(t // tr,),
        in_specs=x_specs + [pl.BlockSpec((1, d), lambda i: (0, 0)),
                            pl.BlockSpec((None, N_MOD, d), lambda i: (ridx(i), 0, 0))],
        out_specs=pl.BlockSpec((tr, d), lambda i: (i, 0)),
        compiler_params=_params(("arbitrary",), blocks, temps=[((tr, d), F32)] * 2),
        name="prenorm",
    )(*x_parts, gain.reshape(1, d), mod)


def _postmix_kernel(*refs, starts, n_parts):
    x_refs = refs[:n_parts]
    y_ref, gpost_ref, gpre_ref, mod_ref, x1_ref, h2_ref, *h2f_ref = refs[n_parts:]
    x = _pick_rows(pl.program_id(0), x_refs, starts)
    x1 = x + mod_ref[2:3, :] * _rms(y_ref[...].astype(F32), gpost_ref[...])
    x1_ref[...] = x1
    h2 = _rms(x1, gpre_ref[...]) * (1.0 + mod_ref[4:5, :]) + mod_ref[3:4, :]
    h2_ref[...] = h2.astype(BF16)
    for ref in h2f_ref:
        _to_slabs(h2, ref)


def _postmix(x_parts, y, g_post, g_pre, mod, n_rows, want_slabs, seq, n_lat_rows, n_batch):
    d = x_parts[0].shape[1]
    tr = EW_ROWS
    x_specs, starts = _row_parts(x_parts, tr)
    assert d % (SLAB_ROWS * V7X_LANES) == 0
    slab = (SLAB_ROWS, d // SLAB_ROWS)
    ridx = functools.partial(_mod_row_index, rows_per_tile=tr, seq=seq, n_lat_rows=n_lat_rows,
                             n_batch=n_batch)
    row = pl.BlockSpec((tr, d), lambda i: (i, 0))
    vec = pl.BlockSpec((1, d), lambda i: (0, 0))
    blocks = ([((tr, d), F32)] * (len(x_parts) + 1 + want_slabs)
              + [((tr, d), BF16)] * 2 + [((N_MOD, d), F32)])
    out_shape = [jax.ShapeDtypeStruct((n_rows, d), F32), jax.ShapeDtypeStruct((n_rows, d), BF16)]
    out_specs = [row, row]
    if want_slabs:
        out_shape.append(jax.ShapeDtypeStruct((n_rows,) + slab, F32))
        out_specs.append(pl.BlockSpec((tr,) + slab, lambda i: (i, 0, 0)))
    return pl.pallas_call(
        functools.partial(_postmix_kernel, starts=starts, n_parts=len(x_parts)),
        out_shape=tuple(out_shape),
        grid=(n_rows // tr,),
        in_specs=x_specs + [row, vec, vec,
                            pl.BlockSpec((None, N_MOD, d), lambda i: (ridx(i), 0, 0))],
        out_specs=tuple(out_specs),
        compiler_params=_params(("arbitrary",), blocks, temps=[((tr, d), F32)] * 3),
        name="postmix",
    )(*x_parts, y, g_post.reshape(1, d), g_pre.reshape(1, d), mod)


def _final_kernel(x_ref, f_ref, g_ref, mod_ref, *rest):
    x = x_ref[...] + mod_ref[5:6, :] * _rms(f_ref[...], g_ref[...])
    if len(rest) == 1:
        (o_ref,) = rest
    else:
        gnext_ref, modnext_ref, o_ref, h_ref = rest
        h_ref[...] = _pre_mix_norm(x, gnext_ref, modnext_ref)
    o_ref[...] = x


def _final(x1, f, g_post, mod, n_rows, next_norm, seq, n_lat_rows, n_batch):
    d = x1.shape[1]
    tr = EW_ROWS
    ridx = functools.partial(_mod_row_index, rows_per_tile=tr, seq=seq, n_lat_rows=n_lat_rows,
                             n_batch=n_batch)
    row = pl.BlockSpec((tr, d), lambda i: (i, 0))
    vec = pl.BlockSpec((1, d), lambda i: (0, 0))
    modspec = pl.BlockSpec((None, N_MOD, d), lambda i: (ridx(i), 0, 0))
    blocks = [((tr, d), F32)] * 3 + [((N_MOD, d), F32)]
    in_specs, args = [row, row, vec, modspec], [x1, f, g_post.reshape(1, d), mod]
    out_shape, out_specs = jax.ShapeDtypeStruct((n_rows, d), F32), row
    if next_norm is not None:
        in_specs += [vec, modspec]
        args += [next_norm[0].reshape(1, d), next_norm[1]]
        out_shape = (out_shape, jax.ShapeDtypeStruct((n_rows, d), BF16))
        out_specs = (row, row)
        blocks += [((N_MOD, d), F32), ((tr, d), BF16)]
    return pl.pallas_call(
        _final_kernel,
        out_shape=out_shape,
        grid=(n_rows // tr,),
        in_specs=in_specs,
        out_specs=out_specs,
        compiler_params=_params(("arbitrary",), blocks, temps=[((tr, d), F32)] * 2),
        name="final_residual",
    )(*args)


def _swap_pairs(x):
    lane = lax.broadcasted_iota(jnp.int32, x.shape, 1)
    quarter = HEAD_DIM // 4
    first = (lane & quarter) == 0
    return jnp.where(first, pltpu.roll(x, HEAD_DIM - quarter, axis=1), pltpu.roll(x, quarter, axis=1))


def _inproj_kernel(*refs, mode, n_q_tiles, q_scale):
    if mode in ("rope", "normrope"):
        h_ref, w_ref, cos_ref, sin_ref, gq_ref, gk_ref, o_ref, wbf_ref = refs
    else:
        h_ref, w_ref, o_ref, wbf_ref = refs
    j = pl.program_id(0)

    @pl.when(pl.program_id(1) == 0)
    def _():
        _cast_weight(w_ref, wbf_ref)

    tm = h_ref.shape[0]
    n_chunks = EPILOGUE_CHUNKS if tm % (EPILOGUE_CHUNKS * ROW_ALIGN) == 0 else 1
    rc = tm // n_chunks
    for c in range(n_chunks):
        rows = slice(c * rc, (c + 1) * rc)
        acc = _dot(h_ref[rows, :], wbf_ref[...])
        if mode == "plain":
            o_ref[rows, :] = acc.astype(BF16)
        elif mode == "gelu":
            o_ref[rows, :] = (0.5 * acc * (1.0 + lax.erf(acc * (2.0 ** -0.5)))).astype(BF16)
        else:
            is_q = j < n_q_tiles
            scale = jnp.where(is_q, q_scale, 1.0).astype(F32)
            cos = cos_ref[rows, :]
            sin = sin_ref[rows, :]
            gain = jnp.where(is_q, gq_ref[...], gk_ref[...])
            for hh in range(acc.shape[1] // HEAD_DIM):
                sl = slice(hh * HEAD_DIM, (hh + 1) * HEAD_DIM)
                xh = acc[:, sl]
                if mode == "normrope":
                    xh = _rms(xh, gain)
                xh = xh * cos + _swap_pairs(xh) * sin
                o_ref[rows, sl] = (xh * scale).astype(BF16)


def _inproj(h, w_in, layer, col_start, width, mode, rope=None, n_q_tiles=0):
    t, d = h.shape
    tm, tn = _row_tile(t), COL_TILE
    assert col_start % tn == 0 and width % tn == 0
    j0 = col_start // tn
    in_specs = [pl.BlockSpec((tm, d), lambda j, i: (i, 0)),
                pl.BlockSpec((None, d, tn), lambda j, i: (layer, 0, j0 + j))]
    args = [h, w_in]
    blocks = [((tm, d), BF16), ((d, tn), F32), ((tm, tn), BF16)]
    if mode in ("rope", "normrope"):
        cos, sin, gq, gk = rope
        tab = pl.BlockSpec((tm, HEAD_DIM), lambda j, i: (i, 0))
        vec = pl.BlockSpec((1, HEAD_DIM), lambda j, i: (0, 0))
        in_specs += [tab, tab, vec, vec]
        args += [cos, sin, gq, gk]
        blocks += [((tm, HEAD_DIM), F32)] * 2
    kern = functools.partial(_inproj_kernel, mode=mode, n_q_tiles=n_q_tiles,
                             q_scale=HEAD_DIM ** -0.5 * LOG2E)
    return pl.pallas_call(
        kern,
        out_shape=jax.ShapeDtypeStruct((t, width), BF16),
        grid=(width // tn, t // tm),
        in_specs=in_specs,
        out_specs=pl.BlockSpec((tm, tn), lambda j, i: (i, j)),
        scratch_shapes=[pltpu.VMEM((d, tn), BF16)],
        compiler_params=_params(("arbitrary", "arbitrary"), blocks, scratch=[((d, tn), BF16)],
                                temps=[((tm, tn), F32)] * 2),
        name="inproj_" + mode,
    )(*args)


def _window_bias(group, n_ctx):
    r = jnp.arange(group * BLOCK)[:, None] % BLOCK
    c = jnp.arange(3 * BLOCK + n_ctx)[None, :]
    in_prev, in_cur = c < BLOCK, (c >= BLOCK) & (c < 2 * BLOCK)
    in_next = (c >= 2 * BLOCK) & (c < 3 * BLOCK)
    band_prev = in_prev & (c < r)
    band_next = in_next & (c - 2 * BLOCK > r)
    hidden = [band_prev | band_next, in_prev | band_next, band_prev | in_next, in_prev | in_next,
              in_prev | in_cur | in_next]
    shape = (group * BLOCK, 3 * BLOCK + n_ctx)
    return jnp.stack([jnp.where(jnp.broadcast_to(h, shape), -jnp.inf, 0.0).astype(F32)
                      for h in hidden])


def _window_attn_kernel(sink_ref, bias_ref, q_ref, kp_ref, kc_ref, kn_ref, kx_ref, vp_ref, vc_ref,
                        vn_ref, vx_ref, o_ref, *, group):
    rows = group * BLOCK
    assert BLOCK & (BLOCK - 1) == 0
    shift = BLOCK.bit_length() - 1
    bias = bias_ref[...]
    rid = lax.broadcasted_iota(jnp.int32, (rows, 1), 0) >> shift
    for hk in range(A_KV_HEADS):
        ksl = slice(hk * HEAD_DIM, (hk + 1) * HEAD_DIM)
        k_all = jnp.concatenate([kp_ref[:, ksl], kc_ref[:, ksl], kn_ref[:, ksl], kx_ref[:, ksl]],
                                axis=0)
        v_all = jnp.concatenate([vp_ref[:, ksl], vc_ref[:, ksl], vn_ref[:, ksl], vx_ref[:, ksl]],
                                axis=0)
        q3 = jnp.concatenate(
            [q_ref[:, (hk * group + g) * HEAD_DIM:(hk * group + g + 1) * HEAD_DIM]
             for g in range(group)], axis=0)
        sink = jnp.zeros((rows, 1), F32)
        for g in range(group):
            sink = jnp.where(rid == g, sink_ref[hk * group + g] * LOG2E, sink)
        s = _dot_nt(q3, k_all) + bias
        m = jnp.maximum(jnp.max(s, axis=-1, keepdims=True), sink)
        p = jnp.exp2(s - m)
        denom = jnp.sum(p, axis=-1, keepdims=True) + jnp.exp2(sink - m)
        o = _dot(p.astype(BF16), v_all) / denom
        for g in range(group):
            osl = slice((hk * group + g) * HEAD_DIM, (hk * group + g + 1) * HEAD_DIM)
            o_ref[:, osl] = o[g * BLOCK:(g + 1) * BLOCK, :].astype(BF16)


def _window_attn(za, av, sink, n_batch, seq, n_ctx, with_ctx):
    n_heads = sink.shape[0]
    group = n_heads // A_KV_HEADS
    qw = n_heads * HEAD_DIM
    kw = A_KV_HEADS * HEAD_DIM
    assert qw % kw == 0 and seq % BLOCK == 0 and (n_batch * seq) % n_ctx == 0 and WINDOW == BLOCK
    assert n_ctx % BLOCK == 0
    nb = seq // BLOCK
    ncb = n_ctx // BLOCK if with_ctx else 0
    kcol = qw // kw
    ctx0 = (n_batch * seq) // n_ctx

    def qblk(b, n):
        return (jnp.where(n < nb, b * nb + n, n_batch * nb + b * ncb + n - nb), 0)

    def blk(shift):
        return lambda b, n: (b * nb + jnp.clip(n + shift, 0, nb - 1), kcol)

    def vblk(shift):
        return lambda b, n: (b * nb + jnp.clip(n + shift, 0, nb - 1), 0)

    def bias_variant(b, n):
        lat = jnp.where(n == 0, 1, 0) + jnp.where(n == nb - 1, 2, 0)
        return (jnp.where(n < nb, lat, 4), 0, 0)

    kspec = [pl.BlockSpec((BLOCK, kw), blk(s)) for s in (-1, 0, 1)]
    vspec = [pl.BlockSpec((BLOCK, kw), vblk(s)) for s in (-1, 0, 1)]
    n_keys = 3 * BLOCK + n_ctx
    rows = group * BLOCK
    blocks = ([((BLOCK, qw), BF16)] * 2 + [((BLOCK, kw), BF16)] * 6 + [((n_ctx, kw), BF16)] * 2
              + [((rows, n_keys), F32)])
    return pl.pallas_call(
        functools.partial(_window_attn_kernel, group=group),
        out_shape=jax.ShapeDtypeStruct((n_batch * (nb + ncb) * BLOCK, qw), BF16),
        grid=(n_batch, nb + ncb),
        in_specs=[pl.BlockSpec(memory_space=pltpu.SMEM),
                  pl.BlockSpec((None, rows, n_keys), bias_variant),
                  pl.BlockSpec((BLOCK, qw), qblk)]
                 + kspec + [pl.BlockSpec((n_ctx, kw), lambda b, n: (ctx0 + b, kcol))]
                 + vspec + [pl.BlockSpec((n_ctx, kw), lambda b, n: (ctx0 + b, 0))],
        out_specs=pl.BlockSpec((BLOCK, qw), qblk),
        compiler_params=_params(("arbitrary", "arbitrary"), blocks,
                                temps=[((rows, n_keys), F32)] * 4),
        name="window_attn",
    )(sink, _window_bias(group, n_ctx), za, za, za, za, za, av, av, av, av)


def _gmlp_kernel(z_ref, g_ref, b_ref, ws_ref, bst_ref, o_ref):
    width = g_ref.shape[1]
    u = z_ref[:, :width].astype(F32)
    v = z_ref[:, width:].astype(F32)
    mu = jnp.mean(v, axis=-1, keepdims=True)
    vc = v - mu
    var = jnp.mean(vc * vc, axis=-1, keepdims=True)
    vn = (vc * lax.rsqrt(var + EPS) * g_ref[...] + b_ref[...]).astype(BF16)
    gd = width // B_GROUPS
    for g in range(B_GROUPS):
        sl = slice(g * gd, (g + 1) * gd)
        mixed = _dot(ws_ref[g].astype(BF16), vn[:, sl]) + bst_ref[:, g:g + 1]
        o_ref[:, sl] = (u[:, sl] * mixed).astype(BF16)


def _gmlp(zb, ln_g, ln_b, ws, bs, n_rows):
    w2 = zb.shape[1]
    width = w2 // 2
    blocks = [((CHUNK, w2), BF16), ((1, width), F32), ((1, width), F32),
              (ws.shape, F32), ((CHUNK, B_GROUPS), F32), ((CHUNK, width), BF16)]
    return pl.pallas_call(
        _gmlp_kernel,
        out_shape=jax.ShapeDtypeStruct((n_rows, width), BF16),
        grid=(n_rows // CHUNK,),
        in_specs=[pl.BlockSpec((CHUNK, w2), lambda i: (i, 0)),
                  pl.BlockSpec((1, width), lambda i: (0, 0)),
                  pl.BlockSpec((1, width), lambda i: (0, 0)),
                  pl.BlockSpec(ws.shape, lambda i: (0, 0, 0)),
                  pl.BlockSpec((CHUNK, B_GROUPS), lambda i: (0, 0))],
        out_specs=pl.BlockSpec((CHUNK, width), lambda i: (i, 0)),
        compiler_params=_params(("arbitrary",), blocks, temps=[((CHUNK, w2), F32)] * 2),
        name="chunk_gmlp",
    )(zb, ln_g.reshape(1, width), ln_b.reshape(1, width), ws, bs.T)


def _global_attn_kernel(q_ref, kx_ref, vx_ref, k_ref, v_ref, o_ref, vext_ref, *, group,
                        n_lat_tiles, has_ctx_tiles):
    tq = q_ref.shape[0]
    n_ctx = kx_ref.shape[0]
    n_lat = k_ref.shape[0]
    rows = group * tq

    @pl.when(pl.program_id(2) == 0)
    def _():
        vext_ref[0:n_ctx, 0:HEAD_DIM] = vx_ref[...]
        vext_ref[n_ctx:, 0:HEAD_DIM] = v_ref[...]
        vext_ref[:, HEAD_DIM:] = jnp.ones((n_ctx + n_lat, HEAD_DIM), BF16)

    def attend(chunks):
        q3 = jnp.concatenate([q_ref[:, g * HEAD_DIM:(g + 1) * HEAD_DIM] for g in range(group)],
                             axis=0)
        m = jnp.full((rows, HEAD_DIM), -jnp.inf, F32)
        acc = jnp.zeros((rows, 2 * HEAD_DIM), F32)
        for k_chunk, v0, nk in chunks:
            s = _dot_nt(q3, k_chunk())
            m_new = jnp.maximum(m, jnp.broadcast_to(jnp.max(s, axis=-1, keepdims=True), m.shape))
            alpha = jnp.exp2(m - m_new)
            p = jnp.exp2(s - jnp.tile(m_new, (1, nk // HEAD_DIM)))
            acc = jnp.tile(alpha, (1, 2)) * acc + _dot(p.astype(BF16), vext_ref[v0:v0 + nk, :])
            m = m_new
        o = acc[:, :HEAD_DIM] / acc[:, HEAD_DIM:]
        for g in range(group):
            o_ref[:, g * HEAD_DIM:(g + 1) * HEAD_DIM] = o[g * tq:(g + 1) * tq, :].astype(BF16)

    ctx_chunk = [(lambda: kx_ref[...], 0, n_ctx)]
    lat_chunks = [(lambda c=c: k_ref[c:c + K_TILE, :], n_ctx + c, K_TILE)
                  for c in range(0, n_lat, K_TILE)]
    is_lat = pl.program_id(2) < n_lat_tiles
    pl.when(is_lat)(lambda: attend(ctx_chunk + lat_chunks))
    if has_ctx_tiles:
        pl.when(jnp.logical_not(is_lat))(lambda: attend(ctx_chunk))


def _global_attn(zc, cv, n_batch, seq, n_ctx, n_heads, with_ctx):
    group = n_heads // C_KV_HEADS
    gw = group * HEAD_DIM
    tq = Q_TILE
    assert seq % tq == 0 and seq % K_TILE == 0 and (n_batch * seq) % n_ctx == 0 and n_ctx % tq == 0
    nq = seq // tq
    ncq = n_ctx // tq if with_ctx else 0
    ctx0 = (n_batch * seq) // n_ctx
    rows = group * tq

    def qblk(b, hk, i):
        return (jnp.where(i < nq, b * nq + i, n_batch * nq + b * ncq + i - nq), hk)

    blocks = [((tq, gw), BF16)] * 2 + [((n_ctx, HEAD_DIM), BF16)] * 2 + [((seq, HEAD_DIM), BF16)] * 2
    scratch = [((n_ctx + seq, 2 * HEAD_DIM), BF16)]
    return pl.pallas_call(
        functools.partial(_global_attn_kernel, group=group, n_lat_tiles=nq, has_ctx_tiles=ncq > 0),
        out_shape=jax.ShapeDtypeStruct((n_batch * (nq + ncq) * tq, n_heads * HEAD_DIM), BF16),
        grid=(n_batch, C_KV_HEADS, nq + ncq),
        in_specs=[pl.BlockSpec((tq, gw), qblk),
                  pl.BlockSpec((n_ctx, HEAD_DIM), lambda b, hk, i: (ctx0 + b, n_heads + hk)),
                  pl.BlockSpec((n_ctx, HEAD_DIM), lambda b, hk, i: (ctx0 + b, hk)),
                  pl.BlockSpec((seq, HEAD_DIM), lambda b, hk, i: (b, n_heads + hk)),
                  pl.BlockSpec((seq, HEAD_DIM), lambda b, hk, i: (b, hk))],
        out_specs=pl.BlockSpec((tq, gw), qblk),
        scratch_shapes=[pltpu.VMEM(s, d) for s, d in scratch],
        compiler_params=_params(("arbitrary", "arbitrary", "arbitrary"), blocks, scratch=scratch,
                                temps=[((rows, K_TILE), F32)] * 6),
        name="global_attn",
    )(zc, zc, cv, zc, cv)


def _outproj_kernel(a_ref, b_ref, c_ref, w_ref, o_ref, wbf_ref):
    @pl.when(pl.program_id(1) == 0)
    def _():
        _cast_weight(w_ref, wbf_ref)

    ka, kb = a_ref.shape[1], b_ref.shape[1]
    acc = _dot(a_ref[...], wbf_ref[0:ka, :])
    acc += _dot(b_ref[...], wbf_ref[ka:ka + kb, :])
    acc += _dot(c_ref[...], wbf_ref[ka + kb:, :])
    o_ref[...] = acc.astype(BF16)


def _outproj(oa, ob, oc, w_out, layer, n_rows):
    t = oa.shape[0]
    _, k, d = w_out.shape
    tm, tn = _row_tile(n_rows), COL_TILE
    assert oa.shape[1] + ob.shape[1] + oc.shape[1] == k and d % tn == 0
    blocks = [((tm, k), BF16), ((k, tn), F32), ((tm, tn), BF16)]
    return pl.pallas_call(
        _outproj_kernel,
        out_shape=jax.ShapeDtypeStruct((t, d), BF16),
        grid=(d // tn, n_rows // tm),
        in_specs=[pl.BlockSpec((tm, oa.shape[1]), lambda j, i: (i, 0)),
                  pl.BlockSpec((tm, ob.shape[1]), lambda j, i: (i, 0)),
                  pl.BlockSpec((tm, oc.shape[1]), lambda j, i: (i, 0)),
                  pl.BlockSpec((None, k, tn), lambda j, i: (layer, 0, j))],
        out_specs=pl.BlockSpec((tm, tn), lambda j, i: (i, j)),
        scratch_shapes=[pltpu.VMEM((k, tn), BF16)],
        compiler_params=_params(("arbitrary", "arbitrary"), blocks, scratch=[((k, tn), BF16)],
                                temps=[((tm, tn), F32)] * 2),
        name="outproj",
    )(oa, ob, oc, w_out)


def _ffn_up_kernel(h_ref, wg_ref, wu_ref, o_ref, wgbf_ref, wubf_ref):
    @pl.when(pl.program_id(1) == 0)
    def _():
        _cast_weight(wg_ref, wgbf_ref)
        _cast_weight(wu_ref, wubf_ref)

    h = h_ref[...]
    gate = _dot(h, wgbf_ref[...])
    up = _dot(h, wubf_ref[...])
    o_ref[...] = (jax.nn.silu(gate) * up).astype(BF16)


def _ffn_up(h2, w_gate, w_up, sel, n_rows):
    t, d = h2.shape
    f = w_gate.shape[-1]
    tm, tn = _row_tile(n_rows), FFN_COL_TILE
    assert f % tn == 0
    lead = (None,) * len(sel)
    wspec = pl.BlockSpec(lead + (d, tn), lambda j, i: sel + (0, j))
    blocks = [((tm, d), BF16), ((d, tn), F32), ((d, tn), F32), ((tm, tn), BF16)]
    return pl.pallas_call(
        _ffn_up_kernel,
        out_shape=jax.ShapeDtypeStruct((t, f), BF16),
        grid=(f // tn, n_rows // tm),
        in_specs=[pl.BlockSpec((tm, d), lambda j, i: (i, 0)), wspec, wspec],
        out_specs=pl.BlockSpec((tm, tn), lambda j, i: (i, j)),
        scratch_shapes=[pltpu.VMEM((d, tn), BF16)] * 2,
        compiler_params=_params(("arbitrary", "arbitrary"), blocks, scratch=[((d, tn), BF16)] * 2,
                                temps=[((tm, tn), F32)] * 3),
        name="ffn_up",
    )(h2, w_gate, w_up)


def _ffn_down_kernel(*refs, has_prev):
    if has_prev:
        a_ref, w_ref, prev_ref, o_ref, wbf_ref = refs
    else:
        a_ref, w_ref, o_ref, wbf_ref = refs

    @pl.when(pl.program_id(1) == 0)
    def _():
        _cast_weight(w_ref, wbf_ref)

    acc = _dot(a_ref[...], wbf_ref[...])
    if has_prev:
        acc = prev_ref[...] + acc
    o_ref[...] = acc


def _ffn_down(a, w_down, sel, k_start, k_size, n_rows, prev=None):
    t = a.shape[0]
    d = w_down.shape[-1]
    tm, tn = ROW_TILE, COL_TILE
    assert k_start % k_size == 0 and d % tn == 0 and n_rows % tm == 0
    kb = k_start // k_size
    lead = (None,) * len(sel)
    in_specs = [pl.BlockSpec((tm, k_size), lambda j, i: (i, kb)),
                pl.BlockSpec(lead + (k_size, tn), lambda j, i: sel + (kb, j))]
    args = [a, w_down]
    blocks = [((tm, k_size), BF16), ((k_size, tn), F32), ((tm, tn), F32)]
    aliases = {}
    if prev is not None:
        in_specs.append(pl.BlockSpec((tm, tn), lambda j, i: (i, j)))
        aliases = {len(args): 0}
        args.append(prev)
        blocks.append(((tm, tn), F32))
    return pl.pallas_call(
        functools.partial(_ffn_down_kernel, has_prev=prev is not None),
        out_shape=jax.ShapeDtypeStruct((t, d), F32),
        grid=(d // tn, n_rows // tm),
        in_specs=in_specs,
        out_specs=pl.BlockSpec((tm, tn), lambda j, i: (i, j)),
        scratch_shapes=[pltpu.VMEM((k_size, tn), BF16)],
        input_output_aliases=aliases,
        compiler_params=_params(("arbitrary", "arbitrary"), blocks, scratch=[((k_size, tn), BF16)],
                                temps=[((tm, tn), F32)] * 2),
        name="ffn_down",
    )(*args)


def _router_kernel(h_ref, w_ref, o_ref, *, n_experts):
    pieces = _dot(h_ref[...], w_ref[...])
    logits = pieces
    for k in range(1, ROUTER_PIECES):
        logits = logits + pltpu.roll(pieces, V7X_LANES - k * n_experts, axis=1)
    lane = lax.broadcasted_iota(jnp.int32, logits.shape, 1).astype(F32)
    neg = -jnp.inf
    logits = jnp.where(lane < n_experts, logits, neg)
    picked = []
    remaining = logits
    for _ in range(TOP_K):
        top = jnp.max(remaining, axis=-1, keepdims=True)
        idx = jnp.min(jnp.where(remaining == top, lane, float(logits.shape[1])), axis=-1,
                      keepdims=True)
        picked.append((top, idx))
        remaining = jnp.where(lane == idx, neg, remaining)
    top0 = picked[0][0]
    denom = sum(jnp.exp(tv - top0) for tv, _ in picked)
    route = jnp.zeros(logits.shape, F32)
    for k, (tv, idx) in enumerate(picked):
        route = jnp.where(lane == k, idx, route)
        route = jnp.where(lane == TOP_K + k, jnp.exp(tv - top0) / denom, route)
    o_ref[...] = route


def _router(h2, w_router, n_rows):
    t, d = h2.shape
    n_experts = w_router.shape[1]
    assert ROUTER_PIECES * n_experts <= V7X_LANES
    pieces, rest = [], w_router
    for _ in range(ROUTER_PIECES):
        pieces.append(rest.astype(BF16))
        rest = rest - pieces[-1].astype(F32)
    wpad = jnp.pad(jnp.concatenate(pieces, axis=1),
                   ((0, 0), (0, V7X_LANES - ROUTER_PIECES * n_experts)))
    tr = EW_ROWS
    blocks = [((tr, d), BF16), ((d, V7X_LANES), BF16), ((tr, V7X_LANES), F32)]
    return pl.pallas_call(
        functools.partial(_router_kernel, n_experts=n_experts),
        out_shape=jax.ShapeDtypeStruct((t, V7X_LANES), F32),
        grid=(n_rows // tr,),
        in_specs=[pl.BlockSpec((tr, d), lambda i: (i, 0)),
                  pl.BlockSpec((d, V7X_LANES), lambda i: (0, 0))],
        out_specs=pl.BlockSpec((tr, V7X_LANES), lambda i: (i, 0)),
        compiler_params=_params(("arbitrary",), blocks, temps=[((tr, d), F32)] * 3),
        name="moe_router",
    )(h2, wpad)


def _route_plan(route, n_experts):
    tile = MOE_ROW_TILE
    n_tok = route.shape[0]
    n_asg = n_tok * TOP_K
    e_flat = route[:, :TOP_K].astype(jnp.int32).reshape(n_asg)
    onehot = (e_flat[:, None] == jnp.arange(n_experts, dtype=jnp.int32)[None, :]).astype(jnp.int32)
    csum = jnp.cumsum(onehot, axis=0)
    rank = jnp.take_along_axis(csum, e_flat[:, None], axis=1)[:, 0] - 1
    counts = csum[-1]
    tiles_per = (counts + tile - 1) // tile
    tile_end = jnp.cumsum(tiles_per)
    tile_start = tile_end - tiles_per
    dest = tile_start[e_flat] * tile + rank

    n_tiles = (n_asg + n_experts * (tile - 1)) // tile
    tile_ids = jnp.arange(n_tiles, dtype=jnp.int32)
    n_used = tile_end[-1]
    valid = tile_ids < n_used
    expert_raw = jnp.sum((tile_end[None, :] <= tile_ids[:, None]).astype(jnp.int32), axis=1)
    tile_expert = jnp.where(valid, expert_raw, expert_raw[n_used - 1])
    tile_first = valid & (tile_ids == tile_start[tile_expert])
    tile_row = jnp.where(valid, tile_ids, n_used - 1)

    tok_of_row = jnp.zeros((n_tiles * tile,), jnp.int32).at[dest].set(
        jnp.arange(n_asg, dtype=jnp.int32) // TOP_K, unique_indices=True)
    tile_first = tile_first.astype(jnp.int32)
    tile_slot = (jnp.cumsum(tile_first) - 1) % 2
    tile_fill = jnp.where(
        valid, jnp.clip(counts[tile_expert] - (tile_ids - tile_start[tile_expert]) * tile, 0, tile), 0)
    tiles = (tile_expert, tile_first, tile_fill, tile_row, tile_slot)
    assert tile % EW_ROWS == 0
    block_start = jnp.arange(n_tiles * tile // EW_ROWS, dtype=jnp.int32) * EW_ROWS
    live = (tile_fill[block_start // tile] > block_start % tile).astype(jnp.int32)
    return tok_of_row, live, dest, tiles


def _split_tiles(tiles, factor):
    tile_expert, tile_first, tile_fill, _, tile_slot = tiles
    n = tile_expert.shape[0] * factor
    sub = MOE_ROW_TILE // factor
    part = jnp.arange(n, dtype=jnp.int32) % factor
    fill = jnp.clip(jnp.repeat(tile_fill, factor) - part * sub, 0, sub)
    first = jnp.where(part == 0, jnp.repeat(tile_first, factor), 0)
    row = lax.cummax(jnp.where(fill > 0, jnp.arange(n, dtype=jnp.int32), 0))
    return jnp.repeat(tile_expert, factor), first, fill, row, jnp.repeat(tile_slot, factor)


def _row_copy(src_hbm, dst_vmem, sem, src_row, dst_row):
    return pltpu.make_async_copy(src_hbm.at[pl.ds(src_row, 1)], dst_vmem.at[pl.ds(dst_row, 1)], sem)


def _gather_rows_kernel(idx_ref, live_ref, src_ref, o_ref, buf_ref, sems):
    tile = buf_ref.shape[1]
    n_blocks = pl.num_programs(0)
    g = pl.program_id(0)

    def start_block(b):
        def issue(r, carry):
            _row_copy(src_ref, buf_ref.at[b % 2], sems.at[b % 2], idx_ref[b * tile + r], r).start()
            return carry
        lax.fori_loop(0, tile, issue, 0, unroll=GATHER_UNROLL)

    @pl.when((g == 0) & (live_ref[0] == 1))
    def _():
        start_block(g)

    nxt = jnp.minimum(g + 1, n_blocks - 1)

    @pl.when((g + 1 < n_blocks) & (live_ref[nxt] == 1))
    def _():
        start_block(nxt)

    @pl.when(live_ref[g] == 1)
    def _():
        def drain(r, carry):
            _row_copy(src_ref, buf_ref.at[g % 2], sems.at[g % 2], 0, r).wait()
            return carry
        lax.fori_loop(0, tile, drain, 0, unroll=GATHER_UNROLL)
        o_ref[...] = _from_slabs(buf_ref.at[g % 2]).astype(BF16)

    @pl.when(live_ref[g] == 0)
    def _():
        o_ref[...] = jnp.zeros(o_ref.shape, BF16)


def _gather_rows(src, idx, live):
    slab = src.shape[1:]
    d = slab[0] * slab[1]
    n = idx.shape[0]
    tg = EW_ROWS
    assert n % tg == 0 and live.shape[0] == n // tg
    blocks = [((tg, d), BF16)]
    return pl.pallas_call(
        _gather_rows_kernel,
        out_shape=jax.ShapeDtypeStruct((n, d), BF16),
        grid_spec=pltpu.PrefetchScalarGridSpec(
            num_scalar_prefetch=2,
            grid=(n // tg,),
            in_specs=[pl.BlockSpec(memory_space=pl.ANY)],
            out_specs=pl.BlockSpec((tg, d), lambda i, idx_ref, live_ref: (i, 0)),
            scratch_shapes=[pltpu.VMEM((2, tg) + slab, F32), pltpu.SemaphoreType.DMA((2,))]),
        compiler_params=_params(("arbitrary",), blocks, scratch=[((2, tg, d), F32)],
                                temps=[((tg, d), F32)]),
        name="moe_dispatch",
    )(idx, live, src)


def _held_tile(s, n_tiles):
    return jnp.minimum(s, n_tiles - 1)


def _done_tile(s):
    return jnp.maximum(s - 1, 0)


def _grouped_step(tables, w_refs, wbf_refs, compute, o_ref):
    _, tf_ref, tn_ref, _, ts_ref = tables
    n_tiles = tf_ref.shape[0]
    n_slots = wbf_refs[0].shape[0]
    tm = o_ref.shape[0]
    s = pl.program_id(1)
    held = _held_tile(s, n_tiles)
    done = _done_tile(s)
    fill = jnp.where(s > 0, tn_ref[done], -MOE_ROW_STEP)

    def slot(tile):
        return ts_ref[tile] % n_slots if n_slots > 1 else 0

    for n in range(0, tm + 1, MOE_ROW_STEP):
        @pl.when((fill > n - MOE_ROW_STEP) & (fill <= n))
        def _(n=n):
            if n > 0:
                o_ref[0:n, :] = compute([wbf_ref[slot(done)] for wbf_ref in wbf_refs], n)
            if n < tm:
                o_ref[n:, :] = jnp.zeros((tm - n, o_ref.shape[1]), o_ref.dtype)

    @pl.when((s < n_tiles) & (tf_ref[held] == 1))
    def _():
        for w_ref, wbf_ref in zip(w_refs, wbf_refs):
            _cast_weight(w_ref, wbf_ref.at[slot(held)])


def _moe_up_kernel(te_ref, tf_ref, tn_ref, tr_ref, ts_ref, x_ref, wg_ref, wu_ref, o_ref, wgbf_ref,
                   wubf_ref):
    def compute(w, n):
        x = x_ref[0:n, :]
        return (jax.nn.silu(_dot(x, w[0])) * _dot(x, w[1])).astype(BF16)

    _grouped_step((te_ref, tf_ref, tn_ref, tr_ref, ts_ref), (wg_ref, wu_ref), (wgbf_ref, wubf_ref),
                  compute, o_ref)


def _moe_up(xs, w_gate, w_up, layer, tiles):
    r, d = xs.shape
    f = w_gate.shape[-1]
    tm, tn = MOE_UP_ROW_TILE, COL_TILE
    assert f % tn == 0 and r % tm == 0
    tiles = _split_tiles(tiles, MOE_ROW_TILE // tm)
    nt = r // tm
    wspec = pl.BlockSpec((None, None, d, tn),
                         lambda j, s, te, tf, tv, tr, ts: (layer, te[_held_tile(s, nt)], 0, j))
    blocks = [((tm, d), BF16), ((d, tn), F32), ((d, tn), F32), ((tm, tn), BF16)]
    return pl.pallas_call(
        _moe_up_kernel,
        out_shape=jax.ShapeDtypeStruct((r, f), BF16),
        grid_spec=pltpu.PrefetchScalarGridSpec(
            num_scalar_prefetch=5,
            grid=(f // tn, nt + 1),
            in_specs=[pl.BlockSpec((tm, d), lambda j, s, te, tf, tv, tr, ts: (tr[_done_tile(s)], 0)),
                      wspec, wspec],
            out_specs=pl.BlockSpec((tm, tn), lambda j, s, te, tf, tv, tr, ts: (_done_tile(s), j)),
            scratch_shapes=[pltpu.VMEM((1, d, tn), BF16)] * 2),
        compiler_params=_params(("arbitrary", "arbitrary"), blocks,
                                scratch=[((1, d, tn), BF16)] * 2, temps=[((tm, tn), F32)] * 3),
        name="moe_up",
    )(*tiles, xs, w_gate, w_up)


def _moe_down_kernel(te_ref, tf_ref, tn_ref, tr_ref, ts_ref, a_ref, w_ref, o_ref, wbf_ref):
    _grouped_step((te_ref, tf_ref, tn_ref, tr_ref, ts_ref), (w_ref,), (wbf_ref,),
                  lambda w, n: _dot(a_ref[0:n, :], w[0]), o_ref)


def _moe_down(hmid, w_down, layer, tiles):
    r, f = hmid.shape
    d = w_down.shape[-1]
    tm, tn = MOE_UP_ROW_TILE, MOE_DOWN_COL_TILE
    assert d % tn == 0 and r % tm == 0
    tiles = _split_tiles(tiles, MOE_ROW_TILE // tm)
    nt = r // tm
    blocks = [((tm, f), BF16), ((f, tn), F32), ((tm, tn), F32)]
    return pl.pallas_call(
        _moe_down_kernel,
        out_shape=jax.ShapeDtypeStruct((r, d), F32),
        grid_spec=pltpu.PrefetchScalarGridSpec(
            num_scalar_prefetch=5,
            grid=(d // tn, nt + 1),
            in_specs=[pl.BlockSpec((tm, f), lambda j, s, te, tf, tv, tr, ts: (tr[_done_tile(s)], 0)),
                      pl.BlockSpec((None, None, f, tn),
                                   lambda j, s, te, tf, tv, tr, ts:
                                   (layer, te[_held_tile(s, nt)], 0, j))],
            out_specs=pl.BlockSpec((tm, tn), lambda j, s, te, tf, tv, tr, ts: (_done_tile(s), j)),
            scratch_shapes=[pltpu.VMEM((1, f, tn), BF16)]),
        compiler_params=_params(("arbitrary", "arbitrary"), blocks, scratch=[((1, f, tn), BF16)],
                                temps=[((tm, tn), F32)] * 2),
        name="moe_down",
    )(*tiles, hmid, w_down)


def _moe_final_kernel(dest_ref, x_ref, y_ref, route_ref, g_ref, mod_ref, o_ref, buf_ref, sems):
    tile = x_ref.shape[0]
    n_blocks = pl.num_programs(0)
    g = pl.program_id(0)

    def start_block(b):
        def issue(r, carry):
            for k in range(TOP_K):
                _row_copy(y_ref, buf_ref.at[b % 2, k], sems.at[b % 2, k],
                          dest_ref[(b * tile + r) * TOP_K + k], r).start()
            return carry
        lax.fori_loop(0, tile, issue, 0, unroll=GATHER_UNROLL)

    @pl.when(g == 0)
    def _():
        start_block(g)

    @pl.when(g + 1 < n_blocks)
    def _():
        start_block(g + 1)

    def drain(r, carry):
        for k in range(TOP_K):
            _row_copy(y_ref, buf_ref.at[g % 2, k], sems.at[g % 2, k], 0, r).wait()
        return carry

    lax.fori_loop(0, tile, drain, 0, unroll=GATHER_UNROLL)
    f = route_ref[:, TOP_K:TOP_K + 1] * buf_ref[g % 2, 0]
    for k in range(1, TOP_K):
        f = f + route_ref[:, TOP_K + k:TOP_K + k + 1] * buf_ref[g % 2, k]
    o_ref[...] = x_ref[...] + mod_ref[5:6, :] * _rms(f, g_ref[...])


def _moe_final(x1, y, dest, route, g_post, mod, n_rows, seq, n_lat_rows, n_batch):
    d = x1.shape[1]
    tr = EW_ROWS
    ridx = functools.partial(_mod_row_index, rows_per_tile=tr, seq=seq, n_lat_rows=n_lat_rows,
                             n_batch=n_batch)
    row = pl.BlockSpec((tr, d), lambda i, dest_ref: (i, 0))
    blocks = [((tr, d), F32)] * 2 + [((N_MOD, d), F32)]
    return pl.pallas_call(
        _moe_final_kernel,
        out_shape=jax.ShapeDtypeStruct((n_rows, d), F32),
        grid_spec=pltpu.PrefetchScalarGridSpec(
            num_scalar_prefetch=1,
            grid=(n_rows // tr,),
            in_specs=[row, pl.BlockSpec(memory_space=pl.ANY),
                      pl.BlockSpec((tr, V7X_LANES), lambda i, dest_ref: (i, 0)),
                      pl.BlockSpec((1, d), lambda i, dest_ref: (0, 0)),
                      pl.BlockSpec((None, N_MOD, d), lambda i, dest_ref: (ridx(i), 0, 0))],
            out_specs=row,
            scratch_shapes=[pltpu.VMEM((2, TOP_K, tr, d), F32),
                            pltpu.SemaphoreType.DMA((2, TOP_K))]),
        compiler_params=_params(("arbitrary",), blocks, scratch=[((2, TOP_K, tr, d), F32)],
                                temps=[((tr, d), F32)] * 2),
        name="moe_combine_final",
    )(dest, x1, y, route, g_post.reshape(1, d), mod)


def _rope_tables(seq, n_batch, ctx_rows):
    n = jnp.arange(seq)
    pos_r = (n // GRID_W).astype(F32)
    pos_w = (n % GRID_W).astype(F32)
    n_freq = HEAD_DIM // 4
    inv_freq = ROPE_THETA ** (-jnp.arange(n_freq, dtype=F32) / n_freq)
    ar = pos_r[:, None] * inv_freq
    aw = pos_w[:, None] * inv_freq
    cos = jnp.concatenate([jnp.cos(ar), jnp.cos(ar), jnp.cos(aw), jnp.cos(aw)], axis=-1)
    sin = jnp.concatenate([-jnp.sin(ar), jnp.sin(ar), -jnp.sin(aw), jnp.sin(aw)], axis=-1)
    cos = jnp.concatenate([cos] * n_batch + [jnp.ones((ctx_rows, HEAD_DIM), F32)], axis=0)
    sin = jnp.concatenate([sin] * n_batch + [jnp.zeros((ctx_rows, HEAD_DIM), F32)], axis=0)
    return cos, sin


def kernel(x, c, ctx, c_ctx, w_mod, b_mod, g_pre_mix, g_post_mix, g_pre_ffn, g_post_ffn, w_in, w_out,
           sink_a, qn_c, kn_c, gm_ln_g, gm_ln_b, gm_ws, gm_bs, ffn_w_gate, ffn_w_up, ffn_w_down,
           moe_router, moe_w_gate, moe_w_up, moe_w_down):
    n_batch, seq, d = x.shape
    n_ctx = ctx.shape[1]
    depth = w_mod.shape[0]
    n_lat = n_batch * seq
    n_ctx_rows = n_batch * n_ctx
    t = n_lat + n_ctx_rows
    a_heads = sink_a.shape[1]
    a_w = a_heads * HEAD_DIM
    akv_w = A_KV_HEADS * HEAD_DIM
    b_w = gm_ln_g.shape[1]
    ckv_w = C_KV_HEADS * HEAD_DIM
    c_w = w_in.shape[2] - a_w - 2 * akv_w - 2 * b_w - 2 * ckv_w
    c_heads = c_w // HEAD_DIM
    assert n_ctx_rows % EW_ROWS == 0 and seq % EW_ROWS == 0 and n_batch + 1 <= 8
    assert a_w == c_w and akv_w == COL_TILE and ckv_w == COL_TILE

    x_parts = (x.reshape(n_lat, d), ctx.reshape(n_ctx_rows, d))
    c8 = jnp.concatenate([c, c_ctx[None, :], jnp.zeros((8 - n_batch - 1, d), F32)], axis=0)
    mod_all = _modulation(c8, w_mod, b_mod).reshape(depth, 8, N_MOD, d)

    cos, sin = _rope_tables(seq, n_batch, n_ctx_rows)
    geo = dict(seq=seq, n_lat_rows=n_lat, n_batch=n_batch)
    h = _prenorm(x_parts, g_pre_mix[0], mod_all[0], **geo)

    for l in range(depth):
        need_ctx = l < depth - 1
        n_rows = t if need_ctx else n_lat
        mod = mod_all[l]

        gq, gk = qn_c[l].reshape(1, HEAD_DIM), kn_c[l].reshape(1, HEAD_DIM)
        rope = (cos, sin, gq, gk)
        col = 0
        za = _inproj(h, w_in, l, col, a_w + akv_w, "rope", rope, n_q_tiles=a_w // COL_TILE)
        col += a_w + akv_w
        av = _inproj(h, w_in, l, col, akv_w, "plain")
        col += akv_w
        zb = _inproj(h, w_in, l, col, 2 * b_w, "gelu")
        col += 2 * b_w
        zc = _inproj(h, w_in, l, col, c_w + ckv_w, "normrope", rope, n_q_tiles=c_w // COL_TILE)
        col += c_w + ckv_w
        cv = _inproj(h, w_in, l, col, ckv_w, "plain")

        oa = _window_attn(za, av, sink_a[l], n_batch, seq, n_ctx, need_ctx)
        ob = _gmlp(zb, gm_ln_g[l], gm_ln_b[l], gm_ws[l], gm_bs[l], n_rows)
        oc = _global_attn(zc, cv, n_batch, seq, n_ctx, c_heads, need_ctx)

        y = _outproj(oa, ob, oc, w_out, l, n_rows)
        i = l // 2
        next_norm = (g_pre_mix[l + 1], mod_all[l + 1]) if need_ctx else None
        if l % 2 == 0:
            x1, h2 = _postmix(x_parts, y, g_post_mix[l], g_pre_ffn[l], mod, n_rows, 0, **geo)
            f_dim = ffn_w_gate.shape[-1]
            hmid = _ffn_up(h2, ffn_w_gate, ffn_w_up, (i,), n_rows)
            half = f_dim // 2
            f = _ffn_down(hmid, ffn_w_down, (i,), 0, half, n_rows)
            f = _ffn_down(hmid, ffn_w_down, (i,), half, half, n_rows, prev=f)
            xall = _final(x1, f, g_post_ffn[l], mod, n_rows, next_norm, **geo)
            if need_ctx:
                xall, h = xall
        else:
            x1, h2, h2f = _postmix(x_parts, y, g_post_mix[l], g_pre_ffn[l], mod, n_rows, 1, **geo)
            route = _router(h2, moe_router[i], n_rows)
            tok_of_row, live, dest, tiles = _route_plan(route, moe_router.shape[-1])
            xs = _gather_rows(h2f, tok_of_row, live)
            hmid = _moe_up(xs, moe_w_gate, moe_w_up, i, tiles)
            ys = _moe_down(hmid, moe_w_down, i, tiles)
            xall = _moe_final(x1, ys, dest, route, g_post_ffn[l], mod, n_rows, **geo)
            if need_ctx:
                h = _prenorm((xall,), next_norm[0], next_norm[1], **geo)
        x_parts = (xall,)
    return xall[:n_lat].reshape(n_batch, seq, d)
```

```python
import functools
import math

import jax
import jax.numpy as jnp
from jax import lax
from jax.experimental import pallas as pl
from jax.experimental.pallas import tpu as pltpu

F32 = jnp.float32
BF16 = jnp.bfloat16

GRID_W = 64
HEAD_DIM = 128
BLOCK = 128
WINDOW = 128
A_KV_HEADS = 4
C_KV_HEADS = 4
B_GROUPS = 8
CHUNK = 128
N_MOD = 6
TOP_K = 2
ROPE_THETA = 10000.0
EPS = 1e-6
LOG2E = math.log2(math.e)

V7X_LANES = 128
V7X_VMEM_SCOPED_CAP = 60000 * 1024

ROW_TILE = 512
MAX_ROW_TILE = 1088
ROW_ALIGN = 16
COL_TILE = 512
FFN_COL_TILE = 256
EW_ROWS = 256
CAST_ROWS = 256
Q_TILE = 256
K_TILE = 512
GATHER_UNROLL = 8
MOE_ROW_TILE = 1024
MOE_UP_ROW_TILE = 512
MOE_DOWN_COL_TILE = 1024
MOE_ROW_STEP = 128
ROUTER_PIECES = 3
EPILOGUE_CHUNKS = 4
SLAB_ROWS = 8


def _nbytes(shape, dtype):
    return math.prod(shape) * jnp.dtype(dtype).itemsize


def _params(semantics, blocks, scratch=(), temps=()):
    need = 2 * sum(_nbytes(s, d) for s, d in blocks)
    need += sum(_nbytes(s, d) for s, d in scratch)
    need += sum(_nbytes(s, d) for s, d in temps)
    limit = min(V7X_VMEM_SCOPED_CAP, max(need + need // 4, 16 * 1024 * 1024))
    return pltpu.CompilerParams(dimension_semantics=semantics, vmem_limit_bytes=limit)


def _row_tile(n_rows):
    for tm in range(MAX_ROW_TILE - MAX_ROW_TILE % ROW_ALIGN, 0, -ROW_ALIGN):
        if n_rows % tm == 0:
            return tm
    raise ValueError(f"no row tile for {n_rows} rows")


def _cast_weight(w_ref, wbf_ref):
    rows = w_ref.shape[0]
    step = CAST_ROWS if rows % CAST_ROWS == 0 else V7X_LANES
    assert rows % step == 0

    def body(r, carry):
        sl = pl.ds(pl.multiple_of(r * step, step), step)
        wbf_ref[sl, :] = w_ref[sl, :].astype(BF16)
        return carry

    lax.fori_loop(0, rows // step, body, 0)


def _dot(a, b):
    return jnp.dot(a, b, preferred_element_type=F32)


def _dot_nt(a, b):
    return lax.dot_general(a, b, (((1,), (1,)), ((), ())), preferred_element_type=F32)


def _rms(x, gain):
    return x * lax.rsqrt(jnp.mean(x * x, axis=-1, keepdims=True) + EPS) * gain


def _to_slabs(x, slab_ref):
    w = slab_ref.shape[2]
    for s in range(slab_ref.shape[1]):
        slab_ref[:, s, :] = x[:, s * w:(s + 1) * w]


def _from_slabs(slab_ref):
    return jnp.concatenate([slab_ref[:, s, :] for s in range(slab_ref.shape[1])], axis=1)


def _mod_kernel(c_ref, w_ref, b_ref, o_ref):
    s = jax.nn.silu(c_ref[...])
    o_ref[...] = _dot(s.astype(BF16), w_ref[...].astype(BF16)) + b_ref[...]


def _modulation(c8, w_mod, b_mod):
    n_layers, d, width = w_mod.shape
    tn = COL_TILE
    assert width % tn == 0
    blocks = [((8, d), F32), ((d, tn), F32), ((1, tn), F32), ((8, tn), F32)]
    return pl.pallas_call(
        _mod_kernel,
        out_shape=jax.ShapeDtypeStruct((n_layers, 8, width), F32),
        grid=(n_layers, width // tn),
        in_specs=[pl.BlockSpec((8, d), lambda l, j: (0, 0)),
                  pl.BlockSpec((None, d, tn), lambda l, j: (l, 0, j)),
                  pl.BlockSpec((None, 1, tn), lambda l, j: (l, 0, j))],
        out_specs=pl.BlockSpec((None, 8, tn), lambda l, j: (l, 0, j)),
        compiler_params=_params(("arbitrary", "arbitrary"), blocks, temps=[((d, tn), BF16)]),
        name="modulation",
    )(c8, w_mod, b_mod.reshape(n_layers, 1, width))


def _mod_row_index(i, rows_per_tile, seq, n_lat_rows, n_batch):
    lat_tiles = n_lat_rows // rows_per_tile
    return jnp.where(i < lat_tiles, (i * rows_per_tile) // seq, n_batch)


def _row_parts(parts, tr):
    specs, starts, start = [], [], 0
    for p in parts:
        n = p.shape[0] // tr
        assert n * tr == p.shape[0]
        specs.append(pl.BlockSpec((tr, p.shape[1]),
                                  lambda i, *_, start=start, n=n: (jnp.clip(i - start, 0, n - 1), 0)))
        starts.append(start)
        start += n
    return specs, tuple(starts)


def _pick_rows(i, x_refs, starts):
    x = x_refs[0][...]
    for ref, start in zip(x_refs[1:], starts[1:]):
        x = jnp.where(i >= start, ref[...], x)
    return x


def _pre_mix_norm(x, g_ref, mod_ref):
    return (_rms(x, g_ref[...]) * (1.0 + mod_ref[1:2, :]) + mod_ref[0:1, :]).astype(BF16)


def _prenorm_kernel(*refs, starts):
    *x_refs, g_ref, mod_ref, o_ref = refs
    o_ref[...] = _pre_mix_norm(_pick_rows(pl.program_id(0), x_refs, starts), g_ref, mod_ref)


def _prenorm(x_parts, gain, mod, seq, n_lat_rows, n_batch):
    d = x_parts[0].shape[1]
    t = sum(p.shape[0] for p in x_parts)
    tr = EW_ROWS
    ridx = functools.partial(_mod_row_index, rows_per_tile=tr, seq=seq, n_lat_rows=n_lat_rows,
                             n_batch=n_batch)
    x_specs, starts = _row_parts(x_parts, tr)
    blocks = [((tr, d), F32)] * len(x_parts) + [((1, d), F32), ((N_MOD, d), F32), ((tr, d), BF16)]
    return pl.pallas_call(
        functools.partial(_prenorm_kernel, starts=starts),
        out_shape=jax.ShapeDtypeStruct((t, d), BF16),
        grid=(t // tr,),
        in_specs=x_specs + [pl.BlockSpec((1, d), lambda i: (0, 0)),
                            pl.BlockSpec((None, N_MOD, d), lambda i: (ridx(i), 0, 0))],
        out_specs=pl.BlockSpec((tr, d), lambda i: (i, 0)),
        compiler_params=_params(("arbitrary",), blocks, temps=[((tr, d), F32)] * 2),
        name="prenorm",
    )(*x_parts, gain.reshape(1, d), mod)


def _postmix_kernel(*refs, starts, n_parts, n_experts):
    x_refs = refs[:n_parts]
    y_ref, gpost_ref, gpre_ref, mod_ref, *rest = refs[n_parts:]
    x = _pick_rows(pl.program_id(0), x_refs, starts)
    x1 = x + mod_ref[2:3, :] * _rms(y_ref[...].astype(F32), gpost_ref[...])
    h2 = _rms(x1, gpre_ref[...]) * (1.0 + mod_ref[4:5, :]) + mod_ref[3:4, :]
    if n_experts is None:
        x1_ref, h2_ref = rest
        h2_ref[...] = h2.astype(BF16)
    else:
        wr_ref, x1_ref, slab_ref, route_ref = rest
        _to_slabs(h2, slab_ref)
        route_ref[...] = _route(h2.astype(BF16), wr_ref[...], n_experts)
    x1_ref[...] = x1


def _postmix(x_parts, y, g_post, g_pre, mod, n_rows, w_router, seq, n_lat_rows, n_batch):
    d = x_parts[0].shape[1]
    tr = EW_ROWS
    x_specs, starts = _row_parts(x_parts, tr)
    ridx = functools.partial(_mod_row_index, rows_per_tile=tr, seq=seq, n_lat_rows=n_lat_rows,
                             n_batch=n_batch)
    row = pl.BlockSpec((tr, d), lambda i: (i, 0))
    vec = pl.BlockSpec((1, d), lambda i: (0, 0))
    in_specs = x_specs + [row, vec, vec, pl.BlockSpec((None, N_MOD, d), lambda i: (ridx(i), 0, 0))]
    args = [*x_parts, y, g_post.reshape(1, d), g_pre.reshape(1, d), mod]
    blocks = ([((tr, d), F32)] * (len(x_parts) + 2) + [((tr, d), BF16)] * 2 + [((N_MOD, d), F32)])
    if w_router is None:
        n_experts = None
        out_shape = (jax.ShapeDtypeStruct((n_rows, d), F32), jax.ShapeDtypeStruct((n_rows, d), BF16))
        out_specs = (row, row)
    else:
        n_experts = w_router.shape[1]
        assert d % (SLAB_ROWS * V7X_LANES) == 0
        slab = (SLAB_ROWS, d // SLAB_ROWS)
        in_specs.append(pl.BlockSpec((d, V7X_LANES), lambda i: (0, 0)))
        args.append(_router_pieces(w_router))
        out_shape = (jax.ShapeDtypeStruct((n_rows, d), F32),
                     jax.ShapeDtypeStruct((n_rows,) + slab, F32),
                     jax.ShapeDtypeStruct((n_rows, V7X_LANES), F32))
        out_specs = (row, pl.BlockSpec((tr,) + slab, lambda i: (i, 0, 0)),
                     pl.BlockSpec((tr, V7X_LANES), lambda i: (i, 0)))
        blocks += [((d, V7X_LANES), BF16), ((tr, V7X_LANES), F32)]
    return pl.pallas_call(
        functools.partial(_postmix_kernel, starts=starts, n_parts=len(x_parts),
                          n_experts=n_experts),
        out_shape=out_shape,
        grid=(n_rows // tr,),
        in_specs=in_specs,
        out_specs=out_specs,
        compiler_params=_params(("arbitrary",), blocks, temps=[((tr, d), F32)] * 3),
        name="postmix" if w_router is None else "postmix_route",
    )(*args)


def _final_kernel(x_ref, f_ref, g_ref, mod_ref, *rest):
    x = x_ref[...] + mod_ref[5:6, :] * _rms(f_ref[...], g_ref[...])
    if len(rest) == 1:
        (o_ref,) = rest
    else:
        gnext_ref, modnext_ref, o_ref, h_ref = rest
        h_ref[...] = _pre_mix_norm(x, gnext_ref, modnext_ref)
    o_ref[...] = x


def _final(x1, f, g_post, mod, n_rows, next_norm, seq, n_lat_rows, n_batch):
    d = x1.shape[1]
    tr = EW_ROWS
    ridx = functools.partial(_mod_row_index, rows_per_tile=tr, seq=seq, n_lat_rows=n_lat_rows,
                             n_batch=n_batch)
    row = pl.BlockSpec((tr, d), lambda i: (i, 0))
    vec = pl.BlockSpec((1, d), lambda i: (0, 0))
    modspec = pl.BlockSpec((None, N_MOD, d), lambda i: (ridx(i), 0, 0))
    blocks = [((tr, d), F32)] * 3 + [((N_MOD, d), F32)]
    in_specs, args = [row, row, vec, modspec], [x1, f, g_post.reshape(1, d), mod]
    out_shape, out_specs = jax.ShapeDtypeStruct((n_rows, d), F32), row
    if next_norm is not None:
        in_specs += [vec, modspec]
        args += [next_norm[0].reshape(1, d), next_norm[1]]
        out_shape = (out_shape, jax.ShapeDtypeStruct((n_rows, d), BF16))
        out_specs = (row, row)
        blocks += [((N_MOD, d), F32), ((tr, d), BF16)]
    return pl.pallas_call(
        _final_kernel,
        out_shape=out_shape,
        grid=(n_rows // tr,),
        in_specs=in_specs,
        out_specs=out_specs,
        compiler_params=_params(("arbitrary",), blocks, temps=[((tr, d), F32)] * 2),
        name="final_residual",
    )(*args)


def _swap_pairs(x):
    lane = lax.broadcasted_iota(jnp.int32, x.shape, 1)
    quarter = HEAD_DIM // 4
    first = (lane & quarter) == 0
    return jnp.where(first, pltpu.roll(x, HEAD_DIM - quarter, axis=1), pltpu.roll(x, quarter, axis=1))


def _inproj_kernel(*refs, mode, n_q_tiles, q_scale):
    if mode in ("rope", "normrope"):
        h_ref, w_ref, cos_ref, sin_ref, gq_ref, gk_ref, o_ref, wbf_ref = refs
    else:
        h_ref, w_ref, o_ref, wbf_ref = refs
    j = pl.program_id(0)

    @pl.when(pl.program_id(1) == 0)
    def _():
        _cast_weight(w_ref, wbf_ref)

    tm = h_ref.shape[0]
    n_chunks = EPILOGUE_CHUNKS if tm % (EPILOGUE_CHUNKS * ROW_ALIGN) == 0 else 1
    rc = tm // n_chunks
    for c in range(n_chunks):
        rows = slice(c * rc, (c + 1) * rc)
        acc = _dot(h_ref[rows, :], wbf_ref[...])
        if mode == "plain":
            o_ref[rows, :] = acc.astype(BF16)
        elif mode == "gelu":
            o_ref[rows, :] = (0.5 * acc * (1.0 + lax.erf(acc * (2.0 ** -0.5)))).astype(BF16)
        else:
            is_q = j < n_q_tiles
            scale = jnp.where(is_q, q_scale, 1.0).astype(F32)
            cos = cos_ref[rows, :]
            sin = sin_ref[rows, :]
            gain = jnp.where(is_q, gq_ref[...], gk_ref[...])
            for hh in range(acc.shape[1] // HEAD_DIM):
                sl = slice(hh * HEAD_DIM, (hh + 1) * HEAD_DIM)
                xh = acc[:, sl]
                if mode == "normrope":
                    xh = _rms(xh, gain)
                xh = xh * cos + _swap_pairs(xh) * sin
                o_ref[rows, sl] = (xh * scale).astype(BF16)


def _inproj(h, w_in, layer, col_start, width, mode, rope=None, n_q_tiles=0):
    t, d = h.shape
    tm, tn = _row_tile(t), COL_TILE
    assert col_start % tn == 0 and width % tn == 0
    j0 = col_start // tn
    in_specs = [pl.BlockSpec((tm, d), lambda j, i: (i, 0)),
                pl.BlockSpec((None, d, tn), lambda j, i: (layer, 0, j0 + j))]
    args = [h, w_in]
    blocks = [((tm, d), BF16), ((d, tn), F32), ((tm, tn), BF16)]
    if mode in ("rope", "normrope"):
        cos, sin, gq, gk = rope
        tab = pl.BlockSpec((tm, HEAD_DIM), lambda j, i: (i, 0))
        vec = pl.BlockSpec((1, HEAD_DIM), lambda j, i: (0, 0))
        in_specs += [tab, tab, vec, vec]
        args += [cos, sin, gq, gk]
        blocks += [((tm, HEAD_DIM), F32)] * 2
    kern = functools.partial(_inproj_kernel, mode=mode, n_q_tiles=n_q_tiles,
                             q_scale=HEAD_DIM ** -0.5 * LOG2E)
    return pl.pallas_call(
        kern,
        out_shape=jax.ShapeDtypeStruct((t, width), BF16),
        grid=(width // tn, t // tm),
        in_specs=in_specs,
        out_specs=pl.BlockSpec((tm, tn), lambda j, i: (i, j)),
        scratch_shapes=[pltpu.VMEM((d, tn), BF16)],
        compiler_params=_params(("arbitrary", "arbitrary"), blocks, scratch=[((d, tn), BF16)],
                                temps=[((tm, tn), F32)] * 2),
        name="inproj_" + mode,
    )(*args)


def _window_bias(group, n_ctx):
    r = jnp.arange(group * BLOCK)[:, None] % BLOCK
    c = jnp.arange(3 * BLOCK + n_ctx)[None, :]
    in_prev, in_cur = c < BLOCK, (c >= BLOCK) & (c < 2 * BLOCK)
    in_next = (c >= 2 * BLOCK) & (c < 3 * BLOCK)
    band_prev = in_prev & (c < r)
    band_next = in_next & (c - 2 * BLOCK > r)
    hidden = [band_prev | band_next, in_prev | band_next, band_prev | in_next, in_prev | in_next,
              in_prev | in_cur | in_next]
    shape = (group * BLOCK, 3 * BLOCK + n_ctx)
    return jnp.stack([jnp.where(jnp.broadcast_to(h, shape), -jnp.inf, 0.0).astype(F32)
                      for h in hidden])


def _window_attn_kernel(sink_ref, bias_ref, q_ref, kp_ref, kc_ref, kn_ref, kx_ref, vp_ref, vc_ref,
                        vn_ref, vx_ref, o_ref, *, group):
    rows = group * BLOCK
    assert BLOCK & (BLOCK - 1) == 0
    shift = BLOCK.bit_length() - 1
    bias = bias_ref[...]
    rid = lax.broadcasted_iota(jnp.int32, (rows, 1), 0) >> shift
    for hk in range(A_KV_HEADS):
        ksl = slice(hk * HEAD_DIM, (hk + 1) * HEAD_DIM)
        k_all = jnp.concatenate([kp_ref[:, ksl], kc_ref[:, ksl], kn_ref[:, ksl], kx_ref[:, ksl]],
                                axis=0)
        v_all = jnp.concatenate([vp_ref[:, ksl], vc_ref[:, ksl], vn_ref[:, ksl], vx_ref[:, ksl]],
                                axis=0)
        q3 = jnp.concatenate(
            [q_ref[:, (hk * group + g) * HEAD_DIM:(hk * group + g + 1) * HEAD_DIM]
             for g in range(group)], axis=0)
        sink = jnp.zeros((rows, 1), F32)
        for g in range(group):
            sink = jnp.where(rid == g, sink_ref[hk * group + g] * LOG2E, sink)
        s = _dot_nt(q3, k_all) + bias
        m = jnp.maximum(jnp.max(s, axis=-1, keepdims=True), sink)
        p = jnp.exp2(s - m)
        denom = jnp.sum(p, axis=-1, keepdims=True) + jnp.exp2(sink - m)
        o = _dot(p.astype(BF16), v_all) / denom
        for g in range(group):
            osl = slice((hk * group + g) * HEAD_DIM, (hk * group + g + 1) * HEAD_DIM)
            o_ref[:, osl] = o[g * BLOCK:(g + 1) * BLOCK, :].astype(BF16)


def _window_attn(za, av, sink, n_batch, seq, n_ctx, with_ctx):
    n_heads = sink.shape[0]
    group = n_heads // A_KV_HEADS
    qw = n_heads * HEAD_DIM
    kw = A_KV_HEADS * HEAD_DIM
    assert qw % kw == 0 and seq % BLOCK == 0 and (n_batch * seq) % n_ctx == 0 and WINDOW == BLOCK
    assert n_ctx % BLOCK == 0
    nb = seq // BLOCK
    ncb = n_ctx // BLOCK if with_ctx else 0
    kcol = qw // kw
    ctx0 = (n_batch * seq) // n_ctx

    def qblk(b, n):
        return (jnp.where(n < nb, b * nb + n, n_batch * nb + b * ncb + n - nb), 0)

    def blk(shift):
        return lambda b, n: (b * nb + jnp.clip(n + shift, 0, nb - 1), kcol)

    def vblk(shift):
        return lambda b, n: (b * nb + jnp.clip(n + shift, 0, nb - 1), 0)

    def bias_variant(b, n):
        lat = jnp.where(n == 0, 1, 0) + jnp.where(n == nb - 1, 2, 0)
        return (jnp.where(n < nb, lat, 4), 0, 0)

    kspec = [pl.BlockSpec((BLOCK, kw), blk(s)) for s in (-1, 0, 1)]
    vspec = [pl.BlockSpec((BLOCK, kw), vblk(s)) for s in (-1, 0, 1)]
    n_keys = 3 * BLOCK + n_ctx
    rows = group * BLOCK
    blocks = ([((BLOCK, qw), BF16)] * 2 + [((BLOCK, kw), BF16)] * 6 + [((n_ctx, kw), BF16)] * 2
              + [((rows, n_keys), F32)])
    return pl.pallas_call(
        functools.partial(_window_attn_kernel, group=group),
        out_shape=jax.ShapeDtypeStruct((n_batch * (nb + ncb) * BLOCK, qw), BF16),
        grid=(n_batch, nb + ncb),
        in_specs=[pl.BlockSpec(memory_space=pltpu.SMEM),
                  pl.BlockSpec((None, rows, n_keys), bias_variant),
                  pl.BlockSpec((BLOCK, qw), qblk)]
                 + kspec + [pl.BlockSpec((n_ctx, kw), lambda b, n: (ctx0 + b, kcol))]
                 + vspec + [pl.BlockSpec((n_ctx, kw), lambda b, n: (ctx0 + b, 0))],
        out_specs=pl.BlockSpec((BLOCK, qw), qblk),
        compiler_params=_params(("arbitrary", "arbitrary"), blocks,
                                temps=[((rows, n_keys), F32)] * 4),
        name="window_attn",
    )(sink, _window_bias(group, n_ctx), za, za, za, za, za, av, av, av, av)


def _gmlp_kernel(z_ref, g_ref, b_ref, ws_ref, bst_ref, o_ref):
    width = g_ref.shape[1]
    u = z_ref[:, :width].astype(F32)
    v = z_ref[:, width:].astype(F32)
    mu = jnp.mean(v, axis=-1, keepdims=True)
    vc = v - mu
    var = jnp.mean(vc * vc, axis=-1, keepdims=True)
    vn = (vc * lax.rsqrt(var + EPS) * g_ref[...] + b_ref[...]).astype(BF16)
    gd = width // B_GROUPS
    for g in range(B_GROUPS):
        sl = slice(g * gd, (g + 1) * gd)
        mixed = _dot(ws_ref[g].astype(BF16), vn[:, sl]) + bst_ref[:, g:g + 1]
        o_ref[:, sl] = (u[:, sl] * mixed).astype(BF16)


def _gmlp(zb, ln_g, ln_b, ws, bs, n_rows):
    w2 = zb.shape[1]
    width = w2 // 2
    blocks = [((CHUNK, w2), BF16), ((1, width), F32), ((1, width), F32),
              (ws.shape, F32), ((CHUNK, B_GROUPS), F32), ((CHUNK, width), BF16)]
    return pl.pallas_call(
        _gmlp_kernel,
        out_shape=jax.ShapeDtypeStruct((n_rows, width), BF16),
        grid=(n_rows // CHUNK,),
        in_specs=[pl.BlockSpec((CHUNK, w2), lambda i: (i, 0)),
                  pl.BlockSpec((1, width), lambda i: (0, 0)),
                  pl.BlockSpec((1, width), lambda i: (0, 0)),
                  pl.BlockSpec(ws.shape, lambda i: (0, 0, 0)),
                  pl.BlockSpec((CHUNK, B_GROUPS), lambda i: (0, 0))],
        out_specs=pl.BlockSpec((CHUNK, width), lambda i: (i, 0)),
        compiler_params=_params(("arbitrary",), blocks, temps=[((CHUNK, w2), F32)] * 2),
        name="chunk_gmlp",
    )(zb, ln_g.reshape(1, width), ln_b.reshape(1, width), ws, bs.T)


def _global_attn_kernel(q_ref, kx_ref, vx_ref, k_ref, v_ref, o_ref, vext_ref, *, group,
                        n_lat_tiles, has_ctx_tiles):
    tq = q_ref.shape[0]
    n_ctx = kx_ref.shape[0]
    n_lat = k_ref.shape[0]
    rows = group * tq

    @pl.when(pl.program_id(2) == 0)
    def _():
        vext_ref[0:n_ctx, 0:HEAD_DIM] = vx_ref[...]
        vext_ref[n_ctx:, 0:HEAD_DIM] = v_ref[...]
        vext_ref[:, HEAD_DIM:] = jnp.ones((n_ctx + n_lat, HEAD_DIM), BF16)

    def attend(chunks):
        q3 = jnp.concatenate([q_ref[:, g * HEAD_DIM:(g + 1) * HEAD_DIM] for g in range(group)],
                             axis=0)
        m = jnp.full((rows, HEAD_DIM), -jnp.inf, F32)
        acc = jnp.zeros((rows, 2 * HEAD_DIM), F32)
        for k_chunk, v0, nk in chunks:
            s = _dot_nt(q3, k_chunk())
            m_new = jnp.maximum(m, jnp.broadcast_to(jnp.max(s, axis=-1, keepdims=True), m.shape))
            alpha = jnp.exp2(m - m_new)
            p = jnp.exp2(s - jnp.tile(m_new, (1, nk // HEAD_DIM)))
            acc = jnp.tile(alpha, (1, 2)) * acc + _dot(p.astype(BF16), vext_ref[v0:v0 + nk, :])
            m = m_new
        o = acc[:, :HEAD_DIM] / acc[:, HEAD_DIM:]
        for g in range(group):
            o_ref[:, g * HEAD_DIM:(g + 1) * HEAD_DIM] = o[g * tq:(g + 1) * tq, :].astype(BF16)

    ctx_chunk = [(lambda: kx_ref[...], 0, n_ctx)]
    lat_chunks = [(lambda c=c: k_ref[c:c + K_TILE, :], n_ctx + c, K_TILE)
                  for c in range(0, n_lat, K_TILE)]
    is_lat = pl.program_id(2) < n_lat_tiles
    pl.when(is_lat)(lambda: attend(ctx_chunk + lat_chunks))
    if has_ctx_tiles:
        pl.when(jnp.logical_not(is_lat))(lambda: attend(ctx_chunk))


def _global_attn(zc, cv, n_batch, seq, n_ctx, n_heads, with_ctx):
    group = n_heads // C_KV_HEADS
    gw = group * HEAD_DIM
    tq = Q_TILE
    assert seq % tq == 0 and seq % K_TILE == 0 and (n_batch * seq) % n_ctx == 0 and n_ctx % tq == 0
    nq = seq // tq
    ncq = n_ctx // tq if with_ctx else 0
    ctx0 = (n_batch * seq) // n_ctx
    rows = group * tq

    def qblk(b, hk, i):
        return (jnp.where(i < nq, b * nq + i, n_batch * nq + b * ncq + i - nq), hk)

    blocks = [((tq, gw), BF16)] * 2 + [((n_ctx, HEAD_DIM), BF16)] * 2 + [((seq, HEAD_DIM), BF16)] * 2
    scratch = [((n_ctx + seq, 2 * HEAD_DIM), BF16)]
    return pl.pallas_call(
        functools.partial(_global_attn_kernel, group=group, n_lat_tiles=nq, has_ctx_tiles=ncq > 0),
        out_shape=jax.ShapeDtypeStruct((n_batch * (nq + ncq) * tq, n_heads * HEAD_DIM), BF16),
        grid=(n_batch, C_KV_HEADS, nq + ncq),
        in_specs=[pl.BlockSpec((tq, gw), qblk),
                  pl.BlockSpec((n_ctx, HEAD_DIM), lambda b, hk, i: (ctx0 + b, n_heads + hk)),
                  pl.BlockSpec((n_ctx, HEAD_DIM), lambda b, hk, i: (ctx0 + b, hk)),
                  pl.BlockSpec((seq, HEAD_DIM), lambda b, hk, i: (b, n_heads + hk)),
                  pl.BlockSpec((seq, HEAD_DIM), lambda b, hk, i: (b, hk))],
        out_specs=pl.BlockSpec((tq, gw), qblk),
        scratch_shapes=[pltpu.VMEM(s, d) for s, d in scratch],
        compiler_params=_params(("arbitrary", "arbitrary", "arbitrary"), blocks, scratch=scratch,
                                temps=[((rows, K_TILE), F32)] * 6),
        name="global_attn",
    )(zc, zc, cv, zc, cv)


def _outproj_kernel(a_ref, b_ref, c_ref, w_ref, o_ref, wbf_ref):
    @pl.when(pl.program_id(1) == 0)
    def _():
        _cast_weight(w_ref, wbf_ref)

    ka, kb = a_ref.shape[1], b_ref.shape[1]
    acc = _dot(a_ref[...], wbf_ref[0:ka, :])
    acc += _dot(b_ref[...], wbf_ref[ka:ka + kb, :])
    acc += _dot(c_ref[...], wbf_ref[ka + kb:, :])
    o_ref[...] = acc.astype(BF16)


def _outproj(oa, ob, oc, w_out, layer, n_rows):
    t = oa.shape[0]
    _, k, d = w_out.shape
    tm, tn = _row_tile(n_rows), COL_TILE
    assert oa.shape[1] + ob.shape[1] + oc.shape[1] == k and d % tn == 0
    blocks = [((tm, k), BF16), ((k, tn), F32), ((tm, tn), BF16)]
    return pl.pallas_call(
        _outproj_kernel,
        out_shape=jax.ShapeDtypeStruct((t, d), BF16),
        grid=(d // tn, n_rows // tm),
        in_specs=[pl.BlockSpec((tm, oa.shape[1]), lambda j, i: (i, 0)),
                  pl.BlockSpec((tm, ob.shape[1]), lambda j, i: (i, 0)),
                  pl.BlockSpec((tm, oc.shape[1]), lambda j, i: (i, 0)),
                  pl.BlockSpec((None, k, tn), lambda j, i: (layer, 0, j))],
        out_specs=pl.BlockSpec((tm, tn), lambda j, i: (i, j)),
        scratch_shapes=[pltpu.VMEM((k, tn), BF16)],
        compiler_params=_params(("arbitrary", "arbitrary"), blocks, scratch=[((k, tn), BF16)],
                                temps=[((tm, tn), F32)] * 2),
        name="outproj",
    )(oa, ob, oc, w_out)


def _ffn_up_kernel(h_ref, wg_ref, wu_ref, o_ref, wgbf_ref, wubf_ref):
    @pl.when(pl.program_id(1) == 0)
    def _():
        _cast_weight(wg_ref, wgbf_ref)
        _cast_weight(wu_ref, wubf_ref)

    h = h_ref[...]
    gate = _dot(h, wgbf_ref[...])
    up = _dot(h, wubf_ref[...])
    o_ref[...] = (jax.nn.silu(gate) * up).astype(BF16)


def _ffn_up(h2, w_gate, w_up, sel, n_rows):
    t, d = h2.shape
    f = w_gate.shape[-1]
    tm, tn = _row_tile(n_rows), FFN_COL_TILE
    assert f % tn == 0
    lead = (None,) * len(sel)
    wspec = pl.BlockSpec(lead + (d, tn), lambda j, i: sel + (0, j))
    blocks = [((tm, d), BF16), ((d, tn), F32), ((d, tn), F32), ((tm, tn), BF16)]
    return pl.pallas_call(
        _ffn_up_kernel,
        out_shape=jax.ShapeDtypeStruct((t, f), BF16),
        grid=(f // tn, n_rows // tm),
        in_specs=[pl.BlockSpec((tm, d), lambda j, i: (i, 0)), wspec, wspec],
        out_specs=pl.BlockSpec((tm, tn), lambda j, i: (i, j)),
        scratch_shapes=[pltpu.VMEM((d, tn), BF16)] * 2,
        compiler_params=_params(("arbitrary", "arbitrary"), blocks, scratch=[((d, tn), BF16)] * 2,
                                temps=[((tm, tn), F32)] * 3),
        name="ffn_up",
    )(h2, w_gate, w_up)


def _ffn_down_kernel(*refs, has_prev):
    if has_prev:
        a_ref, w_ref, prev_ref, o_ref, wbf_ref = refs
    else:
        a_ref, w_ref, o_ref, wbf_ref = refs

    @pl.when(pl.program_id(1) == 0)
    def _():
        _cast_weight(w_ref, wbf_ref)

    acc = _dot(a_ref[...], wbf_ref[...])
    if has_prev:
        acc = prev_ref[...] + acc
    o_ref[...] = acc


def _ffn_down(a, w_down, sel, k_start, k_size, n_rows, prev=None):
    t = a.shape[0]
    d = w_down.shape[-1]
    tm, tn = ROW_TILE, COL_TILE
    assert k_start % k_size == 0 and d % tn == 0 and n_rows % tm == 0
    kb = k_start // k_size
    lead = (None,) * len(sel)
    in_specs = [pl.BlockSpec((tm, k_size), lambda j, i: (i, kb)),
                pl.BlockSpec(lead + (k_size, tn), lambda j, i: sel + (kb, j))]
    args = [a, w_down]
    blocks = [((tm, k_size), BF16), ((k_size, tn), F32), ((tm, tn), F32)]
    aliases = {}
    if prev is not None:
        in_specs.append(pl.BlockSpec((tm, tn), lambda j, i: (i, j)))
        aliases = {len(args): 0}
        args.append(prev)
        blocks.append(((tm, tn), F32))
    return pl.pallas_call(
        functools.partial(_ffn_down_kernel, has_prev=prev is not None),
        out_shape=jax.ShapeDtypeStruct((t, d), F32),
        grid=(d // tn, n_rows // tm),
        in_specs=in_specs,
        out_specs=pl.BlockSpec((tm, tn), lambda j, i: (i, j)),
        scratch_shapes=[pltpu.VMEM((k_size, tn), BF16)],
        input_output_aliases=aliases,
        compiler_params=_params(("arbitrary", "arbitrary"), blocks, scratch=[((k_size, tn), BF16)],
                                temps=[((tm, tn), F32)] * 2),
        name="ffn_down",
    )(*args)


def _route(h, w_pieces, n_experts):
    pieces = _dot(h, w_pieces)
    logits = pieces
    for k in range(1, ROUTER_PIECES):
        logits = logits + pltpu.roll(pieces, V7X_LANES - k * n_experts, axis=1)
    lane = lax.broadcasted_iota(jnp.int32, logits.shape, 1).astype(F32)
    neg = -jnp.inf
    logits = jnp.where(lane < n_experts, logits, neg)
    picked = []
    remaining = logits
    for _ in range(TOP_K):
        top = jnp.max(remaining, axis=-1, keepdims=True)
        idx = jnp.min(jnp.where(remaining == top, lane, float(logits.shape[1])), axis=-1,
                      keepdims=True)
        picked.append((top, idx))
        remaining = jnp.where(lane == idx, neg, remaining)
    top0 = picked[0][0]
    denom = sum(jnp.exp(tv - top0) for tv, _ in picked)
    route = jnp.zeros(logits.shape, F32)
    for k, (tv, idx) in enumerate(picked):
        route = jnp.where(lane == k, idx, route)
        route = jnp.where(lane == TOP_K + k, jnp.exp(tv - top0) / denom, route)
    return route


def _router_pieces(w_router):
    n_experts = w_router.shape[1]
    assert ROUTER_PIECES * n_experts <= V7X_LANES
    pieces, rest = [], w_router
    for _ in range(ROUTER_PIECES):
        pieces.append(rest.astype(BF16))
        rest = rest - pieces[-1].astype(F32)
    return jnp.pad(jnp.concatenate(pieces, axis=1),
                   ((0, 0), (0, V7X_LANES - ROUTER_PIECES * n_experts)))


def _route_plan(route, n_experts):
    tile = MOE_ROW_TILE
    n_tok = route.shape[0]
    n_asg = n_tok * TOP_K
    e_flat = route[:, :TOP_K].astype(jnp.int32).reshape(n_asg)
    onehot = (e_flat[:, None] == jnp.arange(n_experts, dtype=jnp.int32)[None, :]).astype(jnp.int32)
    csum = jnp.cumsum(onehot, axis=0)
    rank = jnp.take_along_axis(csum, e_flat[:, None], axis=1)[:, 0] - 1
    counts = csum[-1]
    tiles_per = (counts + tile - 1) // tile
    tile_end = jnp.cumsum(tiles_per)
    tile_start = tile_end - tiles_per
    dest = tile_start[e_flat] * tile + rank

    n_tiles = (n_asg + n_experts * (tile - 1)) // tile
    tile_ids = jnp.arange(n_tiles, dtype=jnp.int32)
    n_used = tile_end[-1]
    valid = tile_ids < n_used
    expert_raw = jnp.sum((tile_end[None, :] <= tile_ids[:, None]).astype(jnp.int32), axis=1)
    tile_expert = jnp.where(valid, expert_raw, expert_raw[n_used - 1])
    tile_first = valid & (tile_ids == tile_start[tile_expert])
    tile_row = jnp.where(valid, tile_ids, n_used - 1)

    tok_of_row = jnp.zeros((n_tiles * tile,), jnp.int32).at[dest].set(
        jnp.arange(n_asg, dtype=jnp.int32) // TOP_K, unique_indices=True)
    tile_first = tile_first.astype(jnp.int32)
    tile_slot = (jnp.cumsum(tile_first) - 1) % 2
    tile_fill = jnp.where(
        valid, jnp.clip(counts[tile_expert] - (tile_ids - tile_start[tile_expert]) * tile, 0, tile), 0)
    tiles = (tile_expert, tile_first, tile_fill, tile_row, tile_slot)
    assert tile % EW_ROWS == 0
    block_start = jnp.arange(n_tiles * tile // EW_ROWS, dtype=jnp.int32) * EW_ROWS
    live = (tile_fill[block_start // tile] > block_start % tile).astype(jnp.int32)
    return tok_of_row, live, dest, tiles


def _split_tiles(tiles, factor):
    tile_expert, tile_first, tile_fill, _, tile_slot = tiles
    n = tile_expert.shape[0] * factor
    sub = MOE_ROW_TILE // factor
    part = jnp.arange(n, dtype=jnp.int32) % factor
    fill = jnp.clip(jnp.repeat(tile_fill, factor) - part * sub, 0, sub)
    first = jnp.where(part == 0, jnp.repeat(tile_first, factor), 0)
    row = lax.cummax(jnp.where(fill > 0, jnp.arange(n, dtype=jnp.int32), 0))
    return jnp.repeat(tile_expert, factor), first, fill, row, jnp.repeat(tile_slot, factor)


def _row_copy(src_hbm, dst_vmem, sem, src_row, dst_row):
    return pltpu.make_async_copy(src_hbm.at[pl.ds(src_row, 1)], dst_vmem.at[pl.ds(dst_row, 1)], sem)


def _gather_rows_kernel(idx_ref, live_ref, src_ref, o_ref, buf_ref, sems):
    tile = buf_ref.shape[1]
    n_blocks = pl.num_programs(0)
    g = pl.program_id(0)

    def start_block(b):
        def issue(r, carry):
            _row_copy(src_ref, buf_ref.at[b % 2], sems.at[b % 2], idx_ref[b * tile + r], r).start()
            return carry
        lax.fori_loop(0, tile, issue, 0, unroll=GATHER_UNROLL)

    @pl.when((g == 0) & (live_ref[0] == 1))
    def _():
        start_block(g)

    nxt = jnp.minimum(g + 1, n_blocks - 1)

    @pl.when((g + 1 < n_blocks) & (live_ref[nxt] == 1))
    def _():
        start_block(nxt)

    @pl.when(live_ref[g] == 1)
    def _():
        def drain(r, carry):
            _row_copy(src_ref, buf_ref.at[g % 2], sems.at[g % 2], 0, r).wait()
            return carry
        lax.fori_loop(0, tile, drain, 0, unroll=GATHER_UNROLL)
        o_ref[...] = _from_slabs(buf_ref.at[g % 2]).astype(BF16)

    @pl.when(live_ref[g] == 0)
    def _():
        o_ref[...] = jnp.zeros(o_ref.shape, BF16)


def _gather_rows(src, idx, live):
    slab = src.shape[1:]
    d = slab[0] * slab[1]
    n = idx.shape[0]
    tg = EW_ROWS
    assert n % tg == 0 and live.shape[0] == n // tg
    blocks = [((tg, d), BF16)]
    return pl.pallas_call(
        _gather_rows_kernel,
        out_shape=jax.ShapeDtypeStruct((n, d), BF16),
        grid_spec=pltpu.PrefetchScalarGridSpec(
            num_scalar_prefetch=2,
            grid=(n // tg,),
            in_specs=[pl.BlockSpec(memory_space=pl.ANY)],
            out_specs=pl.BlockSpec((tg, d), lambda i, idx_ref, live_ref: (i, 0)),
            scratch_shapes=[pltpu.VMEM((2, tg) + slab, F32), pltpu.SemaphoreType.DMA((2,))]),
        compiler_params=_params(("arbitrary",), blocks, scratch=[((2, tg, d), F32)],
                                temps=[((tg, d), F32)]),
        name="moe_dispatch",
    )(idx, live, src)


def _held_tile(s, n_tiles):
    return jnp.minimum(s, n_tiles - 1)


def _done_tile(s):
    return jnp.maximum(s - 1, 0)


def _grouped_step(tables, w_refs, wbf_refs, compute, o_ref):
    _, tf_ref, tn_ref, _, ts_ref = tables
    n_tiles = tf_ref.shape[0]
    n_slots = wbf_refs[0].shape[0]
    tm = o_ref.shape[0]
    s = pl.program_id(1)
    held = _held_tile(s, n_tiles)
    done = _done_tile(s)
    fill = jnp.where(s > 0, tn_ref[done], -MOE_ROW_STEP)

    def slot(tile):
        return ts_ref[tile] % n_slots if n_slots > 1 else 0

    for n in range(0, tm + 1, MOE_ROW_STEP):
        @pl.when((fill > n - MOE_ROW_STEP) & (fill <= n))
        def _(n=n):
            if n > 0:
                o_ref[0:n, :] = compute([wbf_ref[slot(done)] for wbf_ref in wbf_refs], n)
            if n < tm:
                o_ref[n:, :] = jnp.zeros((tm - n, o_ref.shape[1]), o_ref.dtype)

    @pl.when((s < n_tiles) & (tf_ref[held] == 1))
    def _():
        for w_ref, wbf_ref in zip(w_refs, wbf_refs):
            _cast_weight(w_ref, wbf_ref.at[slot(held)])


def _moe_up_kernel(te_ref, tf_ref, tn_ref, tr_ref, ts_ref, x_ref, wg_ref, wu_ref, o_ref, wgbf_ref,
                   wubf_ref):
    def compute(w, n):
        x = x_ref[0:n, :]
        return (jax.nn.silu(_dot(x, w[0])) * _dot(x, w[1])).astype(BF16)

    _grouped_step((te_ref, tf_ref, tn_ref, tr_ref, ts_ref), (wg_ref, wu_ref), (wgbf_ref, wubf_ref),
                  compute, o_ref)


def _moe_up(xs, w_gate, w_up, layer, tiles):
    r, d = xs.shape
    f = w_gate.shape[-1]
    tm, tn = MOE_UP_ROW_TILE, COL_TILE
    assert f % tn == 0 and r % tm == 0
    tiles = _split_tiles(tiles, MOE_ROW_TILE // tm)
    nt = r // tm
    wspec = pl.BlockSpec((None, None, d, tn),
                         lambda j, s, te, tf, tv, tr, ts: (layer, te[_held_tile(s, nt)], 0, j))
    blocks = [((tm, d), BF16), ((d, tn), F32), ((d, tn), F32), ((tm, tn), BF16)]
    return pl.pallas_call(
        _moe_up_kernel,
        out_shape=jax.ShapeDtypeStruct((r, f), BF16),
        grid_spec=pltpu.PrefetchScalarGridSpec(
            num_scalar_prefetch=5,
            grid=(f // tn, nt + 1),
            in_specs=[pl.BlockSpec((tm, d), lambda j, s, te, tf, tv, tr, ts: (tr[_done_tile(s)], 0)),
                      wspec, wspec],
            out_specs=pl.BlockSpec((tm, tn), lambda j, s, te, tf, tv, tr, ts: (_done_tile(s), j)),
            scratch_shapes=[pltpu.VMEM((1, d, tn), BF16)] * 2),
        compiler_params=_params(("arbitrary", "arbitrary"), blocks,
                                scratch=[((1, d, tn), BF16)] * 2, temps=[((tm, tn), F32)] * 3),
        name="moe_up",
    )(*tiles, xs, w_gate, w_up)


def _moe_down_kernel(te_ref, tf_ref, tn_ref, tr_ref, ts_ref, a_ref, w_ref, o_ref, wbf_ref):
    _grouped_step((te_ref, tf_ref, tn_ref, tr_ref, ts_ref), (w_ref,), (wbf_ref,),
                  lambda w, n: _dot(a_ref[0:n, :], w[0]), o_ref)


def _moe_down(hmid, w_down, layer, tiles):
    r, f = hmid.shape
    d = w_down.shape[-1]
    tm, tn = MOE_UP_ROW_TILE, MOE_DOWN_COL_TILE
    assert d % tn == 0 and r % tm == 0
    tiles = _split_tiles(tiles, MOE_ROW_TILE // tm)
    nt = r // tm
    blocks = [((tm, f), BF16), ((f, tn), F32), ((tm, tn), F32)]
    return pl.pallas_call(
        _moe_down_kernel,
        out_shape=jax.ShapeDtypeStruct((r, d), F32),
        grid_spec=pltpu.PrefetchScalarGridSpec(
            num_scalar_prefetch=5,
            grid=(d // tn, nt + 1),
            in_specs=[pl.BlockSpec((tm, f), lambda j, s, te, tf, tv, tr, ts: (tr[_done_tile(s)], 0)),
                      pl.BlockSpec((None, None, f, tn),
                                   lambda j, s, te, tf, tv, tr, ts:
                                   (layer, te[_held_tile(s, nt)], 0, j))],
            out_specs=pl.BlockSpec((tm, tn), lambda j, s, te, tf, tv, tr, ts: (_done_tile(s), j)),
            scratch_shapes=[pltpu.VMEM((1, f, tn), BF16)]),
        compiler_params=_params(("arbitrary", "arbitrary"), blocks, scratch=[((1, f, tn), BF16)],
                                temps=[((tm, tn), F32)] * 2),
        name="moe_down",
    )(*tiles, hmid, w_down)


def _moe_final_kernel(dest_ref, x_ref, y_ref, route_ref, g_ref, mod_ref, o_ref, buf_ref, sems):
    tile = x_ref.shape[0]
    n_blocks = pl.num_programs(0)
    g = pl.program_id(0)

    def start_block(b):
        def issue(r, carry):
            for k in range(TOP_K):
                _row_copy(y_ref, buf_ref.at[b % 2, k], sems.at[b % 2, k],
                          dest_ref[(b * tile + r) * TOP_K + k], r).start()
            return carry
        lax.fori_loop(0, tile, issue, 0, unroll=GATHER_UNROLL)

    @pl.when(g == 0)
    def _():
        start_block(g)

    @pl.when(g + 1 < n_blocks)
    def _():
        start_block(g + 1)

    def drain(r, carry):
        for k in range(TOP_K):
            _row_copy(y_ref, buf_ref.at[g % 2, k], sems.at[g % 2, k], 0, r).wait()
        return carry

    lax.fori_loop(0, tile, drain, 0, unroll=GATHER_UNROLL)
    f = route_ref[:, TOP_K:TOP_K + 1] * buf_ref[g % 2, 0]
    for k in range(1, TOP_K):
        f = f + route_ref[:, TOP_K + k:TOP_K + k + 1] * buf_ref[g % 2, k]
    o_ref[...] = x_ref[...] + mod_ref[5:6, :] * _rms(f, g_ref[...])


def _moe_final(x1, y, dest, route, g_post, mod, n_rows, seq, n_lat_rows, n_batch):
    d = x1.shape[1]
    tr = EW_ROWS
    ridx = functools.partial(_mod_row_index, rows_per_tile=tr, seq=seq, n_lat_rows=n_lat_rows,
                             n_batch=n_batch)
    row = pl.BlockSpec((tr, d), lambda i, dest_ref: (i, 0))
    blocks = [((tr, d), F32)] * 2 + [((N_MOD, d), F32)]
    return pl.pallas_call(
        _moe_final_kernel,
        out_shape=jax.ShapeDtypeStruct((n_rows, d), F32),
        grid_spec=pltpu.PrefetchScalarGridSpec(
            num_scalar_prefetch=1,
            grid=(n_rows // tr,),
            in_specs=[row, pl.BlockSpec(memory_space=pl.ANY),
                      pl.BlockSpec((tr, V7X_LANES), lambda i, dest_ref: (i, 0)),
                      pl.BlockSpec((1, d), lambda i, dest_ref: (0, 0)),
                      pl.BlockSpec((None, N_MOD, d), lambda i, dest_ref: (ridx(i), 0, 0))],
            out_specs=row,
            scratch_shapes=[pltpu.VMEM((2, TOP_K, tr, d), F32),
                            pltpu.SemaphoreType.DMA((2, TOP_K))]),
        compiler_params=_params(("arbitrary",), blocks, scratch=[((2, TOP_K, tr, d), F32)],
                                temps=[((tr, d), F32)] * 2),
        name="moe_combine_final",
    )(dest, x1, y, route, g_post.reshape(1, d), mod)


def _rope_tables(seq, n_batch, ctx_rows):
    n = jnp.arange(seq)
    pos_r = (n // GRID_W).astype(F32)
    pos_w = (n % GRID_W).astype(F32)
    n_freq = HEAD_DIM // 4
    inv_freq = ROPE_THETA ** (-jnp.arange(n_freq, dtype=F32) / n_freq)
    ar = pos_r[:, None] * inv_freq
    aw = pos_w[:, None] * inv_freq
    cos = jnp.concatenate([jnp.cos(ar), jnp.cos(ar), jnp.cos(aw), jnp.cos(aw)], axis=-1)
    sin = jnp.concatenate([-jnp.sin(ar), jnp.sin(ar), -jnp.sin(aw), jnp.sin(aw)], axis=-1)
    cos = jnp.concatenate([cos] * n_batch + [jnp.ones((ctx_rows, HEAD_DIM), F32)], axis=0)
    sin = jnp.concatenate([sin] * n_batch + [jnp.zeros((ctx_rows, HEAD_DIM), F32)], axis=0)
    return cos, sin


def kernel(x, c, ctx, c_ctx, w_mod, b_mod, g_pre_mix, g_post_mix, g_pre_ffn, g_post_ffn, w_in, w_out,
           sink_a, qn_c, kn_c, gm_ln_g, gm_ln_b, gm_ws, gm_bs, ffn_w_gate, ffn_w_up, ffn_w_down,
           moe_router, moe_w_gate, moe_w_up, moe_w_down):
    n_batch, seq, d = x.shape
    n_ctx = ctx.shape[1]
    depth = w_mod.shape[0]
    n_lat = n_batch * seq
    n_ctx_rows = n_batch * n_ctx
    t = n_lat + n_ctx_rows
    a_heads = sink_a.shape[1]
    a_w = a_heads * HEAD_DIM
    akv_w = A_KV_HEADS * HEAD_DIM
    b_w = gm_ln_g.shape[1]
    ckv_w = C_KV_HEADS * HEAD_DIM
    c_w = w_in.shape[2] - a_w - 2 * akv_w - 2 * b_w - 2 * ckv_w
    c_heads = c_w // HEAD_DIM
    assert n_ctx_rows % EW_ROWS == 0 and seq % EW_ROWS == 0 and n_batch + 1 <= 8
    assert a_w == c_w and akv_w == COL_TILE and ckv_w == COL_TILE

    x_parts = (x.reshape(n_lat, d), ctx.reshape(n_ctx_rows, d))
    c8 = jnp.concatenate([c, c_ctx[None, :], jnp.zeros((8 - n_batch - 1, d), F32)], axis=0)
    mod_all = _modulation(c8, w_mod, b_mod).reshape(depth, 8, N_MOD, d)

    cos, sin = _rope_tables(seq, n_batch, n_ctx_rows)
    geo = dict(seq=seq, n_lat_rows=n_lat, n_batch=n_batch)
    h = _prenorm(x_parts, g_pre_mix[0], mod_all[0], **geo)

    for l in range(depth):
        need_ctx = l < depth - 1
        n_rows = t if need_ctx else n_lat
        mod = mod_all[l]

        gq, gk = qn_c[l].reshape(1, HEAD_DIM), kn_c[l].reshape(1, HEAD_DIM)
        rope = (cos, sin, gq, gk)
        col = 0
        za = _inproj(h, w_in, l, col, a_w + akv_w, "rope", rope, n_q_tiles=a_w // COL_TILE)
        col += a_w + akv_w
        av = _inproj(h, w_in, l, col, akv_w, "plain")
        col += akv_w
        zb = _inproj(h, w_in, l, col, 2 * b_w, "gelu")
        col += 2 * b_w
        zc = _inproj(h, w_in, l, col, c_w + ckv_w, "normrope", rope, n_q_tiles=c_w // COL_TILE)
        col += c_w + ckv_w
        cv = _inproj(h, w_in, l, col, ckv_w, "plain")

        oa = _window_attn(za, av, sink_a[l], n_batch, seq, n_ctx, need_ctx)
        ob = _gmlp(zb, gm_ln_g[l], gm_ln_b[l], gm_ws[l], gm_bs[l], n_rows)
        oc = _global_attn(zc, cv, n_batch, seq, n_ctx, c_heads, need_ctx)

        y = _outproj(oa, ob, oc, w_out, l, n_rows)
        i = l // 2
        next_norm = (g_pre_mix[l + 1], mod_all[l + 1]) if need_ctx else None
        if l % 2 == 0:
            x1, h2 = _postmix(x_parts, y, g_post_mix[l], g_pre_ffn[l], mod, n_rows, None, **geo)
            f_dim = ffn_w_gate.shape[-1]
            hmid = _ffn_up(h2, ffn_w_gate, ffn_w_up, (i,), n_rows)
            half = f_dim // 2
            f = _ffn_down(hmid, ffn_w_down, (i,), 0, half, n_rows)
            f = _ffn_down(hmid, ffn_w_down, (i,), half, half, n_rows, prev=f)
            xall = _final(x1, f, g_post_ffn[l], mod, n_rows, next_norm, **geo)
            if need_ctx:
                xall, h = xall
        else:
            x1, h2f, route = _postmix(x_parts, y, g_post_mix[l], g_pre_ffn[l], mod, n_rows,
                                      moe_router[i], **geo)
            tok_of_row, live, dest, tiles = _route_plan(route, moe_router.shape[-1])
            xs = _gather_rows(h2f, tok_of_row, live)
            hmid = _moe_up(xs, moe_w_gate, moe_w_up, i, tiles)
            ys = _moe_down(hmid, moe_w_down, i, tiles)
            xall = _moe_final(x1, ys, dest, route, g_post_ffn[l], mod, n_rows, **geo)
            if need_ctx:
                h = _prenorm((xall,), next_norm[0], next_norm[1], **geo)
        x_parts = (xall,)
    return xall[:n_lat].reshape(n_batch, seq, d)
```

```python
import functools
import math

import jax
import jax.numpy as jnp
from jax import lax
from jax.experimental import pallas as pl
from jax.experimental.pallas import tpu as pltpu

F32 = jnp.float32
BF16 = jnp.bfloat16

GRID_W = 64
HEAD_DIM = 128
BLOCK = 128
WINDOW = 128
A_KV_HEADS = 4
C_KV_HEADS = 4
B_GROUPS = 8
CHUNK = 128
N_MOD = 6
TOP_K = 2
ROPE_THETA = 10000.0
EPS = 1e-6
LOG2E = math.log2(math.e)

V7X_LANES = 128
V7X_VMEM_SCOPED_CAP = 60000 * 1024

ROW_TILE = 512
MAX_ROW_TILE = 1088
ROW_ALIGN = 16
COL_TILE = 512
FFN_COL_TILE = 256
MOD_COL_TILE = 1024
EW_ROWS = 256
CAST_ROWS = 256
Q_TILE = 256
K_TILE = 512
GATHER_UNROLL = 8
MOE_ROW_TILE = 1024
MOE_UP_ROW_TILE = 512
MOE_DOWN_COL_TILE = 1024
MOE_ROW_STEP = 128
ROUTER_PIECES = 3
EPILOGUE_CHUNKS = 4
SLAB_ROWS = 8


def _nbytes(shape, dtype):
    return math.prod(shape) * jnp.dtype(dtype).itemsize


def _params(semantics, blocks, scratch=(), temps=()):
    need = 2 * sum(_nbytes(s, d) for s, d in blocks)
    need += sum(_nbytes(s, d) for s, d in scratch)
    need += sum(_nbytes(s, d) for s, d in temps)
    limit = min(V7X_VMEM_SCOPED_CAP, max(need + need // 4, 16 * 1024 * 1024))
    return pltpu.CompilerParams(dimension_semantics=semantics, vmem_limit_bytes=limit)


def _row_tile(n_rows):
    for tm in range(MAX_ROW_TILE - MAX_ROW_TILE % ROW_ALIGN, 0, -ROW_ALIGN):
        if n_rows % tm == 0:
            return tm
    raise ValueError(f"no row tile for {n_rows} rows")


def _cast_weight(w_ref, wbf_ref):
    rows = w_ref.shape[0]
    step = CAST_ROWS if rows % CAST_ROWS == 0 else V7X_LANES
    assert rows % step == 0

    def body(r, carry):
        sl = pl.ds(pl.multiple_of(r * step, step), step)
        wbf_ref[sl, :] = w_ref[sl, :].astype(BF16)
        return carry

    lax.fori_loop(0, rows // step, body, 0)


def _dot(a, b):
    return jnp.dot(a, b, preferred_element_type=F32)


def _dot_nt(a, b):
    return lax.dot_general(a, b, (((1,), (1,)), ((), ())), preferred_element_type=F32)


def _rms(x, gain):
    return x * lax.rsqrt(jnp.mean(x * x, axis=-1, keepdims=True) + EPS) * gain


def _to_slabs(x, slab_ref):
    w = slab_ref.shape[2]
    for s in range(slab_ref.shape[1]):
        slab_ref[:, s, :] = x[:, s * w:(s + 1) * w]


def _from_slabs(slab_ref):
    return jnp.concatenate([slab_ref[:, s, :] for s in range(slab_ref.shape[1])], axis=1)


def _mod_kernel(c_ref, w_ref, b_ref, o_ref):
    s = jax.nn.silu(c_ref[...])
    o_ref[...] = _dot(s.astype(BF16), w_ref[...].astype(BF16)) + b_ref[...]


def _modulation(c8, w_mod, b_mod):
    n_layers, d, width = w_mod.shape
    tn = MOD_COL_TILE if width % MOD_COL_TILE == 0 else COL_TILE
    assert width % tn == 0
    blocks = [((8, d), F32), ((d, tn), F32), ((1, tn), F32), ((8, tn), F32)]
    return pl.pallas_call(
        _mod_kernel,
        out_shape=jax.ShapeDtypeStruct((n_layers, 8, width), F32),
        grid=(n_layers, width // tn),
        in_specs=[pl.BlockSpec((8, d), lambda l, j: (0, 0)),
                  pl.BlockSpec((None, d, tn), lambda l, j: (l, 0, j)),
                  pl.BlockSpec((None, 1, tn), lambda l, j: (l, 0, j))],
        out_specs=pl.BlockSpec((None, 8, tn), lambda l, j: (l, 0, j)),
        compiler_params=_params(("arbitrary", "arbitrary"), blocks, temps=[((d, tn), BF16)]),
        name="modulation",
    )(c8, w_mod, b_mod.reshape(n_layers, 1, width))


def _mod_row_index(i, rows_per_tile, seq, n_lat_rows, n_batch):
    lat_tiles = n_lat_rows // rows_per_tile
    return jnp.where(i < lat_tiles, (i * rows_per_tile) // seq, n_batch)


def _row_parts(parts, tr):
    specs, starts, start = [], [], 0
    for p in parts:
        n = p.shape[0] // tr
        assert n * tr == p.shape[0]
        specs.append(pl.BlockSpec((tr, p.shape[1]),
                                  lambda i, *_, start=start, n=n: (jnp.clip(i - start, 0, n - 1), 0)))
        starts.append(start)
        start += n
    return specs, tuple(starts)


def _pick_rows(i, x_refs, starts):
    x = x_refs[0][...]
    for ref, start in zip(x_refs[1:], starts[1:]):
        x = jnp.where(i >= start, ref[...], x)
    return x


def _pre_mix_norm(x, g_ref, mod_ref):
    return (_rms(x, g_ref[...]) * (1.0 + mod_ref[1:2, :]) + mod_ref[0:1, :]).astype(BF16)


def _prenorm_kernel(*refs, starts):
    *x_refs, g_ref, mod_ref, o_ref = refs
    o_ref[...] = _pre_mix_norm(_pick_rows(pl.program_id(0), x_refs, starts), g_ref, mod_ref)


def _prenorm(x_parts, gain, mod, seq, n_lat_rows, n_batch):
    d = x_parts[0].shape[1]
    t = sum(p.shape[0] for p in x_parts)
    tr = EW_ROWS
    ridx = functools.partial(_mod_row_index, rows_per_tile=tr, seq=seq, n_lat_rows=n_lat_rows,
                             n_batch=n_batch)
    x_specs, starts = _row_parts(x_parts, tr)
    blocks = [((tr, d), F32)] * len(x_parts) + [((1, d), F32), ((N_MOD, d), F32), ((tr, d), BF16)]
    return pl.pallas_call(
        functools.partial(_prenorm_kernel, starts=starts),
        out_shape=jax.ShapeDtypeStruct((t, d), BF16),
        grid=(t // tr,),
        in_specs=x_specs + [pl.BlockSpec((1, d), lambda i: (0, 0)),
                            pl.BlockSpec((None, N_MOD, d), lambda i: (ridx(i), 0, 0))],
        out_specs=pl.BlockSpec((tr, d), lambda i: (i, 0)),
        compiler_params=_params(("arbitrary",), blocks, temps=[((tr, d), F32)] * 2),
        name="prenorm",
    )(*x_parts, gain.reshape(1, d), mod)


def _postmix_kernel(*refs, starts, n_parts, n_experts):
    x_refs = refs[:n_parts]
    y_ref, gpost_ref, gpre_ref, mod_ref, *rest = refs[n_parts:]
    x = _pick_rows(pl.program_id(0), x_refs, starts)
    x1 = x + mod_ref[2:3, :] * _rms(y_ref[...].astype(F32), gpost_ref[...])
    h2 = _rms(x1, gpre_ref[...]) * (1.0 + mod_ref[4:5, :]) + mod_ref[3:4, :]
    if n_experts is None:
        x1_ref, h2_ref = rest
        h2_ref[...] = h2.astype(BF16)
    else:
        wr_ref, x1_ref, slab_ref, route_ref = rest
        _to_slabs(h2, slab_ref)
        route_ref[...] = _route(h2.astype(BF16), wr_ref[...], n_experts)
    x1_ref[...] = x1


def _postmix(x_parts, y, g_post, g_pre, mod, n_rows, w_router, seq, n_lat_rows, n_batch):
    d = x_parts[0].shape[1]
    tr = EW_ROWS
    x_specs, starts = _row_parts(x_parts, tr)
    ridx = functools.partial(_mod_row_index, rows_per_tile=tr, seq=seq, n_lat_rows=n_lat_rows,
                             n_batch=n_batch)
    row = pl.BlockSpec((tr, d), lambda i: (i, 0))
    vec = pl.BlockSpec((1, d), lambda i: (0, 0))
    in_specs = x_specs + [row, vec, vec, pl.BlockSpec((None, N_MOD, d), lambda i: (ridx(i), 0, 0))]
    args = [*x_parts, y, g_post.reshape(1, d), g_pre.reshape(1, d), mod]
    blocks = ([((tr, d), F32)] * (len(x_parts) + 2) + [((tr, d), BF16)] * 2 + [((N_MOD, d), F32)])
    if w_router is None:
        n_experts = None
        out_shape = (jax.ShapeDtypeStruct((n_rows, d), F32), jax.ShapeDtypeStruct((n_rows, d), BF16))
        out_specs = (row, row)
    else:
        n_experts = w_router.shape[1]
        assert d % (SLAB_ROWS * V7X_LANES) == 0
        slab = (SLAB_ROWS, d // SLAB_ROWS)
        in_specs.append(pl.BlockSpec((d, V7X_LANES), lambda i: (0, 0)))
        args.append(_router_pieces(w_router))
        out_shape = (jax.ShapeDtypeStruct((n_rows, d), F32),
                     jax.ShapeDtypeStruct((n_rows,) + slab, F32),
                     jax.ShapeDtypeStruct((n_rows, V7X_LANES), F32))
        out_specs = (row, pl.BlockSpec((tr,) + slab, lambda i: (i, 0, 0)),
                     pl.BlockSpec((tr, V7X_LANES), lambda i: (i, 0)))
        blocks += [((d, V7X_LANES), BF16), ((tr, V7X_LANES), F32)]
    return pl.pallas_call(
        functools.partial(_postmix_kernel, starts=starts, n_parts=len(x_parts),
                          n_experts=n_experts),
        out_shape=out_shape,
        grid=(n_rows // tr,),
        in_specs=in_specs,
        out_specs=out_specs,
        compiler_params=_params(("arbitrary",), blocks, temps=[((tr, d), F32)] * 3),
        name="postmix" if w_router is None else "postmix_route",
    )(*args)


def _final_kernel(x_ref, f_ref, g_ref, mod_ref, *rest):
    x = x_ref[...] + mod_ref[5:6, :] * _rms(f_ref[...], g_ref[...])
    if len(rest) == 1:
        (o_ref,) = rest
    else:
        gnext_ref, modnext_ref, o_ref, h_ref = rest
        h_ref[...] = _pre_mix_norm(x, gnext_ref, modnext_ref)
    o_ref[...] = x


def _final(x1, f, g_post, mod, n_rows, next_norm, seq, n_lat_rows, n_batch):
    d = x1.shape[1]
    tr = EW_ROWS
    ridx = functools.partial(_mod_row_index, rows_per_tile=tr, seq=seq, n_lat_rows=n_lat_rows,
                             n_batch=n_batch)
    row = pl.BlockSpec((tr, d), lambda i: (i, 0))
    vec = pl.BlockSpec((1, d), lambda i: (0, 0))
    modspec = pl.BlockSpec((None, N_MOD, d), lambda i: (ridx(i), 0, 0))
    blocks = [((tr, d), F32)] * 3 + [((N_MOD, d), F32)]
    in_specs, args = [row, row, vec, modspec], [x1, f, g_post.reshape(1, d), mod]
    out_shape, out_specs = jax.ShapeDtypeStruct((n_rows, d), F32), row
    if next_norm is not None:
        in_specs += [vec, modspec]
        args += [next_norm[0].reshape(1, d), next_norm[1]]
        out_shape = (out_shape, jax.ShapeDtypeStruct((n_rows, d), BF16))
        out_specs = (row, row)
        blocks += [((N_MOD, d), F32), ((tr, d), BF16)]
    return pl.pallas_call(
        _final_kernel,
        out_shape=out_shape,
        grid=(n_rows // tr,),
        in_specs=in_specs,
        out_specs=out_specs,
        compiler_params=_params(("arbitrary",), blocks, temps=[((tr, d), F32)] * 2),
        name="final_residual",
    )(*args)


def _swap_pairs(x):
    lane = lax.broadcasted_iota(jnp.int32, x.shape, 1)
    quarter = HEAD_DIM // 4
    first = (lane & quarter) == 0
    return jnp.where(first, pltpu.roll(x, HEAD_DIM - quarter, axis=1), pltpu.roll(x, quarter, axis=1))


def _inproj_kernel(*refs, mode, n_q_tiles, q_scale):
    if mode in ("rope", "normrope"):
        h_ref, w_ref, cos_ref, sin_ref, gq_ref, gk_ref, o_ref, wbf_ref = refs
    else:
        h_ref, w_ref, o_ref, wbf_ref = refs
    j = pl.program_id(0)

    @pl.when(pl.program_id(1) == 0)
    def _():
        _cast_weight(w_ref, wbf_ref)

    tm = h_ref.shape[0]
    n_chunks = EPILOGUE_CHUNKS if tm % (EPILOGUE_CHUNKS * ROW_ALIGN) == 0 else 1
    rc = tm // n_chunks
    for c in range(n_chunks):
        rows = slice(c * rc, (c + 1) * rc)
        acc = _dot(h_ref[rows, :], wbf_ref[...])
        if mode == "plain":
            o_ref[rows, :] = acc.astype(BF16)
        elif mode == "gelu":
            o_ref[rows, :] = (0.5 * acc * (1.0 + lax.erf(acc * (2.0 ** -0.5)))).astype(BF16)
        else:
            is_q = j < n_q_tiles
            scale = jnp.where(is_q, q_scale, 1.0).astype(F32)
            cos = cos_ref[rows, :]
            sin = sin_ref[rows, :]
            gain = jnp.where(is_q, gq_ref[...], gk_ref[...])
            for hh in range(acc.shape[1] // HEAD_DIM):
                sl = slice(hh * HEAD_DIM, (hh + 1) * HEAD_DIM)
                xh = acc[:, sl]
                if mode == "normrope":
                    xh = _rms(xh, gain)
                xh = xh * cos + _swap_pairs(xh) * sin
                o_ref[rows, sl] = (xh * scale).astype(BF16)


def _inproj(h, w_in, layer, col_start, width, mode, rope=None, n_q_tiles=0):
    t, d = h.shape
    tm, tn = _row_tile(t), COL_TILE
    assert col_start % tn == 0 and width % tn == 0
    j0 = col_start // tn
    in_specs = [pl.BlockSpec((tm, d), lambda j, i: (i, 0)),
                pl.BlockSpec((None, d, tn), lambda j, i: (layer, 0, j0 + j))]
    args = [h, w_in]
    blocks = [((tm, d), BF16), ((d, tn), F32), ((tm, tn), BF16)]
    if mode in ("rope", "normrope"):
        cos, sin, gq, gk = rope
        tab = pl.BlockSpec((tm, HEAD_DIM), lambda j, i: (i, 0))
        vec = pl.BlockSpec((1, HEAD_DIM), lambda j, i: (0, 0))
        in_specs += [tab, tab, vec, vec]
        args += [cos, sin, gq, gk]
        blocks += [((tm, HEAD_DIM), F32)] * 2
    kern = functools.partial(_inproj_kernel, mode=mode, n_q_tiles=n_q_tiles,
                             q_scale=HEAD_DIM ** -0.5 * LOG2E)
    return pl.pallas_call(
        kern,
        out_shape=jax.ShapeDtypeStruct((t, width), BF16),
        grid=(width // tn, t // tm),
        in_specs=in_specs,
        out_specs=pl.BlockSpec((tm, tn), lambda j, i: (i, j)),
        scratch_shapes=[pltpu.VMEM((d, tn), BF16)],
        compiler_params=_params(("arbitrary", "arbitrary"), blocks, scratch=[((d, tn), BF16)],
                                temps=[((tm, tn), F32)] * 2),
        name="inproj_" + mode,
    )(*args)


def _window_bias(group, n_ctx):
    r = jnp.arange(group * BLOCK)[:, None] % BLOCK
    c = jnp.arange(3 * BLOCK + n_ctx)[None, :]
    in_prev, in_cur = c < BLOCK, (c >= BLOCK) & (c < 2 * BLOCK)
    in_next = (c >= 2 * BLOCK) & (c < 3 * BLOCK)
    band_prev = in_prev & (c < r)
    band_next = in_next & (c - 2 * BLOCK > r)
    hidden = [band_prev | band_next, in_prev | band_next, band_prev | in_next, in_prev | in_next,
              in_prev | in_cur | in_next]
    shape = (group * BLOCK, 3 * BLOCK + n_ctx)
    return jnp.stack([jnp.where(jnp.broadcast_to(h, shape), -jnp.inf, 0.0).astype(F32)
                      for h in hidden])


def _window_attn_kernel(sink_ref, bias_ref, q_ref, kp_ref, kc_ref, kn_ref, kx_ref, vp_ref, vc_ref,
                        vn_ref, vx_ref, o_ref, *, group):
    rows = group * BLOCK
    assert BLOCK & (BLOCK - 1) == 0
    shift = BLOCK.bit_length() - 1
    bias = bias_ref[...]
    rid = lax.broadcasted_iota(jnp.int32, (rows, 1), 0) >> shift
    n_keys = bias.shape[1]
    ones = jnp.ones((n_keys, HEAD_DIM), BF16)
    for hk in range(A_KV_HEADS):
        ksl = slice(hk * HEAD_DIM, (hk + 1) * HEAD_DIM)
        k_all = jnp.concatenate([kp_ref[:, ksl], kc_ref[:, ksl], kn_ref[:, ksl], kx_ref[:, ksl]],
                                axis=0)
        v_all = jnp.concatenate([vp_ref[:, ksl], vc_ref[:, ksl], vn_ref[:, ksl], vx_ref[:, ksl]],
                                axis=0)
        q3 = jnp.concatenate(
            [q_ref[:, (hk * group + g) * HEAD_DIM:(hk * group + g + 1) * HEAD_DIM]
             for g in range(group)], axis=0)
        sink = jnp.zeros((rows, 1), F32)
        for g in range(group):
            sink = jnp.where(rid == g, sink_ref[hk * group + g] * LOG2E, sink)
        s = _dot_nt(q3, k_all) + bias
        m = jnp.maximum(jnp.max(s, axis=-1, keepdims=True), sink)
        p = jnp.exp2(s - m)
        o_ext = _dot(p.astype(BF16), jnp.concatenate([v_all, ones], axis=1))
        o = o_ext[:, :HEAD_DIM] / (o_ext[:, HEAD_DIM:] + jnp.exp2(sink - m))
        for g in range(group):
            osl = slice((hk * group + g) * HEAD_DIM, (hk * group + g + 1) * HEAD_DIM)
            o_ref[:, osl] = o[g * BLOCK:(g + 1) * BLOCK, :].astype(BF16)


def _window_attn(za, av, sink, n_batch, seq, n_ctx, with_ctx):
    n_heads = sink.shape[0]
    group = n_heads // A_KV_HEADS
    qw = n_heads * HEAD_DIM
    kw = A_KV_HEADS * HEAD_DIM
    assert qw % kw == 0 and seq % BLOCK == 0 and (n_batch * seq) % n_ctx == 0 and WINDOW == BLOCK
    assert n_ctx % BLOCK == 0
    nb = seq // BLOCK
    ncb = n_ctx // BLOCK if with_ctx else 0
    kcol = qw // kw
    ctx0 = (n_batch * seq) // n_ctx

    def qblk(b, n):
        return (jnp.where(n < nb, b * nb + n, n_batch * nb + b * ncb + n - nb), 0)

    def blk(shift):
        return lambda b, n: (b * nb + jnp.clip(n + shift, 0, nb - 1), kcol)

    def vblk(shift):
        return lambda b, n: (b * nb + jnp.clip(n + shift, 0, nb - 1), 0)

    def bias_variant(b, n):
        lat = jnp.where(n == 0, 1, 0) + jnp.where(n == nb - 1, 2, 0)
        return (jnp.where(n < nb, lat, 4), 0, 0)

    kspec = [pl.BlockSpec((BLOCK, kw), blk(s)) for s in (-1, 0, 1)]
    vspec = [pl.BlockSpec((BLOCK, kw), vblk(s)) for s in (-1, 0, 1)]
    n_keys = 3 * BLOCK + n_ctx
    rows = group * BLOCK
    blocks = ([((BLOCK, qw), BF16)] * 2 + [((BLOCK, kw), BF16)] * 6 + [((n_ctx, kw), BF16)] * 2
              + [((rows, n_keys), F32)])
    return pl.pallas_call(
        functools.partial(_window_attn_kernel, group=group),
        out_shape=jax.ShapeDtypeStruct((n_batch * (nb + ncb) * BLOCK, qw), BF16),
        grid=(n_batch, nb + ncb),
        in_specs=[pl.BlockSpec(memory_space=pltpu.SMEM),
                  pl.BlockSpec((None, rows, n_keys), bias_variant),
                  pl.BlockSpec((BLOCK, qw), qblk)]
                 + kspec + [pl.BlockSpec((n_ctx, kw), lambda b, n: (ctx0 + b, kcol))]
                 + vspec + [pl.BlockSpec((n_ctx, kw), lambda b, n: (ctx0 + b, 0))],
        out_specs=pl.BlockSpec((BLOCK, qw), qblk),
        compiler_params=_params(("arbitrary", "arbitrary"), blocks,
                                temps=[((rows, n_keys), F32)] * 4),
        name="window_attn",
    )(sink, _window_bias(group, n_ctx), za, za, za, za, za, av, av, av, av)


def _gmlp_kernel(z_ref, g_ref, b_ref, ws_ref, bst_ref, o_ref):
    width = g_ref.shape[1]
    u = z_ref[:, :width].astype(F32)
    v = z_ref[:, width:].astype(F32)
    mu = jnp.mean(v, axis=-1, keepdims=True)
    vc = v - mu
    var = jnp.mean(vc * vc, axis=-1, keepdims=True)
    vn = (vc * lax.rsqrt(var + EPS) * g_ref[...] + b_ref[...]).astype(BF16)
    gd = width // B_GROUPS
    for g in range(B_GROUPS):
        sl = slice(g * gd, (g + 1) * gd)
        mixed = _dot(ws_ref[g].astype(BF16), vn[:, sl]) + bst_ref[:, g:g + 1]
        o_ref[:, sl] = (u[:, sl] * mixed).astype(BF16)


def _gmlp(zb, ln_g, ln_b, ws, bs, n_rows):
    w2 = zb.shape[1]
    width = w2 // 2
    blocks = [((CHUNK, w2), BF16), ((1, width), F32), ((1, width), F32),
              (ws.shape, F32), ((CHUNK, B_GROUPS), F32), ((CHUNK, width), BF16)]
    return pl.pallas_call(
        _gmlp_kernel,
        out_shape=jax.ShapeDtypeStruct((n_rows, width), BF16),
        grid=(n_rows // CHUNK,),
        in_specs=[pl.BlockSpec((CHUNK, w2), lambda i: (i, 0)),
                  pl.BlockSpec((1, width), lambda i: (0, 0)),
                  pl.BlockSpec((1, width), lambda i: (0, 0)),
                  pl.BlockSpec(ws.shape, lambda i: (0, 0, 0)),
                  pl.BlockSpec((CHUNK, B_GROUPS), lambda i: (0, 0))],
        out_specs=pl.BlockSpec((CHUNK, width), lambda i: (i, 0)),
        compiler_params=_params(("arbitrary",), blocks, temps=[((CHUNK, w2), F32)] * 2),
        name="chunk_gmlp",
    )(zb, ln_g.reshape(1, width), ln_b.reshape(1, width), ws, bs.T)


def _global_attn_kernel(q_ref, kx_ref, vx_ref, k_ref, v_ref, o_ref, vext_ref, *, group,
                        n_lat_tiles, has_ctx_tiles):
    tq = q_ref.shape[0]
    n_ctx = kx_ref.shape[0]
    n_lat = k_ref.shape[0]
    rows = group * tq

    @pl.when(pl.program_id(2) == 0)
    def _():
        vext_ref[0:n_ctx, 0:HEAD_DIM] = vx_ref[...]
        vext_ref[n_ctx:, 0:HEAD_DIM] = v_ref[...]
        vext_ref[:, HEAD_DIM:] = jnp.ones((n_ctx + n_lat, HEAD_DIM), BF16)

    def attend(chunks):
        q3 = jnp.concatenate([q_ref[:, g * HEAD_DIM:(g + 1) * HEAD_DIM] for g in range(group)],
                             axis=0)
        m = jnp.full((rows, HEAD_DIM), -jnp.inf, F32)
        acc = jnp.zeros((rows, 2 * HEAD_DIM), F32)
        for k_chunk, v0, nk in chunks:
            s = _dot_nt(q3, k_chunk())
            m_new = jnp.maximum(m, jnp.broadcast_to(jnp.max(s, axis=-1, keepdims=True), m.shape))
            alpha = jnp.exp2(m - m_new)
            p = jnp.exp2(s - jnp.tile(m_new, (1, nk // HEAD_DIM)))
            acc = jnp.tile(alpha, (1, 2)) * acc + _dot(p.astype(BF16), vext_ref[v0:v0 + nk, :])
            m = m_new
        o = acc[:, :HEAD_DIM] / acc[:, HEAD_DIM:]
        for g in range(group):
            o_ref[:, g * HEAD_DIM:(g + 1) * HEAD_DIM] = o[g * tq:(g + 1) * tq, :].astype(BF16)

    ctx_chunk = [(lambda: kx_ref[...], 0, n_ctx)]
    lat_chunks = [(lambda c=c: k_ref[c:c + K_TILE, :], n_ctx + c, K_TILE)
                  for c in range(0, n_lat, K_TILE)]
    is_lat = pl.program_id(2) < n_lat_tiles
    pl.when(is_lat)(lambda: attend(ctx_chunk + lat_chunks))
    if has_ctx_tiles:
        pl.when(jnp.logical_not(is_lat))(lambda: attend(ctx_chunk))


def _global_attn(zc, cv, n_batch, seq, n_ctx, n_heads, with_ctx):
    group = n_heads // C_KV_HEADS
    gw = group * HEAD_DIM
    tq = Q_TILE
    assert seq % tq == 0 and seq % K_TILE == 0 and (n_batch * seq) % n_ctx == 0 and n_ctx % tq == 0
    nq = seq // tq
    ncq = n_ctx // tq if with_ctx else 0
    ctx0 = (n_batch * seq) // n_ctx
    rows = group * tq

    def qblk(b, hk, i):
        return (jnp.where(i < nq, b * nq + i, n_batch * nq + b * ncq + i - nq), hk)

    blocks = [((tq, gw), BF16)] * 2 + [((n_ctx, HEAD_DIM), BF16)] * 2 + [((seq, HEAD_DIM), BF16)] * 2
    scratch = [((n_ctx + seq, 2 * HEAD_DIM), BF16)]
    return pl.pallas_call(
        functools.partial(_global_attn_kernel, group=group, n_lat_tiles=nq, has_ctx_tiles=ncq > 0),
        out_shape=jax.ShapeDtypeStruct((n_batch * (nq + ncq) * tq, n_heads * HEAD_DIM), BF16),
        grid=(n_batch, C_KV_HEADS, nq + ncq),
        in_specs=[pl.BlockSpec((tq, gw), qblk),
                  pl.BlockSpec((n_ctx, HEAD_DIM), lambda b, hk, i: (ctx0 + b, n_heads + hk)),
                  pl.BlockSpec((n_ctx, HEAD_DIM), lambda b, hk, i: (ctx0 + b, hk)),
                  pl.BlockSpec((seq, HEAD_DIM), lambda b, hk, i: (b, n_heads + hk)),
                  pl.BlockSpec((seq, HEAD_DIM), lambda b, hk, i: (b, hk))],
        out_specs=pl.BlockSpec((tq, gw), qblk),
        scratch_shapes=[pltpu.VMEM(s, d) for s, d in scratch],
        compiler_params=_params(("arbitrary", "arbitrary", "arbitrary"), blocks, scratch=scratch,
                                temps=[((rows, K_TILE), F32)] * 6),
        name="global_attn",
    )(zc, zc, cv, zc, cv)


def _outproj_kernel(a_ref, b_ref, c_ref, w_ref, o_ref, wbf_ref):
    @pl.when(pl.program_id(1) == 0)
    def _():
        _cast_weight(w_ref, wbf_ref)

    ka, kb = a_ref.shape[1], b_ref.shape[1]
    acc = _dot(a_ref[...], wbf_ref[0:ka, :])
    acc += _dot(b_ref[...], wbf_ref[ka:ka + kb, :])
    acc += _dot(c_ref[...], wbf_ref[ka + kb:, :])
    o_ref[...] = acc.astype(BF16)


def _outproj(oa, ob, oc, w_out, layer, n_rows):
    t = oa.shape[0]
    _, k, d = w_out.shape
    tm, tn = _row_tile(n_rows), COL_TILE
    assert oa.shape[1] + ob.shape[1] + oc.shape[1] == k and d % tn == 0
    blocks = [((tm, k), BF16), ((k, tn), F32), ((tm, tn), BF16)]
    return pl.pallas_call(
        _outproj_kernel,
        out_shape=jax.ShapeDtypeStruct((t, d), BF16),
        grid=(d // tn, n_rows // tm),
        in_specs=[pl.BlockSpec((tm, oa.shape[1]), lambda j, i: (i, 0)),
                  pl.BlockSpec((tm, ob.shape[1]), lambda j, i: (i, 0)),
                  pl.BlockSpec((tm, oc.shape[1]), lambda j, i: (i, 0)),
                  pl.BlockSpec((None, k, tn), lambda j, i: (layer, 0, j))],
        out_specs=pl.BlockSpec((tm, tn), lambda j, i: (i, j)),
        scratch_shapes=[pltpu.VMEM((k, tn), BF16)],
        compiler_params=_params(("arbitrary", "arbitrary"), blocks, scratch=[((k, tn), BF16)],
                                temps=[((tm, tn), F32)] * 2),
        name="outproj",
    )(oa, ob, oc, w_out)


def _ffn_up_kernel(h_ref, wg_ref, wu_ref, o_ref, wgbf_ref, wubf_ref):
    @pl.when(pl.program_id(1) == 0)
    def _():
        _cast_weight(wg_ref, wgbf_ref)
        _cast_weight(wu_ref, wubf_ref)

    h = h_ref[...]
    gate = _dot(h, wgbf_ref[...])
    up = _dot(h, wubf_ref[...])
    o_ref[...] = (jax.nn.silu(gate) * up).astype(BF16)


def _ffn_up(h2, w_gate, w_up, sel, n_rows):
    t, d = h2.shape
    f = w_gate.shape[-1]
    tm, tn = _row_tile(n_rows), FFN_COL_TILE
    assert f % tn == 0
    lead = (None,) * len(sel)
    wspec = pl.BlockSpec(lead + (d, tn), lambda j, i: sel + (0, j))
    blocks = [((tm, d), BF16), ((d, tn), F32), ((d, tn), F32), ((tm, tn), BF16)]
    return pl.pallas_call(
        _ffn_up_kernel,
        out_shape=jax.ShapeDtypeStruct((t, f), BF16),
        grid=(f // tn, n_rows // tm),
        in_specs=[pl.BlockSpec((tm, d), lambda j, i: (i, 0)), wspec, wspec],
        out_specs=pl.BlockSpec((tm, tn), lambda j, i: (i, j)),
        scratch_shapes=[pltpu.VMEM((d, tn), BF16)] * 2,
        compiler_params=_params(("arbitrary", "arbitrary"), blocks, scratch=[((d, tn), BF16)] * 2,
                                temps=[((tm, tn), F32)] * 3),
        name="ffn_up",
    )(h2, w_gate, w_up)


def _ffn_down_kernel(*refs, has_prev):
    if has_prev:
        a_ref, w_ref, prev_ref, o_ref, wbf_ref = refs
    else:
        a_ref, w_ref, o_ref, wbf_ref = refs

    @pl.when(pl.program_id(1) == 0)
    def _():
        _cast_weight(w_ref, wbf_ref)

    acc = _dot(a_ref[...], wbf_ref[...])
    if has_prev:
        acc = prev_ref[...] + acc
    o_ref[...] = acc


def _ffn_down(a, w_down, sel, k_start, k_size, n_rows, prev=None):
    t = a.shape[0]
    d = w_down.shape[-1]
    tm, tn = ROW_TILE, COL_TILE
    assert k_start % k_size == 0 and d % tn == 0 and n_rows % tm == 0
    kb = k_start // k_size
    lead = (None,) * len(sel)
    in_specs = [pl.BlockSpec((tm, k_size), lambda j, i: (i, kb)),
                pl.BlockSpec(lead + (k_size, tn), lambda j, i: sel + (kb, j))]
    args = [a, w_down]
    blocks = [((tm, k_size), BF16), ((k_size, tn), F32), ((tm, tn), F32)]
    aliases = {}
    if prev is not None:
        in_specs.append(pl.BlockSpec((tm, tn), lambda j, i: (i, j)))
        aliases = {len(args): 0}
        args.append(prev)
        blocks.append(((tm, tn), F32))
    return pl.pallas_call(
        functools.partial(_ffn_down_kernel, has_prev=prev is not None),
        out_shape=jax.ShapeDtypeStruct((t, d), F32),
        grid=(d // tn, n_rows // tm),
        in_specs=in_specs,
        out_specs=pl.BlockSpec((tm, tn), lambda j, i: (i, j)),
        scratch_shapes=[pltpu.VMEM((k_size, tn), BF16)],
        input_output_aliases=aliases,
        compiler_params=_params(("arbitrary", "arbitrary"), blocks, scratch=[((k_size, tn), BF16)],
                                temps=[((tm, tn), F32)] * 2),
        name="ffn_down",
    )(*args)


def _route(h, w_pieces, n_experts):
    pieces = _dot(h, w_pieces)
    logits = pieces
    for k in range(1, ROUTER_PIECES):
        logits = logits + pltpu.roll(pieces, V7X_LANES - k * n_experts, axis=1)
    lane = lax.broadcasted_iota(jnp.int32, logits.shape, 1).astype(F32)
    neg = -jnp.inf
    logits = jnp.where(lane < n_experts, logits, neg)
    picked = []
    remaining = logits
    for _ in range(TOP_K):
        top = jnp.max(remaining, axis=-1, keepdims=True)
        idx = jnp.min(jnp.where(remaining == top, lane, float(logits.shape[1])), axis=-1,
                      keepdims=True)
        picked.append((top, idx))
        remaining = jnp.where(lane == idx, neg, remaining)
    top0 = picked[0][0]
    denom = sum(jnp.exp(tv - top0) for tv, _ in picked)
    route = jnp.zeros(logits.shape, F32)
    for k, (tv, idx) in enumerate(picked):
        route = jnp.where(lane == k, idx, route)
        route = jnp.where(lane == TOP_K + k, jnp.exp(tv - top0) / denom, route)
    return route


def _router_pieces(w_router):
    n_experts = w_router.shape[1]
    assert ROUTER_PIECES * n_experts <= V7X_LANES
    pieces, rest = [], w_router
    for _ in range(ROUTER_PIECES):
        pieces.append(rest.astype(BF16))
        rest = rest - pieces[-1].astype(F32)
    return jnp.pad(jnp.concatenate(pieces, axis=1),
                   ((0, 0), (0, V7X_LANES - ROUTER_PIECES * n_experts)))


def _route_plan(route, n_experts):
    tile = MOE_ROW_TILE
    n_tok = route.shape[0]
    n_asg = n_tok * TOP_K
    e_flat = route[:, :TOP_K].astype(jnp.int32).reshape(n_asg)
    onehot = (e_flat[:, None] == jnp.arange(n_experts, dtype=jnp.int32)[None, :]).astype(jnp.int32)
    csum = jnp.cumsum(onehot, axis=0)
    rank = jnp.take_along_axis(csum, e_flat[:, None], axis=1)[:, 0] - 1
    counts = csum[-1]
    tiles_per = (counts + tile - 1) // tile
    tile_end = jnp.cumsum(tiles_per)
    tile_start = tile_end - tiles_per
    dest = tile_start[e_flat] * tile + rank

    n_tiles = (n_asg + n_experts * (tile - 1)) // tile
    tile_ids = jnp.arange(n_tiles, dtype=jnp.int32)
    n_used = tile_end[-1]
    valid = tile_ids < n_used
    expert_raw = jnp.sum((tile_end[None, :] <= tile_ids[:, None]).astype(jnp.int32), axis=1)
    tile_expert = jnp.where(valid, expert_raw, expert_raw[n_used - 1])
    tile_first = valid & (tile_ids == tile_start[tile_expert])
    tile_row = jnp.where(valid, tile_ids, n_used - 1)

    tok_of_row = jnp.zeros((n_tiles * tile,), jnp.int32).at[dest].set(
        jnp.arange(n_asg, dtype=jnp.int32) // TOP_K, unique_indices=True)
    tile_first = tile_first.astype(jnp.int32)
    tile_slot = (jnp.cumsum(tile_first) - 1) % 2
    tile_fill = jnp.where(
        valid, jnp.clip(counts[tile_expert] - (tile_ids - tile_start[tile_expert]) * tile, 0, tile), 0)
    tiles = (tile_expert, tile_first, tile_fill, tile_row, tile_slot)
    assert tile % EW_ROWS == 0
    block_start = jnp.arange(n_tiles * tile // EW_ROWS, dtype=jnp.int32) * EW_ROWS
    live = (tile_fill[block_start // tile] > block_start % tile).astype(jnp.int32)
    return tok_of_row, live, dest, tiles


def _split_tiles(tiles, factor):
    tile_expert, tile_first, tile_fill, _, tile_slot = tiles
    n = tile_expert.shape[0] * factor
    sub = MOE_ROW_TILE // factor
    part = jnp.arange(n, dtype=jnp.int32) % factor
    fill = jnp.clip(jnp.repeat(tile_fill, factor) - part * sub, 0, sub)
    first = jnp.where(part == 0, jnp.repeat(tile_first, factor), 0)
    row = lax.cummax(jnp.where(fill > 0, jnp.arange(n, dtype=jnp.int32), 0))
    return jnp.repeat(tile_expert, factor), first, fill, row, jnp.repeat(tile_slot, factor)


def _row_copy(src_hbm, dst_vmem, sem, src_row, dst_row):
    return pltpu.make_async_copy(src_hbm.at[pl.ds(src_row, 1)], dst_vmem.at[pl.ds(dst_row, 1)], sem)


def _gather_rows_kernel(idx_ref, live_ref, src_ref, o_ref, buf_ref, sems):
    tile = buf_ref.shape[1]
    n_blocks = pl.num_programs(0)
    g = pl.program_id(0)

    def start_block(b):
        def issue(r, carry):
            _row_copy(src_ref, buf_ref.at[b % 2], sems.at[b % 2], idx_ref[b * tile + r], r).start()
            return carry
        lax.fori_loop(0, tile, issue, 0, unroll=GATHER_UNROLL)

    @pl.when((g == 0) & (live_ref[0] == 1))
    def _():
        start_block(g)

    nxt = jnp.minimum(g + 1, n_blocks - 1)

    @pl.when((g + 1 < n_blocks) & (live_ref[nxt] == 1))
    def _():
        start_block(nxt)

    @pl.when(live_ref[g] == 1)
    def _():
        def drain(r, carry):
            _row_copy(src_ref, buf_ref.at[g % 2], sems.at[g % 2], 0, r).wait()
            return carry
        lax.fori_loop(0, tile, drain, 0, unroll=GATHER_UNROLL)
        o_ref[...] = _from_slabs(buf_ref.at[g % 2]).astype(BF16)

    @pl.when(live_ref[g] == 0)
    def _():
        o_ref[...] = jnp.zeros(o_ref.shape, BF16)


def _gather_rows(src, idx, live):
    slab = src.shape[1:]
    d = slab[0] * slab[1]
    n = idx.shape[0]
    tg = EW_ROWS
    assert n % tg == 0 and live.shape[0] == n // tg
    blocks = [((tg, d), BF16)]
    return pl.pallas_call(
        _gather_rows_kernel,
        out_shape=jax.ShapeDtypeStruct((n, d), BF16),
        grid_spec=pltpu.PrefetchScalarGridSpec(
            num_scalar_prefetch=2,
            grid=(n // tg,),
            in_specs=[pl.BlockSpec(memory_space=pl.ANY)],
            out_specs=pl.BlockSpec((tg, d), lambda i, idx_ref, live_ref: (i, 0)),
            scratch_shapes=[pltpu.VMEM((2, tg) + slab, F32), pltpu.SemaphoreType.DMA((2,))]),
        compiler_params=_params(("arbitrary",), blocks, scratch=[((2, tg, d), F32)],
                                temps=[((tg, d), F32)]),
        name="moe_dispatch",
    )(idx, live, src)


def _held_tile(s, n_tiles):
    return jnp.minimum(s, n_tiles - 1)


def _done_tile(s):
    return jnp.maximum(s - 1, 0)


def _grouped_step(tables, w_refs, wbf_refs, compute, o_ref):
    _, tf_ref, tn_ref, _, ts_ref = tables
    n_tiles = tf_ref.shape[0]
    n_slots = wbf_refs[0].shape[0]
    tm = o_ref.shape[0]
    s = pl.program_id(1)
    held = _held_tile(s, n_tiles)
    done = _done_tile(s)
    fill = jnp.where(s > 0, tn_ref[done], -MOE_ROW_STEP)

    def slot(tile):
        return ts_ref[tile] % n_slots if n_slots > 1 else 0

    for n in range(0, tm + 1, MOE_ROW_STEP):
        @pl.when((fill > n - MOE_ROW_STEP) & (fill <= n))
        def _(n=n):
            if n > 0:
                o_ref[0:n, :] = compute([wbf_ref[slot(done)] for wbf_ref in wbf_refs], n)
            if n < tm:
                o_ref[n:, :] = jnp.zeros((tm - n, o_ref.shape[1]), o_ref.dtype)

    @pl.when((s < n_tiles) & (tf_ref[held] == 1))
    def _():
        for w_ref, wbf_ref in zip(w_refs, wbf_refs):
            _cast_weight(w_ref, wbf_ref.at[slot(held)])


def _moe_up_kernel(te_ref, tf_ref, tn_ref, tr_ref, ts_ref, x_ref, wg_ref, wu_ref, o_ref, wgbf_ref,
                   wubf_ref):
    def compute(w, n):
        x = x_ref[0:n, :]
        return (jax.nn.silu(_dot(x, w[0])) * _dot(x, w[1])).astype(BF16)

    _grouped_step((te_ref, tf_ref, tn_ref, tr_ref, ts_ref), (wg_ref, wu_ref), (wgbf_ref, wubf_ref),
                  compute, o_ref)


def _moe_up(xs, w_gate, w_up, layer, tiles):
    r, d = xs.shape
    f = w_gate.shape[-1]
    tm, tn = MOE_UP_ROW_TILE, COL_TILE
    assert f % tn == 0 and r % tm == 0
    tiles = _split_tiles(tiles, MOE_ROW_TILE // tm)
    nt = r // tm
    wspec = pl.BlockSpec((None, None, d, tn),
                         lambda j, s, te, tf, tv, tr, ts: (layer, te[_held_tile(s, nt)], 0, j))
    blocks = [((tm, d), BF16), ((d, tn), F32), ((d, tn), F32), ((tm, tn), BF16)]
    return pl.pallas_call(
        _moe_up_kernel,
        out_shape=jax.ShapeDtypeStruct((r, f), BF16),
        grid_spec=pltpu.PrefetchScalarGridSpec(
            num_scalar_prefetch=5,
            grid=(f // tn, nt + 1),
            in_specs=[pl.BlockSpec((tm, d), lambda j, s, te, tf, tv, tr, ts: (tr[_done_tile(s)], 0)),
                      wspec, wspec],
            out_specs=pl.BlockSpec((tm, tn), lambda j, s, te, tf, tv, tr, ts: (_done_tile(s), j)),
            scratch_shapes=[pltpu.VMEM((1, d, tn), BF16)] * 2),
        compiler_params=_params(("arbitrary", "arbitrary"), blocks,
                                scratch=[((1, d, tn), BF16)] * 2, temps=[((tm, tn), F32)] * 3),
        name="moe_up",
    )(*tiles, xs, w_gate, w_up)


def _moe_down_kernel(te_ref, tf_ref, tn_ref, tr_ref, ts_ref, a_ref, w_ref, o_ref, wbf_ref):
    _grouped_step((te_ref, tf_ref, tn_ref, tr_ref, ts_ref), (w_ref,), (wbf_ref,),
                  lambda w, n: _dot(a_ref[0:n, :], w[0]), o_ref)


def _moe_down(hmid, w_down, layer, tiles):
    r, f = hmid.shape
    d = w_down.shape[-1]
    tm, tn = MOE_UP_ROW_TILE, MOE_DOWN_COL_TILE
    assert d % tn == 0 and r % tm == 0
    tiles = _split_tiles(tiles, MOE_ROW_TILE // tm)
    nt = r // tm
    blocks = [((tm, f), BF16), ((f, tn), F32), ((tm, tn), F32)]
    return pl.pallas_call(
        _moe_down_kernel,
        out_shape=jax.ShapeDtypeStruct((r, d), F32),
        grid_spec=pltpu.PrefetchScalarGridSpec(
            num_scalar_prefetch=5,
            grid=(d // tn, nt + 1),
            in_specs=[pl.BlockSpec((tm, f), lambda j, s, te, tf, tv, tr, ts: (tr[_done_tile(s)], 0)),
                      pl.BlockSpec((None, None, f, tn),
                                   lambda j, s, te, tf, tv, tr, ts:
                                   (layer, te[_held_tile(s, nt)], 0, j))],
            out_specs=pl.BlockSpec((tm, tn), lambda j, s, te, tf, tv, tr, ts: (_done_tile(s), j)),
            scratch_shapes=[pltpu.VMEM((1, f, tn), BF16)]),
        compiler_params=_params(("arbitrary", "arbitrary"), blocks, scratch=[((1, f, tn), BF16)],
                                temps=[((tm, tn), F32)] * 2),
        name="moe_down",
    )(*tiles, hmid, w_down)


def _moe_final_kernel(dest_ref, x_ref, y_ref, route_ref, g_ref, mod_ref, o_ref, buf_ref, sems):
    tile = x_ref.shape[0]
    n_blocks = pl.num_programs(0)
    g = pl.program_id(0)

    def start_block(b):
        def issue(r, carry):
            for k in range(TOP_K):
                _row_copy(y_ref, buf_ref.at[b % 2, k], sems.at[b % 2, k],
                          dest_ref[(b * tile + r) * TOP_K + k], r).start()
            return carry
        lax.fori_loop(0, tile, issue, 0, unroll=GATHER_UNROLL)

    @pl.when(g == 0)
    def _():
        start_block(g)

    @pl.when(g + 1 < n_blocks)
    def _():
        start_block(g + 1)

    def drain(r, carry):
        for k in range(TOP_K):
            _row_copy(y_ref, buf_ref.at[g % 2, k], sems.at[g % 2, k], 0, r).wait()
        return carry

    lax.fori_loop(0, tile, drain, 0, unroll=GATHER_UNROLL)
    f = route_ref[:, TOP_K:TOP_K + 1] * buf_ref[g % 2, 0]
    for k in range(1, TOP_K):
        f = f + route_ref[:, TOP_K + k:TOP_K + k + 1] * buf_ref[g % 2, k]
    o_ref[...] = x_ref[...] + mod_ref[5:6, :] * _rms(f, g_ref[...])


def _moe_final(x1, y, dest, route, g_post, mod, n_rows, seq, n_lat_rows, n_batch):
    d = x1.shape[1]
    tr = EW_ROWS
    ridx = functools.partial(_mod_row_index, rows_per_tile=tr, seq=seq, n_lat_rows=n_lat_rows,
                             n_batch=n_batch)
    row = pl.BlockSpec((tr, d), lambda i, dest_ref: (i, 0))
    blocks = [((tr, d), F32)] * 2 + [((N_MOD, d), F32)]
    return pl.pallas_call(
        _moe_final_kernel,
        out_shape=jax.ShapeDtypeStruct((n_rows, d), F32),
        grid_spec=pltpu.PrefetchScalarGridSpec(
            num_scalar_prefetch=1,
            grid=(n_rows // tr,),
            in_specs=[row, pl.BlockSpec(memory_space=pl.ANY),
                      pl.BlockSpec((tr, V7X_LANES), lambda i, dest_ref: (i, 0)),
                      pl.BlockSpec((1, d), lambda i, dest_ref: (0, 0)),
                      pl.BlockSpec((None, N_MOD, d), lambda i, dest_ref: (ridx(i), 0, 0))],
            out_specs=row,
            scratch_shapes=[pltpu.VMEM((2, TOP_K, tr, d), F32),
                            pltpu.SemaphoreType.DMA((2, TOP_K))]),
        compiler_params=_params(("arbitrary",), blocks, scratch=[((2, TOP_K, tr, d), F32)],
                                temps=[((tr, d), F32)] * 2),
        name="moe_combine_final",
    )(dest, x1, y, route, g_post.reshape(1, d), mod)


def _rope_tables(seq, n_batch, ctx_rows):
    n = jnp.arange(seq)
    pos_r = (n // GRID_W).astype(F32)
    pos_w = (n % GRID_W).astype(F32)
    n_freq = HEAD_DIM // 4
    inv_freq = ROPE_THETA ** (-jnp.arange(n_freq, dtype=F32) / n_freq)
    ar = pos_r[:, None] * inv_freq
    aw = pos_w[:, None] * inv_freq
    cos = jnp.concatenate([jnp.cos(ar), jnp.cos(ar), jnp.cos(aw), jnp.cos(aw)], axis=-1)
    sin = jnp.concatenate([-jnp.sin(ar), jnp.sin(ar), -jnp.sin(aw), jnp.sin(aw)], axis=-1)
    cos = jnp.concatenate([cos] * n_batch + [jnp.ones((ctx_rows, HEAD_DIM), F32)], axis=0)
    sin = jnp.concatenate([sin] * n_batch + [jnp.zeros((ctx_rows, HEAD_DIM), F32)], axis=0)
    return cos, sin


def kernel(x, c, ctx, c_ctx, w_mod, b_mod, g_pre_mix, g_post_mix, g_pre_ffn, g_post_ffn, w_in, w_out,
           sink_a, qn_c, kn_c, gm_ln_g, gm_ln_b, gm_ws, gm_bs, ffn_w_gate, ffn_w_up, ffn_w_down,
           moe_router, moe_w_gate, moe_w_up, moe_w_down):
    n_batch, seq, d = x.shape
    n_ctx = ctx.shape[1]
    depth = w_mod.shape[0]
    n_lat = n_batch * seq
    n_ctx_rows = n_batch * n_ctx
    t = n_lat + n_ctx_rows
    a_heads = sink_a.shape[1]
    a_w = a_heads * HEAD_DIM
    akv_w = A_KV_HEADS * HEAD_DIM
    b_w = gm_ln_g.shape[1]
    ckv_w = C_KV_HEADS * HEAD_DIM
    c_w = w_in.shape[2] - a_w - 2 * akv_w - 2 * b_w - 2 * ckv_w
    c_heads = c_w // HEAD_DIM
    assert n_ctx_rows % EW_ROWS == 0 and seq % EW_ROWS == 0 and n_batch + 1 <= 8
    assert a_w == c_w and akv_w == COL_TILE and ckv_w == COL_TILE

    x_parts = (x.reshape(n_lat, d), ctx.reshape(n_ctx_rows, d))
    c8 = jnp.concatenate([c, c_ctx[None, :], jnp.zeros((8 - n_batch - 1, d), F32)], axis=0)
    mod_all = _modulation(c8, w_mod, b_mod).reshape(depth, 8, N_MOD, d)

    cos, sin = _rope_tables(seq, n_batch, n_ctx_rows)
    geo = dict(seq=seq, n_lat_rows=n_lat, n_batch=n_batch)
    h = _prenorm(x_parts, g_pre_mix[0], mod_all[0], **geo)

    for l in range(depth):
        need_ctx = l < depth - 1
        n_rows = t if need_ctx else n_lat
        mod = mod_all[l]

        gq, gk = qn_c[l].reshape(1, HEAD_DIM), kn_c[l].reshape(1, HEAD_DIM)
        rope = (cos, sin, gq, gk)
        col = 0
        za = _inproj(h, w_in, l, col, a_w + akv_w, "rope", rope, n_q_tiles=a_w // COL_TILE)
        col += a_w + akv_w
        av = _inproj(h, w_in, l, col, akv_w, "plain")
        col += akv_w
        zb = _inproj(h, w_in, l, col, 2 * b_w, "gelu")
        col += 2 * b_w
        zc = _inproj(h, w_in, l, col, c_w + ckv_w, "normrope", rope, n_q_tiles=c_w // COL_TILE)
        col += c_w + ckv_w
        cv = _inproj(h, w_in, l, col, ckv_w, "plain")

        oa = _window_attn(za, av, sink_a[l], n_batch, seq, n_ctx, need_ctx)
        ob = _gmlp(zb, gm_ln_g[l], gm_ln_b[l], gm_ws[l], gm_bs[l], n_rows)
        oc = _global_attn(zc, cv, n_batch, seq, n_ctx, c_heads, need_ctx)

        y = _outproj(oa, ob, oc, w_out, l, n_rows)
        i = l // 2
        next_norm = (g_pre_mix[l + 1], mod_all[l + 1]) if need_ctx else None
        if l % 2 == 0:
            x1, h2 = _postmix(x_parts, y, g_post_mix[l], g_pre_ffn[l], mod, n_rows, None, **geo)
            f_dim = ffn_w_gate.shape[-1]
            hmid = _ffn_up(h2, ffn_w_gate, ffn_w_up, (i,), n_rows)
            half = f_dim // 2
            f = _ffn_down(hmid, ffn_w_down, (i,), 0, half, n_rows)
            f = _ffn_down(hmid, ffn_w_down, (i,), half, half, n_rows, prev=f)
            xall = _final(x1, f, g_post_ffn[l], mod, n_rows, next_norm, **geo)
            if need_ctx:
                xall, h = xall
        else:
            x1, h2f, route = _postmix(x_parts, y, g_post_mix[l], g_pre_ffn[l], mod, n_rows,
                                      moe_router[i], **geo)
            tok_of_row, live, dest, tiles = _route_plan(route, moe_router.shape[-1])
            xs = _gather_rows(h2f, tok_of_row, live)
            hmid = _moe_up(xs, moe_w_gate, moe_w_up, i, tiles)
            ys = _moe_down(hmid, moe_w_down, i, tiles)
            xall = _moe_final(x1, ys, dest, route, g_post_ffn[l], mod, n_rows, **geo)
            if need_ctx:
                h = _prenorm((xall,), next_norm[0], next_norm[1], **geo)
        x_parts = (xall,)
    return xall[:n_lat].reshape(n_batch, seq, d)
```

```python
import functools
import math

import jax
import jax.numpy as jnp
from jax import lax
from jax.experimental import pallas as pl
from jax.experimental.pallas import tpu as pltpu

F32 = jnp.float32
BF16 = jnp.bfloat16

GRID_W = 64
HEAD_DIM = 128
BLOCK = 128
WINDOW = 128
A_KV_HEADS = 4
C_KV_HEADS = 4
B_GROUPS = 8
CHUNK = 128
N_MOD = 6
TOP_K = 2
ROPE_THETA = 10000.0
EPS = 1e-6
LOG2E = math.log2(math.e)

V7X_LANES = 128
V7X_VMEM_SCOPED_CAP = 60000 * 1024

ROW_TILE = 512
MAX_ROW_TILE = 1088
ROW_ALIGN = 16
COL_TILE = 512
FFN_COL_TILE = 256
MOD_COL_TILE = 1024
EW_ROWS = 256
CAST_ROWS = 256
Q_TILE = 256
K_TILE = 512
GATHER_UNROLL = 8
MOE_ROW_TILE = 1024
MOE_UP_ROW_TILE = 512
MOE_DOWN_COL_TILE = 1024
MOE_ROW_STEP = 128
ROUTER_PIECES = 3
EPILOGUE_CHUNKS = 4
SLAB_ROWS = 8


def _nbytes(shape, dtype):
    return math.prod(shape) * jnp.dtype(dtype).itemsize


def _params(semantics, blocks, scratch=(), temps=()):
    need = 2 * sum(_nbytes(s, d) for s, d in blocks)
    need += sum(_nbytes(s, d) for s, d in scratch)
    need += sum(_nbytes(s, d) for s, d in temps)
    limit = min(V7X_VMEM_SCOPED_CAP, max(need + need // 4, 16 * 1024 * 1024))
    return pltpu.CompilerParams(dimension_semantics=semantics, vmem_limit_bytes=limit)


def _row_tile(n_rows):
    for tm in range(MAX_ROW_TILE - MAX_ROW_TILE % ROW_ALIGN, 0, -ROW_ALIGN):
        if n_rows % tm == 0:
            return tm
    raise ValueError(f"no row tile for {n_rows} rows")


def _cast_weight(w_ref, wbf_ref):
    rows = w_ref.shape[0]
    step = CAST_ROWS if rows % CAST_ROWS == 0 else V7X_LANES
    assert rows % step == 0

    def body(r, carry):
        sl = pl.ds(pl.multiple_of(r * step, step), step)
        wbf_ref[sl, :] = w_ref[sl, :].astype(BF16)
        return carry

    lax.fori_loop(0, rows // step, body, 0)


def _dot(a, b):
    return jnp.dot(a, b, preferred_element_type=F32)


def _dot_nt(a, b):
    return lax.dot_general(a, b, (((1,), (1,)), ((), ())), preferred_element_type=F32)


def _rms(x, gain):
    return x * lax.rsqrt(jnp.mean(x * x, axis=-1, keepdims=True) + EPS) * gain


def _to_slabs(x, slab_ref):
    w = slab_ref.shape[2]
    for s in range(slab_ref.shape[1]):
        slab_ref[:, s, :] = x[:, s * w:(s + 1) * w]


def _from_slabs(slab_ref):
    return jnp.concatenate([slab_ref[:, s, :] for s in range(slab_ref.shape[1])], axis=1)


def _mod_kernel(c_ref, w_ref, b_ref, o_ref):
    s = jax.nn.silu(c_ref[...])
    o_ref[...] = _dot(s.astype(BF16), w_ref[...].astype(BF16)) + b_ref[...]


def _modulation(c8, w_mod, b_mod):
    n_layers, d, width = w_mod.shape
    tn = MOD_COL_TILE if width % MOD_COL_TILE == 0 else COL_TILE
    assert width % tn == 0
    blocks = [((8, d), F32), ((d, tn), F32), ((1, tn), F32), ((8, tn), F32)]
    return pl.pallas_call(
        _mod_kernel,
        out_shape=jax.ShapeDtypeStruct((n_layers, 8, width), F32),
        grid=(n_layers, width // tn),
        in_specs=[pl.BlockSpec((8, d), lambda l, j: (0, 0)),
                  pl.BlockSpec((None, d, tn), lambda l, j: (l, 0, j)),
                  pl.BlockSpec((None, 1, tn), lambda l, j: (l, 0, j))],
        out_specs=pl.BlockSpec((None, 8, tn), lambda l, j: (l, 0, j)),
        compiler_params=_params(("arbitrary", "arbitrary"), blocks, temps=[((d, tn), BF16)]),
        name="modulation",
    )(c8, w_mod, b_mod.reshape(n_layers, 1, width))


def _mod_row_index(i, rows_per_tile, seq, n_lat_rows, n_batch):
    lat_tiles = n_lat_rows // rows_per_tile
    return jnp.where(i < lat_tiles, (i * rows_per_tile) // seq, n_batch)


def _row_parts(parts, tr):
    specs, starts, start = [], [], 0
    for p in parts:
        n = p.shape[0] // tr
        assert n * tr == p.shape[0]
        specs.append(pl.BlockSpec((tr, p.shape[1]),
                                  lambda i, *_, start=start, n=n: (jnp.clip(i - start, 0, n - 1), 0)))
        starts.append(start)
        start += n
    return specs, tuple(starts)


def _pick_rows(i, x_refs, starts):
    x = x_refs[0][...]
    for ref, start in zip(x_refs[1:], starts[1:]):
        x = jnp.where(i >= start, ref[...], x)
    return x


def _pre_mix_norm(x, g_ref, mod_ref):
    return (_rms(x, g_ref[...]) * (1.0 + mod_ref[1:2, :]) + mod_ref[0:1, :]).astype(BF16)


def _prenorm_kernel(*refs, starts):
    *x_refs, g_ref, mod_ref, o_ref = refs
    o_ref[...] = _pre_mix_norm(_pick_rows(pl.program_id(0), x_refs, starts), g_ref, mod_ref)


def _prenorm(x_parts, gain, mod, seq, n_lat_rows, n_batch):
    d = x_parts[0].shape[1]
    t = sum(p.shape[0] for p in x_parts)
    tr = EW_ROWS
    ridx = functools.partial(_mod_row_index, rows_per_tile=tr, seq=seq, n_lat_rows=n_lat_rows,
                             n_batch=n_batch)
    x_specs, starts = _row_parts(x_parts, tr)
    blocks = [((tr, d), F32)] * len(x_parts) + [((1, d), F32), ((N_MOD, d), F32), ((tr, d), BF16)]
    return pl.pallas_call(
        functools.partial(_prenorm_kernel, starts=starts),
        out_shape=jax.ShapeDtypeStruct((t, d), BF16),
        grid=(t // tr,),
        in_specs=x_specs + [pl.BlockSpec((1, d), lambda i: (0, 0)),
                            pl.BlockSpec((None, N_MOD, d), lambda i: (ridx(i), 0, 0))],
        out_specs=pl.BlockSpec((tr, d), lambda i: (i, 0)),
        compiler_params=_params(("arbitrary",), blocks, temps=[((tr, d), F32)] * 2),
        name="prenorm",
    )(*x_parts, gain.reshape(1, d), mod)


def _postmix_kernel(*refs, starts, n_parts, n_experts):
    x_refs = refs[:n_parts]
    y_ref, gpost_ref, gpre_ref, mod_ref, *rest = refs[n_parts:]
    x = _pick_rows(pl.program_id(0), x_refs, starts)
    x1 = x + mod_ref[2:3, :] * _rms(y_ref[...].astype(F32), gpost_ref[...])
    h2 = _rms(x1, gpre_ref[...]) * (1.0 + mod_ref[4:5, :]) + mod_ref[3:4, :]
    if n_experts is None:
        x1_ref, h2_ref = rest
        h2_ref[...] = h2.astype(BF16)
    else:
        wr_ref, x1_ref, slab_ref, route_ref = rest
        _to_slabs(h2, slab_ref)
        route_ref[...] = _route(h2.astype(BF16), wr_ref[...], n_experts)
    x1_ref[...] = x1


def _postmix(x_parts, y, g_post, g_pre, mod, n_rows, w_router, seq, n_lat_rows, n_batch):
    d = x_parts[0].shape[1]
    tr = EW_ROWS
    x_specs, starts = _row_parts(x_parts, tr)
    ridx = functools.partial(_mod_row_index, rows_per_tile=tr, seq=seq, n_lat_rows=n_lat_rows,
                             n_batch=n_batch)
    row = pl.BlockSpec((tr, d), lambda i: (i, 0))
    vec = pl.BlockSpec((1, d), lambda i: (0, 0))
    in_specs = x_specs + [row, vec, vec, pl.BlockSpec((None, N_MOD, d), lambda i: (ridx(i), 0, 0))]
    args = [*x_parts, y, g_post.reshape(1, d), g_pre.reshape(1, d), mod]
    blocks = ([((tr, d), F32)] * (len(x_parts) + 2) + [((tr, d), BF16)] * 2 + [((N_MOD, d), F32)])
    if w_router is None:
        n_experts = None
        out_shape = (jax.ShapeDtypeStruct((n_rows, d), F32), jax.ShapeDtypeStruct((n_rows, d), BF16))
        out_specs = (row, row)
    else:
        n_experts = w_router.shape[1]
        assert d % (SLAB_ROWS * V7X_LANES) == 0
        slab = (SLAB_ROWS, d // SLAB_ROWS)
        in_specs.append(pl.BlockSpec((d, V7X_LANES), lambda i: (0, 0)))
        args.append(_router_pieces(w_router))
        out_shape = (jax.ShapeDtypeStruct((n_rows, d), F32),
                     jax.ShapeDtypeStruct((n_rows,) + slab, F32),
                     jax.ShapeDtypeStruct((n_rows, V7X_LANES), F32))
        out_specs = (row, pl.BlockSpec((tr,) + slab, lambda i: (i, 0, 0)),
                     pl.BlockSpec((tr, V7X_LANES), lambda i: (i, 0)))
        blocks += [((d, V7X_LANES), BF16), ((tr, V7X_LANES), F32)]
    return pl.pallas_call(
        functools.partial(_postmix_kernel, starts=starts, n_parts=len(x_parts),
                          n_experts=n_experts),
        out_shape=out_shape,
        grid=(n_rows // tr,),
        in_specs=in_specs,
        out_specs=out_specs,
        compiler_params=_params(("arbitrary",), blocks, temps=[((tr, d), F32)] * 3),
        name="postmix" if w_router is None else "postmix_route",
    )(*args)


def _final_kernel(x_ref, f_ref, g_ref, mod_ref, *rest):
    x = x_ref[...] + mod_ref[5:6, :] * _rms(f_ref[...], g_ref[...])
    if len(rest) == 1:
        (o_ref,) = rest
    else:
        gnext_ref, modnext_ref, o_ref, h_ref = rest
        h_ref[...] = _pre_mix_norm(x, gnext_ref, modnext_ref)
    o_ref[...] = x


def _final(x1, f, g_post, mod, n_rows, next_norm, seq, n_lat_rows, n_batch):
    d = x1.shape[1]
    tr = EW_ROWS
    ridx = functools.partial(_mod_row_index, rows_per_tile=tr, seq=seq, n_lat_rows=n_lat_rows,
                             n_batch=n_batch)
    row = pl.BlockSpec((tr, d), lambda i: (i, 0))
    vec = pl.BlockSpec((1, d), lambda i: (0, 0))
    modspec = pl.BlockSpec((None, N_MOD, d), lambda i: (ridx(i), 0, 0))
    blocks = [((tr, d), F32)] * 3 + [((N_MOD, d), F32)]
    in_specs, args = [row, row, vec, modspec], [x1, f, g_post.reshape(1, d), mod]
    out_shape, out_specs = jax.ShapeDtypeStruct((n_rows, d), F32), row
    if next_norm is not None:
        in_specs += [vec, modspec]
        args += [next_norm[0].reshape(1, d), next_norm[1]]
        out_shape = (out_shape, jax.ShapeDtypeStruct((n_rows, d), BF16))
        out_specs = (row, row)
        blocks += [((N_MOD, d), F32), ((tr, d), BF16)]
    return pl.pallas_call(
        _final_kernel,
        out_shape=out_shape,
        grid=(n_rows // tr,),
        in_specs=in_specs,
        out_specs=out_specs,
        compiler_params=_params(("arbitrary",), blocks, temps=[((tr, d), F32)] * 2),
        name="final_residual",
    )(*args)


def _swap_pairs(x):
    lane = lax.broadcasted_iota(jnp.int32, x.shape, 1)
    quarter = HEAD_DIM // 4
    first = (lane & quarter) == 0
    return jnp.where(first, pltpu.roll(x, HEAD_DIM - quarter, axis=1), pltpu.roll(x, quarter, axis=1))


def _inproj_kernel(*refs, mode, n_q_tiles, q_scale):
    if mode in ("rope", "normrope"):
        h_ref, w_ref, cos_ref, sin_ref, gq_ref, gk_ref, o_ref, wbf_ref = refs
    else:
        h_ref, w_ref, o_ref, wbf_ref = refs
    j = pl.program_id(0)

    @pl.when(pl.program_id(1) == 0)
    def _():
        _cast_weight(w_ref, wbf_ref)

    tm = h_ref.shape[0]
    n_chunks = EPILOGUE_CHUNKS if tm % (EPILOGUE_CHUNKS * ROW_ALIGN) == 0 else 1
    rc = tm // n_chunks
    for c in range(n_chunks):
        rows = slice(c * rc, (c + 1) * rc)
        acc = _dot(h_ref[rows, :], wbf_ref[...])
        if mode == "plain":
            o_ref[rows, :] = acc.astype(BF16)
        elif mode == "gelu":
            o_ref[rows, :] = (0.5 * acc * (1.0 + lax.erf(acc * (2.0 ** -0.5)))).astype(BF16)
        else:
            is_q = j < n_q_tiles
            scale = jnp.where(is_q, q_scale, 1.0).astype(F32)
            cos = cos_ref[rows, :]
            sin = sin_ref[rows, :]
            gain = jnp.where(is_q, gq_ref[...], gk_ref[...])
            for hh in range(acc.shape[1] // HEAD_DIM):
                sl = slice(hh * HEAD_DIM, (hh + 1) * HEAD_DIM)
                xh = acc[:, sl]
                if mode == "normrope":
                    xh = _rms(xh, gain)
                xh = xh * cos + _swap_pairs(xh) * sin
                o_ref[rows, sl] = (xh * scale).astype(BF16)


def _inproj(h, w_in, layer, col_start, width, mode, rope=None, n_q_tiles=0):
    t, d = h.shape
    tm, tn = _row_tile(t), COL_TILE
    assert col_start % tn == 0 and width % tn == 0
    j0 = col_start // tn
    in_specs = [pl.BlockSpec((tm, d), lambda j, i: (i, 0)),
                pl.BlockSpec((None, d, tn), lambda j, i: (layer, 0, j0 + j))]
    args = [h, w_in]
    blocks = [((tm, d), BF16), ((d, tn), F32), ((tm, tn), BF16)]
    if mode in ("rope", "normrope"):
        cos, sin, gq, gk = rope
        tab = pl.BlockSpec((tm, HEAD_DIM), lambda j, i: (i, 0))
        vec = pl.BlockSpec((1, HEAD_DIM), lambda j, i: (0, 0))
        in_specs += [tab, tab, vec, vec]
        args += [cos, sin, gq, gk]
        blocks += [((tm, HEAD_DIM), F32)] * 2
    kern = functools.partial(_inproj_kernel, mode=mode, n_q_tiles=n_q_tiles,
                             q_scale=HEAD_DIM ** -0.5 * LOG2E)
    return pl.pallas_call(
        kern,
        out_shape=jax.ShapeDtypeStruct((t, width), BF16),
        grid=(width // tn, t // tm),
        in_specs=in_specs,
        out_specs=pl.BlockSpec((tm, tn), lambda j, i: (i, j)),
        scratch_shapes=[pltpu.VMEM((d, tn), BF16)],
        compiler_params=_params(("arbitrary", "arbitrary"), blocks, scratch=[((d, tn), BF16)],
                                temps=[((tm, tn), F32)] * 2),
        name="inproj_" + mode,
    )(*args)


def _window_bias(group, n_ctx):
    r = jnp.arange(group * BLOCK)[:, None] % BLOCK
    c = jnp.arange(3 * BLOCK + n_ctx)[None, :]
    in_prev, in_cur = c < BLOCK, (c >= BLOCK) & (c < 2 * BLOCK)
    in_next = (c >= 2 * BLOCK) & (c < 3 * BLOCK)
    band_prev = in_prev & (c < r)
    band_next = in_next & (c - 2 * BLOCK > r)
    hidden = [band_prev | band_next, in_prev | band_next, band_prev | in_next, in_prev | in_next,
              in_prev | in_cur | in_next]
    shape = (group * BLOCK, 3 * BLOCK + n_ctx)
    return jnp.stack([jnp.where(jnp.broadcast_to(h, shape), -jnp.inf, 0.0).astype(F32)
                      for h in hidden])


def _window_attn_kernel(sink_ref, bias_ref, q_ref, kp_ref, kc_ref, kn_ref, kx_ref, vp_ref, vc_ref,
                        vn_ref, vx_ref, o_ref, *, group):
    rows = group * BLOCK
    assert BLOCK & (BLOCK - 1) == 0
    shift = BLOCK.bit_length() - 1
    bias = bias_ref[...]
    rid = lax.broadcasted_iota(jnp.int32, (rows, 1), 0) >> shift
    n_keys = bias.shape[1]
    ones = jnp.ones((n_keys, HEAD_DIM), BF16)
    scores = []
    for hk in range(A_KV_HEADS):
        ksl = slice(hk * HEAD_DIM, (hk + 1) * HEAD_DIM)
        k_all = jnp.concatenate([kp_ref[:, ksl], kc_ref[:, ksl], kn_ref[:, ksl], kx_ref[:, ksl]],
                                axis=0)
        q3 = jnp.concatenate(
            [q_ref[:, (hk * group + g) * HEAD_DIM:(hk * group + g + 1) * HEAD_DIM]
             for g in range(group)], axis=0)
        scores.append(_dot_nt(q3, k_all) + bias)
    for hk in range(A_KV_HEADS):
        ksl = slice(hk * HEAD_DIM, (hk + 1) * HEAD_DIM)
        v_all = jnp.concatenate([vp_ref[:, ksl], vc_ref[:, ksl], vn_ref[:, ksl], vx_ref[:, ksl]],
                                axis=0)
        sink = jnp.zeros((rows, 1), F32)
        for g in range(group):
            sink = jnp.where(rid == g, sink_ref[hk * group + g] * LOG2E, sink)
        s = scores[hk]
        m = jnp.maximum(jnp.max(s, axis=-1, keepdims=True), sink)
        p = jnp.exp2(s - m)
        o_ext = _dot(p.astype(BF16), jnp.concatenate([v_all, ones], axis=1))
        o = o_ext[:, :HEAD_DIM] / (o_ext[:, HEAD_DIM:] + jnp.exp2(sink - m))
        for g in range(group):
            osl = slice((hk * group + g) * HEAD_DIM, (hk * group + g + 1) * HEAD_DIM)
            o_ref[:, osl] = o[g * BLOCK:(g + 1) * BLOCK, :].astype(BF16)


def _window_attn(za, av, sink, n_batch, seq, n_ctx, with_ctx):
    n_heads = sink.shape[0]
    group = n_heads // A_KV_HEADS
    qw = n_heads * HEAD_DIM
    kw = A_KV_HEADS * HEAD_DIM
    assert qw % kw == 0 and seq % BLOCK == 0 and (n_batch * seq) % n_ctx == 0 and WINDOW == BLOCK
    assert n_ctx % BLOCK == 0
    nb = seq // BLOCK
    ncb = n_ctx // BLOCK if with_ctx else 0
    kcol = qw // kw
    ctx0 = (n_batch * seq) // n_ctx

    def qblk(b, n):
        return (jnp.where(n < nb, b * nb + n, n_batch * nb + b * ncb + n - nb), 0)

    def blk(shift):
        return lambda b, n: (b * nb + jnp.clip(n + shift, 0, nb - 1), kcol)

    def vblk(shift):
        return lambda b, n: (b * nb + jnp.clip(n + shift, 0, nb - 1), 0)

    def bias_variant(b, n):
        lat = jnp.where(n == 0, 1, 0) + jnp.where(n == nb - 1, 2, 0)
        return (jnp.where(n < nb, lat, 4), 0, 0)

    kspec = [pl.BlockSpec((BLOCK, kw), blk(s)) for s in (-1, 0, 1)]
    vspec = [pl.BlockSpec((BLOCK, kw), vblk(s)) for s in (-1, 0, 1)]
    n_keys = 3 * BLOCK + n_ctx
    rows = group * BLOCK
    blocks = ([((BLOCK, qw), BF16)] * 2 + [((BLOCK, kw), BF16)] * 6 + [((n_ctx, kw), BF16)] * 2
              + [((rows, n_keys), F32)])
    return pl.pallas_call(
        functools.partial(_window_attn_kernel, group=group),
        out_shape=jax.ShapeDtypeStruct((n_batch * (nb + ncb) * BLOCK, qw), BF16),
        grid=(n_batch, nb + ncb),
        in_specs=[pl.BlockSpec(memory_space=pltpu.SMEM),
                  pl.BlockSpec((None, rows, n_keys), bias_variant),
                  pl.BlockSpec((BLOCK, qw), qblk)]
                 + kspec + [pl.BlockSpec((n_ctx, kw), lambda b, n: (ctx0 + b, kcol))]
                 + vspec + [pl.BlockSpec((n_ctx, kw), lambda b, n: (ctx0 + b, 0))],
        out_specs=pl.BlockSpec((BLOCK, qw), qblk),
        compiler_params=_params(("arbitrary", "arbitrary"), blocks,
                                temps=[((rows, n_keys), F32)] * 4),
        name="window_attn",
    )(sink, _window_bias(group, n_ctx), za, za, za, za, za, av, av, av, av)


def _gmlp_kernel(z_ref, g_ref, b_ref, ws_ref, bst_ref, o_ref):
    width = g_ref.shape[1]
    u = z_ref[:, :width].astype(F32)
    v = z_ref[:, width:].astype(F32)
    mu = jnp.mean(v, axis=-1, keepdims=True)
    vc = v - mu
    var = jnp.mean(vc * vc, axis=-1, keepdims=True)
    vn = (vc * lax.rsqrt(var + EPS) * g_ref[...] + b_ref[...]).astype(BF16)
    gd = width // B_GROUPS
    for g in range(B_GROUPS):
        sl = slice(g * gd, (g + 1) * gd)
        mixed = _dot(ws_ref[g].astype(BF16), vn[:, sl]) + bst_ref[:, g:g + 1]
        o_ref[:, sl] = (u[:, sl] * mixed).astype(BF16)


def _gmlp(zb, ln_g, ln_b, ws, bs, n_rows):
    w2 = zb.shape[1]
    width = w2 // 2
    blocks = [((CHUNK, w2), BF16), ((1, width), F32), ((1, width), F32),
              (ws.shape, F32), ((CHUNK, B_GROUPS), F32), ((CHUNK, width), BF16)]
    return pl.pallas_call(
        _gmlp_kernel,
        out_shape=jax.ShapeDtypeStruct((n_rows, width), BF16),
        grid=(n_rows // CHUNK,),
        in_specs=[pl.BlockSpec((CHUNK, w2), lambda i: (i, 0)),
                  pl.BlockSpec((1, width), lambda i: (0, 0)),
                  pl.BlockSpec((1, width), lambda i: (0, 0)),
                  pl.BlockSpec(ws.shape, lambda i: (0, 0, 0)),
                  pl.BlockSpec((CHUNK, B_GROUPS), lambda i: (0, 0))],
        out_specs=pl.BlockSpec((CHUNK, width), lambda i: (i, 0)),
        compiler_params=_params(("arbitrary",), blocks, temps=[((CHUNK, w2), F32)] * 2),
        name="chunk_gmlp",
    )(zb, ln_g.reshape(1, width), ln_b.reshape(1, width), ws, bs.T)


def _global_attn_kernel(q_ref, kx_ref, vx_ref, k_ref, v_ref, o_ref, vext_ref, *, group,
                        n_lat_tiles, has_ctx_tiles):
    tq = q_ref.shape[0]
    n_ctx = kx_ref.shape[0]
    n_lat = k_ref.shape[0]
    rows = group * tq

    @pl.when(pl.program_id(2) == 0)
    def _():
        vext_ref[0:n_ctx, 0:HEAD_DIM] = vx_ref[...]
        vext_ref[n_ctx:, 0:HEAD_DIM] = v_ref[...]
        vext_ref[:, HEAD_DIM:] = jnp.ones((n_ctx + n_lat, HEAD_DIM), BF16)

    def attend(chunks):
        q3 = jnp.concatenate([q_ref[:, g * HEAD_DIM:(g + 1) * HEAD_DIM] for g in range(group)],
                             axis=0)
        m = jnp.full((rows, HEAD_DIM), -jnp.inf, F32)
        acc = jnp.zeros((rows, 2 * HEAD_DIM), F32)
        for k_chunk, v0, nk in chunks:
            s = _dot_nt(q3, k_chunk())
            m_new = jnp.maximum(m, jnp.broadcast_to(jnp.max(s, axis=-1, keepdims=True), m.shape))
            alpha = jnp.exp2(m - m_new)
            p = jnp.exp2(s - jnp.tile(m_new, (1, nk // HEAD_DIM)))
            acc = jnp.tile(alpha, (1, 2)) * acc + _dot(p.astype(BF16), vext_ref[v0:v0 + nk, :])
            m = m_new
        o = acc[:, :HEAD_DIM] / acc[:, HEAD_DIM:]
        for g in range(group):
            o_ref[:, g * HEAD_DIM:(g + 1) * HEAD_DIM] = o[g * tq:(g + 1) * tq, :].astype(BF16)

    ctx_chunk = [(lambda: kx_ref[...], 0, n_ctx)]
    lat_chunks = [(lambda c=c: k_ref[c:c + K_TILE, :], n_ctx + c, K_TILE)
                  for c in range(0, n_lat, K_TILE)]
    is_lat = pl.program_id(2) < n_lat_tiles
    pl.when(is_lat)(lambda: attend(ctx_chunk + lat_chunks))
    if has_ctx_tiles:
        pl.when(jnp.logical_not(is_lat))(lambda: attend(ctx_chunk))


def _global_attn(zc, cv, n_batch, seq, n_ctx, n_heads, with_ctx):
    group = n_heads // C_KV_HEADS
    gw = group * HEAD_DIM
    tq = Q_TILE
    assert seq % tq == 0 and seq % K_TILE == 0 and (n_batch * seq) % n_ctx == 0 and n_ctx % tq == 0
    nq = seq // tq
    ncq = n_ctx // tq if with_ctx else 0
    ctx0 = (n_batch * seq) // n_ctx
    rows = group * tq

    def qblk(b, hk, i):
        return (jnp.where(i < nq, b * nq + i, n_batch * nq + b * ncq + i - nq), hk)

    blocks = [((tq, gw), BF16)] * 2 + [((n_ctx, HEAD_DIM), BF16)] * 2 + [((seq, HEAD_DIM), BF16)] * 2
    scratch = [((n_ctx + seq, 2 * HEAD_DIM), BF16)]
    return pl.pallas_call(
        functools.partial(_global_attn_kernel, group=group, n_lat_tiles=nq, has_ctx_tiles=ncq > 0),
        out_shape=jax.ShapeDtypeStruct((n_batch * (nq + ncq) * tq, n_heads * HEAD_DIM), BF16),
        grid=(n_batch, C_KV_HEADS, nq + ncq),
        in_specs=[pl.BlockSpec((tq, gw), qblk),
                  pl.BlockSpec((n_ctx, HEAD_DIM), lambda b, hk, i: (ctx0 + b, n_heads + hk)),
                  pl.BlockSpec((n_ctx, HEAD_DIM), lambda b, hk, i: (ctx0 + b, hk)),
                  pl.BlockSpec((seq, HEAD_DIM), lambda b, hk, i: (b, n_heads + hk)),
                  pl.BlockSpec((seq, HEAD_DIM), lambda b, hk, i: (b, hk))],
        out_specs=pl.BlockSpec((tq, gw), qblk),
        scratch_shapes=[pltpu.VMEM(s, d) for s, d in scratch],
        compiler_params=_params(("arbitrary", "arbitrary", "arbitrary"), blocks, scratch=scratch,
                                temps=[((rows, K_TILE), F32)] * 6),
        name="global_attn",
    )(zc, zc, cv, zc, cv)


def _outproj_kernel(a_ref, b_ref, c_ref, w_ref, o_ref, wbf_ref):
    @pl.when(pl.program_id(1) == 0)
    def _():
        _cast_weight(w_ref, wbf_ref)

    ka, kb = a_ref.shape[1], b_ref.shape[1]
    acc = _dot(a_ref[...], wbf_ref[0:ka, :])
    acc += _dot(b_ref[...], wbf_ref[ka:ka + kb, :])
    acc += _dot(c_ref[...], wbf_ref[ka + kb:, :])
    o_ref[...] = acc.astype(BF16)


def _outproj(oa, ob, oc, w_out, layer, n_rows):
    t = oa.shape[0]
    _, k, d = w_out.shape
    tm, tn = _row_tile(n_rows), COL_TILE
    assert oa.shape[1] + ob.shape[1] + oc.shape[1] == k and d % tn == 0
    blocks = [((tm, k), BF16), ((k, tn), F32), ((tm, tn), BF16)]
    return pl.pallas_call(
        _outproj_kernel,
        out_shape=jax.ShapeDtypeStruct((t, d), BF16),
        grid=(d // tn, n_rows // tm),
        in_specs=[pl.BlockSpec((tm, oa.shape[1]), lambda j, i: (i, 0)),
                  pl.BlockSpec((tm, ob.shape[1]), lambda j, i: (i, 0)),
                  pl.BlockSpec((tm, oc.shape[1]), lambda j, i: (i, 0)),
                  pl.BlockSpec((None, k, tn), lambda j, i: (layer, 0, j))],
        out_specs=pl.BlockSpec((tm, tn), lambda j, i: (i, j)),
        scratch_shapes=[pltpu.VMEM((k, tn), BF16)],
        compiler_params=_params(("arbitrary", "arbitrary"), blocks, scratch=[((k, tn), BF16)],
                                temps=[((tm, tn), F32)] * 2),
        name="outproj",
    )(oa, ob, oc, w_out)


def _ffn_up_kernel(h_ref, wg_ref, wu_ref, o_ref, wgbf_ref, wubf_ref):
    @pl.when(pl.program_id(1) == 0)
    def _():
        _cast_weight(wg_ref, wgbf_ref)
        _cast_weight(wu_ref, wubf_ref)

    h = h_ref[...]
    gate = _dot(h, wgbf_ref[...])
    up = _dot(h, wubf_ref[...])
    o_ref[...] = (jax.nn.silu(gate) * up).astype(BF16)


def _ffn_up(h2, w_gate, w_up, sel, n_rows):
    t, d = h2.shape
    f = w_gate.shape[-1]
    tm, tn = _row_tile(n_rows), FFN_COL_TILE
    assert f % tn == 0
    lead = (None,) * len(sel)
    wspec = pl.BlockSpec(lead + (d, tn), lambda j, i: sel + (0, j))
    blocks = [((tm, d), BF16), ((d, tn), F32), ((d, tn), F32), ((tm, tn), BF16)]
    return pl.pallas_call(
        _ffn_up_kernel,
        out_shape=jax.ShapeDtypeStruct((t, f), BF16),
        grid=(f // tn, n_rows // tm),
        in_specs=[pl.BlockSpec((tm, d), lambda j, i: (i, 0)), wspec, wspec],
        out_specs=pl.BlockSpec((tm, tn), lambda j, i: (i, j)),
        scratch_shapes=[pltpu.VMEM((d, tn), BF16)] * 2,
        compiler_params=_params(("arbitrary", "arbitrary"), blocks, scratch=[((d, tn), BF16)] * 2,
                                temps=[((tm, tn), F32)] * 3),
        name="ffn_up",
    )(h2, w_gate, w_up)


def _ffn_down_kernel(*refs, has_prev):
    if has_prev:
        a_ref, w_ref, prev_ref, o_ref, wbf_ref = refs
    else:
        a_ref, w_ref, o_ref, wbf_ref = refs

    @pl.when(pl.program_id(1) == 0)
    def _():
        _cast_weight(w_ref, wbf_ref)

    acc = _dot(a_ref[...], wbf_ref[...])
    if has_prev:
        acc = prev_ref[...] + acc
    o_ref[...] = acc


def _ffn_down(a, w_down, sel, k_start, k_size, n_rows, prev=None):
    t = a.shape[0]
    d = w_down.shape[-1]
    tm, tn = ROW_TILE, COL_TILE
    assert k_start % k_size == 0 and d % tn == 0 and n_rows % tm == 0
    kb = k_start // k_size
    lead = (None,) * len(sel)
    in_specs = [pl.BlockSpec((tm, k_size), lambda j, i: (i, kb)),
                pl.BlockSpec(lead + (k_size, tn), lambda j, i: sel + (kb, j))]
    args = [a, w_down]
    blocks = [((tm, k_size), BF16), ((k_size, tn), F32), ((tm, tn), F32)]
    aliases = {}
    if prev is not None:
        in_specs.append(pl.BlockSpec((tm, tn), lambda j, i: (i, j)))
        aliases = {len(args): 0}
        args.append(prev)
        blocks.append(((tm, tn), F32))
    return pl.pallas_call(
        functools.partial(_ffn_down_kernel, has_prev=prev is not None),
        out_shape=jax.ShapeDtypeStruct((t, d), F32),
        grid=(d // tn, n_rows // tm),
        in_specs=in_specs,
        out_specs=pl.BlockSpec((tm, tn), lambda j, i: (i, j)),
        scratch_shapes=[pltpu.VMEM((k_size, tn), BF16)],
        input_output_aliases=aliases,
        compiler_params=_params(("arbitrary", "arbitrary"), blocks, scratch=[((k_size, tn), BF16)],
                                temps=[((tm, tn), F32)] * 2),
        name="ffn_down",
    )(*args)


def _route(h, w_pieces, n_experts):
    pieces = _dot(h, w_pieces)
    logits = pieces
    for k in range(1, ROUTER_PIECES):
        logits = logits + pltpu.roll(pieces, V7X_LANES - k * n_experts, axis=1)
    lane = lax.broadcasted_iota(jnp.int32, logits.shape, 1).astype(F32)
    neg = -jnp.inf
    logits = jnp.where(lane < n_experts, logits, neg)
    picked = []
    remaining = logits
    for _ in range(TOP_K):
        top = jnp.max(remaining, axis=-1, keepdims=True)
        idx = jnp.min(jnp.where(remaining == top, lane, float(logits.shape[1])), axis=-1,
                      keepdims=True)
        picked.append((top, idx))
        remaining = jnp.where(lane == idx, neg, remaining)
    top0 = picked[0][0]
    denom = sum(jnp.exp(tv - top0) for tv, _ in picked)
    route = jnp.zeros(logits.shape, F32)
    for k, (tv, idx) in enumerate(picked):
        route = jnp.where(lane == k, idx, route)
        route = jnp.where(lane == TOP_K + k, jnp.exp(tv - top0) / denom, route)
    return route


def _router_pieces(w_router):
    n_experts = w_router.shape[1]
    assert ROUTER_PIECES * n_experts <= V7X_LANES
    pieces, rest = [], w_router
    for _ in range(ROUTER_PIECES):
        pieces.append(rest.astype(BF16))
        rest = rest - pieces[-1].astype(F32)
    return jnp.pad(jnp.concatenate(pieces, axis=1),
                   ((0, 0), (0, V7X_LANES - ROUTER_PIECES * n_experts)))


def _route_plan(route, n_experts):
    tile = MOE_ROW_TILE
    n_tok = route.shape[0]
    n_asg = n_tok * TOP_K
    e_flat = route[:, :TOP_K].astype(jnp.int32).reshape(n_asg)
    onehot = (e_flat[:, None] == jnp.arange(n_experts, dtype=jnp.int32)[None, :]).astype(jnp.int32)
    csum = jnp.cumsum(onehot, axis=0)
    rank = jnp.take_along_axis(csum, e_flat[:, None], axis=1)[:, 0] - 1
    counts = csum[-1]
    tiles_per = (counts + tile - 1) // tile
    tile_end = jnp.cumsum(tiles_per)
    tile_start = tile_end - tiles_per
    dest = tile_start[e_flat] * tile + rank

    n_tiles = (n_asg + n_experts * (tile - 1)) // tile
    tile_ids = jnp.arange(n_tiles, dtype=jnp.int32)
    n_used = tile_end[-1]
    valid = tile_ids < n_used
    expert_raw = jnp.sum((tile_end[None, :] <= tile_ids[:, None]).astype(jnp.int32), axis=1)
    tile_expert = jnp.where(valid, expert_raw, expert_raw[n_used - 1])
    tile_first = valid & (tile_ids == tile_start[tile_expert])
    tile_row = jnp.where(valid, tile_ids, n_used - 1)

    tok_of_row = jnp.zeros((n_tiles * tile,), jnp.int32).at[dest].set(
        jnp.arange(n_asg, dtype=jnp.int32) // TOP_K, unique_indices=True)
    tile_first = tile_first.astype(jnp.int32)
    tile_slot = (jnp.cumsum(tile_first) - 1) % 2
    tile_fill = jnp.where(
        valid, jnp.clip(counts[tile_expert] - (tile_ids - tile_start[tile_expert]) * tile, 0, tile), 0)
    tiles = (tile_expert, tile_first, tile_fill, tile_row, tile_slot)
    assert tile % EW_ROWS == 0
    block_start = jnp.arange(n_tiles * tile // EW_ROWS, dtype=jnp.int32) * EW_ROWS
    live = (tile_fill[block_start // tile] > block_start % tile).astype(jnp.int32)
    return tok_of_row, live, dest, tiles


def _split_tiles(tiles, factor):
    tile_expert, tile_first, tile_fill, _, tile_slot = tiles
    n = tile_expert.shape[0] * factor
    sub = MOE_ROW_TILE // factor
    part = jnp.arange(n, dtype=jnp.int32) % factor
    fill = jnp.clip(jnp.repeat(tile_fill, factor) - part * sub, 0, sub)
    first = jnp.where(part == 0, jnp.repeat(tile_first, factor), 0)
    row = lax.cummax(jnp.where(fill > 0, jnp.arange(n, dtype=jnp.int32), 0))
    return jnp.repeat(tile_expert, factor), first, fill, row, jnp.repeat(tile_slot, factor)


def _row_copy(src_hbm, dst_vmem, sem, src_row, dst_row):
    return pltpu.make_async_copy(src_hbm.at[pl.ds(src_row, 1)], dst_vmem.at[pl.ds(dst_row, 1)], sem)


def _gather_rows_kernel(idx_ref, live_ref, src_ref, o_ref, buf_ref, sems):
    tile = buf_ref.shape[1]
    n_blocks = pl.num_programs(0)
    g = pl.program_id(0)

    def start_block(b):
        def issue(r, carry):
            _row_copy(src_ref, buf_ref.at[b % 2], sems.at[b % 2], idx_ref[b * tile + r], r).start()
            return carry
        lax.fori_loop(0, tile, issue, 0, unroll=GATHER_UNROLL)

    @pl.when((g == 0) & (live_ref[0] == 1))
    def _():
        start_block(g)

    nxt = jnp.minimum(g + 1, n_blocks - 1)

    @pl.when((g + 1 < n_blocks) & (live_ref[nxt] == 1))
    def _():
        start_block(nxt)

    @pl.when(live_ref[g] == 1)
    def _():
        def drain(r, carry):
            _row_copy(src_ref, buf_ref.at[g % 2], sems.at[g % 2], 0, r).wait()
            return carry
        lax.fori_loop(0, tile, drain, 0, unroll=GATHER_UNROLL)
        o_ref[...] = _from_slabs(buf_ref.at[g % 2]).astype(BF16)

    @pl.when(live_ref[g] == 0)
    def _():
        o_ref[...] = jnp.zeros(o_ref.shape, BF16)


def _gather_rows(src, idx, live):
    slab = src.shape[1:]
    d = slab[0] * slab[1]
    n = idx.shape[0]
    tg = EW_ROWS
    assert n % tg == 0 and live.shape[0] == n // tg
    blocks = [((tg, d), BF16)]
    return pl.pallas_call(
        _gather_rows_kernel,
        out_shape=jax.ShapeDtypeStruct((n, d), BF16),
        grid_spec=pltpu.PrefetchScalarGridSpec(
            num_scalar_prefetch=2,
            grid=(n // tg,),
            in_specs=[pl.BlockSpec(memory_space=pl.ANY)],
            out_specs=pl.BlockSpec((tg, d), lambda i, idx_ref, live_ref: (i, 0)),
            scratch_shapes=[pltpu.VMEM((2, tg) + slab, F32), pltpu.SemaphoreType.DMA((2,))]),
        compiler_params=_params(("arbitrary",), blocks, scratch=[((2, tg, d), F32)],
                                temps=[((tg, d), F32)]),
        name="moe_dispatch",
    )(idx, live, src)


def _held_tile(s, n_tiles):
    return jnp.minimum(s, n_tiles - 1)


def _done_tile(s):
    return jnp.maximum(s - 1, 0)


def _grouped_step(tables, w_refs, wbf_refs, compute, o_ref):
    _, tf_ref, tn_ref, _, ts_ref = tables
    n_tiles = tf_ref.shape[0]
    n_slots = wbf_refs[0].shape[0]
    tm = o_ref.shape[0]
    s = pl.program_id(1)
    held = _held_tile(s, n_tiles)
    done = _done_tile(s)
    fill = jnp.where(s > 0, tn_ref[done], -MOE_ROW_STEP)

    def slot(tile):
        return ts_ref[tile] % n_slots if n_slots > 1 else 0

    for n in range(0, tm + 1, MOE_ROW_STEP):
        @pl.when((fill > n - MOE_ROW_STEP) & (fill <= n))
        def _(n=n):
            if n > 0:
                o_ref[0:n, :] = compute([wbf_ref[slot(done)] for wbf_ref in wbf_refs], n)
            if n < tm:
                o_ref[n:, :] = jnp.zeros((tm - n, o_ref.shape[1]), o_ref.dtype)

    @pl.when((s < n_tiles) & (tf_ref[held] == 1))
    def _():
        for w_ref, wbf_ref in zip(w_refs, wbf_refs):
            _cast_weight(w_ref, wbf_ref.at[slot(held)])


def _moe_up_kernel(te_ref, tf_ref, tn_ref, tr_ref, ts_ref, x_ref, wg_ref, wu_ref, o_ref, wgbf_ref,
                   wubf_ref):
    def compute(w, n):
        x = x_ref[0:n, :]
        return (jax.nn.silu(_dot(x, w[0])) * _dot(x, w[1])).astype(BF16)

    _grouped_step((te_ref, tf_ref, tn_ref, tr_ref, ts_ref), (wg_ref, wu_ref), (wgbf_ref, wubf_ref),
                  compute, o_ref)


def _moe_up(xs, w_gate, w_up, layer, tiles):
    r, d = xs.shape
    f = w_gate.shape[-1]
    tm, tn = MOE_UP_ROW_TILE, COL_TILE
    assert f % tn == 0 and r % tm == 0
    tiles = _split_tiles(tiles, MOE_ROW_TILE // tm)
    nt = r // tm
    wspec = pl.BlockSpec((None, None, d, tn),
                         lambda j, s, te, tf, tv, tr, ts: (layer, te[_held_tile(s, nt)], 0, j))
    blocks = [((tm, d), BF16), ((d, tn), F32), ((d, tn), F32), ((tm, tn), BF16)]
    return pl.pallas_call(
        _moe_up_kernel,
        out_shape=jax.ShapeDtypeStruct((r, f), BF16),
        grid_spec=pltpu.PrefetchScalarGridSpec(
            num_scalar_prefetch=5,
            grid=(f // tn, nt + 1),
            in_specs=[pl.BlockSpec((tm, d), lambda j, s, te, tf, tv, tr, ts: (tr[_done_tile(s)], 0)),
                      wspec, wspec],
            out_specs=pl.BlockSpec((tm, tn), lambda j, s, te, tf, tv, tr, ts: (_done_tile(s), j)),
            scratch_shapes=[pltpu.VMEM((1, d, tn), BF16)] * 2),
        compiler_params=_params(("arbitrary", "arbitrary"), blocks,
                                scratch=[((1, d, tn), BF16)] * 2, temps=[((tm, tn), F32)] * 3),
        name="moe_up",
    )(*tiles, xs, w_gate, w_up)


def _moe_down_kernel(te_ref, tf_ref, tn_ref, tr_ref, ts_ref, a_ref, w_ref, o_ref, wbf_ref):
    _grouped_step((te_ref, tf_ref, tn_ref, tr_ref, ts_ref), (w_ref,), (wbf_ref,),
                  lambda w, n: _dot(a_ref[0:n, :], w[0]), o_ref)


def _moe_down(hmid, w_down, layer, tiles):
    r, f = hmid.shape
    d = w_down.shape[-1]
    tm, tn = MOE_UP_ROW_TILE, MOE_DOWN_COL_TILE
    assert d % tn == 0 and r % tm == 0
    tiles = _split_tiles(tiles, MOE_ROW_TILE // tm)
    nt = r // tm
    blocks = [((tm, f), BF16), ((f, tn), F32), ((tm, tn), F32)]
    return pl.pallas_call(
        _moe_down_kernel,
        out_shape=jax.ShapeDtypeStruct((r, d), F32),
        grid_spec=pltpu.PrefetchScalarGridSpec(
            num_scalar_prefetch=5,
            grid=(d // tn, nt + 1),
            in_specs=[pl.BlockSpec((tm, f), lambda j, s, te, tf, tv, tr, ts: (tr[_done_tile(s)], 0)),
                      pl.BlockSpec((None, None, f, tn),
                                   lambda j, s, te, tf, tv, tr, ts:
                                   (layer, te[_held_tile(s, nt)], 0, j))],
            out_specs=pl.BlockSpec((tm, tn), lambda j, s, te, tf, tv, tr, ts: (_done_tile(s), j)),
            scratch_shapes=[pltpu.VMEM((1, f, tn), BF16)]),
        compiler_params=_params(("arbitrary", "arbitrary"), blocks, scratch=[((1, f, tn), BF16)],
                                temps=[((tm, tn), F32)] * 2),
        name="moe_down",
    )(*tiles, hmid, w_down)


def _moe_final_kernel(dest_ref, x_ref, y_ref, route_ref, g_ref, mod_ref, o_ref, buf_ref, sems):
    tile = x_ref.shape[0]
    n_blocks = pl.num_programs(0)
    g = pl.program_id(0)

    def start_block(b):
        def issue(r, carry):
            for k in range(TOP_K):
                _row_copy(y_ref, buf_ref.at[b % 2, k], sems.at[b % 2, k],
                          dest_ref[(b * tile + r) * TOP_K + k], r).start()
            return carry
        lax.fori_loop(0, tile, issue, 0, unroll=GATHER_UNROLL)

    @pl.when(g == 0)
    def _():
        start_block(g)

    @pl.when(g + 1 < n_blocks)
    def _():
        start_block(g + 1)

    def drain(r, carry):
        for k in range(TOP_K):
            _row_copy(y_ref, buf_ref.at[g % 2, k], sems.at[g % 2, k], 0, r).wait()
        return carry

    lax.fori_loop(0, tile, drain, 0, unroll=GATHER_UNROLL)
    f = route_ref[:, TOP_K:TOP_K + 1] * buf_ref[g % 2, 0]
    for k in range(1, TOP_K):
        f = f + route_ref[:, TOP_K + k:TOP_K + k + 1] * buf_ref[g % 2, k]
    o_ref[...] = x_ref[...] + mod_ref[5:6, :] * _rms(f, g_ref[...])


def _moe_final(x1, y, dest, route, g_post, mod, n_rows, seq, n_lat_rows, n_batch):
    d = x1.shape[1]
    tr = EW_ROWS
    ridx = functools.partial(_mod_row_index, rows_per_tile=tr, seq=seq, n_lat_rows=n_lat_rows,
                             n_batch=n_batch)
    row = pl.BlockSpec((tr, d), lambda i, dest_ref: (i, 0))
    blocks = [((tr, d), F32)] * 2 + [((N_MOD, d), F32)]
    return pl.pallas_call(
        _moe_final_kernel,
        out_shape=jax.ShapeDtypeStruct((n_rows, d), F32),
        grid_spec=pltpu.PrefetchScalarGridSpec(
            num_scalar_prefetch=1,
            grid=(n_rows // tr,),
            in_specs=[row, pl.BlockSpec(memory_space=pl.ANY),
                      pl.BlockSpec((tr, V7X_LANES), lambda i, dest_ref: (i, 0)),
                      pl.BlockSpec((1, d), lambda i, dest_ref: (0, 0)),
                      pl.BlockSpec((None, N_MOD, d), lambda i, dest_ref: (ridx(i), 0, 0))],
            out_specs=row,
            scratch_shapes=[pltpu.VMEM((2, TOP_K, tr, d), F32),
                            pltpu.SemaphoreType.DMA((2, TOP_K))]),
        compiler_params=_params(("arbitrary",), blocks, scratch=[((2, TOP_K, tr, d), F32)],
                                temps=[((tr, d), F32)] * 2),
        name="moe_combine_final",
    )(dest, x1, y, route, g_post.reshape(1, d), mod)


def _rope_tables(seq, n_batch, ctx_rows):
    n = jnp.arange(seq)
    pos_r = (n // GRID_W).astype(F32)
    pos_w = (n % GRID_W).astype(F32)
    n_freq = HEAD_DIM // 4
    inv_freq = ROPE_THETA ** (-jnp.arange(n_freq, dtype=F32) / n_freq)
    ar = pos_r[:, None] * inv_freq
    aw = pos_w[:, None] * inv_freq
    cos = jnp.concatenate([jnp.cos(ar), jnp.cos(ar), jnp.cos(aw), jnp.cos(aw)], axis=-1)
    sin = jnp.concatenate([-jnp.sin(ar), jnp.sin(ar), -jnp.sin(aw), jnp.sin(aw)], axis=-1)
    cos = jnp.concatenate([cos] * n_batch + [jnp.ones((ctx_rows, HEAD_DIM), F32)], axis=0)
    sin = jnp.concatenate([sin] * n_batch + [jnp.zeros((ctx_rows, HEAD_DIM), F32)], axis=0)
    return cos, sin


def kernel(x, c, ctx, c_ctx, w_mod, b_mod, g_pre_mix, g_post_mix, g_pre_ffn, g_post_ffn, w_in, w_out,
           sink_a, qn_c, kn_c, gm_ln_g, gm_ln_b, gm_ws, gm_bs, ffn_w_gate, ffn_w_up, ffn_w_down,
           moe_router, moe_w_gate, moe_w_up, moe_w_down):
    n_batch, seq, d = x.shape
    n_ctx = ctx.shape[1]
    depth = w_mod.shape[0]
    n_lat = n_batch * seq
    n_ctx_rows = n_batch * n_ctx
    t = n_lat + n_ctx_rows
    a_heads = sink_a.shape[1]
    a_w = a_heads * HEAD_DIM
    akv_w = A_KV_HEADS * HEAD_DIM
    b_w = gm_ln_g.shape[1]
    ckv_w = C_KV_HEADS * HEAD_DIM
    c_w = w_in.shape[2] - a_w - 2 * akv_w - 2 * b_w - 2 * ckv_w
    c_heads = c_w // HEAD_DIM
    assert n_ctx_rows % EW_ROWS == 0 and seq % EW_ROWS == 0 and n_batch + 1 <= 8
    assert a_w == c_w and akv_w == COL_TILE and ckv_w == COL_TILE

    x_parts = (x.reshape(n_lat, d), ctx.reshape(n_ctx_rows, d))
    c8 = jnp.concatenate([c, c_ctx[None, :], jnp.zeros((8 - n_batch - 1, d), F32)], axis=0)
    mod_all = _modulation(c8, w_mod, b_mod).reshape(depth, 8, N_MOD, d)

    cos, sin = _rope_tables(seq, n_batch, n_ctx_rows)
    geo = dict(seq=seq, n_lat_rows=n_lat, n_batch=n_batch)
    h = _prenorm(x_parts, g_pre_mix[0], mod_all[0], **geo)

    for l in range(depth):
        need_ctx = l < depth - 1
        n_rows = t if need_ctx else n_lat
        mod = mod_all[l]

        gq, gk = qn_c[l].reshape(1, HEAD_DIM), kn_c[l].reshape(1, HEAD_DIM)
        rope = (cos, sin, gq, gk)
        col = 0
        za = _inproj(h, w_in, l, col, a_w + akv_w, "rope", rope, n_q_tiles=a_w // COL_TILE)
        col += a_w + akv_w
        av = _inproj(h, w_in, l, col, akv_w, "plain")
        col += akv_w
        zb = _inproj(h, w_in, l, col, 2 * b_w, "gelu")
        col += 2 * b_w
        zc = _inproj(h, w_in, l, col, c_w + ckv_w, "normrope", rope, n_q_tiles=c_w // COL_TILE)
        col += c_w + ckv_w
        cv = _inproj(h, w_in, l, col, ckv_w, "plain")

        oa = _window_attn(za, av, sink_a[l], n_batch, seq, n_ctx, need_ctx)
        ob = _gmlp(zb, gm_ln_g[l], gm_ln_b[l], gm_ws[l], gm_bs[l], n_rows)
        oc = _global_attn(zc, cv, n_batch, seq, n_ctx, c_heads, need_ctx)

        y = _outproj(oa, ob, oc, w_out, l, n_rows)
        i = l // 2
        next_norm = (g_pre_mix[l + 1], mod_all[l + 1]) if need_ctx else None
        if l % 2 == 0:
            x1, h2 = _postmix(x_parts, y, g_post_mix[l], g_pre_ffn[l], mod, n_rows, None, **geo)
            f_dim = ffn_w_gate.shape[-1]
            hmid = _ffn_up(h2, ffn_w_gate, ffn_w_up, (i,), n_rows)
            half = f_dim // 2
            f = _ffn_down(hmid, ffn_w_down, (i,), 0, half, n_rows)
            f = _ffn_down(hmid, ffn_w_down, (i,), half, half, n_rows, prev=f)
            xall = _final(x1, f, g_post_ffn[l], mod, n_rows, next_norm, **geo)
            if need_ctx:
                xall, h = xall
        else:
            x1, h2f, route = _postmix(x_parts, y, g_post_mix[l], g_pre_ffn[l], mod, n_rows,
                                      moe_router[i], **geo)
            tok_of_row, live, dest, tiles = _route_plan(route, moe_router.shape[-1])
            xs = _gather_rows(h2f, tok_of_row, live)
            hmid = _moe_up(xs, moe_w_gate, moe_w_up, i, tiles)
            ys = _moe_down(hmid, moe_w_down, i, tiles)
            xall = _moe_final(x1, ys, dest, route, g_post_ffn[l], mod, n_rows, **geo)
            if need_ctx:
                h = _prenorm((xall,), next_norm[0], next_norm[1], **geo)
        x_parts = (xall,)
    return xall[:n_lat].reshape(n_batch, seq, d)
```
